```python
import math
import jax
import jax.numpy as jnp
from jax import lax
import numpy as np

D_MODEL = 1024
BATCH = 2
SEQ = 16384
DEPTH = 4

GRID_W = 64
CTX_LEN = 256
N_MOD = 9
D_FF = 2816
RMS_EPS = 1e-6
NEG_INF = -1e30
ROPE_THETA = 10000.0
N_MIX_GROUPS = 4
GROUP_W = D_MODEL // N_MIX_GROUPS
MIX_W = N_MIX_GROUPS * GROUP_W

MLA_HEADS = 4
MLA_NOPE = 64
MLA_ROPE = 32
MLA_V = GROUP_W // MLA_HEADS
MLA_Q_LORA = 256
MLA_KV_LORA = 128
MLA_QBLK = 128
NAT_HEADS = 4
NAT_HD = GROUP_W // NAT_HEADS
NAT_KR = 8
NAT_KC = 16
GDN_HEADS = 4
GDN_DK = 64
GDN_DV = GROUP_W // GDN_HEADS
GDN_CHUNK = 64
SSM_HEADS = 4
SSM_P = GROUP_W // SSM_HEADS
SSM_GROUPS = 2
SSM_STATE = 128
SSM_CHUNK = 128
CONV_K = 5

MLA_COLS = MLA_Q_LORA + MLA_KV_LORA + MLA_ROPE
NAT_COLS = 3 * GROUP_W
GDN_QK = GDN_HEADS * GDN_DK
GDN_QKV = 2 * GDN_QK + GDN_HEADS * GDN_DV
GDN_COLS = GDN_QKV + GROUP_W + 4 * GDN_HEADS
SSM_XBC = GROUP_W + 2 * SSM_GROUPS * SSM_STATE
SSM_COLS = GROUP_W + SSM_XBC + 2 * SSM_HEADS
IN_COLS = MLA_COLS + NAT_COLS + GDN_COLS + SSM_COLS

kernel_name = 'hybrid_parallel_group_diffusion_trunk'


def rms_norm(x, g):
    xf = x.astype(jnp.float32)
    y = xf * lax.rsqrt(jnp.mean(xf * xf, axis=-1, keepdims=True) + RMS_EPS)
    return (y * g.astype(jnp.float32)).astype(x.dtype)


def l2_norm(x):
    xf = x.astype(jnp.float32)
    return xf * lax.rsqrt(jnp.sum(xf * xf, axis=-1, keepdims=True) + RMS_EPS)


def modulate(x, g_pre, shift, scale):
    return rms_norm(x, g_pre) * (1.0 + scale) + shift


def swiglu(h, w1, w3, w2):
    return (jax.nn.silu(h @ w1) * (h @ w3)) @ w2


def half_ffn(v, shift, scale, gate, g_pre, g_post, w1, w3, w2):
    out = swiglu(modulate(v, g_pre, shift, scale), w1, w3, w2)
    return v + 0.5 * gate * rms_norm(out, g_post)


def flip_t(t):
    return jnp.flip(t, axis=1)


def centred_dwconv(x, w, b=None):
    y = lax.conv_general_dilated(
        x, w[:, None, :].astype(x.dtype), window_strides=(1,),
        padding=[(CONV_K // 2, CONV_K // 2)],
        dimension_numbers=('NWC', 'WIO', 'NWC'), feature_group_count=x.shape[-1])
    return y if b is None else y + b


def axial_rope(x, rows, cols):
    half = x.shape[-1] // 2
    quarter = half // 2
    freqs = ROPE_THETA ** (-jnp.arange(quarter, dtype=jnp.float32) / quarter)

    def rot(xa, pos):
        ang = pos[:, None] * freqs
        cos = jnp.cos(ang)[None, :, None, :]
        sin = jnp.sin(ang)[None, :, None, :]
        x1, x2 = xa[..., :quarter], xa[..., quarter:]
        return jnp.concatenate([x1 * cos - x2 * sin, x2 * cos + x1 * sin], axis=-1)

    out = jnp.concatenate([rot(x[..., :half], rows), rot(x[..., half:], cols)], axis=-1)
    return out.astype(x.dtype)


def mla_mixer(zl, zc, rows, cols, gq, gkv, wuq, wukv, gout, need_ctx):
    n_b, n_t = zl.shape[:2]
    dqk = MLA_NOPE + MLA_ROPE
    scale = dqk ** -0.5

    def project(z):
        t = z.shape[1]
        cq, ckv, k_rope = jnp.split(z, [MLA_Q_LORA, MLA_Q_LORA + MLA_KV_LORA], axis=-1)
        q = (rms_norm(cq, gq) @ wuq).reshape(n_b, t, MLA_HEADS, dqk)
        kv = (rms_norm(ckv, gkv) @ wukv).reshape(n_b, t, MLA_HEADS, MLA_NOPE + MLA_V)
        return q, kv[..., :MLA_NOPE], kv[..., MLA_NOPE:], k_rope[:, :, None, :]

    def keys(k_nope, k_rope):
        return jnp.concatenate(
            [k_nope, jnp.broadcast_to(k_rope, k_nope.shape[:-1] + (MLA_ROPE,))], axis=-1)

    q_l, kn_l, v_l, kr_l = project(zl)
    q_c, kn_c, v_c, kr_c = project(zc)
    q_l = jnp.concatenate([q_l[..., :MLA_NOPE], axial_rope(q_l[..., MLA_NOPE:], rows, cols)], axis=-1)
    k_l = keys(kn_l, axial_rope(kr_l, rows, cols))
    k_c = keys(kn_c, kr_c)

    n_blk = n_t // MLA_QBLK
    q_blocks = jnp.moveaxis(q_l.reshape(n_b, n_blk, MLA_QBLK, MLA_HEADS, dqk), 1, 0)

    def attend_block(qb):
        s = jnp.concatenate([jnp.einsum('bqhd,bkhd->bhqk', qb, k_l),
                             jnp.einsum('bqhd,bkhd->bhqk', qb, k_c)], axis=-1)
        p = jax.nn.softmax(s.astype(jnp.float32) * scale, axis=-1).astype(v_l.dtype)
        return (jnp.einsum('bhqk,bkhd->bqhd', p[..., :n_t], v_l)
                + jnp.einsum('bhqk,bkhd->bqhd', p[..., n_t:], v_c))

    o_l = jnp.moveaxis(lax.map(attend_block, q_blocks), 0, 1).reshape(n_b, n_t, GROUP_W)
    out_l = rms_norm(o_l, gout)
    out_c = None
    if need_ctx:
        s = jnp.einsum('bqhd,bkhd->bhqk', q_c, k_c).astype(jnp.float32) * scale
        p = jax.nn.softmax(s, axis=-1).astype(v_c.dtype)
        out_c = rms_norm(jnp.einsum('bhqk,bkhd->bqhd', p, v_c).reshape(n_b, -1, GROUP_W), gout)
    return out_l, out_c


def nat_mixer(zl, zc, rpb, gout, need_ctx):
    n_b, n_t = zl.shape[:2]
    n_rows = n_t // GRID_W
    kr = min(NAT_KR, n_rows)
    n_ctx = zc.shape[1]
    scale = NAT_HD ** -0.5
    ql, kl, vl = [t.reshape(n_b, n_rows, GRID_W, NAT_HEADS, NAT_HD) for t in jnp.split(zl, 3, axis=-1)]
    qc, kc, vc = [t.reshape(n_b, n_ctx, NAT_HEADS, NAT_HD) for t in jnp.split(zc, 3, axis=-1)]

    r = jnp.arange(n_rows)
    r0 = jnp.clip(r - kr // 2, 0, n_rows - kr)
    ridx = r0[:, None] + jnp.arange(kr)[None, :]
    cq = jnp.arange(GRID_W)
    c0 = jnp.clip(cq - NAT_KC // 2, 0, GRID_W - NAT_KC)
    cmask = (cq[None, :] >= c0[:, None]) & (cq[None, :] < c0[:, None] + NAT_KC)

    k_win = kl[:, ridx]
    v_win = vl[:, ridx]
    dr = ridx - r[:, None] + (NAT_KR - 1)
    dc = jnp.clip(cq[None, :] - cq[:, None] + (NAT_KC - 1), 0, 2 * NAT_KC - 2)
    bias = rpb[:, dr[:, None, :, None], dc[None, :, None, :]]
    bias = jnp.moveaxis(bias, 0, 1).astype(jnp.float32)

    s_win = jnp.einsum('brqhd,brjkhd->brhqjk', ql, k_win).astype(jnp.float32) * scale + bias
    s_win = jnp.where(cmask[:, None, :], s_win, NEG_INF)
    s_ctx = jnp.einsum('brqhd,bchd->brhqc', ql, kc).astype(jnp.float32) * scale
    n_win = kr * GRID_W
    s = jnp.concatenate([s_win.reshape(n_b, n_rows, NAT_HEADS, GRID_W, n_win), s_ctx], axis=-1)
    p = jax.nn.softmax(s, axis=-1).astype(vl.dtype)
    p_win = p[..., :n_win].reshape(n_b, n_rows, NAT_HEADS, GRID_W, kr, GRID_W)
    o = (jnp.einsum('brhqjk,brjkhd->brqhd', p_win, v_win)
         + jnp.einsum('brhqc,bchd->brqhd', p[..., n_win:], vc))
    out_l = rms_norm(o.reshape(n_b, n_t, GROUP_W), gout)
    out_c = None
    if need_ctx:
        sc = jnp.einsum('bqhd,bkhd->bhqk', qc, kc).astype(jnp.float32) * scale
        pc = jax.nn.softmax(sc, axis=-1).astype(vc.dtype)
        out_c = rms_norm(jnp.einsum('bhqk,bkhd->bqhd', pc, vc).reshape(n_b, n_ctx, GROUP_W), gout)
    return out_l, out_c


def gdn_chunked(q, k, v, g, beta, s0):
    n_b, n_t, n_h, _ = q.shape
    dv = v.shape[-1]
    cs = GDN_CHUNK
    n_c = n_t // cs

    def chunks(t):
        return jnp.moveaxis(t.reshape((n_b, n_c, cs) + t.shape[2:]), 3, 2)

    qc, kc, vc = chunks(q), chunks(k), chunks(v)
    gc = jnp.cumsum(chunks(g), axis=-1)
    bc = chunks(beta)
    incl = jnp.tril(jnp.ones((cs, cs), bool))
    strict = jnp.tril(jnp.ones((cs, cs), bool), -1)
    diff = gc[..., :, None] - gc[..., None, :]
    decay = jnp.where(incl, jnp.exp(jnp.where(incl, diff, 0.0)), 0.0)
    kb = kc * bc[..., None]
    lmat = jnp.where(strict, jnp.einsum('bnhid,bnhjd->bnhij', kb, kc) * decay, 0.0)
    rhs = jnp.concatenate([vc * bc[..., None], kb * jnp.exp(gc)[..., None]], axis=-1)
    sol = lax.linalg.triangular_solve(lmat, rhs, left_side=True, lower=True, unit_diagonal=True)
    u, w = sol[..., :dv], sol[..., dv:]
    a_intra = jnp.einsum('bnhid,bnhjd->bnhij', qc, kc) * decay
    q_dec = qc * jnp.exp(gc)[..., None]
    k_dec = kc * jnp.exp(gc[..., -1:] - gc)[..., None]
    g_last = jnp.exp(gc[..., -1])

    def step(s, inp):
        qd, kd, uu, ww, aa, gl = inp
        v_new = uu - jnp.einsum('bhck,bhkv->bhcv', ww, s)
        o = jnp.einsum('bhck,bhkv->bhcv', qd, s) + jnp.einsum('bhij,bhjv->bhiv', aa, v_new)
        s = s * gl[..., None, None] + jnp.einsum('bhck,bhcv->bhkv', kd, v_new)
        return s, o

    xs = tuple(jnp.moveaxis(t, 1, 0) for t in (q_dec, k_dec, u, w, a_intra, g_last))
    s_fin, o = lax.scan(step, s0, xs)
    o = jnp.moveaxis(jnp.moveaxis(o, 0, 1), 2, 3).reshape(n_b, n_t, n_h, dv)
    return o, s_fin


def gdn_mixer(zl, zc, conv_w, a_log, dt_bias, g_norm, need_ctx):
    f32 = jnp.float32
    n_b = zl.shape[0]

    def prep(z):
        n_t = z.shape[1]
        qkv, zg, b_raw, a_raw = jnp.split(
            z, [GDN_QKV, GDN_QKV + GROUP_W, GDN_QKV + GROUP_W + 2 * GDN_HEADS], axis=-1)
        qkv = jax.nn.silu(centred_dwconv(qkv, conv_w)).astype(f32)
        q, k, v = jnp.split(qkv, [GDN_QK, 2 * GDN_QK], axis=-1)
        q = l2_norm(q.reshape(n_b, n_t, GDN_HEADS, GDN_DK)) * (GDN_DK ** -0.5)
        k = l2_norm(k.reshape(n_b, n_t, GDN_HEADS, GDN_DK))
        v = v.reshape(n_b, n_t, GDN_HEADS, GDN_DV)
        beta = jax.nn.sigmoid(b_raw.astype(f32)).reshape(n_b, n_t, 2, GDN_HEADS)
        g = -jnp.exp(a_log.astype(f32)) * jax.nn.softplus(
            a_raw.astype(f32).reshape(n_b, n_t, 2, GDN_HEADS) + dt_bias.astype(f32))
        return q, k, v, beta, g, zg

    ql, kl, vl, bl, gl, zgl = prep(zl)
    qc, kc, vc, bc, gcx, zgc = prep(zc)
    s0 = jnp.zeros((n_b, GDN_HEADS, GDN_DK, GDN_DV), f32)
    oc_f, sc_f = gdn_chunked(qc, kc, vc, gcx[:, :, 0], bc[:, :, 0], s0)
    ol_f, _ = gdn_chunked(ql, kl, vl, gl[:, :, 0], bl[:, :, 0], sc_f)
    oc_b, sc_b = gdn_chunked(flip_t(qc), flip_t(kc), flip_t(vc), flip_t(gcx[:, :, 1]), flip_t(bc[:, :, 1]), s0)
    ol_b, _ = gdn_chunked(flip_t(ql), flip_t(kl), flip_t(vl), flip_t(gl[:, :, 1]), flip_t(bl[:, :, 1]), sc_b)

    def finish(o, zg):
        n_t = o.shape[1]
        gate = jax.nn.silu(zg.astype(f32)).reshape(n_b, n_t, GDN_HEADS, GDN_DV)
        return (rms_norm(o, g_norm) * gate).reshape(n_b, n_t, GROUP_W).astype(zg.dtype)

    out_l = finish(ol_f + flip_t(ol_b), zgl)
    out_c = finish(oc_f + flip_t(oc_b), zgc) if need_ctx else None
    return out_l, out_c


def ssd_chunked(x, dt, a, bm, cm, s0):
    n_b, n_t, n_h, n_p = x.shape
    cs = SSM_CHUNK
    n_c = n_t // cs
    xc = (x * dt[..., None]).reshape(n_b, n_c, cs, n_h, n_p)
    bc = bm.reshape(n_b, n_c, cs, n_h, SSM_STATE)
    cc = cm.reshape(n_b, n_c, cs, n_h, SSM_STATE)
    acum = jnp.cumsum(jnp.moveaxis((dt * a).reshape(n_b, n_c, cs, n_h), 3, 2), axis=-1)
    incl = jnp.tril(jnp.ones((cs, cs), bool))
    seg = jnp.where(incl, jnp.exp(jnp.where(incl, acum[..., :, None] - acum[..., None, :], 0.0)), 0.0)
    scores = jnp.einsum('bclhn,bcshn->bchls', cc, bc) * seg
    y_diag = jnp.einsum('bchls,bcshp->bclhp', scores, xc)
    decay_states = jnp.exp(acum[..., -1:] - acum)
    states = jnp.einsum('bclhn,bchl,bclhp->bchpn', bc, decay_states, xc)
    chunk_decay = jnp.exp(acum[..., -1])

    def step(s, inp):
        st, dec = inp
        return s * dec[..., None, None] + st, s

    s_fin, s_in = lax.scan(step, s0, (jnp.moveaxis(states, 1, 0), jnp.moveaxis(chunk_decay, 1, 0)))
    y_off = jnp.einsum('bclhn,bchpn,bchl->bclhp', cc, jnp.moveaxis(s_in, 0, 1), jnp.exp(acum))
    return (y_diag + y_off).reshape(n_b, n_t, n_h, n_p), s_fin


def ssm_mixer(zl, zc, conv_w, conv_b, a_log, dt_bias, d_skip, g_norm, need_ctx):
    f32 = jnp.float32
    n_b = zl.shape[0]
    rep = SSM_HEADS // SSM_GROUPS
    grp_w = GROUP_W // SSM_GROUPS

    def prep(z):
        n_t = z.shape[1]
        zg, xbc, dt_raw = jnp.split(z, [GROUP_W, GROUP_W + SSM_XBC], axis=-1)
        xbc = jax.nn.silu(centred_dwconv(xbc, conv_w, conv_b)).astype(f32)
        xs, bm, cm = jnp.split(xbc, [GROUP_W, GROUP_W + SSM_GROUPS * SSM_STATE], axis=-1)
        bm = jnp.repeat(bm.reshape(n_b, n_t, SSM_GROUPS, SSM_STATE), rep, axis=2)
        cm = jnp.repeat(cm.reshape(n_b, n_t, SSM_GROUPS, SSM_STATE), rep, axis=2)
        dt = jax.nn.softplus(dt_raw.astype(f32).reshape(n_b, n_t, 2, SSM_HEADS) + dt_bias.astype(f32))
        return xs.reshape(n_b, n_t, SSM_HEADS, SSM_P), bm, cm, dt, zg

    a = -jnp.exp(a_log.astype(f32))
    xl, bl, cl, dtl, zgl = prep(zl)
    xc, bc, cc, dtc, zgc = prep(zc)
    s0 = jnp.zeros((n_b, SSM_HEADS, SSM_P, SSM_STATE), f32)
    yc_f, sc_f = ssd_chunked(xc, dtc[:, :, 0], a[0], bc, cc, s0)
    yl_f, _ = ssd_chunked(xl, dtl[:, :, 0], a[0], bl, cl, sc_f)
    yc_b, sc_b = ssd_chunked(flip_t(xc), flip_t(dtc[:, :, 1]), a[1], flip_t(bc), flip_t(cc), s0)
    yl_b, _ = ssd_chunked(flip_t(xl), flip_t(dtl[:, :, 1]), a[1], flip_t(bl), flip_t(cl), sc_b)
    d = d_skip.astype(f32)[:, None]

    def finish(y, xs, zg):
        n_t = y.shape[1]
        y = (y + xs * d).reshape(n_b, n_t, GROUP_W) * jax.nn.silu(zg.astype(f32))
        y = rms_norm(y.reshape(n_b, n_t, SSM_GROUPS, grp_w), g_norm.reshape(SSM_GROUPS, grp_w))
        return y.reshape(n_b, n_t, GROUP_W).astype(zg.dtype)

    out_l = finish(yl_f + flip_t(yl_b), xl, zgl)
    out_c = finish(yc_f + flip_t(yc_b), xc, zgc) if need_ctx else None
    return out_l, out_c


def token_mixing(ul, uc, rows, cols, w_in, w_out, mla_gq, mla_gkv, mla_wuq, mla_wukv, mla_gout,
                 nat_rpb, nat_gout, gdn_conv_w, gdn_a_log, gdn_dt_bias, gdn_gnorm,
                 ssm_conv_w, ssm_conv_b, ssm_a_log, ssm_dt_bias, ssm_d, ssm_gnorm, need_ctx):
    bounds = [MLA_COLS, MLA_COLS + NAT_COLS, MLA_COLS + NAT_COLS + GDN_COLS]
    zla, zlb, zlc, zld = jnp.split(ul @ w_in, bounds, axis=-1)
    zca, zcb, zcc, zcd = jnp.split(uc @ w_in, bounds, axis=-1)
    a_l, a_c = mla_mixer(zla, zca, rows, cols, mla_gq, mla_gkv, mla_wuq, mla_wukv, mla_gout, need_ctx)
    b_l, b_c = nat_mixer(zlb, zcb, nat_rpb, nat_gout, need_ctx)
    c_l, c_c = gdn_mixer(zlc, zcc, gdn_conv_w, gdn_a_log, gdn_dt_bias, gdn_gnorm, need_ctx)
    d_l, d_c = ssm_mixer(zld, zcd, ssm_conv_w, ssm_conv_b, ssm_a_log, ssm_dt_bias, ssm_d, ssm_gnorm, need_ctx)
    y_lat = jnp.concatenate([a_l, b_l, c_l, d_l], axis=-1) @ w_out
    y_ctx = jnp.concatenate([a_c, b_c, c_c, d_c], axis=-1) @ w_out if need_ctx else None
    return y_lat, y_ctx


def setup_inputs(seed: int = 0) -> dict:
    key = jax.random.key(seed)
    ks = iter(jax.random.split(key, 48))
    f32 = jnp.float32
    dm, nl = D_MODEL, DEPTH

    def nrm(shape, scale):
        return jax.random.normal(next(ks), shape, f32) * scale

    def gain(shape):
        return 1.0 + nrm(shape, 0.02)

    def log_a(shape):
        return jnp.log(jax.random.uniform(next(ks), shape, f32, 1.0, 16.0))

    def dt_bias(shape):
        dt = jnp.exp(jax.random.uniform(next(ks), shape, f32, math.log(1e-3), math.log(1e-1)))
        return dt + jnp.log(-jnp.expm1(-dt))

    return {
        'x': nrm((BATCH, SEQ, dm), 1.0),
        'c': nrm((BATCH, dm), 1.0),
        'ctx': nrm((BATCH, CTX_LEN, dm), 1.0),
        'c_ctx': nrm((dm,), 1.0),
        'w_mod': nrm((nl, dm, N_MOD * dm), 0.5 * dm ** -0.5),
        'b_mod': nrm((nl, N_MOD * dm), 0.01),
        'norm_g': gain((nl, 6, dm)),
        'ffn1_w1': nrm((nl, dm, D_FF), dm ** -0.5),
        'ffn1_w3': nrm((nl, dm, D_FF), dm ** -0.5),
        'ffn1_w2': nrm((nl, D_FF, dm), D_FF ** -0.5),
        'ffn2_w1': nrm((nl, dm, D_FF), dm ** -0.5),
        'ffn2_w3': nrm((nl, dm, D_FF), dm ** -0.5),
        'ffn2_w2': nrm((nl, D_FF, dm), D_FF ** -0.5),
        'w_in': nrm((nl, dm, IN_COLS), dm ** -0.5),
        'w_out': nrm((nl, MIX_W, dm), MIX_W ** -0.5),
        'mla_gq': gain((nl, MLA_Q_LORA)),
        'mla_gkv': gain((nl, MLA_KV_LORA)),
        'mla_wuq': nrm((nl, MLA_Q_LORA, MLA_HEADS * (MLA_NOPE + MLA_ROPE)), MLA_Q_LORA ** -0.5),
        'mla_wukv': nrm((nl, MLA_KV_LORA, MLA_HEADS * (MLA_NOPE + MLA_V)), MLA_KV_LORA ** -0.5),
        'mla_gout': gain((nl, GROUP_W)),
        'nat_rpb': nrm((nl, NAT_HEADS, 2 * NAT_KR - 1, 2 * NAT_KC - 1), 0.05),
        'nat_gout': gain((nl, GROUP_W)),
        'gdn_conv_w': nrm((nl, CONV_K, GDN_QKV), CONV_K ** -0.5),
        'gdn_a_log': log_a((nl, 2, GDN_HEADS)),
        'gdn_dt_bias': dt_bias((nl, 2, GDN_HEADS)),
        'gdn_gnorm': gain((nl, GDN_DV)),
        'ssm_conv_w': nrm((nl, CONV_K, SSM_XBC), CONV_K ** -0.5),
        'ssm_conv_b': nrm((nl, SSM_XBC), 0.01),
        'ssm_a_log': log_a((nl, 2, SSM_HEADS)),
        'ssm_dt_bias': dt_bias((nl, 2, SSM_HEADS)),
        'ssm_d': 1.0 + nrm((nl, SSM_HEADS), 0.1),
        'ssm_gnorm': gain((nl, GROUP_W)),
    }


def reference(x, c, ctx, c_ctx, w_mod, b_mod, norm_g, ffn1_w1, ffn1_w3, ffn1_w2,
              ffn2_w1, ffn2_w3, ffn2_w2, w_in, w_out, mla_gq, mla_gkv, mla_wuq, mla_wukv,
              mla_gout, nat_rpb, nat_gout, gdn_conv_w, gdn_a_log, gdn_dt_bias, gdn_gnorm,
              ssm_conv_w, ssm_conv_b, ssm_a_log, ssm_dt_bias, ssm_d, ssm_gnorm):
    n_b, n_tok, _ = x.shape
    pos = jnp.arange(n_tok)
    rows = (pos // GRID_W).astype(jnp.float32)
    cols = (pos % GRID_W).astype(jnp.float32)
    sc = jax.nn.silu(c)
    scc = jax.nn.silu(c_ctx)
    h, hc = x, ctx
    for l in range(DEPTH):
        need_ctx = l < DEPTH - 1
        m_lat = (sc @ w_mod[l] + b_mod[l]).reshape(n_b, N_MOD, 1, D_MODEL)
        m_ctx = (scc @ w_mod[l] + b_mod[l]).reshape(N_MOD, 1, D_MODEL)
        ml = [m_lat[:, i] for i in range(N_MOD)]
        mc = [m_ctx[i] for i in range(N_MOD)]
        g = norm_g[l]
        h = half_ffn(h, ml[0], ml[1], ml[2], g[0], g[1], ffn1_w1[l], ffn1_w3[l], ffn1_w2[l])
        hc = half_ffn(hc, mc[0], mc[1], mc[2], g[0], g[1], ffn1_w1[l], ffn1_w3[l], ffn1_w2[l])
        ul = modulate(h, g[2], ml[3], ml[4])
        uc = modulate(hc, g[2], mc[3], mc[4])
        y_lat, y_ctx = token_mixing(
            ul, uc, rows, cols, w_in[l], w_out[l], mla_gq[l], mla_gkv[l], mla_wuq[l], mla_wukv[l],
            mla_gout[l], nat_rpb[l], nat_gout[l], gdn_conv_w[l], gdn_a_log[l], gdn_dt_bias[l],
            gdn_gnorm[l], ssm_conv_w[l], ssm_conv_b[l], ssm_a_log[l], ssm_dt_bias[l], ssm_d[l],
            ssm_gnorm[l], need_ctx)
        h = h + ml[5] * rms_norm(y_lat, g[3])
        h = half_ffn(h, ml[6], ml[7], ml[8], g[4], g[5], ffn2_w1[l], ffn2_w3[l], ffn2_w2[l])
        if need_ctx:
            hc = hc + mc[5] * rms_norm(y_ctx, g[3])
            hc = half_ffn(hc, mc[6], mc[7], mc[8], g[4], g[5], ffn2_w1[l], ffn2_w3[l], ffn2_w2[l])
    return h
```

```python
import functools
import math

import jax
import jax.numpy as jnp
import numpy as np
from jax import lax
from jax.experimental import pallas as pl
from jax.experimental.pallas import tpu as pltpu

F32 = jnp.float32
BF16 = jnp.bfloat16

D_MODEL = 1024
DEPTH = 4
GRID_W = 64
N_MOD = 9
D_FF = 2816
RMS_EPS = 1e-6
NEG_INF = -1e30
ROPE_THETA = 10000.0
GROUP_W = 256
N_HEADS = 4
HEAD_W = 64
MLA_NOPE = 64
MLA_ROPE = 32
MLA_Q_LORA = 256
MLA_KV_LORA = 128
NAT_KR = 8
NAT_KC = 16
GDN_CHUNK = 64
SSM_STATE = 128
SSM_CHUNK = 128
CONV_K = 5
GDN_QKV = 768
SSM_XBC = 768

LANES = 128
SUBLANES = 8
VMEM_LIMIT = 56 * 1024 * 1024

C_MLA = 0
C_NAT = 640
C_GQKV = 1408
C_GZG = 2176
C_SZG = 2432
C_SXBC = 2688
C_SMALL = 3456
C_TOTAL = 3584
L_BETA, L_G, L_DT = 0, 8, 16

MLA_QSCALE = (MLA_NOPE + MLA_ROPE) ** -0.5 * math.log2(math.e)
NAT_QSCALE = HEAD_W ** -0.5


def _dot(a, b):
    return jnp.dot(a, b, preferred_element_type=F32)


def _dot_nt(a, b):
    return lax.dot_general(a, b, (((1,), (1,)), ((), ())), preferred_element_type=F32)


def _dot_tn(a, b):
    return lax.dot_general(a, b, (((0,), (0,)), ((), ())), preferred_element_type=F32)


def _split(x):
    hi = x.astype(BF16)
    lo = (x - hi.astype(F32)).astype(BF16)
    return hi, lo


def _dot2(a, m):
    hi, lo = _split(a)
    return _dot(hi, m) + _dot(lo, m)


def _dot2l(m, a):
    hi, lo = _split(a)
    return _dot(m, hi) + _dot(m, lo)


def _mm3(a, b):
    ah, al = _split(a)
    bh, bl = _split(b)
    return _dot(ah, bh) + (_dot(ah, bl) + _dot(al, bh))


def _unit_triangular_inverse(a_bd, blk):
    r = _iota(a_bd.shape, 0)
    c = _iota(a_bd.shape, 1)
    base = SUBLANES
    m = jnp.where(r // base == c // base, -a_bd, 0.0)
    t = jnp.where(r == c, 1.0, 0.0) + m
    m = _mm3(m, m)
    t = t + _mm3(t, m)
    m = _mm3(m, m)
    t = t + _mm3(t, m)
    size = 2 * base
    while size <= blk:
        off = (r // size == c // size) & (r // (size // 2) != c // (size // 2))
        t = t - _mm3(t, _mm3(jnp.where(off, a_bd, 0.0), t))
        size *= 2
    return t


def _rms(x, g):
    return x * lax.rsqrt(jnp.mean(x * x, axis=-1, keepdims=True) + RMS_EPS) * g


def _silu(x):
    return x * jax.nn.sigmoid(x)


def _softplus(x):
    return jnp.maximum(x, 0.0) + jnp.log1p(jnp.exp(-jnp.abs(x)))


def _iota(shape, dim):
    return lax.broadcasted_iota(jnp.int32, shape, dim)


def _block_diag(x, n, blk_r, blk_c):
    t = jnp.concatenate([x] * n, axis=0)
    keep = (_iota(t.shape, 0) // blk_r) == (_iota(t.shape, 1) // blk_c)
    return jnp.where(keep, t, 0.0)


def _full(shape):
    nd = len(shape)
    return pl.BlockSpec(shape, lambda *_: (0,) * nd)


def _resident(shape):
    nd = len(shape)
    return pl.BlockSpec(shape, lambda *_: (0,) * nd, pipeline_mode=pl.Buffered(1))


def _mod_kernel(c_ref, w_ref, b_ref, o_ref):
    s = _silu(c_ref[...])
    o_ref[0] = jnp.dot(s, w_ref[0], preferred_element_type=F32,
                       precision=lax.Precision.HIGHEST) + b_ref[0]


def _modulation(cvec, w_mod, b_mod):
    nl, dm, nm = w_mod.shape
    rows = cvec.shape[0]
    tn = 1536
    return pl.pallas_call(
        _mod_kernel,
        out_shape=jax.ShapeDtypeStruct((nl, rows, nm), F32),
        grid=(nl, nm // tn),
        in_specs=[pl.BlockSpec((rows, dm), lambda l, j: (0, 0)),
                  pl.BlockSpec((1, dm, tn), lambda l, j: (l, 0, j)),
                  pl.BlockSpec((1, 1, tn), lambda l, j: (l, 0, j))],
        out_specs=pl.BlockSpec((1, rows, tn), lambda l, j: (l, 0, j)),
        name="modulation",
    )(cvec, w_mod, b_mod.reshape(nl, 1, nm))


FFN_CHUNK = 256


def _ffn_kernel(h_ref, mod_ref, g_ref, w1_ref, w3_ref, w2_ref, o_ref, *, k0, gp):
    x = h_ref[0]
    m = mod_ref[0]
    g = g_ref[...]
    u = _rms(x, g[gp:gp + 1]) * (1.0 + m[k0 + 1:k0 + 2]) + m[k0:k0 + 1]
    ub = u.astype(BF16)
    acc = None
    for c in range(D_FF // FFN_CHUNK):
        sl = slice(c * FFN_CHUNK, (c + 1) * FFN_CHUNK)
        a = _dot(ub, w1_ref[:, sl])
        b = _dot(ub, w3_ref[:, sl])
        hid = (_silu(a) * b).astype(BF16)
        part = _dot(hid, w2_ref[sl, :])
        acc = part if acc is None else acc + part
    o_ref[0] = x + 0.5 * m[k0 + 2:k0 + 3] * _rms(acc, g[gp + 1:gp + 2])


def _half_ffn(h, mod, g, w1, w3, w2, *, k0, gp, tm):
    nb, nt, dm = h.shape
    per_batch = mod.shape[0] > 1
    return pl.pallas_call(
        functools.partial(_ffn_kernel, k0=k0, gp=gp),
        out_shape=jax.ShapeDtypeStruct(h.shape, F32),
        grid=(nb, nt // tm),
        in_specs=[pl.BlockSpec((1, tm, dm), lambda b, i: (b, i, 0)),
                  pl.BlockSpec((1, N_MOD, dm), lambda b, i: (b if per_batch else 0, 0, 0)),
                  _full(g.shape), _resident(w1.shape), _resident(w3.shape), _resident(w2.shape)],
        out_specs=pl.BlockSpec((1, tm, dm), lambda b, i: (b, i, 0)),
        compiler_params=pltpu.CompilerParams(vmem_limit_bytes=VMEM_LIMIT),
        name="half_ffn",
    )(h, mod, g, w1, w3, w2)


def _inproj_kernel(h_ref, mod_ref, g_ref, w_ref, gq_ref, gkv_ref, wq1_ref, wq2_ref, wk_ref, wv_ref,
                   cos_ref, sin_ref,
                   mq_ref, mk_ref, mv_ref, nq_ref, nk_ref, nv_ref, gqkv_ref, gzg_ref, szg_ref,
                   sxbc_ref, small_ref):
    x = h_ref[0]
    m = mod_ref[0]
    g = g_ref[...]
    u = _rms(x, g[2:3]) * (1.0 + m[4:5]) + m[3:4]
    ub = u.astype(BF16)

    def proj(a, b):
        return _dot(ub, w_ref[:, a:b])

    zm = proj(C_MLA, C_NAT)
    cqn = _rms(zm[:, 0:256], gq_ref[...]).astype(BF16)
    ckvn = _rms(zm[:, 256:384], gkv_ref[...]).astype(BF16)
    cos = cos_ref[...]
    sin = sin_ref[...]
    k_rope = zm[:, 384:512] * cos + zm[:, 512:640] * sin
    q1 = _dot(cqn, wq1_ref[...])
    q2 = _dot(cqn, wq2_ref[...])
    kn = _dot(ckvn, wk_ref[...])
    vv = _dot(ckvn, wv_ref[...])
    ones_col = (_iota((1, LANES), 1) == HEAD_W).astype(F32)
    for hh in range(N_HEADS):
        sl = slice(hh * LANES, (hh + 1) * LANES)
        mq_ref[0, hh] = ((q1[:, sl] * cos + q2[:, sl] * sin) * MLA_QSCALE).astype(BF16)
        mk_ref[0, hh] = (kn[:, sl] + k_rope).astype(BF16)
        mv_ref[0, hh] = (vv[:, sl] + ones_col).astype(BF16)

    nq_ref[0] = (proj(C_NAT, C_NAT + 256) * NAT_QSCALE).astype(BF16)
    nk_ref[0] = proj(C_NAT + 256, C_NAT + 512).astype(BF16)
    nv_ref[0] = proj(C_NAT + 512, C_NAT + 768).astype(BF16)
    gqkv_ref[0] = proj(C_GQKV, C_GZG)
    gzg_ref[0] = proj(C_GZG, C_SZG)
    szg_ref[0] = proj(C_SZG, C_SXBC)
    sxbc_ref[0] = proj(C_SXBC, C_SMALL)
    small_ref[0] = proj(C_SMALL, C_TOTAL)


def _in_projection(h, mod, g, lw, cos, sin, *, tm):
    nb, nt, dm = h.shape
    per_batch = mod.shape[0] > 1
    tok = lambda w, dt: jax.ShapeDtypeStruct((nb, nt, w), dt)
    head = jax.ShapeDtypeStruct((nb, N_HEADS, nt, LANES), BF16)
    tspec = lambda w: pl.BlockSpec((1, tm, w), lambda b, i: (b, i, 0))
    hspec = pl.BlockSpec((1, N_HEADS, tm, LANES), lambda b, i: (b, 0, i, 0))
    return pl.pallas_call(
        _inproj_kernel,
        out_shape=(head, head, head, tok(256, BF16), tok(256, BF16), tok(256, BF16),
                   tok(768, F32), tok(256, F32), tok(256, F32), tok(768, F32), tok(LANES, F32)),
        grid=(nb, nt // tm),
        in_specs=[tspec(dm),
                  pl.BlockSpec((1, N_MOD, dm), lambda b, i: (b if per_batch else 0, 0, 0)),
                  _full(g.shape), _resident(lw["w_in"].shape),
                  _full(lw["mla_gq"].shape), _full(lw["mla_gkv"].shape),
                  _full(lw["wq1"].shape), _full(lw["wq2"].shape),
                  _full(lw["wk"].shape), _full(lw["wv"].shape),
                  pl.BlockSpec((tm, LANES), lambda b, i: (i, 0)),
                  pl.BlockSpec((tm, LANES), lambda b, i: (i, 0))],
        out_specs=(hspec, hspec, hspec, tspec(256), tspec(256), tspec(256),
                   tspec(768), tspec(256), tspec(256), tspec(768), tspec(LANES)),
        compiler_params=pltpu.CompilerParams(vmem_limit_bytes=VMEM_LIMIT),
        name="in_projection",
    )(h, mod, g, lw["w_in"], lw["mla_gq"], lw["mla_gkv"], lw["wq1"], lw["wq2"], lw["wk"], lw["wv"],
      cos, sin)


def _mla_kernel(*refs, n_lat_chunks, tk):
    if n_lat_chunks:
        q_ref, kl_ref, vl_ref, kc_ref, vc_ref, o_ref = refs
    else:
        q_ref, kc_ref, vc_ref, o_ref = refs
    q = q_ref[0, 0]
    tq = q.shape[0]

    def step(kb, vb, carry):
        m, acc = carry
        s = _dot_nt(q, kb)
        mn = jnp.maximum(m, jnp.max(s, axis=-1, keepdims=True))
        p = jnp.exp2(s - mn)
        acc = jnp.exp2(m - mn) * acc + _dot(p.astype(BF16), vb)
        return mn, acc

    carry = (jnp.full((tq, 1), NEG_INF, F32), jnp.zeros((tq, LANES), F32))
    if n_lat_chunks:
        def body(j, carry):
            off = pl.multiple_of(j * tk, tk)
            return step(kl_ref[0, 0, pl.ds(off, tk), :], vl_ref[0, 0, pl.ds(off, tk), :], carry)
        carry = lax.fori_loop(0, n_lat_chunks, body, carry)
    _, acc = step(kc_ref[0, 0], vc_ref[0, 0], carry)
    o_ref[0, 0] = acc / acc[:, HEAD_W:HEAD_W + 1]


def _mla_attention(q, k_ctx, v_ctx, k_lat=None, v_lat=None, *, tq, tk):
    nb, nh, nq, _ = q.shape
    nc = k_ctx.shape[2]
    qspec = pl.BlockSpec((1, 1, tq, LANES), lambda b, h, i: (b, h, i, 0))
    cspec = pl.BlockSpec((1, 1, nc, LANES), lambda b, h, i: (b, h, 0, 0))
    if k_lat is None:
        args, specs, n_chunks = (q, k_ctx, v_ctx), [qspec, cspec, cspec], 0
    else:
        nk = k_lat.shape[2]
        lspec = pl.BlockSpec((1, 1, nk, LANES), lambda b, h, i: (b, h, 0, 0))
        args, specs, n_chunks = (q, k_lat, v_lat, k_ctx, v_ctx), [qspec, lspec, lspec, cspec, cspec], nk // tk
    return pl.pallas_call(
        functools.partial(_mla_kernel, n_lat_chunks=n_chunks, tk=tk),
        out_shape=jax.ShapeDtypeStruct((nb, nh, nq, LANES), F32),
        grid=(nb, nh, nq // tq),
        in_specs=specs,
        out_specs=qspec,
        compiler_params=pltpu.CompilerParams(vmem_limit_bytes=VMEM_LIMIT),
        name="mla_attention",
    )(*args)


NAT_QROWS = 4
NAT_QB = NAT_QROWS * GRID_W
NAT_KBLKS = 3


def _heads_attention(q, parts, gout):
    lane_head = _iota((1, GROUP_W), 1) // HEAD_W
    out = jnp.zeros(q.shape, F32)
    for hh in range(N_HEADS):
        sel = lane_head == hh
        qh = jnp.where(sel, q, jnp.zeros_like(q))
        scores = []
        for k, _, bias in parts:
            s = _dot_nt(qh, k)
            scores.append(s if bias is None else s + bias[hh])
        m = functools.reduce(jnp.maximum, [jnp.max(s, axis=-1, keepdims=True) for s in scores])
        ps = [jnp.exp(s - m) for s in scores]
        l = functools.reduce(jnp.add, [jnp.sum(p, axis=-1, keepdims=True) for p in ps])
        o = functools.reduce(jnp.add, [_dot(p.astype(BF16), v) for p, (_, v, _) in zip(ps, parts)])
        out = jnp.where(sel, o / l, out)
    return _rms(out, gout)


def _nat_kernel(q_ref, k0_ref, k1_ref, k2_ref, v0_ref, v1_ref, v2_ref, kc_ref, vc_ref, bias_ref, g_ref,
                o_ref):
    kw = jnp.concatenate([k0_ref[0], k1_ref[0], k2_ref[0]], axis=0)
    vw = jnp.concatenate([v0_ref[0], v1_ref[0], v2_ref[0]], axis=0)
    parts = [(kw, vw, bias_ref[0]), (kc_ref[0], vc_ref[0], None)]
    o_ref[0] = _heads_attention(q_ref[0], parts, g_ref[...])


def _nat_ctx_kernel(q_ref, k_ref, v_ref, g_ref, o_ref):
    o_ref[0] = _heads_attention(q_ref[0], [(k_ref[0], v_ref[0], None)], g_ref[...])


def _nat_attention(q, k, v, kc, vc, bias, gout):
    nb, nt, _ = q.shape
    nblk = nt // NAT_QB
    nc = kc.shape[1]
    start = lambda i: jnp.clip(i - 1, 0, nblk - NAT_KBLKS)
    variant = lambda i: jnp.where(i == 0, 0, jnp.where(i == nblk - 1, 2, 1))
    qspec = pl.BlockSpec((1, NAT_QB, GROUP_W), lambda b, i: (b, i, 0))
    kspecs = [pl.BlockSpec((1, NAT_QB, GROUP_W), lambda b, i, j=j: (b, start(i) + j, 0))
              for j in range(NAT_KBLKS)]
    cspec = pl.BlockSpec((1, nc, GROUP_W), lambda b, i: (b, 0, 0))
    bspec = pl.BlockSpec((1, N_HEADS, NAT_QB, NAT_KBLKS * NAT_QB), lambda b, i: (variant(i), 0, 0, 0))
    return pl.pallas_call(
        _nat_kernel,
        out_shape=jax.ShapeDtypeStruct((nb, nt, GROUP_W), F32),
        grid=(nb, nblk),
        in_specs=[qspec] + kspecs + kspecs + [cspec, cspec, bspec, _full(gout.shape)],
        out_specs=qspec,
        name="nat_attention",
    )(q, k, k, k, v, v, v, kc, vc, bias, gout)


def _nat_ctx_attention(q, k, v, gout):
    nb, nc, _ = q.shape
    spec = pl.BlockSpec((1, nc, GROUP_W), lambda b: (b, 0, 0))
    return pl.pallas_call(
        _nat_ctx_kernel,
        out_shape=jax.ShapeDtypeStruct((nb, nc, GROUP_W), F32),
        grid=(nb,),
        in_specs=[spec, spec, spec, _full(gout.shape)],
        out_specs=spec,
        name="nat_ctx_attention",
    )(q, k, v, gout)


def _nat_bias_tables(rpb, n_rows):
    krows = NAT_KBLKS * NAT_QROWS
    qr = jnp.arange(NAT_QROWS)[:, None, None, None]
    cq = jnp.arange(GRID_W)[None, :, None, None]
    kk = jnp.arange(krows)[None, None, :, None]
    ck = jnp.arange(GRID_W)[None, None, None, :]
    c0 = jnp.clip(cq - NAT_KC // 2, 0, GRID_W - NAT_KC)
    col_ok = (ck >= c0) & (ck < c0 + NAT_KC)
    dc = jnp.clip(ck - cq + (NAT_KC - 1), 0, 2 * NAT_KC - 2)

    def table(r_base, k_start, rows_total):
        r = r_base + qr
        k_abs = k_start + kk
        r0 = jnp.clip(r - NAT_KR // 2, 0, rows_total - NAT_KR)
        row_ok = (k_abs >= r0) & (k_abs < r0 + NAT_KR)
        dr = jnp.clip(k_abs - r + (NAT_KR - 1), 0, 2 * NAT_KR - 2)
        ok = jnp.broadcast_to(row_ok & col_ok, (NAT_QROWS, GRID_W, krows, GRID_W))
        b = rpb[:, jnp.broadcast_to(dr, ok.shape), jnp.broadcast_to(dc, ok.shape)]
        b = jnp.where(ok[None], b.astype(F32), NEG_INF)
        return b.reshape(N_HEADS, NAT_QB, krows * GRID_W)

    big = 4 * n_rows + 64
    return jnp.stack([table(0, 0, n_rows),
                      table(big // 2, big // 2 - NAT_QROWS, big),
                      table(n_rows - NAT_QROWS, n_rows - krows, n_rows)])


HALO = SUBLANES


def _conv_kernel(x_ref, prev_ref, next_ref, w_ref, b_ref, small_ref, pv_ref, y_ref, oa_ref, ob_ref,
                 xpad_ref, *, l2norm):
    i = pl.program_id(1)
    last = pl.num_programs(1) - 1
    tm = x_ref.shape[1]
    xpad_ref[0:HALO] = jnp.where(i > 0, prev_ref[0], 0.0)
    xpad_ref[HALO:HALO + tm] = x_ref[0]
    xpad_ref[HALO + tm:2 * HALO + tm] = jnp.where(i < last, next_ref[0], 0.0)
    w = w_ref[...]
    acc = jnp.broadcast_to(b_ref[...], (tm, x_ref.shape[2]))
    for j in range(CONV_K):
        acc = acc + xpad_ref[pl.ds(HALO - CONV_K // 2 + j, tm), :] * w[j:j + 1]
    y = _silu(acc)
    if l2norm:
        grp = (_iota((GROUP_W, GROUP_W), 0) // HEAD_W == _iota((GROUP_W, GROUP_W), 1) // HEAD_W).astype(BF16)
        q = y[:, 0:256]
        k = y[:, 256:512]
        y_ref[0, :, 0:256] = q * lax.rsqrt(_dot2(q * q, grp) + RMS_EPS) * (HEAD_W ** -0.5)
        y_ref[0, :, 256:512] = k * lax.rsqrt(_dot2(k * k, grp) + RMS_EPS)
        y_ref[0, :, 512:768] = y[:, 512:768]
    else:
        y_ref[0] = y
    s = small_ref[0]
    pv = pv_ref[...]
    sp = _softplus(s + pv[1:2])
    lane = _iota((1, LANES), 1)
    oa_ref[0] = jnp.where(lane < L_G, jax.nn.sigmoid(s), -jnp.exp(pv[0:1]) * sp)
    ob_ref[0] = sp


def _conv_prep(x, small, w, b, pv, *, tm, l2norm):
    nb, nt, nc = x.shape
    hb = tm // HALO
    nhalo = nt // HALO
    return pl.pallas_call(
        functools.partial(_conv_kernel, l2norm=l2norm),
        out_shape=(jax.ShapeDtypeStruct(x.shape, F32), jax.ShapeDtypeStruct(small.shape, F32),
                   jax.ShapeDtypeStruct(small.shape, F32)),
        grid=(nb, nt // tm),
        in_specs=[pl.BlockSpec((1, tm, nc), lambda bb, i: (bb, i, 0)),
                  pl.BlockSpec((1, HALO, nc), lambda bb, i: (bb, jnp.maximum(i * hb - 1, 0), 0)),
                  pl.BlockSpec((1, HALO, nc), lambda bb, i: (bb, jnp.minimum((i + 1) * hb, nhalo - 1), 0)),
                  _full(w.shape), _full(b.shape),
                  pl.BlockSpec((1, tm, LANES), lambda bb, i: (bb, i, 0)),
                  _full(pv.shape)],
        out_specs=(pl.BlockSpec((1, tm, nc), lambda bb, i: (bb, i, 0)),
                   pl.BlockSpec((1, tm, LANES), lambda bb, i: (bb, i, 0)),
                   pl.BlockSpec((1, tm, LANES), lambda bb, i: (bb, i, 0))),
        scratch_shapes=[pltpu.VMEM((tm + 2 * HALO, nc), F32)],
        name="conv_prep",
    )(x, x, x, w, b, small, pv)


def _chunk_cumsum_matrix(n, chunk, reverse):
    r = _iota((n, n), 0)
    c = _iota((n, n), 1)
    same = (r // chunk) == (c // chunk)
    return (same & ((c >= r) if reverse else (c <= r))).astype(BF16)


def _expand_matrix(lane0, group, width):
    return (_iota((LANES, width), 0) == lane0 + _iota((LANES, width), 1) // group).astype(BF16)


def _row_form(col_vals, chunk):
    pick = _iota(col_vals.shape, 0) == (_iota(col_vals.shape, 1) % chunk)
    ones = jnp.ones((SUBLANES, chunk), BF16)
    return _dot2l(ones, jnp.where(pick, col_vals, 0.0))[0:1]


def _gdn_kernel(y_ref, sm_ref, s0_ref, o_ref, sfin_ref, state_ref, *, d, n_chunks):
    ch = GDN_CHUNK
    cb = n_chunks * ch
    reverse = d == 1
    step = pl.program_id(1)

    @pl.when(step == 0)
    def _():
        state_ref[...] = s0_ref[0]

    sm = sm_ref[0]
    cs = _dot2l(_chunk_cumsum_matrix(cb, ch, reverse), sm)
    gc_all = _dot2(cs, _expand_matrix(L_G + N_HEADS * d, HEAD_W, GROUP_W))
    beta_all = _dot2(sm, _expand_matrix(L_BETA + N_HEADS * d, HEAD_W, GROUP_W))
    y = y_ref[0]
    q_all, k_all, v_all = y[:, 0:256], y[:, 256:512], y[:, 512:768]
    kb_all = k_all * beta_all
    vb_all = v_all * beta_all
    eg_all = jnp.exp(gc_all)

    ti = _iota((ch, GROUP_W), 0)
    tj = _iota((ch, GROUP_W), 1) % ch
    incl = (tj >= ti) if reverse else (tj <= ti)
    strict = (tj > ti) if reverse else (tj < ti)
    last = 0 if reverse else ch - 1

    order = range(n_chunks - 1, -1, -1) if reverse else range(n_chunks)
    for c in order:
        rows = slice(c * ch, (c + 1) * ch)
        gc = gc_all[rows]
        k = k_all[rows]
        diff = gc - _row_form(gc, ch)
        decay = jnp.where(incl, jnp.exp(jnp.where(incl, diff, 0.0)), 0.0)
        k_bd = _block_diag(k, N_HEADS, ch, HEAD_W).astype(BF16)
        both = jnp.concatenate([kb_all[rows], q_all[rows]], axis=0).astype(BF16)
        qk = _dot_nt(both, k_bd)
        a_mat = jnp.where(strict, qk[0:ch] * decay, 0.0)
        a_intra = qk[ch:2 * ch] * decay
        tinv = _unit_triangular_inverse(_block_diag(a_mat, N_HEADS, ch, ch), ch)
        rhs = jnp.concatenate([_block_diag(vb_all[rows], N_HEADS, ch, HEAD_W),
                               _block_diag(kb_all[rows] * eg_all[rows], N_HEADS, ch, HEAD_W)], axis=1)
        uw = _mm3(tinv, rhs)
        state = state_ref[...]
        sb = state.astype(BF16)
        v_new = uw[:, 0:GROUP_W] - _dot(uw[:, GROUP_W:].astype(BF16), sb)
        vb = v_new.astype(BF16)
        o_ref[0, rows, :] = _dot((q_all[rows] * eg_all[rows]).astype(BF16), sb) + _dot(a_intra.astype(BF16), vb)
        g_last = gc[last:last + 1]
        k_dec = _block_diag(k * jnp.exp(g_last - gc), N_HEADS, ch, HEAD_W).astype(BF16)
        state_ref[...] = state * jnp.exp(g_last) + _dot_tn(k_dec, vb)

    @pl.when(step == pl.num_programs(1) - 1)
    def _():
        sfin_ref[0] = state_ref[...]


def _ssd_kernel(x_ref, dt_ref, da_ref, s0_ref, o_ref, sfin_ref, state_ref, *, d, n_chunks):
    ch = SSM_CHUNK
    cb = n_chunks * ch
    reverse = d == 1
    step = pl.program_id(1)

    @pl.when(step == 0)
    def _():
        state_ref[...] = s0_ref[0]

    lane0 = L_DT + N_HEADS * d
    cs = _dot2l(_chunk_cumsum_matrix(cb, ch, reverse), da_ref[0])
    e64 = _expand_matrix(lane0, HEAD_W, GROUP_W)
    ac_all = _dot2(cs, e64)
    ac5_all = _dot2(cs, _expand_matrix(lane0, ch, N_HEADS * ch))
    dt_all = _dot2(dt_ref[0], e64)
    xbc = x_ref[0]
    xdt_all = xbc[:, 0:256] * dt_all
    b_all, c_all = xbc[:, 256:512], xbc[:, 512:768]

    ti = _iota((ch, N_HEADS * ch), 0)
    tj = _iota((ch, N_HEADS * ch), 1) % ch
    incl = (tj >= ti) if reverse else (tj <= ti)
    grp_keep = (_iota((GROUP_W, GROUP_W), 0) // ch) == (_iota((GROUP_W, GROUP_W), 1) // ch)
    last = 0 if reverse else ch - 1

    order = range(n_chunks - 1, -1, -1) if reverse else range(n_chunks)
    for c in order:
        rows = slice(c * ch, (c + 1) * ch)
        ac = ac_all[rows]
        ac5 = ac5_all[rows]
        diff = ac5 - _row_form(ac5, ch)
        seg = jnp.where(incl, jnp.exp(jnp.where(incl, diff, 0.0)), 0.0)
        bm = b_all[rows]
        cm = c_all[rows].astype(BF16)
        cb_g = _dot_nt(cm, _block_diag(bm, 2, ch, SSM_STATE).astype(BF16))
        scores = jnp.concatenate([cb_g[:, 0:ch], cb_g[:, 0:ch], cb_g[:, ch:], cb_g[:, ch:]], axis=1) * seg
        xdt = xdt_all[rows]
        y_diag = _dot(scores.astype(BF16), _block_diag(xdt, N_HEADS, ch, HEAD_W).astype(BF16))
        a_last = ac[last:last + 1]
        states = jnp.where(grp_keep, _dot_tn(bm.astype(BF16), (xdt * jnp.exp(a_last - ac)).astype(BF16)), 0.0)
        state = state_ref[...]
        o_ref[0, rows, :] = y_diag + _dot(cm, state.astype(BF16)) * jnp.exp(ac)
        state_ref[...] = state * jnp.exp(a_last) + states

    @pl.when(step == pl.num_programs(1) - 1)
    def _():
        sfin_ref[0] = state_ref[...]


def _scan_call(kernel, name, seq, smalls, s0, *, d, cb, chunk):
    nb, nt, nc = seq.shape
    nblk = nt // cb
    blk = (lambda bb, i: (bb, nblk - 1 - i, 0)) if d == 1 else (lambda bb, i: (bb, i, 0))
    sspec = pl.BlockSpec((1, GROUP_W, GROUP_W), lambda bb, i: (bb, 0, 0))
    return pl.pallas_call(
        functools.partial(kernel, d=d, n_chunks=cb // chunk),
        out_shape=(jax.ShapeDtypeStruct((nb, nt, GROUP_W), F32),
                   jax.ShapeDtypeStruct((nb, GROUP_W, GROUP_W), F32)),
        grid=(nb, nblk),
        in_specs=[pl.BlockSpec((1, cb, nc), blk)] + [pl.BlockSpec((1, cb, LANES), blk) for _ in smalls]
                 + [sspec],
        out_specs=(pl.BlockSpec((1, cb, GROUP_W), blk), sspec),
        scratch_shapes=[pltpu.VMEM((GROUP_W, GROUP_W), F32)],
        compiler_params=pltpu.CompilerParams(dimension_semantics=("arbitrary", "arbitrary")),
        name=name,
    )(seq, *smalls, s0)


def _bidir_scan(kernel, name, seq_l, smalls_l, seq_c, smalls_c, *, cb_l, cb_c, chunk):
    zero = jnp.zeros((seq_l.shape[0], GROUP_W, GROUP_W), F32)
    outs = []
    for d in (0, 1):
        o_c, s_c = _scan_call(kernel, name, seq_c, smalls_c, zero, d=d, cb=cb_c, chunk=chunk)
        o_l, _ = _scan_call(kernel, name, seq_l, smalls_l, s_c, d=d, cb=cb_l, chunk=chunk)
        outs.append((o_l, o_c))
    return outs


def _outproj_kernel(h_ref, mod_ref, g_ref, mla_ref, nat_ref, gof_ref, gob_ref, gzg_ref,
                    yf_ref, yb_ref, sx_ref, szg_ref, pv_ref, wm_ref, wr_ref, o_ref):
    pv = pv_ref[...]
    valid = _iota((1, LANES), 1) < HEAD_W
    slabs = [jnp.where(valid, mla_ref[0, hh], 0.0) for hh in range(N_HEADS)]
    ss = functools.reduce(jnp.add, [jnp.sum(s * s, axis=-1, keepdims=True) for s in slabs])
    scale = lax.rsqrt(ss / GROUP_W + RMS_EPS)
    y = None
    for hh in range(N_HEADS):
        part = _dot((slabs[hh] * scale * pv[hh:hh + 1, 0:LANES]).astype(BF16), wm_ref[hh])
        y = part if y is None else y + part
    y = y + _dot(nat_ref[0].astype(BF16), wr_ref[0:256, :])
    grp = (_iota((GROUP_W, GROUP_W), 0) // HEAD_W == _iota((GROUP_W, GROUP_W), 1) // HEAD_W).astype(BF16)
    o = gof_ref[0] + gob_ref[0]
    on = o * lax.rsqrt(_dot2(o * o, grp) / HEAD_W + RMS_EPS) * pv[4:5]
    y = y + _dot((on * _silu(gzg_ref[0])).astype(BF16), wr_ref[256:512, :])
    s = (yf_ref[0] + yb_ref[0] + sx_ref[0] * pv[5:6]) * _silu(szg_ref[0])
    sn = jnp.concatenate([_rms(s[:, 0:LANES], pv[6:7, 0:LANES]),
                          _rms(s[:, LANES:], pv[6:7, LANES:])], axis=1)
    y = y + _dot(sn.astype(BF16), wr_ref[512:768, :])
    m = mod_ref[0]
    g = g_ref[...]
    o_ref[0] = h_ref[0] + m[5:6] * _rms(y, g[3:4])


def _out_projection(h, mod, g, mla_o, nat_o, gdn_f, gdn_b, gdn_zg, ssm_f, ssm_b, ssm_xbc, ssm_zg,
                    pv, w_mla, w_rest, *, tm):
    nb, nt, dm = h.shape
    per_batch = mod.shape[0] > 1
    tspec = lambda w: pl.BlockSpec((1, tm, w), lambda b, i: (b, i, 0))
    return pl.pallas_call(
        _outproj_kernel,
        out_shape=jax.ShapeDtypeStruct(h.shape, F32),
        grid=(nb, nt // tm),
        in_specs=[tspec(dm),
                  pl.BlockSpec((1, N_MOD, dm), lambda b, i: (b if per_batch else 0, 0, 0)),
                  _full(g.shape),
                  pl.BlockSpec((1, N_HEADS, tm, LANES), lambda b, i: (b, 0, i, 0)),
                  tspec(256), tspec(256), tspec(256), tspec(256), tspec(256), tspec(256),
                  tspec(256),
                  tspec(256),
                  _full(pv.shape), _full(w_mla.shape), _full(w_rest.shape)],
        out_specs=tspec(dm),
        name="out_projection",
    )(h, mod, g, mla_o, nat_o, gdn_f, gdn_b, gdn_zg, ssm_f, ssm_b, ssm_xbc, ssm_zg, pv, w_mla, w_rest)


_ROPE_SWAP = np.array(list(range(8, 16)) + list(range(0, 8)) + list(range(24, 32)) + list(range(16, 24)))


def _pack_layer(p, l):
    w_in = p["w_in"][l]
    dm = w_in.shape[0]
    o_nat = MLA_Q_LORA + MLA_KV_LORA + MLA_ROPE
    o_gdn = o_nat + 768
    o_ssm = o_gdn + GDN_QKV + GROUP_W + 4 * N_HEADS
    kr = w_in[:, MLA_Q_LORA + MLA_KV_LORA:o_nat]
    z64 = jnp.zeros((dm, 64), F32)
    z32 = jnp.zeros((dm, 32), F32)
    small = jnp.concatenate([w_in[:, o_gdn + 1024:o_gdn + 1040],
                             w_in[:, o_ssm + 1024:o_ssm + 1032],
                             jnp.zeros((dm, LANES - 24), F32)], axis=1)
    w_packed = jnp.concatenate([
        w_in[:, 0:384], z64, kr, z32, z64, kr[:, _ROPE_SWAP], z32,
        w_in[:, o_nat:o_gdn],
        w_in[:, o_gdn:o_gdn + 1024],
        w_in[:, o_ssm:o_ssm + 1024],
        small], axis=1).astype(BF16)
    assert w_packed.shape[1] == C_TOTAL

    wuq = p["mla_wuq"][l].reshape(MLA_Q_LORA, N_HEADS, MLA_NOPE + MLA_ROPE)
    zq = jnp.zeros((MLA_Q_LORA, N_HEADS, 32), F32)
    wq1 = jnp.concatenate([wuq, zq], axis=2).reshape(MLA_Q_LORA, N_HEADS * LANES)
    wq2 = jnp.concatenate([jnp.zeros((MLA_Q_LORA, N_HEADS, 64), F32), wuq[:, :, MLA_NOPE:][:, :, _ROPE_SWAP], zq],
                          axis=2).reshape(MLA_Q_LORA, N_HEADS * LANES)
    wukv = p["mla_wukv"][l].reshape(MLA_KV_LORA, N_HEADS, MLA_NOPE + HEAD_W)
    zk = jnp.zeros((MLA_KV_LORA, N_HEADS, 64), F32)
    wk = jnp.concatenate([wukv[:, :, :MLA_NOPE], zk], axis=2).reshape(MLA_KV_LORA, N_HEADS * LANES)
    wv = jnp.concatenate([wukv[:, :, MLA_NOPE:], zk], axis=2).reshape(MLA_KV_LORA, N_HEADS * LANES)

    def lanes(vals, lane0):
        v = vals.reshape(-1)
        return jnp.zeros((LANES,), F32).at[lane0:lane0 + v.shape[0]].set(v)

    w_out = p["w_out"][l]
    w_mla = jnp.concatenate([w_out[0:256].reshape(N_HEADS, HEAD_W, dm),
                             jnp.zeros((N_HEADS, LANES - HEAD_W, dm), F32)], axis=1).astype(BF16)
    gout = jnp.concatenate([p["mla_gout"][l].reshape(N_HEADS, HEAD_W),
                            jnp.zeros((N_HEADS, GROUP_W - HEAD_W), F32)], axis=1)
    pv_out = jnp.concatenate([gout,
                              jnp.tile(p["gdn_gnorm"][l], N_HEADS)[None],
                              jnp.repeat(p["ssm_d"][l], HEAD_W)[None],
                              p["ssm_gnorm"][l][None],
                              jnp.zeros((1, GROUP_W), F32)], axis=0)
    return dict(
        ffn1=(p["ffn1_w1"][l].astype(BF16), p["ffn1_w3"][l].astype(BF16), p["ffn1_w2"][l].astype(BF16)),
        ffn2=(p["ffn2_w1"][l].astype(BF16), p["ffn2_w3"][l].astype(BF16), p["ffn2_w2"][l].astype(BF16)),
        g=p["norm_g"][l],
        w_in=w_packed, wq1=wq1.astype(BF16), wq2=wq2.astype(BF16), wk=wk.astype(BF16), wv=wv.astype(BF16),
        mla_gq=p["mla_gq"][l][None], mla_gkv=p["mla_gkv"][l][None],
        nat_bias_rpb=p["nat_rpb"][l], nat_gout=p["nat_gout"][l][None],
        gdn_conv_w=p["gdn_conv_w"][l], gdn_conv_b=jnp.zeros((1, GDN_QKV), F32),
        gdn_pv=jnp.stack([lanes(p["gdn_a_log"][l], L_G), lanes(p["gdn_dt_bias"][l], L_G)]),
        ssm_conv_w=p["ssm_conv_w"][l], ssm_conv_b=p["ssm_conv_b"][l][None],
        ssm_pv=jnp.stack([lanes(p["ssm_a_log"][l], L_DT), lanes(p["ssm_dt_bias"][l], L_DT)]),
        pv_out=pv_out, w_mla=w_mla, w_rest=w_out[256:].astype(BF16),
    )


def _rope_tables(n_tok):
    pos = jnp.arange(n_tok)
    rows = (pos // GRID_W).astype(F32)
    cols = (pos % GRID_W).astype(F32)
    quarter = MLA_ROPE // 4
    freqs = ROPE_THETA ** (-jnp.arange(quarter, dtype=F32) / quarter)
    ar = rows[:, None] * freqs
    ac = cols[:, None] * freqs
    cos = jnp.concatenate([jnp.cos(ar), jnp.cos(ar), jnp.cos(ac), jnp.cos(ac)], axis=1)
    sin = jnp.concatenate([-jnp.sin(ar), jnp.sin(ar), -jnp.sin(ac), jnp.sin(ac)], axis=1)
    ones = jnp.ones((n_tok, MLA_NOPE), F32)
    zeros = jnp.zeros((n_tok, MLA_NOPE), F32)
    pad = jnp.zeros((n_tok, LANES - MLA_NOPE - MLA_ROPE), F32)
    return jnp.concatenate([ones, cos, pad], axis=1), jnp.concatenate([zeros, sin, pad], axis=1)


def _tiles(n_tok):
    return dict(tm=min(512, n_tok), tq=min(512, n_tok), cb_gdn=min(256, n_tok), cb_ssd=min(512, n_tok))


def _mixers(hl, hc, ml, mc, lw, ropes, need_ctx):
    tl, tc = _tiles(hl.shape[1]), _tiles(hc.shape[1])
    (cos_l, sin_l), (cos_c, sin_c) = ropes
    zl = _in_projection(hl, ml, lw["g"], lw, cos_l, sin_l, tm=tl["tm"])
    zc = _in_projection(hc, mc, lw["g"], lw, cos_c, sin_c, tm=tc["tm"])
    (mq_l, mk_l, mv_l, nq_l, nk_l, nv_l, gqkv_l, gzg_l, szg_l, sxbc_l, small_l) = zl
    (mq_c, mk_c, mv_c, nq_c, nk_c, nv_c, gqkv_c, gzg_c, szg_c, sxbc_c, small_c) = zc

    n_lat = hl.shape[1]
    tk = min(1024, n_lat)
    mla_l = _mla_attention(mq_l, mk_c, mv_c, mk_l, mv_l, tq=tl["tq"], tk=tk)
    bias = _nat_bias_tables(lw["nat_bias_rpb"], n_lat // GRID_W)
    nat_l = _nat_attention(nq_l, nk_l, nv_l, nk_c, nv_c, bias, lw["nat_gout"])

    gy_l, ga_l, _ = _conv_prep(gqkv_l, small_l, lw["gdn_conv_w"], lw["gdn_conv_b"], lw["gdn_pv"],
                               tm=tl["tm"], l2norm=True)
    gy_c, ga_c, _ = _conv_prep(gqkv_c, small_c, lw["gdn_conv_w"], lw["gdn_conv_b"], lw["gdn_pv"],
                               tm=tc["tm"], l2norm=True)
    (gf_l, gf_c), (gb_l, gb_c) = _bidir_scan(_gdn_kernel, "gdn_scan", gy_l, (ga_l,), gy_c, (ga_c,),
                                             cb_l=tl["cb_gdn"], cb_c=tc["cb_gdn"], chunk=GDN_CHUNK)

    sy_l, sa_l, sd_l = _conv_prep(sxbc_l, small_l, lw["ssm_conv_w"], lw["ssm_conv_b"], lw["ssm_pv"],
                                  tm=tl["tm"], l2norm=False)
    sy_c, sa_c, sd_c = _conv_prep(sxbc_c, small_c, lw["ssm_conv_w"], lw["ssm_conv_b"], lw["ssm_pv"],
                                  tm=tc["tm"], l2norm=False)
    (sf_l, sf_c), (sb_l, sb_c) = _bidir_scan(_ssd_kernel, "ssd_scan", sy_l, (sd_l, sa_l), sy_c, (sd_c, sa_c),
                                             cb_l=tl["cb_ssd"], cb_c=tc["cb_ssd"], chunk=SSM_CHUNK)

    hl = _out_projection(hl, ml, lw["g"], mla_l, nat_l, gf_l, gb_l, gzg_l, sf_l, sb_l, sy_l, szg_l,
                         lw["pv_out"], lw["w_mla"], lw["w_rest"], tm=tl["tm"])
    if need_ctx:
        mla_c = _mla_attention(mq_c, mk_c, mv_c, tq=tc["tq"], tk=tk)
        nat_c = _nat_ctx_attention(nq_c, nk_c, nv_c, lw["nat_gout"])
        hc = _out_projection(hc, mc, lw["g"], mla_c, nat_c, gf_c, gb_c, gzg_c, sf_c, sb_c, sy_c, szg_c,
                             lw["pv_out"], lw["w_mla"], lw["w_rest"], tm=tc["tm"])
    return hl, hc


def _layer(hl, hc, ml, mc, lw, ropes, need_ctx):
    tl, tc = _tiles(hl.shape[1]), _tiles(hc.shape[1])
    hl = _half_ffn(hl, ml, lw["g"], *lw["ffn1"], k0=0, gp=0, tm=tl["tm"])
    hc = _half_ffn(hc, mc, lw["g"], *lw["ffn1"], k0=0, gp=0, tm=tc["tm"])
    hl, hc = _mixers(hl, hc, ml, mc, lw, ropes, need_ctx)
    hl = _half_ffn(hl, ml, lw["g"], *lw["ffn2"], k0=6, gp=4, tm=tl["tm"])
    if need_ctx:
        hc = _half_ffn(hc, mc, lw["g"], *lw["ffn2"], k0=6, gp=4, tm=tc["tm"])
    return hl, hc


def kernel(x, c, ctx, c_ctx, w_mod, b_mod, norm_g, ffn1_w1, ffn1_w3, ffn1_w2, ffn2_w1, ffn2_w3, ffn2_w2,
           w_in, w_out, mla_gq, mla_gkv, mla_wuq, mla_wukv, mla_gout, nat_rpb, nat_gout, gdn_conv_w,
           gdn_a_log, gdn_dt_bias, gdn_gnorm, ssm_conv_w, ssm_conv_b, ssm_a_log, ssm_dt_bias, ssm_d,
           ssm_gnorm):
    p = dict(norm_g=norm_g, ffn1_w1=ffn1_w1, ffn1_w3=ffn1_w3, ffn1_w2=ffn1_w2, ffn2_w1=ffn2_w1,
             ffn2_w3=ffn2_w3, ffn2_w2=ffn2_w2, w_in=w_in, w_out=w_out, mla_gq=mla_gq, mla_gkv=mla_gkv,
             mla_wuq=mla_wuq, mla_wukv=mla_wukv, mla_gout=mla_gout, nat_rpb=nat_rpb, nat_gout=nat_gout,
             gdn_conv_w=gdn_conv_w, gdn_a_log=gdn_a_log, gdn_dt_bias=gdn_dt_bias, gdn_gnorm=gdn_gnorm,
             ssm_conv_w=ssm_conv_w, ssm_conv_b=ssm_conv_b, ssm_a_log=ssm_a_log, ssm_dt_bias=ssm_dt_bias,
             ssm_d=ssm_d, ssm_gnorm=ssm_gnorm)
    nb, n_lat, dm = x.shape
    n_ctx = ctx.shape[1]
    depth = w_mod.shape[0]
    cvec = jnp.concatenate([c, c_ctx[None], jnp.zeros((SUBLANES - nb - 1, dm), F32)], axis=0)
    mods = _modulation(cvec, w_mod, b_mod).reshape(depth, SUBLANES, N_MOD, dm)
    cos_c = jnp.concatenate([jnp.ones((n_ctx, MLA_NOPE + MLA_ROPE), F32),
                             jnp.zeros((n_ctx, LANES - MLA_NOPE - MLA_ROPE), F32)], axis=1)
    ropes = (_rope_tables(n_lat), (cos_c, jnp.zeros((n_ctx, LANES), F32)))
    hl, hc = x, ctx
    for l in range(depth):
        lw = _pack_layer(p, l)
        hl, hc = _layer(hl, hc, mods[l, 0:nb], mods[l, nb:nb + 1], lw, ropes, need_ctx=l < depth - 1)
    return hl
```

```python
import functools
import math

import jax
import jax.numpy as jnp
import numpy as np
from jax import lax
from jax.experimental import pallas as pl
from jax.experimental.pallas import tpu as pltpu

F32 = jnp.float32
BF16 = jnp.bfloat16

D_MODEL = 1024
DEPTH = 4
GRID_W = 64
N_MOD = 9
D_FF = 2816
RMS_EPS = 1e-6
NEG_INF = -1e30
ROPE_THETA = 10000.0
GROUP_W = 256
N_HEADS = 4
HEAD_W = 64
MLA_NOPE = 64
MLA_ROPE = 32
MLA_Q_LORA = 256
MLA_KV_LORA = 128
NAT_KR = 8
NAT_KC = 16
GDN_CHUNK = 64
SSM_STATE = 128
SSM_CHUNK = 128
CONV_K = 5
GDN_QKV = 768
SSM_XBC = 768

LANES = 128
SUBLANES = 8
VMEM_LIMIT = 56 * 1024 * 1024

C_MLA = 0
C_NAT = 640
C_GQKV = 1408
C_GZG = 2176
C_SZG = 2432
C_SXBC = 2688
C_SMALL = 3456
C_TOTAL = 3584
L_BETA, L_G, L_DT = 0, 8, 16

MLA_QSCALE = (MLA_NOPE + MLA_ROPE) ** -0.5 * math.log2(math.e)
NAT_QSCALE = HEAD_W ** -0.5


def _dot(a, b):
    return jnp.dot(a, b, preferred_element_type=F32)


def _dot_nt(a, b):
    return lax.dot_general(a, b, (((1,), (1,)), ((), ())), preferred_element_type=F32)


def _dot_tn(a, b):
    return lax.dot_general(a, b, (((0,), (0,)), ((), ())), preferred_element_type=F32)


def _split(x):
    hi = x.astype(BF16)
    lo = (x - hi.astype(F32)).astype(BF16)
    return hi, lo


def _dot2(a, m):
    hi, lo = _split(a)
    return _dot(hi, m) + _dot(lo, m)


def _dot2l(m, a):
    hi, lo = _split(a)
    return _dot(m, hi) + _dot(m, lo)


def _mm3(a, b):
    ah, al = _split(a)
    bh, bl = _split(b)
    return _dot(ah, bh) + (_dot(ah, bl) + _dot(al, bh))


def _unit_triangular_inverse(a_bd, blk):
    r = _iota(a_bd.shape, 0)
    c = _iota(a_bd.shape, 1)
    base = SUBLANES
    m = jnp.where(r // base == c // base, -a_bd, 0.0)
    t = jnp.where(r == c, 1.0, 0.0) + m
    m = _mm3(m, m)
    t = t + _mm3(t, m)
    m = _mm3(m, m)
    t = t + _mm3(t, m)
    size = 2 * base
    while size <= blk:
        off = (r // size == c // size) & (r // (size // 2) != c // (size // 2))
        t = t - _mm3(t, _mm3(jnp.where(off, a_bd, 0.0), t))
        size *= 2
    return t


def _rms(x, g):
    return x * lax.rsqrt(jnp.mean(x * x, axis=-1, keepdims=True) + RMS_EPS) * g


def _silu(x):
    return x * jax.nn.sigmoid(x)


def _softplus(x):
    return jnp.maximum(x, 0.0) + jnp.log1p(jnp.exp(-jnp.abs(x)))


def _iota(shape, dim):
    return lax.broadcasted_iota(jnp.int32, shape, dim)


def _block_diag(x, n, blk_r, blk_c):
    t = jnp.concatenate([x] * n, axis=0)
    keep = (_iota(t.shape, 0) // blk_r) == (_iota(t.shape, 1) // blk_c)
    return jnp.where(keep, t, 0.0)


def _full(shape):
    nd = len(shape)
    return pl.BlockSpec(shape, lambda *_: (0,) * nd)


def _resident(shape):
    nd = len(shape)
    return pl.BlockSpec(shape, lambda *_: (0,) * nd, pipeline_mode=pl.Buffered(1))


def _mod_kernel(c_ref, w_ref, b_ref, o_ref):
    s = _silu(c_ref[...])
    o_ref[0] = jnp.dot(s, w_ref[0], preferred_element_type=F32,
                       precision=lax.Precision.HIGHEST) + b_ref[0]


def _modulation(cvec, w_mod, b_mod):
    nl, dm, nm = w_mod.shape
    rows = cvec.shape[0]
    tn = 1536
    return pl.pallas_call(
        _mod_kernel,
        out_shape=jax.ShapeDtypeStruct((nl, rows, nm), F32),
        grid=(nl, nm // tn),
        in_specs=[pl.BlockSpec((rows, dm), lambda l, j: (0, 0)),
                  pl.BlockSpec((1, dm, tn), lambda l, j: (l, 0, j)),
                  pl.BlockSpec((1, 1, tn), lambda l, j: (l, 0, j))],
        out_specs=pl.BlockSpec((1, rows, tn), lambda l, j: (l, 0, j)),
        name="modulation",
    )(cvec, w_mod, b_mod.reshape(nl, 1, nm))


FFN_CHUNK = 256


def _ffn_kernel(h_ref, mod_ref, g_ref, w1_ref, w3_ref, w2_ref, o_ref, *, k0, gp):
    x = h_ref[0]
    m = mod_ref[0]
    g = g_ref[...]
    u = _rms(x, g[gp:gp + 1]) * (1.0 + m[k0 + 1:k0 + 2]) + m[k0:k0 + 1]
    ub = u.astype(BF16)
    acc = None
    for c in range(D_FF // FFN_CHUNK):
        sl = slice(c * FFN_CHUNK, (c + 1) * FFN_CHUNK)
        a = _dot(ub, w1_ref[:, sl])
        b = _dot(ub, w3_ref[:, sl])
        hid = (_silu(a) * b).astype(BF16)
        part = _dot(hid, w2_ref[sl, :])
        acc = part if acc is None else acc + part
    o_ref[0] = x + 0.5 * m[k0 + 2:k0 + 3] * _rms(acc, g[gp + 1:gp + 2])


def _half_ffn(h, mod, g, w1, w3, w2, *, k0, gp, tm):
    nb, nt, dm = h.shape
    per_batch = mod.shape[0] > 1
    return pl.pallas_call(
        functools.partial(_ffn_kernel, k0=k0, gp=gp),
        out_shape=jax.ShapeDtypeStruct(h.shape, F32),
        grid=(nb, nt // tm),
        in_specs=[pl.BlockSpec((1, tm, dm), lambda b, i: (b, i, 0)),
                  pl.BlockSpec((1, N_MOD, dm), lambda b, i: (b if per_batch else 0, 0, 0)),
                  _full(g.shape), _resident(w1.shape), _resident(w3.shape), _resident(w2.shape)],
        out_specs=pl.BlockSpec((1, tm, dm), lambda b, i: (b, i, 0)),
        compiler_params=pltpu.CompilerParams(vmem_limit_bytes=VMEM_LIMIT),
        name="half_ffn",
    )(h, mod, g, w1, w3, w2)


def _inproj_kernel(h_ref, mod_ref, g_ref, w_ref, gq_ref, gkv_ref, wq1_ref, wq2_ref, wk_ref, wv_ref,
                   cos_ref, sin_ref,
                   mq_ref, mk_ref, mv_ref, nq_ref, nk_ref, nv_ref, gqkv_ref, gzg_ref, szg_ref,
                   sxbc_ref, small_ref):
    x = h_ref[0]
    m = mod_ref[0]
    g = g_ref[...]
    u = _rms(x, g[2:3]) * (1.0 + m[4:5]) + m[3:4]
    ub = u.astype(BF16)

    def proj(a, b):
        return _dot(ub, w_ref[:, a:b])

    zm = proj(C_MLA, C_NAT)
    cqn = _rms(zm[:, 0:256], gq_ref[...]).astype(BF16)
    ckvn = _rms(zm[:, 256:384], gkv_ref[...]).astype(BF16)
    cos = cos_ref[...]
    sin = sin_ref[...]
    k_rope = zm[:, 384:512] * cos + zm[:, 512:640] * sin
    q1 = _dot(cqn, wq1_ref[...])
    q2 = _dot(cqn, wq2_ref[...])
    kn = _dot(ckvn, wk_ref[...])
    vv = _dot(ckvn, wv_ref[...])
    ones_col = (_iota((1, LANES), 1) == HEAD_W).astype(F32)
    for hh in range(N_HEADS):
        sl = slice(hh * LANES, (hh + 1) * LANES)
        mq_ref[0, hh] = ((q1[:, sl] * cos + q2[:, sl] * sin) * MLA_QSCALE).astype(BF16)
        mk_ref[0, hh] = (kn[:, sl] + k_rope).astype(BF16)
        mv_ref[0, hh] = (vv[:, sl] + ones_col).astype(BF16)

    nq_ref[0] = (proj(C_NAT, C_NAT + 256) * NAT_QSCALE).astype(BF16)
    nk_ref[0] = proj(C_NAT + 256, C_NAT + 512).astype(BF16)
    nv_ref[0] = proj(C_NAT + 512, C_NAT + 768).astype(BF16)
    gqkv_ref[0] = proj(C_GQKV, C_GZG)
    gzg_ref[0] = proj(C_GZG, C_SZG)
    szg_ref[0] = proj(C_SZG, C_SXBC)
    sxbc_ref[0] = proj(C_SXBC, C_SMALL)
    small_ref[0] = proj(C_SMALL, C_TOTAL)


def _in_projection(h, mod, g, lw, cos, sin, *, tm):
    nb, nt, dm = h.shape
    per_batch = mod.shape[0] > 1
    tok = lambda w, dt: jax.ShapeDtypeStruct((nb, nt, w), dt)
    head = jax.ShapeDtypeStruct((nb, N_HEADS, nt, LANES), BF16)
    tspec = lambda w: pl.BlockSpec((1, tm, w), lambda b, i: (b, i, 0))
    hspec = pl.BlockSpec((1, N_HEADS, tm, LANES), lambda b, i: (b, 0, i, 0))
    return pl.pallas_call(
        _inproj_kernel,
        out_shape=(head, head, head, tok(256, BF16), tok(256, BF16), tok(256, BF16),
                   tok(768, F32), tok(256, F32), tok(256, F32), tok(768, F32), tok(LANES, F32)),
        grid=(nb, nt // tm),
        in_specs=[tspec(dm),
                  pl.BlockSpec((1, N_MOD, dm), lambda b, i: (b if per_batch else 0, 0, 0)),
                  _full(g.shape), _resident(lw["w_in"].shape),
                  _full(lw["mla_gq"].shape), _full(lw["mla_gkv"].shape),
                  _full(lw["wq1"].shape), _full(lw["wq2"].shape),
                  _full(lw["wk"].shape), _full(lw["wv"].shape),
                  pl.BlockSpec((tm, LANES), lambda b, i: (i, 0)),
                  pl.BlockSpec((tm, LANES), lambda b, i: (i, 0))],
        out_specs=(hspec, hspec, hspec, tspec(256), tspec(256), tspec(256),
                   tspec(768), tspec(256), tspec(256), tspec(768), tspec(LANES)),
        compiler_params=pltpu.CompilerParams(vmem_limit_bytes=VMEM_LIMIT),
        name="in_projection",
    )(h, mod, g, lw["w_in"], lw["mla_gq"], lw["mla_gkv"], lw["wq1"], lw["wq2"], lw["wk"], lw["wv"],
      cos, sin)


def _mla_kernel(*refs, n_lat_chunks, tk):
    if n_lat_chunks:
        q_ref, kl_ref, vl_ref, kc_ref, vc_ref, o_ref = refs
    else:
        q_ref, kc_ref, vc_ref, o_ref = refs
    q = q_ref[0, 0]
    tq = q.shape[0]

    def step(kb, vb, carry):
        m, acc = carry
        s = _dot_nt(q, kb)
        mn = jnp.maximum(m, jnp.max(s, axis=-1, keepdims=True))
        p = jnp.exp2(s - mn)
        acc = jnp.exp2(m - mn) * acc + _dot(p.astype(BF16), vb)
        return mn, acc

    carry = (jnp.full((tq, 1), NEG_INF, F32), jnp.zeros((tq, LANES), F32))
    if n_lat_chunks:
        def body(j, carry):
            off = pl.multiple_of(j * tk, tk)
            return step(kl_ref[0, 0, pl.ds(off, tk), :], vl_ref[0, 0, pl.ds(off, tk), :], carry)
        carry = lax.fori_loop(0, n_lat_chunks, body, carry, unroll=4)
    _, acc = step(kc_ref[0, 0], vc_ref[0, 0], carry)
    o_ref[0, 0] = acc / acc[:, HEAD_W:HEAD_W + 1]


def _mla_attention(q, k_ctx, v_ctx, k_lat=None, v_lat=None, *, tq, tk):
    nb, nh, nq, _ = q.shape
    nc = k_ctx.shape[2]
    qspec = pl.BlockSpec((1, 1, tq, LANES), lambda b, h, i: (b, h, i, 0))
    cspec = pl.BlockSpec((1, 1, nc, LANES), lambda b, h, i: (b, h, 0, 0))
    if k_lat is None:
        args, specs, n_chunks = (q, k_ctx, v_ctx), [qspec, cspec, cspec], 0
    else:
        nk = k_lat.shape[2]
        lspec = pl.BlockSpec((1, 1, nk, LANES), lambda b, h, i: (b, h, 0, 0))
        args, specs, n_chunks = (q, k_lat, v_lat, k_ctx, v_ctx), [qspec, lspec, lspec, cspec, cspec], nk // tk
    return pl.pallas_call(
        functools.partial(_mla_kernel, n_lat_chunks=n_chunks, tk=tk),
        out_shape=jax.ShapeDtypeStruct((nb, nh, nq, LANES), F32),
        grid=(nb, nh, nq // tq),
        in_specs=specs,
        out_specs=qspec,
        compiler_params=pltpu.CompilerParams(vmem_limit_bytes=VMEM_LIMIT),
        name="mla_attention",
    )(*args)


NAT_QROWS = 4
NAT_QB = NAT_QROWS * GRID_W
NAT_KBLKS = 3


def _heads_attention(q, parts, gout):
    lane_head = _iota((1, GROUP_W), 1) // HEAD_W
    out = jnp.zeros(q.shape, F32)
    for hh in range(N_HEADS):
        sel = lane_head == hh
        qh = jnp.where(sel, q, jnp.zeros_like(q))
        scores = []
        for k, _, bias in parts:
            s = _dot_nt(qh, k)
            scores.append(s if bias is None else s + bias[hh])
        m = functools.reduce(jnp.maximum, [jnp.max(s, axis=-1, keepdims=True) for s in scores])
        ps = [jnp.exp(s - m) for s in scores]
        l = functools.reduce(jnp.add, [jnp.sum(p, axis=-1, keepdims=True) for p in ps])
        o = functools.reduce(jnp.add, [_dot(p.astype(BF16), v) for p, (_, v, _) in zip(ps, parts)])
        out = jnp.where(sel, o / l, out)
    return _rms(out, gout)


def _nat_kernel(q_ref, k0_ref, k1_ref, k2_ref, v0_ref, v1_ref, v2_ref, kc_ref, vc_ref, bias_ref, g_ref,
                o_ref):
    kw = jnp.concatenate([k0_ref[0], k1_ref[0], k2_ref[0]], axis=0)
    vw = jnp.concatenate([v0_ref[0], v1_ref[0], v2_ref[0]], axis=0)
    parts = [(kw, vw, bias_ref[0]), (kc_ref[0], vc_ref[0], None)]
    o_ref[0] = _heads_attention(q_ref[0], parts, g_ref[...])


def _nat_ctx_kernel(q_ref, k_ref, v_ref, g_ref, o_ref):
    o_ref[0] = _heads_attention(q_ref[0], [(k_ref[0], v_ref[0], None)], g_ref[...])


def _nat_attention(q, k, v, kc, vc, bias, gout):
    nb, nt, _ = q.shape
    nblk = nt // NAT_QB
    nc = kc.shape[1]
    start = lambda i: jnp.clip(i - 1, 0, nblk - NAT_KBLKS)
    variant = lambda i: jnp.where(i == 0, 0, jnp.where(i == nblk - 1, 2, 1))
    qspec = pl.BlockSpec((1, NAT_QB, GROUP_W), lambda b, i: (b, i, 0))
    kspecs = [pl.BlockSpec((1, NAT_QB, GROUP_W), lambda b, i, j=j: (b, start(i) + j, 0))
              for j in range(NAT_KBLKS)]
    cspec = pl.BlockSpec((1, nc, GROUP_W), lambda b, i: (b, 0, 0))
    bspec = pl.BlockSpec((1, N_HEADS, NAT_QB, NAT_KBLKS * NAT_QB), lambda b, i: (variant(i), 0, 0, 0))
    return pl.pallas_call(
        _nat_kernel,
        out_shape=jax.ShapeDtypeStruct((nb, nt, GROUP_W), F32),
        grid=(nb, nblk),
        in_specs=[qspec] + kspecs + kspecs + [cspec, cspec, bspec, _full(gout.shape)],
        out_specs=qspec,
        name="nat_attention",
    )(q, k, k, k, v, v, v, kc, vc, bias, gout)


def _nat_ctx_attention(q, k, v, gout):
    nb, nc, _ = q.shape
    spec = pl.BlockSpec((1, nc, GROUP_W), lambda b: (b, 0, 0))
    return pl.pallas_call(
        _nat_ctx_kernel,
        out_shape=jax.ShapeDtypeStruct((nb, nc, GROUP_W), F32),
        grid=(nb,),
        in_specs=[spec, spec, spec, _full(gout.shape)],
        out_specs=spec,
        name="nat_ctx_attention",
    )(q, k, v, gout)


def _nat_bias_constants(n_rows):
    krows = NAT_KBLKS * NAT_QROWS
    qr = np.arange(NAT_QROWS)[:, None]
    kk = np.arange(krows)[None, :]
    cq = np.arange(GRID_W)[:, None]
    ck = np.arange(GRID_W)[None, :]
    c0 = np.clip(cq - NAT_KC // 2, 0, GRID_W - NAT_KC)
    col_ok = ((ck >= c0) & (ck < c0 + NAT_KC)).reshape(-1)
    dc = np.clip(ck - cq + (NAT_KC - 1), 0, 2 * NAT_KC - 2).reshape(-1)
    e_col = np.zeros((2 * NAT_KC, GRID_W * GRID_W), np.float32)
    e_col[dc, np.arange(GRID_W * GRID_W)] = 1.0
    big = 4 * n_rows + 64
    placements = [(0, 0, n_rows), (big // 2, big // 2 - NAT_QROWS, big),
                  (n_rows - NAT_QROWS, n_rows - krows, n_rows)]
    e_row = np.zeros((3, NAT_QROWS * krows, 2 * NAT_KR), np.float32)
    ok = np.zeros((3, NAT_QROWS * krows, GRID_W * GRID_W), np.float32)
    for v, (r_base, k_start, rows_total) in enumerate(placements):
        r = r_base + qr
        k_abs = k_start + kk
        r0 = np.clip(r - NAT_KR // 2, 0, rows_total - NAT_KR)
        row_ok = ((k_abs >= r0) & (k_abs < r0 + NAT_KR)).reshape(-1)
        dr = np.clip(k_abs - r + (NAT_KR - 1), 0, 2 * NAT_KR - 2).reshape(-1)
        e_row[v, np.arange(NAT_QROWS * krows), dr] = 1.0
        ok[v] = row_ok[:, None] & col_ok[None, :]
    return e_row, e_col, ok


def _split3(x):
    h1 = x.astype(BF16)
    r1 = x - h1.astype(F32)
    h2 = r1.astype(BF16)
    h3 = (r1 - h2.astype(F32)).astype(BF16)
    return h1, h2, h3


def _nat_bias_kernel(er_ref, rpb_ref, ec_ref, ok_ref, o_ref):
    er = er_ref[0]
    rows = functools.reduce(jnp.add, [_dot(er, p) for p in _split3(rpb_ref[0, 0])])
    ec = ec_ref[...]
    b = functools.reduce(jnp.add, [_dot(p, ec) for p in _split3(rows)])
    o_ref[0, 0, 0] = jnp.where(ok_ref[0] > 0.0, b, NEG_INF)


def _nat_bias_tables(rpb_all, n_rows):
    nl, nh = rpb_all.shape[:2]
    krows = NAT_KBLKS * NAT_QROWS
    e_row, e_col, ok = _nat_bias_constants(n_rows)
    rpb = jnp.pad(rpb_all, ((0, 0), (0, 0), (0, 1), (0, 1)))
    nr, ncol = NAT_QROWS * krows, GRID_W * GRID_W
    out = pl.pallas_call(
        _nat_bias_kernel,
        out_shape=jax.ShapeDtypeStruct((nl, 3, nh, nr, ncol), F32),
        grid=(nl, 3, nh),
        in_specs=[pl.BlockSpec((1, nr, 2 * NAT_KR), lambda l, v, h: (v, 0, 0)),
                  pl.BlockSpec((1, 1, 2 * NAT_KR, 2 * NAT_KC), lambda l, v, h: (l, h, 0, 0)),
                  pl.BlockSpec((2 * NAT_KC, ncol), lambda l, v, h: (0, 0)),
                  pl.BlockSpec((1, nr, ncol), lambda l, v, h: (v, 0, 0))],
        out_specs=pl.BlockSpec((1, 1, 1, nr, ncol), lambda l, v, h: (l, v, h, 0, 0)),
        name="nat_bias_tables",
    )(jnp.asarray(e_row, BF16), rpb, jnp.asarray(e_col, BF16), jnp.asarray(ok))
    out = out.reshape(nl, 3, nh, NAT_QROWS, krows, GRID_W, GRID_W).transpose(0, 1, 2, 3, 5, 4, 6)
    return out.reshape(nl, 3, nh, NAT_QB, krows * GRID_W)


HALO = SUBLANES


def _conv_kernel(x_ref, prev_ref, next_ref, w_ref, b_ref, small_ref, pv_ref, y_ref, oa_ref, ob_ref,
                 xpad_ref, *, l2norm):
    i = pl.program_id(1)
    last = pl.num_programs(1) - 1
    tm = x_ref.shape[1]
    xpad_ref[0:HALO] = jnp.where(i > 0, prev_ref[0], 0.0)
    xpad_ref[HALO:HALO + tm] = x_ref[0]
    xpad_ref[HALO + tm:2 * HALO + tm] = jnp.where(i < last, next_ref[0], 0.0)
    w = w_ref[...]
    acc = jnp.broadcast_to(b_ref[...], (tm, x_ref.shape[2]))
    for j in range(CONV_K):
        acc = acc + xpad_ref[pl.ds(HALO - CONV_K // 2 + j, tm), :] * w[j:j + 1]
    y = _silu(acc)
    if l2norm:
        grp = (_iota((GROUP_W, GROUP_W), 0) // HEAD_W == _iota((GROUP_W, GROUP_W), 1) // HEAD_W).astype(BF16)
        q = y[:, 0:256]
        k = y[:, 256:512]
        y_ref[0, :, 0:256] = q * lax.rsqrt(_dot2(q * q, grp) + RMS_EPS) * (HEAD_W ** -0.5)
        y_ref[0, :, 256:512] = k * lax.rsqrt(_dot2(k * k, grp) + RMS_EPS)
        y_ref[0, :, 512:768] = y[:, 512:768]
    else:
        y_ref[0] = y
    s = small_ref[0]
    pv = pv_ref[...]
    sp = _softplus(s + pv[1:2])
    lane = _iota((1, LANES), 1)
    oa_ref[0] = jnp.where(lane < L_G, jax.nn.sigmoid(s), -jnp.exp(pv[0:1]) * sp)
    ob_ref[0] = sp


def _conv_prep(x, small, w, b, pv, *, tm, l2norm):
    nb, nt, nc = x.shape
    hb = tm // HALO
    nhalo = nt // HALO
    return pl.pallas_call(
        functools.partial(_conv_kernel, l2norm=l2norm),
        out_shape=(jax.ShapeDtypeStruct(x.shape, F32), jax.ShapeDtypeStruct(small.shape, F32),
                   jax.ShapeDtypeStruct(small.shape, F32)),
        grid=(nb, nt // tm),
        in_specs=[pl.BlockSpec((1, tm, nc), lambda bb, i: (bb, i, 0)),
                  pl.BlockSpec((1, HALO, nc), lambda bb, i: (bb, jnp.maximum(i * hb - 1, 0), 0)),
                  pl.BlockSpec((1, HALO, nc), lambda bb, i: (bb, jnp.minimum((i + 1) * hb, nhalo - 1), 0)),
                  _full(w.shape), _full(b.shape),
                  pl.BlockSpec((1, tm, LANES), lambda bb, i: (bb, i, 0)),
                  _full(pv.shape)],
        out_specs=(pl.BlockSpec((1, tm, nc), lambda bb, i: (bb, i, 0)),
                   pl.BlockSpec((1, tm, LANES), lambda bb, i: (bb, i, 0)),
                   pl.BlockSpec((1, tm, LANES), lambda bb, i: (bb, i, 0))),
        scratch_shapes=[pltpu.VMEM((tm + 2 * HALO, nc), F32)],
        name="conv_prep",
    )(x, x, x, w, b, small, pv)


def _chunk_cumsum_matrix(n, chunk, reverse):
    r = _iota((n, n), 0)
    c = _iota((n, n), 1)
    same = (r // chunk) == (c // chunk)
    return (same & ((c >= r) if reverse else (c <= r))).astype(BF16)


def _expand_matrix(lane0, group, width):
    return (_iota((LANES, width), 0) == lane0 + _iota((LANES, width), 1) // group).astype(BF16)


def _row_form(col_vals, chunk):
    pick = _iota(col_vals.shape, 0) == (_iota(col_vals.shape, 1) % chunk)
    ones = jnp.ones((SUBLANES, chunk), BF16)
    return _dot2l(ones, jnp.where(pick, col_vals, 0.0))[0:1]


def _gdn_kernel(y_ref, sm_ref, s0_ref, o_ref, sfin_ref, state_ref, *, d, n_chunks):
    ch = GDN_CHUNK
    cb = n_chunks * ch
    reverse = d == 1
    step = pl.program_id(1)

    @pl.when(step == 0)
    def _():
        state_ref[...] = s0_ref[0]

    sm = sm_ref[0]
    cs = _dot2l(_chunk_cumsum_matrix(cb, ch, reverse), sm)
    gc_all = _dot2(cs, _expand_matrix(L_G + N_HEADS * d, HEAD_W, GROUP_W))
    beta_all = _dot2(sm, _expand_matrix(L_BETA + N_HEADS * d, HEAD_W, GROUP_W))
    y = y_ref[0]
    q_all, k_all, v_all = y[:, 0:256], y[:, 256:512], y[:, 512:768]
    kb_all = k_all * beta_all
    vb_all = v_all * beta_all
    eg_all = jnp.exp(gc_all)

    ti = _iota((ch, GROUP_W), 0)
    tj = _iota((ch, GROUP_W), 1) % ch
    incl = (tj >= ti) if reverse else (tj <= ti)
    strict = (tj > ti) if reverse else (tj < ti)
    last = 0 if reverse else ch - 1

    order = range(n_chunks - 1, -1, -1) if reverse else range(n_chunks)
    for c in order:
        rows = slice(c * ch, (c + 1) * ch)
        gc = gc_all[rows]
        k = k_all[rows]
        diff = gc - _row_form(gc, ch)
        decay = jnp.where(incl, jnp.exp(jnp.where(incl, diff, 0.0)), 0.0)
        k_bd = _block_diag(k, N_HEADS, ch, HEAD_W).astype(BF16)
        both = jnp.concatenate([kb_all[rows], q_all[rows]], axis=0).astype(BF16)
        qk = _dot_nt(both, k_bd)
        a_mat = jnp.where(strict, qk[0:ch] * decay, 0.0)
        a_intra = qk[ch:2 * ch] * decay
        tinv = _unit_triangular_inverse(_block_diag(a_mat, N_HEADS, ch, ch), ch)
        rhs = jnp.concatenate([_block_diag(vb_all[rows], N_HEADS, ch, HEAD_W),
                               _block_diag(kb_all[rows] * eg_all[rows], N_HEADS, ch, HEAD_W)], axis=1)
        uw = _mm3(tinv, rhs)
        state = state_ref[...]
        sb = state.astype(BF16)
        v_new = uw[:, 0:GROUP_W] - _dot(uw[:, GROUP_W:].astype(BF16), sb)
        vb = v_new.astype(BF16)
        o_ref[0, rows, :] = _dot((q_all[rows] * eg_all[rows]).astype(BF16), sb) + _dot(a_intra.astype(BF16), vb)
        g_last = gc[last:last + 1]
        k_dec = _block_diag(k * jnp.exp(g_last - gc), N_HEADS, ch, HEAD_W).astype(BF16)
        state_ref[...] = state * jnp.exp(g_last) + _dot_tn(k_dec, vb)

    @pl.when(step == pl.num_programs(1) - 1)
    def _():
        sfin_ref[0] = state_ref[...]


def _ssd_kernel(x_ref, dt_ref, da_ref, s0_ref, o_ref, sfin_ref, state_ref, *, d, n_chunks):
    ch = SSM_CHUNK
    cb = n_chunks * ch
    reverse = d == 1
    step = pl.program_id(1)

    @pl.when(step == 0)
    def _():
        state_ref[...] = s0_ref[0]

    lane0 = L_DT + N_HEADS * d
    cs = _dot2l(_chunk_cumsum_matrix(cb, ch, reverse), da_ref[0])
    e64 = _expand_matrix(lane0, HEAD_W, GROUP_W)
    ac_all = _dot2(cs, e64)
    ac5_all = _dot2(cs, _expand_matrix(lane0, ch, N_HEADS * ch))
    dt_all = _dot2(dt_ref[0], e64)
    xbc = x_ref[0]
    xdt_all = xbc[:, 0:256] * dt_all
    b_all, c_all = xbc[:, 256:512], xbc[:, 512:768]

    ti = _iota((ch, N_HEADS * ch), 0)
    tj = _iota((ch, N_HEADS * ch), 1) % ch
    incl = (tj >= ti) if reverse else (tj <= ti)
    grp_keep = (_iota((GROUP_W, GROUP_W), 0) // ch) == (_iota((GROUP_W, GROUP_W), 1) // ch)
    last = 0 if reverse else ch - 1

    order = range(n_chunks - 1, -1, -1) if reverse else range(n_chunks)
    for c in order:
        rows = slice(c * ch, (c + 1) * ch)
        ac = ac_all[rows]
        ac5 = ac5_all[rows]
        diff = ac5 - _row_form(ac5, ch)
        seg = jnp.where(incl, jnp.exp(jnp.where(incl, diff, 0.0)), 0.0)
        bm = b_all[rows]
        cm = c_all[rows].astype(BF16)
        cb_g = _dot_nt(cm, _block_diag(bm, 2, ch, SSM_STATE).astype(BF16))
        scores = jnp.concatenate([cb_g[:, 0:ch], cb_g[:, 0:ch], cb_g[:, ch:], cb_g[:, ch:]], axis=1) * seg
        xdt = xdt_all[rows]
        y_diag = _dot(scores.astype(BF16), _block_diag(xdt, N_HEADS, ch, HEAD_W).astype(BF16))
        a_last = ac[last:last + 1]
        states = jnp.where(grp_keep, _dot_tn(bm.astype(BF16), (xdt * jnp.exp(a_last - ac)).astype(BF16)), 0.0)
        state = state_ref[...]
        o_ref[0, rows, :] = y_diag + _dot(cm, state.astype(BF16)) * jnp.exp(ac)
        state_ref[...] = state * jnp.exp(a_last) + states

    @pl.when(step == pl.num_programs(1) - 1)
    def _():
        sfin_ref[0] = state_ref[...]


def _scan_call(kernel, name, seq, smalls, s0, *, d, cb, chunk):
    nb, nt, nc = seq.shape
    nblk = nt // cb
    blk = (lambda bb, i: (bb, nblk - 1 - i, 0)) if d == 1 else (lambda bb, i: (bb, i, 0))
    sspec = pl.BlockSpec((1, GROUP_W, GROUP_W), lambda bb, i: (bb, 0, 0))
    return pl.pallas_call(
        functools.partial(kernel, d=d, n_chunks=cb // chunk),
        out_shape=(jax.ShapeDtypeStruct((nb, nt, GROUP_W), F32),
                   jax.ShapeDtypeStruct((nb, GROUP_W, GROUP_W), F32)),
        grid=(nb, nblk),
        in_specs=[pl.BlockSpec((1, cb, nc), blk)] + [pl.BlockSpec((1, cb, LANES), blk) for _ in smalls]
                 + [sspec],
        out_specs=(pl.BlockSpec((1, cb, GROUP_W), blk), sspec),
        scratch_shapes=[pltpu.VMEM((GROUP_W, GROUP_W), F32)],
        compiler_params=pltpu.CompilerParams(dimension_semantics=("arbitrary", "arbitrary")),
        name=name,
    )(seq, *smalls, s0)


def _bidir_scan(kernel, name, seq_l, smalls_l, seq_c, smalls_c, *, cb_l, cb_c, chunk):
    zero = jnp.zeros((seq_l.shape[0], GROUP_W, GROUP_W), F32)
    outs = []
    for d in (0, 1):
        o_c, s_c = _scan_call(kernel, name, seq_c, smalls_c, zero, d=d, cb=cb_c, chunk=chunk)
        o_l, _ = _scan_call(kernel, name, seq_l, smalls_l, s_c, d=d, cb=cb_l, chunk=chunk)
        outs.append((o_l, o_c))
    return outs


def _outproj_kernel(h_ref, mod_ref, g_ref, mla_ref, nat_ref, gof_ref, gob_ref, gzg_ref,
                    yf_ref, yb_ref, sx_ref, szg_ref, pv_ref, wm_ref, wr_ref, o_ref):
    pv = pv_ref[...]
    valid = _iota((1, LANES), 1) < HEAD_W
    slabs = [jnp.where(valid, mla_ref[0, hh], 0.0) for hh in range(N_HEADS)]
    ss = functools.reduce(jnp.add, [jnp.sum(s * s, axis=-1, keepdims=True) for s in slabs])
    scale = lax.rsqrt(ss / GROUP_W + RMS_EPS)
    y = None
    for hh in range(N_HEADS):
        part = _dot((slabs[hh] * scale * pv[hh:hh + 1, 0:LANES]).astype(BF16), wm_ref[hh])
        y = part if y is None else y + part
    y = y + _dot(nat_ref[0].astype(BF16), wr_ref[0:256, :])
    grp = (_iota((GROUP_W, GROUP_W), 0) // HEAD_W == _iota((GROUP_W, GROUP_W), 1) // HEAD_W).astype(BF16)
    o = gof_ref[0] + gob_ref[0]
    on = o * lax.rsqrt(_dot2(o * o, grp) / HEAD_W + RMS_EPS) * pv[4:5]
    y = y + _dot((on * _silu(gzg_ref[0])).astype(BF16), wr_ref[256:512, :])
    s = (yf_ref[0] + yb_ref[0] + sx_ref[0] * pv[5:6]) * _silu(szg_ref[0])
    sn = jnp.concatenate([_rms(s[:, 0:LANES], pv[6:7, 0:LANES]),
                          _rms(s[:, LANES:], pv[6:7, LANES:])], axis=1)
    y = y + _dot(sn.astype(BF16), wr_ref[512:768, :])
    m = mod_ref[0]
    g = g_ref[...]
    o_ref[0] = h_ref[0] + m[5:6] * _rms(y, g[3:4])


def _out_projection(h, mod, g, mla_o, nat_o, gdn_f, gdn_b, gdn_zg, ssm_f, ssm_b, ssm_xbc, ssm_zg,
                    pv, w_mla, w_rest, *, tm):
    nb, nt, dm = h.shape
    per_batch = mod.shape[0] > 1
    tspec = lambda w: pl.BlockSpec((1, tm, w), lambda b, i: (b, i, 0))
    return pl.pallas_call(
        _outproj_kernel,
        out_shape=jax.ShapeDtypeStruct(h.shape, F32),
        grid=(nb, nt // tm),
        in_specs=[tspec(dm),
                  pl.BlockSpec((1, N_MOD, dm), lambda b, i: (b if per_batch else 0, 0, 0)),
                  _full(g.shape),
                  pl.BlockSpec((1, N_HEADS, tm, LANES), lambda b, i: (b, 0, i, 0)),
                  tspec(256), tspec(256), tspec(256), tspec(256), tspec(256), tspec(256),
                  tspec(256),
                  tspec(256),
                  _full(pv.shape), _full(w_mla.shape), _full(w_rest.shape)],
        out_specs=tspec(dm),
        name="out_projection",
    )(h, mod, g, mla_o, nat_o, gdn_f, gdn_b, gdn_zg, ssm_f, ssm_b, ssm_xbc, ssm_zg, pv, w_mla, w_rest)


_ROPE_SWAP = np.array(list(range(8, 16)) + list(range(0, 8)) + list(range(24, 32)) + list(range(16, 24)))


def _pack_layer(p, l):
    w_in = p["w_in"][l]
    dm = w_in.shape[0]
    o_nat = MLA_Q_LORA + MLA_KV_LORA + MLA_ROPE
    o_gdn = o_nat + 768
    o_ssm = o_gdn + GDN_QKV + GROUP_W + 4 * N_HEADS
    kr = w_in[:, MLA_Q_LORA + MLA_KV_LORA:o_nat]
    z64 = jnp.zeros((dm, 64), F32)
    z32 = jnp.zeros((dm, 32), F32)
    small = jnp.concatenate([w_in[:, o_gdn + 1024:o_gdn + 1040],
                             w_in[:, o_ssm + 1024:o_ssm + 1032],
                             jnp.zeros((dm, LANES - 24), F32)], axis=1)
    w_packed = jnp.concatenate([
        w_in[:, 0:384], z64, kr, z32, z64, kr[:, _ROPE_SWAP], z32,
        w_in[:, o_nat:o_gdn],
        w_in[:, o_gdn:o_gdn + 1024],
        w_in[:, o_ssm:o_ssm + 1024],
        small], axis=1).astype(BF16)
    assert w_packed.shape[1] == C_TOTAL

    wuq = p["mla_wuq"][l].reshape(MLA_Q_LORA, N_HEADS, MLA_NOPE + MLA_ROPE)
    zq = jnp.zeros((MLA_Q_LORA, N_HEADS, 32), F32)
    wq1 = jnp.concatenate([wuq, zq], axis=2).reshape(MLA_Q_LORA, N_HEADS * LANES)
    wq2 = jnp.concatenate([jnp.zeros((MLA_Q_LORA, N_HEADS, 64), F32), wuq[:, :, MLA_NOPE:][:, :, _ROPE_SWAP], zq],
                          axis=2).reshape(MLA_Q_LORA, N_HEADS * LANES)
    wukv = p["mla_wukv"][l].reshape(MLA_KV_LORA, N_HEADS, MLA_NOPE + HEAD_W)
    zk = jnp.zeros((MLA_KV_LORA, N_HEADS, 64), F32)
    wk = jnp.concatenate([wukv[:, :, :MLA_NOPE], zk], axis=2).reshape(MLA_KV_LORA, N_HEADS * LANES)
    wv = jnp.concatenate([wukv[:, :, MLA_NOPE:], zk], axis=2).reshape(MLA_KV_LORA, N_HEADS * LANES)

    def lanes(vals, lane0):
        v = vals.reshape(-1)
        return jnp.zeros((LANES,), F32).at[lane0:lane0 + v.shape[0]].set(v)

    w_out = p["w_out"][l]
    w_mla = jnp.concatenate([w_out[0:256].reshape(N_HEADS, HEAD_W, dm),
                             jnp.zeros((N_HEADS, LANES - HEAD_W, dm), F32)], axis=1).astype(BF16)
    gout = jnp.concatenate([p["mla_gout"][l].reshape(N_HEADS, HEAD_W),
                            jnp.zeros((N_HEADS, GROUP_W - HEAD_W), F32)], axis=1)
    pv_out = jnp.concatenate([gout,
                              jnp.tile(p["gdn_gnorm"][l], N_HEADS)[None],
                              jnp.repeat(p["ssm_d"][l], HEAD_W)[None],
                              p["ssm_gnorm"][l][None],
                              jnp.zeros((1, GROUP_W), F32)], axis=0)
    return dict(
        ffn1=(p["ffn1_w1"][l].astype(BF16), p["ffn1_w3"][l].astype(BF16), p["ffn1_w2"][l].astype(BF16)),
        ffn2=(p["ffn2_w1"][l].astype(BF16), p["ffn2_w3"][l].astype(BF16), p["ffn2_w2"][l].astype(BF16)),
        g=p["norm_g"][l],
        w_in=w_packed, wq1=wq1.astype(BF16), wq2=wq2.astype(BF16), wk=wk.astype(BF16), wv=wv.astype(BF16),
        mla_gq=p["mla_gq"][l][None], mla_gkv=p["mla_gkv"][l][None],
        nat_bias=p["nat_bias"][l], nat_gout=p["nat_gout"][l][None],
        gdn_conv_w=p["gdn_conv_w"][l], gdn_conv_b=jnp.zeros((1, GDN_QKV), F32),
        gdn_pv=jnp.stack([lanes(p["gdn_a_log"][l], L_G), lanes(p["gdn_dt_bias"][l], L_G)]),
        ssm_conv_w=p["ssm_conv_w"][l], ssm_conv_b=p["ssm_conv_b"][l][None],
        ssm_pv=jnp.stack([lanes(p["ssm_a_log"][l], L_DT), lanes(p["ssm_dt_bias"][l], L_DT)]),
        pv_out=pv_out, w_mla=w_mla, w_rest=w_out[256:].astype(BF16),
    )


def _rope_tables(n_tok):
    pos = jnp.arange(n_tok)
    rows = (pos // GRID_W).astype(F32)
    cols = (pos % GRID_W).astype(F32)
    quarter = MLA_ROPE // 4
    freqs = ROPE_THETA ** (-jnp.arange(quarter, dtype=F32) / quarter)
    ar = rows[:, None] * freqs
    ac = cols[:, None] * freqs
    cos = jnp.concatenate([jnp.cos(ar), jnp.cos(ar), jnp.cos(ac), jnp.cos(ac)], axis=1)
    sin = jnp.concatenate([-jnp.sin(ar), jnp.sin(ar), -jnp.sin(ac), jnp.sin(ac)], axis=1)
    ones = jnp.ones((n_tok, MLA_NOPE), F32)
    zeros = jnp.zeros((n_tok, MLA_NOPE), F32)
    pad = jnp.zeros((n_tok, LANES - MLA_NOPE - MLA_ROPE), F32)
    return jnp.concatenate([ones, cos, pad], axis=1), jnp.concatenate([zeros, sin, pad], axis=1)


def _tiles(n_tok):
    return dict(tm=min(512, n_tok), tq=min(512, n_tok), cb_gdn=min(256, n_tok), cb_ssd=min(512, n_tok))


def _mixers(hl, hc, ml, mc, lw, ropes, need_ctx):
    tl, tc = _tiles(hl.shape[1]), _tiles(hc.shape[1])
    (cos_l, sin_l), (cos_c, sin_c) = ropes
    zl = _in_projection(hl, ml, lw["g"], lw, cos_l, sin_l, tm=tl["tm"])
    zc = _in_projection(hc, mc, lw["g"], lw, cos_c, sin_c, tm=tc["tm"])
    (mq_l, mk_l, mv_l, nq_l, nk_l, nv_l, gqkv_l, gzg_l, szg_l, sxbc_l, small_l) = zl
    (mq_c, mk_c, mv_c, nq_c, nk_c, nv_c, gqkv_c, gzg_c, szg_c, sxbc_c, small_c) = zc

    n_lat = hl.shape[1]
    tk = min(1024, n_lat)
    mla_l = _mla_attention(mq_l, mk_c, mv_c, mk_l, mv_l, tq=tl["tq"], tk=tk)
    nat_l = _nat_attention(nq_l, nk_l, nv_l, nk_c, nv_c, lw["nat_bias"], lw["nat_gout"])

    gy_l, ga_l, _ = _conv_prep(gqkv_l, small_l, lw["gdn_conv_w"], lw["gdn_conv_b"], lw["gdn_pv"],
                               tm=tl["tm"], l2norm=True)
    gy_c, ga_c, _ = _conv_prep(gqkv_c, small_c, lw["gdn_conv_w"], lw["gdn_conv_b"], lw["gdn_pv"],
                               tm=tc["tm"], l2norm=True)
    (gf_l, gf_c), (gb_l, gb_c) = _bidir_scan(_gdn_kernel, "gdn_scan", gy_l, (ga_l,), gy_c, (ga_c,),
                                             cb_l=tl["cb_gdn"], cb_c=tc["cb_gdn"], chunk=GDN_CHUNK)

    sy_l, sa_l, sd_l = _conv_prep(sxbc_l, small_l, lw["ssm_conv_w"], lw["ssm_conv_b"], lw["ssm_pv"],
                                  tm=tl["tm"], l2norm=False)
    sy_c, sa_c, sd_c = _conv_prep(sxbc_c, small_c, lw["ssm_conv_w"], lw["ssm_conv_b"], lw["ssm_pv"],
                                  tm=tc["tm"], l2norm=False)
    (sf_l, sf_c), (sb_l, sb_c) = _bidir_scan(_ssd_kernel, "ssd_scan", sy_l, (sd_l, sa_l), sy_c, (sd_c, sa_c),
                                             cb_l=tl["cb_ssd"], cb_c=tc["cb_ssd"], chunk=SSM_CHUNK)

    hl = _out_projection(hl, ml, lw["g"], mla_l, nat_l, gf_l, gb_l, gzg_l, sf_l, sb_l, sy_l, szg_l,
                         lw["pv_out"], lw["w_mla"], lw["w_rest"], tm=tl["tm"])
    if need_ctx:
        mla_c = _mla_attention(mq_c, mk_c, mv_c, tq=tc["tq"], tk=tk)
        nat_c = _nat_ctx_attention(nq_c, nk_c, nv_c, lw["nat_gout"])
        hc = _out_projection(hc, mc, lw["g"], mla_c, nat_c, gf_c, gb_c, gzg_c, sf_c, sb_c, sy_c, szg_c,
                             lw["pv_out"], lw["w_mla"], lw["w_rest"], tm=tc["tm"])
    return hl, hc


def _layer(hl, hc, ml, mc, lw, ropes, need_ctx):
    tl, tc = _tiles(hl.shape[1]), _tiles(hc.shape[1])
    hl = _half_ffn(hl, ml, lw["g"], *lw["ffn1"], k0=0, gp=0, tm=tl["tm"])
    hc = _half_ffn(hc, mc, lw["g"], *lw["ffn1"], k0=0, gp=0, tm=tc["tm"])
    hl, hc = _mixers(hl, hc, ml, mc, lw, ropes, need_ctx)
    hl = _half_ffn(hl, ml, lw["g"], *lw["ffn2"], k0=6, gp=4, tm=tl["tm"])
    if need_ctx:
        hc = _half_ffn(hc, mc, lw["g"], *lw["ffn2"], k0=6, gp=4, tm=tc["tm"])
    return hl, hc


def kernel(x, c, ctx, c_ctx, w_mod, b_mod, norm_g, ffn1_w1, ffn1_w3, ffn1_w2, ffn2_w1, ffn2_w3, ffn2_w2,
           w_in, w_out, mla_gq, mla_gkv, mla_wuq, mla_wukv, mla_gout, nat_rpb, nat_gout, gdn_conv_w,
           gdn_a_log, gdn_dt_bias, gdn_gnorm, ssm_conv_w, ssm_conv_b, ssm_a_log, ssm_dt_bias, ssm_d,
           ssm_gnorm):
    p = dict(norm_g=norm_g, ffn1_w1=ffn1_w1, ffn1_w3=ffn1_w3, ffn1_w2=ffn1_w2, ffn2_w1=ffn2_w1,
             ffn2_w3=ffn2_w3, ffn2_w2=ffn2_w2, w_in=w_in, w_out=w_out, mla_gq=mla_gq, mla_gkv=mla_gkv,
             mla_wuq=mla_wuq, mla_wukv=mla_wukv, mla_gout=mla_gout, nat_rpb=nat_rpb, nat_gout=nat_gout,
             gdn_conv_w=gdn_conv_w, gdn_a_log=gdn_a_log, gdn_dt_bias=gdn_dt_bias, gdn_gnorm=gdn_gnorm,
             ssm_conv_w=ssm_conv_w, ssm_conv_b=ssm_conv_b, ssm_a_log=ssm_a_log, ssm_dt_bias=ssm_dt_bias,
             ssm_d=ssm_d, ssm_gnorm=ssm_gnorm)
    nb, n_lat, dm = x.shape
    n_ctx = ctx.shape[1]
    depth = w_mod.shape[0]
    cvec = jnp.concatenate([c, c_ctx[None], jnp.zeros((SUBLANES - nb - 1, dm), F32)], axis=0)
    mods = _modulation(cvec, w_mod, b_mod).reshape(depth, SUBLANES, N_MOD, dm)
    cos_c = jnp.concatenate([jnp.ones((n_ctx, MLA_NOPE + MLA_ROPE), F32),
                             jnp.zeros((n_ctx, LANES - MLA_NOPE - MLA_ROPE), F32)], axis=1)
    ropes = (_rope_tables(n_lat), (cos_c, jnp.zeros((n_ctx, LANES), F32)))
    p["nat_bias"] = _nat_bias_tables(nat_rpb, n_lat // GRID_W)
    hl, hc = x, ctx
    for l in range(depth):
        lw = _pack_layer(p, l)
        hl, hc = _layer(hl, hc, mods[l, 0:nb], mods[l, nb:nb + 1], lw, ropes, need_ctx=l < depth - 1)
    return hl
```

```python
import functools
import math

import jax
import jax.numpy as jnp
import numpy as np
from jax import lax
from jax.experimental import pallas as pl
from jax.experimental.pallas import tpu as pltpu

F32 = jnp.float32
BF16 = jnp.bfloat16

D_MODEL = 1024
DEPTH = 4
GRID_W = 64
N_MOD = 9
D_FF = 2816
RMS_EPS = 1e-6
NEG_INF = -1e30
ROPE_THETA = 10000.0
GROUP_W = 256
N_HEADS = 4
HEAD_W = 64
MLA_NOPE = 64
MLA_ROPE = 32
MLA_Q_LORA = 256
MLA_KV_LORA = 128
NAT_KR = 8
NAT_KC = 16
GDN_CHUNK = 64
SSM_STATE = 128
SSM_CHUNK = 128
CONV_K = 5
GDN_QKV = 768
SSM_XBC = 768

LANES = 128
SUBLANES = 8
VMEM_LIMIT = 56 * 1024 * 1024

C_MLA = 0
C_NAT = 640
C_GQKV = 1408
C_GZG = 2176
C_SZG = 2432
C_SXBC = 2688
C_SMALL = 3456
C_TOTAL = 3584
L_BETA, L_G, L_DT = 0, 8, 16

MLA_QSCALE = (MLA_NOPE + MLA_ROPE) ** -0.5 * math.log2(math.e)
NAT_QSCALE = HEAD_W ** -0.5


def _dot(a, b):
    return jnp.dot(a, b, preferred_element_type=F32)


def _dot_nt(a, b):
    return lax.dot_general(a, b, (((1,), (1,)), ((), ())), preferred_element_type=F32)


def _dot_tn(a, b):
    return lax.dot_general(a, b, (((0,), (0,)), ((), ())), preferred_element_type=F32)


def _split(x):
    hi = x.astype(BF16)
    lo = (x - hi.astype(F32)).astype(BF16)
    return hi, lo


def _dot2(a, m):
    hi, lo = _split(a)
    return _dot(hi, m) + _dot(lo, m)


def _dot2l(m, a):
    hi, lo = _split(a)
    return _dot(m, hi) + _dot(m, lo)


def _rms(x, g):
    return x * lax.rsqrt(jnp.mean(x * x, axis=-1, keepdims=True) + RMS_EPS) * g


def _silu(x):
    return x * jax.nn.sigmoid(x)


def _softplus(x):
    return jnp.maximum(x, 0.0) + jnp.log1p(jnp.exp(-jnp.abs(x)))


def _iota(shape, dim):
    return lax.broadcasted_iota(jnp.int32, shape, dim)


def _block_diag(x, n, blk_r, blk_c):
    t = jnp.concatenate([x] * n, axis=0)
    keep = (_iota(t.shape, 0) // blk_r) == (_iota(t.shape, 1) // blk_c)
    return jnp.where(keep, t, 0.0)


def _full(shape):
    nd = len(shape)
    return pl.BlockSpec(shape, lambda *_: (0,) * nd)


def _resident(shape):
    nd = len(shape)
    return pl.BlockSpec(shape, lambda *_: (0,) * nd, pipeline_mode=pl.Buffered(1))


def _mod_kernel(c_ref, w_ref, b_ref, o_ref):
    s = _silu(c_ref[...])
    o_ref[0] = jnp.dot(s, w_ref[0], preferred_element_type=F32,
                       precision=lax.Precision.HIGHEST) + b_ref[0]


def _modulation(cvec, w_mod, b_mod):
    nl, dm, nm = w_mod.shape
    rows = cvec.shape[0]
    tn = 1536
    return pl.pallas_call(
        _mod_kernel,
        out_shape=jax.ShapeDtypeStruct((nl, rows, nm), F32),
        grid=(nl, nm // tn),
        in_specs=[pl.BlockSpec((rows, dm), lambda l, j: (0, 0)),
                  pl.BlockSpec((1, dm, tn), lambda l, j: (l, 0, j)),
                  pl.BlockSpec((1, 1, tn), lambda l, j: (l, 0, j))],
        out_specs=pl.BlockSpec((1, rows, tn), lambda l, j: (l, 0, j)),
        name="modulation",
    )(cvec, w_mod, b_mod.reshape(nl, 1, nm))


FFN_CHUNK = 256


def _ffn_kernel(h_ref, mod_ref, g_ref, w1_ref, w3_ref, w2_ref, o_ref, *, k0, gp):
    x = h_ref[0]
    m = mod_ref[0]
    g = g_ref[...]
    u = _rms(x, g[gp:gp + 1]) * (1.0 + m[k0 + 1:k0 + 2]) + m[k0:k0 + 1]
    ub = u.astype(BF16)
    acc = None
    for c in range(D_FF // FFN_CHUNK):
        sl = slice(c * FFN_CHUNK, (c + 1) * FFN_CHUNK)
        a = _dot(ub, w1_ref[:, sl])
        b = _dot(ub, w3_ref[:, sl])
        hid = (_silu(a) * b).astype(BF16)
        part = _dot(hid, w2_ref[sl, :])
        acc = part if acc is None else acc + part
    o_ref[0] = x + 0.5 * m[k0 + 2:k0 + 3] * _rms(acc, g[gp + 1:gp + 2])


def _half_ffn(h, mod, g, w1, w3, w2, *, k0, gp, tm):
    nb, nt, dm = h.shape
    per_batch = mod.shape[0] > 1
    return pl.pallas_call(
        functools.partial(_ffn_kernel, k0=k0, gp=gp),
        out_shape=jax.ShapeDtypeStruct(h.shape, F32),
        grid=(nb, nt // tm),
        in_specs=[pl.BlockSpec((1, tm, dm), lambda b, i: (b, i, 0)),
                  pl.BlockSpec((1, N_MOD, dm), lambda b, i: (b if per_batch else 0, 0, 0)),
                  _full(g.shape), _resident(w1.shape), _resident(w3.shape), _resident(w2.shape)],
        out_specs=pl.BlockSpec((1, tm, dm), lambda b, i: (b, i, 0)),
        compiler_params=pltpu.CompilerParams(vmem_limit_bytes=VMEM_LIMIT),
        name="half_ffn",
    )(h, mod, g, w1, w3, w2)


def _inproj_kernel(h_ref, mod_ref, g_ref, w_ref, gq_ref, gkv_ref, wq1_ref, wq2_ref, wk_ref, wv_ref,
                   cos_ref, sin_ref,
                   mq_ref, mk_ref, mv_ref, nq_ref, nk_ref, nv_ref, gqkv_ref, gzg_ref, szg_ref,
                   sxbc_ref, small_ref):
    x = h_ref[0]
    m = mod_ref[0]
    g = g_ref[...]
    u = _rms(x, g[2:3]) * (1.0 + m[4:5]) + m[3:4]
    ub = u.astype(BF16)

    def proj(a, b):
        return _dot(ub, w_ref[:, a:b])

    zm = proj(C_MLA, C_NAT)
    cqn = _rms(zm[:, 0:256], gq_ref[...]).astype(BF16)
    ckvn = _rms(zm[:, 256:384], gkv_ref[...]).astype(BF16)
    cos = cos_ref[...]
    sin = sin_ref[...]
    k_rope = zm[:, 384:512] * cos + zm[:, 512:640] * sin
    q1 = _dot(cqn, wq1_ref[...])
    q2 = _dot(cqn, wq2_ref[...])
    kn = _dot(ckvn, wk_ref[...])
    vv = _dot(ckvn, wv_ref[...])
    ones_col = (_iota((1, LANES), 1) == HEAD_W).astype(F32)
    for hh in range(N_HEADS):
        sl = slice(hh * LANES, (hh + 1) * LANES)
        mq_ref[0, hh] = ((q1[:, sl] * cos + q2[:, sl] * sin) * MLA_QSCALE).astype(BF16)
        mk_ref[0, hh] = (kn[:, sl] + k_rope).astype(BF16)
        mv_ref[0, hh] = (vv[:, sl] + ones_col).astype(BF16)

    nq_ref[0] = (proj(C_NAT, C_NAT + 256) * NAT_QSCALE).astype(BF16)
    nk_ref[0] = proj(C_NAT + 256, C_NAT + 512).astype(BF16)
    nv_ref[0] = proj(C_NAT + 512, C_NAT + 768).astype(BF16)
    gqkv_ref[0] = proj(C_GQKV, C_GZG)
    gzg_ref[0] = proj(C_GZG, C_SZG)
    szg_ref[0] = proj(C_SZG, C_SXBC)
    sxbc_ref[0] = proj(C_SXBC, C_SMALL)
    small_ref[0] = proj(C_SMALL, C_TOTAL)


def _in_projection(h, mod, g, lw, cos, sin, *, tm):
    nb, nt, dm = h.shape
    per_batch = mod.shape[0] > 1
    tok = lambda w, dt: jax.ShapeDtypeStruct((nb, nt, w), dt)
    head = jax.ShapeDtypeStruct((nb, N_HEADS, nt, LANES), BF16)
    tspec = lambda w: pl.BlockSpec((1, tm, w), lambda b, i: (b, i, 0))
    hspec = pl.BlockSpec((1, N_HEADS, tm, LANES), lambda b, i: (b, 0, i, 0))
    return pl.pallas_call(
        _inproj_kernel,
        out_shape=(head, head, head, tok(256, BF16), tok(256, BF16), tok(256, BF16),
                   tok(768, F32), tok(256, F32), tok(256, F32), tok(768, F32), tok(LANES, F32)),
        grid=(nb, nt // tm),
        in_specs=[tspec(dm),
                  pl.BlockSpec((1, N_MOD, dm), lambda b, i: (b if per_batch else 0, 0, 0)),
                  _full(g.shape), _resident(lw["w_in"].shape),
                  _full(lw["mla_gq"].shape), _full(lw["mla_gkv"].shape),
                  _full(lw["wq1"].shape), _full(lw["wq2"].shape),
                  _full(lw["wk"].shape), _full(lw["wv"].shape),
                  pl.BlockSpec((tm, LANES), lambda b, i: (i, 0)),
                  pl.BlockSpec((tm, LANES), lambda b, i: (i, 0))],
        out_specs=(hspec, hspec, hspec, tspec(256), tspec(256), tspec(256),
                   tspec(768), tspec(256), tspec(256), tspec(768), tspec(LANES)),
        compiler_params=pltpu.CompilerParams(vmem_limit_bytes=VMEM_LIMIT),
        name="in_projection",
    )(h, mod, g, lw["w_in"], lw["mla_gq"], lw["mla_gkv"], lw["wq1"], lw["wq2"], lw["wk"], lw["wv"],
      cos, sin)


def _mla_kernel(*refs, n_lat_chunks, tk):
    if n_lat_chunks:
        q_ref, kl_ref, vl_ref, kc_ref, vc_ref, o_ref = refs
    else:
        q_ref, kc_ref, vc_ref, o_ref = refs
    q = q_ref[0, 0]
    tq = q.shape[0]

    def step(kb, vb, carry):
        m, acc = carry
        s = _dot_nt(q, kb)
        mn = jnp.maximum(m, jnp.max(s, axis=-1, keepdims=True))
        p = jnp.exp2(s - mn)
        acc = jnp.exp2(m - mn) * acc + _dot(p.astype(BF16), vb)
        return mn, acc

    carry = (jnp.full((tq, 1), NEG_INF, F32), jnp.zeros((tq, LANES), F32))
    if n_lat_chunks:
        def body(j, carry):
            off = pl.multiple_of(j * tk, tk)
            return step(kl_ref[0, 0, pl.ds(off, tk), :], vl_ref[0, 0, pl.ds(off, tk), :], carry)
        carry = lax.fori_loop(0, n_lat_chunks, body, carry, unroll=4)
    _, acc = step(kc_ref[0, 0], vc_ref[0, 0], carry)
    o_ref[0, 0] = acc / acc[:, HEAD_W:HEAD_W + 1]


def _mla_attention(q, k_ctx, v_ctx, k_lat=None, v_lat=None, *, tq, tk):
    nb, nh, nq, _ = q.shape
    nc = k_ctx.shape[2]
    qspec = pl.BlockSpec((1, 1, tq, LANES), lambda b, h, i: (b, h, i, 0))
    cspec = pl.BlockSpec((1, 1, nc, LANES), lambda b, h, i: (b, h, 0, 0))
    if k_lat is None:
        args, specs, n_chunks = (q, k_ctx, v_ctx), [qspec, cspec, cspec], 0
    else:
        nk = k_lat.shape[2]
        lspec = pl.BlockSpec((1, 1, nk, LANES), lambda b, h, i: (b, h, 0, 0))
        args, specs, n_chunks = (q, k_lat, v_lat, k_ctx, v_ctx), [qspec, lspec, lspec, cspec, cspec], nk // tk
    return pl.pallas_call(
        functools.partial(_mla_kernel, n_lat_chunks=n_chunks, tk=tk),
        out_shape=jax.ShapeDtypeStruct((nb, nh, nq, LANES), F32),
        grid=(nb, nh, nq // tq),
        in_specs=specs,
        out_specs=qspec,
        compiler_params=pltpu.CompilerParams(vmem_limit_bytes=VMEM_LIMIT),
        name="mla_attention",
    )(*args)


NAT_QROWS = 4
NAT_QB = NAT_QROWS * GRID_W
NAT_KBLKS = 3


def _heads_attention(q, parts, gout):
    lane_head = _iota((1, GROUP_W), 1) // HEAD_W
    out = jnp.zeros(q.shape, F32)
    for hh in range(N_HEADS):
        sel = lane_head == hh
        qh = jnp.where(sel, q, jnp.zeros_like(q))
        scores = []
        for k, _, bias in parts:
            s = _dot_nt(qh, k)
            scores.append(s if bias is None else s + bias[hh])
        m = functools.reduce(jnp.maximum, [jnp.max(s, axis=-1, keepdims=True) for s in scores])
        ps = [jnp.exp(s - m) for s in scores]
        l = functools.reduce(jnp.add, [jnp.sum(p, axis=-1, keepdims=True) for p in ps])
        o = functools.reduce(jnp.add, [_dot(p.astype(BF16), v) for p, (_, v, _) in zip(ps, parts)])
        out = jnp.where(sel, o / l, out)
    return _rms(out, gout)


def _nat_kernel(q_ref, k0_ref, k1_ref, k2_ref, v0_ref, v1_ref, v2_ref, kc_ref, vc_ref, bias_ref, g_ref,
                o_ref):
    kw = jnp.concatenate([k0_ref[0], k1_ref[0], k2_ref[0]], axis=0)
    vw = jnp.concatenate([v0_ref[0], v1_ref[0], v2_ref[0]], axis=0)
    parts = [(kw, vw, bias_ref[0]), (kc_ref[0], vc_ref[0], None)]
    o_ref[0] = _heads_attention(q_ref[0], parts, g_ref[...])


def _nat_ctx_kernel(q_ref, k_ref, v_ref, g_ref, o_ref):
    o_ref[0] = _heads_attention(q_ref[0], [(k_ref[0], v_ref[0], None)], g_ref[...])


def _nat_attention(q, k, v, kc, vc, bias, gout):
    nb, nt, _ = q.shape
    nblk = nt // NAT_QB
    nc = kc.shape[1]
    start = lambda i: jnp.clip(i - 1, 0, nblk - NAT_KBLKS)
    variant = lambda i: jnp.where(i == 0, 0, jnp.where(i == nblk - 1, 2, 1))
    qspec = pl.BlockSpec((1, NAT_QB, GROUP_W), lambda b, i: (b, i, 0))
    kspecs = [pl.BlockSpec((1, NAT_QB, GROUP_W), lambda b, i, j=j: (b, start(i) + j, 0))
              for j in range(NAT_KBLKS)]
    cspec = pl.BlockSpec((1, nc, GROUP_W), lambda b, i: (b, 0, 0))
    bspec = pl.BlockSpec((1, N_HEADS, NAT_QB, NAT_KBLKS * NAT_QB), lambda b, i: (variant(i), 0, 0, 0))
    return pl.pallas_call(
        _nat_kernel,
        out_shape=jax.ShapeDtypeStruct((nb, nt, GROUP_W), F32),
        grid=(nb, nblk),
        in_specs=[qspec] + kspecs + kspecs + [cspec, cspec, bspec, _full(gout.shape)],
        out_specs=qspec,
        name="nat_attention",
    )(q, k, k, k, v, v, v, kc, vc, bias, gout)


def _nat_ctx_attention(q, k, v, gout):
    nb, nc, _ = q.shape
    spec = pl.BlockSpec((1, nc, GROUP_W), lambda b: (b, 0, 0))
    return pl.pallas_call(
        _nat_ctx_kernel,
        out_shape=jax.ShapeDtypeStruct((nb, nc, GROUP_W), F32),
        grid=(nb,),
        in_specs=[spec, spec, spec, _full(gout.shape)],
        out_specs=spec,
        name="nat_ctx_attention",
    )(q, k, v, gout)


def _nat_bias_constants(n_rows):
    krows = NAT_KBLKS * NAT_QROWS
    qr = np.arange(NAT_QROWS)[:, None]
    kk = np.arange(krows)[None, :]
    cq = np.arange(GRID_W)[:, None]
    ck = np.arange(GRID_W)[None, :]
    c0 = np.clip(cq - NAT_KC // 2, 0, GRID_W - NAT_KC)
    col_ok = ((ck >= c0) & (ck < c0 + NAT_KC)).reshape(-1)
    dc = np.clip(ck - cq + (NAT_KC - 1), 0, 2 * NAT_KC - 2).reshape(-1)
    e_col = np.zeros((2 * NAT_KC, GRID_W * GRID_W), np.float32)
    e_col[dc, np.arange(GRID_W * GRID_W)] = 1.0
    big = 4 * n_rows + 64
    placements = [(0, 0, n_rows), (big // 2, big // 2 - NAT_QROWS, big),
                  (n_rows - NAT_QROWS, n_rows - krows, n_rows)]
    e_row = np.zeros((3, NAT_QROWS * krows, 2 * NAT_KR), np.float32)
    ok = np.zeros((3, NAT_QROWS * krows, GRID_W * GRID_W), np.float32)
    for v, (r_base, k_start, rows_total) in enumerate(placements):
        r = r_base + qr
        k_abs = k_start + kk
        r0 = np.clip(r - NAT_KR // 2, 0, rows_total - NAT_KR)
        row_ok = ((k_abs >= r0) & (k_abs < r0 + NAT_KR)).reshape(-1)
        dr = np.clip(k_abs - r + (NAT_KR - 1), 0, 2 * NAT_KR - 2).reshape(-1)
        e_row[v, np.arange(NAT_QROWS * krows), dr] = 1.0
        ok[v] = row_ok[:, None] & col_ok[None, :]
    return e_row, e_col, ok


def _split3(x):
    h1 = x.astype(BF16)
    r1 = x - h1.astype(F32)
    h2 = r1.astype(BF16)
    h3 = (r1 - h2.astype(F32)).astype(BF16)
    return h1, h2, h3


def _nat_bias_kernel(er_ref, rpb_ref, ec_ref, ok_ref, o_ref):
    er = er_ref[0]
    rows = functools.reduce(jnp.add, [_dot(er, p) for p in _split3(rpb_ref[0, 0])])
    ec = ec_ref[...]
    b = functools.reduce(jnp.add, [_dot(p, ec) for p in _split3(rows)])
    o_ref[0, 0, 0] = jnp.where(ok_ref[0] > 0.0, b, NEG_INF)


def _nat_bias_tables(rpb_all, n_rows):
    nl, nh = rpb_all.shape[:2]
    krows = NAT_KBLKS * NAT_QROWS
    e_row, e_col, ok = _nat_bias_constants(n_rows)
    rpb = jnp.pad(rpb_all, ((0, 0), (0, 0), (0, 1), (0, 1)))
    nr, ncol = NAT_QROWS * krows, GRID_W * GRID_W
    out = pl.pallas_call(
        _nat_bias_kernel,
        out_shape=jax.ShapeDtypeStruct((nl, 3, nh, nr, ncol), F32),
        grid=(nl, 3, nh),
        in_specs=[pl.BlockSpec((1, nr, 2 * NAT_KR), lambda l, v, h: (v, 0, 0)),
                  pl.BlockSpec((1, 1, 2 * NAT_KR, 2 * NAT_KC), lambda l, v, h: (l, h, 0, 0)),
                  pl.BlockSpec((2 * NAT_KC, ncol), lambda l, v, h: (0, 0)),
                  pl.BlockSpec((1, nr, ncol), lambda l, v, h: (v, 0, 0))],
        out_specs=pl.BlockSpec((1, 1, 1, nr, ncol), lambda l, v, h: (l, v, h, 0, 0)),
        name="nat_bias_tables",
    )(jnp.asarray(e_row, BF16), rpb, jnp.asarray(e_col, BF16), jnp.asarray(ok))
    out = out.reshape(nl, 3, nh, NAT_QROWS, krows, GRID_W, GRID_W).transpose(0, 1, 2, 3, 5, 4, 6)
    return out.reshape(nl, 3, nh, NAT_QB, krows * GRID_W)


HALO = SUBLANES


def _conv_kernel(x_ref, prev_ref, next_ref, w_ref, b_ref, small_ref, pv_ref, y_ref, oa_ref, ob_ref,
                 xpad_ref, *, l2norm):
    i = pl.program_id(1)
    last = pl.num_programs(1) - 1
    tm = x_ref.shape[1]
    xpad_ref[0:HALO] = jnp.where(i > 0, prev_ref[0], 0.0)
    xpad_ref[HALO:HALO + tm] = x_ref[0]
    xpad_ref[HALO + tm:2 * HALO + tm] = jnp.where(i < last, next_ref[0], 0.0)
    w = w_ref[...]
    acc = jnp.broadcast_to(b_ref[...], (tm, x_ref.shape[2]))
    for j in range(CONV_K):
        acc = acc + xpad_ref[pl.ds(HALO - CONV_K // 2 + j, tm), :] * w[j:j + 1]
    y = _silu(acc)
    if l2norm:
        grp = (_iota((GROUP_W, GROUP_W), 0) // HEAD_W == _iota((GROUP_W, GROUP_W), 1) // HEAD_W).astype(BF16)
        q = y[:, 0:256]
        k = y[:, 256:512]
        y_ref[0, :, 0:256] = q * lax.rsqrt(_dot2(q * q, grp) + RMS_EPS) * (HEAD_W ** -0.5)
        y_ref[0, :, 256:512] = k * lax.rsqrt(_dot2(k * k, grp) + RMS_EPS)
        y_ref[0, :, 512:768] = y[:, 512:768]
    else:
        y_ref[0] = y
    s = small_ref[0]
    pv = pv_ref[...]
    sp = _softplus(s + pv[1:2])
    lane = _iota((1, LANES), 1)
    oa_ref[0] = jnp.where(lane < L_G, jax.nn.sigmoid(s), -jnp.exp(pv[0:1]) * sp)
    ob_ref[0] = sp


def _conv_prep(x, small, w, b, pv, *, tm, l2norm):
    nb, nt, nc = x.shape
    hb = tm // HALO
    nhalo = nt // HALO
    return pl.pallas_call(
        functools.partial(_conv_kernel, l2norm=l2norm),
        out_shape=(jax.ShapeDtypeStruct(x.shape, F32), jax.ShapeDtypeStruct(small.shape, F32),
                   jax.ShapeDtypeStruct(small.shape, F32)),
        grid=(nb, nt // tm),
        in_specs=[pl.BlockSpec((1, tm, nc), lambda bb, i: (bb, i, 0)),
                  pl.BlockSpec((1, HALO, nc), lambda bb, i: (bb, jnp.maximum(i * hb - 1, 0), 0)),
                  pl.BlockSpec((1, HALO, nc), lambda bb, i: (bb, jnp.minimum((i + 1) * hb, nhalo - 1), 0)),
                  _full(w.shape), _full(b.shape),
                  pl.BlockSpec((1, tm, LANES), lambda bb, i: (bb, i, 0)),
                  _full(pv.shape)],
        out_specs=(pl.BlockSpec((1, tm, nc), lambda bb, i: (bb, i, 0)),
                   pl.BlockSpec((1, tm, LANES), lambda bb, i: (bb, i, 0)),
                   pl.BlockSpec((1, tm, LANES), lambda bb, i: (bb, i, 0))),
        scratch_shapes=[pltpu.VMEM((tm + 2 * HALO, nc), F32)],
        name="conv_prep",
    )(x, x, x, w, b, small, pv)


def _chunk_cumsum_matrix(n, chunk, reverse):
    r = _iota((n, n), 0)
    c = _iota((n, n), 1)
    same = (r // chunk) == (c // chunk)
    return (same & ((c >= r) if reverse else (c <= r))).astype(BF16)


def _expand_matrix(lane0, group, width):
    return (_iota((LANES, width), 0) == lane0 + _iota((LANES, width), 1) // group).astype(BF16)


def _row_form(col_vals, chunk):
    pick = _iota(col_vals.shape, 0) == (_iota(col_vals.shape, 1) % chunk)
    ones = jnp.ones((SUBLANES, chunk), BF16)
    return _dot2l(ones, jnp.where(pick, col_vals, 0.0))[0:1]


def _heads_diag(x):
    return _block_diag(x, N_HEADS, GDN_CHUNK, HEAD_W)


def _heads_undiag(x):
    keep = (_iota(x.shape, 0) // GDN_CHUNK) == (_iota(x.shape, 1) // HEAD_W)
    x = jnp.where(keep, x, 0.0)
    return functools.reduce(jnp.add, [x[hh * GDN_CHUNK:(hh + 1) * GDN_CHUNK] for hh in range(N_HEADS)])


def _pc3(x, y):
    xh, xl = _split(x)
    yh, yl = _split(y)
    r = _dot(jnp.concatenate([xh, xl], axis=0), _heads_diag(yh))
    n = x.shape[0]
    return r[0:n] + r[n:2 * n] + _dot(xh, _heads_diag(yl))


def _gdn_chunk_kernel(y_ref, sm_ref, g_ref, qs_ref, qe_ref, o0_ref, gl_ref, *, d, n_chunks):
    ch = GDN_CHUNK
    cb = n_chunks * ch
    reverse = d == 1
    sm = sm_ref[0]
    cs = _dot2l(_chunk_cumsum_matrix(cb, ch, reverse), sm)
    gc_all = _dot2(cs, _expand_matrix(L_G + N_HEADS * d, HEAD_W, GROUP_W))
    beta_all = _dot2(sm, _expand_matrix(L_BETA + N_HEADS * d, HEAD_W, GROUP_W))
    y = y_ref[0]
    q_all, k_all, v_all = y[:, 0:256], y[:, 256:512], y[:, 512:768]
    kb_all = k_all * beta_all
    vb_all = v_all * beta_all
    eg_all = jnp.exp(gc_all)

    ti = _iota((ch, GROUP_W), 0)
    tj = _iota((ch, GROUP_W), 1) % ch
    incl = (tj >= ti) if reverse else (tj <= ti)
    strict = (tj > ti) if reverse else (tj < ti)
    last = 0 if reverse else ch - 1
    cs_ = range(n_chunks)
    rows = [slice(c * ch, (c + 1) * ch) for c in cs_]

    gc = [gc_all[r] for r in rows]
    decay = [jnp.where(incl, jnp.exp(jnp.where(incl, g - _row_form(g, ch), 0.0)), 0.0) for g in gc]
    qk = [_dot_nt(jnp.concatenate([kb_all[r], q_all[r]], axis=0).astype(BF16),
                  _heads_diag(k_all[r].astype(BF16))) for r in rows]
    a_mat = [jnp.where(strict, x[0:ch] * dc, 0.0) for x, dc in zip(qk, decay)]
    a_intra = [(x[ch:2 * ch] * dc).astype(BF16) for x, dc in zip(qk, decay)]

    base = SUBLANES
    m = [jnp.where(ti // base == tj // base, -a, 0.0) for a in a_mat]
    t = [jnp.where(ti == tj, 1.0, 0.0) + x for x in m]
    for _ in range(2):
        m = [_pc3(x, x) for x in m]
        t = [x + _pc3(x, p) for x, p in zip(t, m)]
    size = 2 * base
    while size <= ch:
        off = (ti // size == tj // size) & (ti // (size // 2) != tj // (size // 2))
        ct = [_pc3(jnp.where(off, a, 0.0), x) for a, x in zip(a_mat, t)]
        t = [x - _pc3(x, p) for x, p in zip(t, ct)]
        size *= 2

    u = [_pc3(x, vb_all[r]) for x, r in zip(t, rows)]
    w = [_pc3(x, kb_all[r] * eg_all[r]) for x, r in zip(t, rows)]
    for c in cs_:
        r = rows[c]
        g_last = gc[c][last:last + 1]
        k_dec = (k_all[r] * jnp.exp(g_last - gc[c])).astype(BF16)
        wu = jnp.concatenate([w[c], u[c]], axis=1).astype(BF16)
        full = _dot_tn(k_dec, wu)
        g_ref[0, r, :] = _heads_undiag(full[:, 0:GROUP_W]).astype(BF16)
        qs_ref[0, r, :] = _heads_undiag(full[:, GROUP_W:])
        wu_bd = jnp.concatenate([_heads_diag(wu[:, 0:GROUP_W]), _heads_diag(wu[:, GROUP_W:])], axis=1)
        aw = _dot(a_intra[c], wu_bd)
        qe_ref[0, r, :] = (q_all[r] * eg_all[r] - aw[:, 0:GROUP_W]).astype(BF16)
        o0_ref[0, r, :] = aw[:, GROUP_W:]
        gl_ref[0, c:c + 1, :] = jnp.exp(g_last)


def _gdn_state_kernel(g_ref, qs_ref, qe_ref, o0_ref, gl_ref, s0_ref, o_ref, sfin_ref, state_ref,
                      *, d, n_chunks):
    ch = GDN_CHUNK
    step = pl.program_id(1)

    @pl.when(step == 0)
    def _():
        state_ref[...] = s0_ref[0]

    order = range(n_chunks - 1, -1, -1) if d == 1 else range(n_chunks)
    state = state_ref[...]
    for c in order:
        r = slice(c * ch, (c + 1) * ch)
        both = jnp.concatenate([g_ref[0, r, :], qe_ref[0, r, :]], axis=0)
        res = _dot(both, _heads_diag(state.astype(BF16)))
        o_ref[0, r, :] = res[ch:2 * ch] + o0_ref[0, r, :]
        state = state * gl_ref[0, c:c + 1, :] - res[0:ch] + qs_ref[0, r, :]
    state_ref[...] = state

    @pl.when(step == pl.num_programs(1) - 1)
    def _():
        sfin_ref[0] = state


def _gdn_chunk_call(y, sm, *, d, cb):
    nb, nt, nc = y.shape
    n_chunks = cb // GDN_CHUNK
    tok = lambda dt: jax.ShapeDtypeStruct((nb, nt, GROUP_W), dt)
    tspec = pl.BlockSpec((1, cb, GROUP_W), lambda bb, i: (bb, i, 0))
    return pl.pallas_call(
        functools.partial(_gdn_chunk_kernel, d=d, n_chunks=n_chunks),
        out_shape=(tok(BF16), tok(F32), tok(BF16), tok(F32),
                   jax.ShapeDtypeStruct((nb, nt // GDN_CHUNK, GROUP_W), F32)),
        grid=(nb, nt // cb),
        in_specs=[pl.BlockSpec((1, cb, nc), lambda bb, i: (bb, i, 0)),
                  pl.BlockSpec((1, cb, LANES), lambda bb, i: (bb, i, 0))],
        out_specs=(tspec, tspec, tspec, tspec,
                   pl.BlockSpec((1, n_chunks, GROUP_W), lambda bb, i: (bb, i, 0))),
        name="gdn_chunk",
    )(y, sm)


def _gdn_state_call(parts, s0, *, d, cb):
    g, qs, qe, o0, gl = parts
    nb, nt, _ = g.shape
    n_chunks = cb // GDN_CHUNK
    nblk = nt // cb
    blk = (lambda bb, i: (bb, nblk - 1 - i, 0)) if d == 1 else (lambda bb, i: (bb, i, 0))
    tspec = pl.BlockSpec((1, cb, GROUP_W), blk)
    sspec = pl.BlockSpec((1, GDN_CHUNK, GROUP_W), lambda bb, i: (bb, 0, 0))
    return pl.pallas_call(
        functools.partial(_gdn_state_kernel, d=d, n_chunks=n_chunks),
        out_shape=(jax.ShapeDtypeStruct((nb, nt, GROUP_W), F32),
                   jax.ShapeDtypeStruct((nb, GDN_CHUNK, GROUP_W), F32)),
        grid=(nb, nblk),
        in_specs=[tspec, tspec, tspec, tspec, pl.BlockSpec((1, n_chunks, GROUP_W), blk), sspec],
        out_specs=(tspec, sspec),
        scratch_shapes=[pltpu.VMEM((GDN_CHUNK, GROUP_W), F32)],
        compiler_params=pltpu.CompilerParams(dimension_semantics=("arbitrary", "arbitrary")),
        name="gdn_state_scan",
    )(g, qs, qe, o0, gl, s0)


def _gdn_bidir(y_l, sm_l, y_c, sm_c, *, cb_l, cb_c):
    zero = jnp.zeros((y_l.shape[0], GDN_CHUNK, GROUP_W), F32)
    outs = []
    for d in (0, 1):
        o_c, s_c = _gdn_state_call(_gdn_chunk_call(y_c, sm_c, d=d, cb=cb_c), zero, d=d, cb=cb_c)
        o_l, _ = _gdn_state_call(_gdn_chunk_call(y_l, sm_l, d=d, cb=cb_l), s_c, d=d, cb=cb_l)
        outs.append((o_l, o_c))
    return outs


def _ssd_kernel(x_ref, dt_ref, da_ref, s0_ref, o_ref, sfin_ref, state_ref, *, d, n_chunks):
    ch = SSM_CHUNK
    cb = n_chunks * ch
    reverse = d == 1
    step = pl.program_id(1)

    @pl.when(step == 0)
    def _():
        state_ref[...] = s0_ref[0]

    lane0 = L_DT + N_HEADS * d
    cs = _dot2l(_chunk_cumsum_matrix(cb, ch, reverse), da_ref[0])
    e64 = _expand_matrix(lane0, HEAD_W, GROUP_W)
    ac_all = _dot2(cs, e64)
    ac5_all = _dot2(cs, _expand_matrix(lane0, ch, N_HEADS * ch))
    dt_all = _dot2(dt_ref[0], e64)
    xbc = x_ref[0]
    xdt_all = xbc[:, 0:256] * dt_all
    b_all, c_all = xbc[:, 256:512], xbc[:, 512:768]

    ti = _iota((ch, N_HEADS * ch), 0)
    tj = _iota((ch, N_HEADS * ch), 1) % ch
    incl = (tj >= ti) if reverse else (tj <= ti)
    grp_keep = (_iota((GROUP_W, GROUP_W), 0) // ch) == (_iota((GROUP_W, GROUP_W), 1) // ch)
    last = 0 if reverse else ch - 1

    order = range(n_chunks - 1, -1, -1) if reverse else range(n_chunks)
    for c in order:
        rows = slice(c * ch, (c + 1) * ch)
        ac = ac_all[rows]
        ac5 = ac5_all[rows]
        diff = ac5 - _row_form(ac5, ch)
        seg = jnp.where(incl, jnp.exp(jnp.where(incl, diff, 0.0)), 0.0)
        bm = b_all[rows]
        cm = c_all[rows].astype(BF16)
        cb_g = _dot_nt(cm, _block_diag(bm, 2, ch, SSM_STATE).astype(BF16))
        scores = jnp.concatenate([cb_g[:, 0:ch], cb_g[:, 0:ch], cb_g[:, ch:], cb_g[:, ch:]], axis=1) * seg
        xdt = xdt_all[rows]
        y_diag = _dot(scores.astype(BF16), _block_diag(xdt, N_HEADS, ch, HEAD_W).astype(BF16))
        a_last = ac[last:last + 1]
        states = jnp.where(grp_keep, _dot_tn(bm.astype(BF16), (xdt * jnp.exp(a_last - ac)).astype(BF16)), 0.0)
        state = state_ref[...]
        o_ref[0, rows, :] = y_diag + _dot(cm, state.astype(BF16)) * jnp.exp(ac)
        state_ref[...] = state * jnp.exp(a_last) + states

    @pl.when(step == pl.num_programs(1) - 1)
    def _():
        sfin_ref[0] = state_ref[...]


def _scan_call(kernel, name, seq, smalls, s0, *, d, cb, chunk):
    nb, nt, nc = seq.shape
    nblk = nt // cb
    blk = (lambda bb, i: (bb, nblk - 1 - i, 0)) if d == 1 else (lambda bb, i: (bb, i, 0))
    sspec = pl.BlockSpec((1, GROUP_W, GROUP_W), lambda bb, i: (bb, 0, 0))
    return pl.pallas_call(
        functools.partial(kernel, d=d, n_chunks=cb // chunk),
        out_shape=(jax.ShapeDtypeStruct((nb, nt, GROUP_W), F32),
                   jax.ShapeDtypeStruct((nb, GROUP_W, GROUP_W), F32)),
        grid=(nb, nblk),
        in_specs=[pl.BlockSpec((1, cb, nc), blk)] + [pl.BlockSpec((1, cb, LANES), blk) for _ in smalls]
                 + [sspec],
        out_specs=(pl.BlockSpec((1, cb, GROUP_W), blk), sspec),
        scratch_shapes=[pltpu.VMEM((GROUP_W, GROUP_W), F32)],
        compiler_params=pltpu.CompilerParams(dimension_semantics=("arbitrary", "arbitrary")),
        name=name,
    )(seq, *smalls, s0)


def _bidir_scan(kernel, name, seq_l, smalls_l, seq_c, smalls_c, *, cb_l, cb_c, chunk):
    zero = jnp.zeros((seq_l.shape[0], GROUP_W, GROUP_W), F32)
    outs = []
    for d in (0, 1):
        o_c, s_c = _scan_call(kernel, name, seq_c, smalls_c, zero, d=d, cb=cb_c, chunk=chunk)
        o_l, _ = _scan_call(kernel, name, seq_l, smalls_l, s_c, d=d, cb=cb_l, chunk=chunk)
        outs.append((o_l, o_c))
    return outs


def _outproj_kernel(h_ref, mod_ref, g_ref, mla_ref, nat_ref, gof_ref, gob_ref, gzg_ref,
                    yf_ref, yb_ref, sx_ref, szg_ref, pv_ref, wm_ref, wr_ref, o_ref):
    pv = pv_ref[...]
    valid = _iota((1, LANES), 1) < HEAD_W
    slabs = [jnp.where(valid, mla_ref[0, hh], 0.0) for hh in range(N_HEADS)]
    ss = functools.reduce(jnp.add, [jnp.sum(s * s, axis=-1, keepdims=True) for s in slabs])
    scale = lax.rsqrt(ss / GROUP_W + RMS_EPS)
    y = None
    for hh in range(N_HEADS):
        part = _dot((slabs[hh] * scale * pv[hh:hh + 1, 0:LANES]).astype(BF16), wm_ref[hh])
        y = part if y is None else y + part
    y = y + _dot(nat_ref[0].astype(BF16), wr_ref[0:256, :])
    grp = (_iota((GROUP_W, GROUP_W), 0) // HEAD_W == _iota((GROUP_W, GROUP_W), 1) // HEAD_W).astype(BF16)
    o = gof_ref[0] + gob_ref[0]
    on = o * lax.rsqrt(_dot2(o * o, grp) / HEAD_W + RMS_EPS) * pv[4:5]
    y = y + _dot((on * _silu(gzg_ref[0])).astype(BF16), wr_ref[256:512, :])
    s = (yf_ref[0] + yb_ref[0] + sx_ref[0] * pv[5:6]) * _silu(szg_ref[0])
    sn = jnp.concatenate([_rms(s[:, 0:LANES], pv[6:7, 0:LANES]),
                          _rms(s[:, LANES:], pv[6:7, LANES:])], axis=1)
    y = y + _dot(sn.astype(BF16), wr_ref[512:768, :])
    m = mod_ref[0]
    g = g_ref[...]
    o_ref[0] = h_ref[0] + m[5:6] * _rms(y, g[3:4])


def _out_projection(h, mod, g, mla_o, nat_o, gdn_f, gdn_b, gdn_zg, ssm_f, ssm_b, ssm_xbc, ssm_zg,
                    pv, w_mla, w_rest, *, tm):
    nb, nt, dm = h.shape
    per_batch = mod.shape[0] > 1
    tspec = lambda w: pl.BlockSpec((1, tm, w), lambda b, i: (b, i, 0))
    return pl.pallas_call(
        _outproj_kernel,
        out_shape=jax.ShapeDtypeStruct(h.shape, F32),
        grid=(nb, nt // tm),
        in_specs=[tspec(dm),
                  pl.BlockSpec((1, N_MOD, dm), lambda b, i: (b if per_batch else 0, 0, 0)),
                  _full(g.shape),
                  pl.BlockSpec((1, N_HEADS, tm, LANES), lambda b, i: (b, 0, i, 0)),
                  tspec(256), tspec(256), tspec(256), tspec(256), tspec(256), tspec(256),
                  tspec(256),
                  tspec(256),
                  _full(pv.shape), _full(w_mla.shape), _full(w_rest.shape)],
        out_specs=tspec(dm),
        name="out_projection",
    )(h, mod, g, mla_o, nat_o, gdn_f, gdn_b, gdn_zg, ssm_f, ssm_b, ssm_xbc, ssm_zg, pv, w_mla, w_rest)


_ROPE_SWAP = np.array(list(range(8, 16)) + list(range(0, 8)) + list(range(24, 32)) + list(range(16, 24)))


def _pack_layer(p, l):
    w_in = p["w_in"][l]
    dm = w_in.shape[0]
    o_nat = MLA_Q_LORA + MLA_KV_LORA + MLA_ROPE
    o_gdn = o_nat + 768
    o_ssm = o_gdn + GDN_QKV + GROUP_W + 4 * N_HEADS
    kr = w_in[:, MLA_Q_LORA + MLA_KV_LORA:o_nat]
    z64 = jnp.zeros((dm, 64), F32)
    z32 = jnp.zeros((dm, 32), F32)
    small = jnp.concatenate([w_in[:, o_gdn + 1024:o_gdn + 1040],
                             w_in[:, o_ssm + 1024:o_ssm + 1032],
                             jnp.zeros((dm, LANES - 24), F32)], axis=1)
    w_packed = jnp.concatenate([
        w_in[:, 0:384], z64, kr, z32, z64, kr[:, _ROPE_SWAP], z32,
        w_in[:, o_nat:o_gdn],
        w_in[:, o_gdn:o_gdn + 1024],
        w_in[:, o_ssm:o_ssm + 1024],
        small], axis=1).astype(BF16)
    assert w_packed.shape[1] == C_TOTAL

    wuq = p["mla_wuq"][l].reshape(MLA_Q_LORA, N_HEADS, MLA_NOPE + MLA_ROPE)
    zq = jnp.zeros((MLA_Q_LORA, N_HEADS, 32), F32)
    wq1 = jnp.concatenate([wuq, zq], axis=2).reshape(MLA_Q_LORA, N_HEADS * LANES)
    wq2 = jnp.concatenate([jnp.zeros((MLA_Q_LORA, N_HEADS, 64), F32), wuq[:, :, MLA_NOPE:][:, :, _ROPE_SWAP], zq],
                          axis=2).reshape(MLA_Q_LORA, N_HEADS * LANES)
    wukv = p["mla_wukv"][l].reshape(MLA_KV_LORA, N_HEADS, MLA_NOPE + HEAD_W)
    zk = jnp.zeros((MLA_KV_LORA, N_HEADS, 64), F32)
    wk = jnp.concatenate([wukv[:, :, :MLA_NOPE], zk], axis=2).reshape(MLA_KV_LORA, N_HEADS * LANES)
    wv = jnp.concatenate([wukv[:, :, MLA_NOPE:], zk], axis=2).reshape(MLA_KV_LORA, N_HEADS * LANES)

    def lanes(vals, lane0):
        v = vals.reshape(-1)
        return jnp.zeros((LANES,), F32).at[lane0:lane0 + v.shape[0]].set(v)

    w_out = p["w_out"][l]
    w_mla = jnp.concatenate([w_out[0:256].reshape(N_HEADS, HEAD_W, dm),
                             jnp.zeros((N_HEADS, LANES - HEAD_W, dm), F32)], axis=1).astype(BF16)
    gout = jnp.concatenate([p["mla_gout"][l].reshape(N_HEADS, HEAD_W),
                            jnp.zeros((N_HEADS, GROUP_W - HEAD_W), F32)], axis=1)
    pv_out = jnp.concatenate([gout,
                              jnp.tile(p["gdn_gnorm"][l], N_HEADS)[None],
                              jnp.repeat(p["ssm_d"][l], HEAD_W)[None],
                              p["ssm_gnorm"][l][None],
                              jnp.zeros((1, GROUP_W), F32)], axis=0)
    return dict(
        ffn1=(p["ffn1_w1"][l].astype(BF16), p["ffn1_w3"][l].astype(BF16), p["ffn1_w2"][l].astype(BF16)),
        ffn2=(p["ffn2_w1"][l].astype(BF16), p["ffn2_w3"][l].astype(BF16), p["ffn2_w2"][l].astype(BF16)),
        g=p["norm_g"][l],
        w_in=w_packed, wq1=wq1.astype(BF16), wq2=wq2.astype(BF16), wk=wk.astype(BF16), wv=wv.astype(BF16),
        mla_gq=p["mla_gq"][l][None], mla_gkv=p["mla_gkv"][l][None],
        nat_bias=p["nat_bias"][l], nat_gout=p["nat_gout"][l][None],
        gdn_conv_w=p["gdn_conv_w"][l], gdn_conv_b=jnp.zeros((1, GDN_QKV), F32),
        gdn_pv=jnp.stack([lanes(p["gdn_a_log"][l], L_G), lanes(p["gdn_dt_bias"][l], L_G)]),
        ssm_conv_w=p["ssm_conv_w"][l], ssm_conv_b=p["ssm_conv_b"][l][None],
        ssm_pv=jnp.stack([lanes(p["ssm_a_log"][l], L_DT), lanes(p["ssm_dt_bias"][l], L_DT)]),
        pv_out=pv_out, w_mla=w_mla, w_rest=w_out[256:].astype(BF16),
    )


def _rope_tables(n_tok):
    pos = jnp.arange(n_tok)
    rows = (pos // GRID_W).astype(F32)
    cols = (pos % GRID_W).astype(F32)
    quarter = MLA_ROPE // 4
    freqs = ROPE_THETA ** (-jnp.arange(quarter, dtype=F32) / quarter)
    ar = rows[:, None] * freqs
    ac = cols[:, None] * freqs
    cos = jnp.concatenate([jnp.cos(ar), jnp.cos(ar), jnp.cos(ac), jnp.cos(ac)], axis=1)
    sin = jnp.concatenate([-jnp.sin(ar), jnp.sin(ar), -jnp.sin(ac), jnp.sin(ac)], axis=1)
    ones = jnp.ones((n_tok, MLA_NOPE), F32)
    zeros = jnp.zeros((n_tok, MLA_NOPE), F32)
    pad = jnp.zeros((n_tok, LANES - MLA_NOPE - MLA_ROPE), F32)
    return jnp.concatenate([ones, cos, pad], axis=1), jnp.concatenate([zeros, sin, pad], axis=1)


def _tiles(n_tok):
    return dict(tm=min(512, n_tok), tq=min(512, n_tok), cb_gdn=min(512, n_tok), cb_ssd=min(512, n_tok))


def _mixers(hl, hc, ml, mc, lw, ropes, need_ctx):
    tl, tc = _tiles(hl.shape[1]), _tiles(hc.shape[1])
    (cos_l, sin_l), (cos_c, sin_c) = ropes
    zl = _in_projection(hl, ml, lw["g"], lw, cos_l, sin_l, tm=tl["tm"])
    zc = _in_projection(hc, mc, lw["g"], lw, cos_c, sin_c, tm=tc["tm"])
    (mq_l, mk_l, mv_l, nq_l, nk_l, nv_l, gqkv_l, gzg_l, szg_l, sxbc_l, small_l) = zl
    (mq_c, mk_c, mv_c, nq_c, nk_c, nv_c, gqkv_c, gzg_c, szg_c, sxbc_c, small_c) = zc

    n_lat = hl.shape[1]
    tk = min(1024, n_lat)
    mla_l = _mla_attention(mq_l, mk_c, mv_c, mk_l, mv_l, tq=tl["tq"], tk=tk)
    nat_l = _nat_attention(nq_l, nk_l, nv_l, nk_c, nv_c, lw["nat_bias"], lw["nat_gout"])

    gy_l, ga_l, _ = _conv_prep(gqkv_l, small_l, lw["gdn_conv_w"], lw["gdn_conv_b"], lw["gdn_pv"],
                               tm=tl["tm"], l2norm=True)
    gy_c, ga_c, _ = _conv_prep(gqkv_c, small_c, lw["gdn_conv_w"], lw["gdn_conv_b"], lw["gdn_pv"],
                               tm=tc["tm"], l2norm=True)
    (gf_l, gf_c), (gb_l, gb_c) = _gdn_bidir(gy_l, ga_l, gy_c, ga_c, cb_l=tl["cb_gdn"], cb_c=tc["cb_gdn"])

    sy_l, sa_l, sd_l = _conv_prep(sxbc_l, small_l, lw["ssm_conv_w"], lw["ssm_conv_b"], lw["ssm_pv"],
                                  tm=tl["tm"], l2norm=False)
    sy_c, sa_c, sd_c = _conv_prep(sxbc_c, small_c, lw["ssm_conv_w"], lw["ssm_conv_b"], lw["ssm_pv"],
                                  tm=tc["tm"], l2norm=False)
    (sf_l, sf_c), (sb_l, sb_c) = _bidir_scan(_ssd_kernel, "ssd_scan", sy_l, (sd_l, sa_l), sy_c, (sd_c, sa_c),
                                             cb_l=tl["cb_ssd"], cb_c=tc["cb_ssd"], chunk=SSM_CHUNK)

    hl = _out_projection(hl, ml, lw["g"], mla_l, nat_l, gf_l, gb_l, gzg_l, sf_l, sb_l, sy_l, szg_l,
                         lw["pv_out"], lw["w_mla"], lw["w_rest"], tm=tl["tm"])
    if need_ctx:
        mla_c = _mla_attention(mq_c, mk_c, mv_c, tq=tc["tq"], tk=tk)
        nat_c = _nat_ctx_attention(nq_c, nk_c, nv_c, lw["nat_gout"])
        hc = _out_projection(hc, mc, lw["g"], mla_c, nat_c, gf_c, gb_c, gzg_c, sf_c, sb_c, sy_c, szg_c,
                             lw["pv_out"], lw["w_mla"], lw["w_rest"], tm=tc["tm"])
    return hl, hc


def _layer(hl, hc, ml, mc, lw, ropes, need_ctx):
    tl, tc = _tiles(hl.shape[1]), _tiles(hc.shape[1])
    hl = _half_ffn(hl, ml, lw["g"], *lw["ffn1"], k0=0, gp=0, tm=tl["tm"])
    hc = _half_ffn(hc, mc, lw["g"], *lw["ffn1"], k0=0, gp=0, tm=tc["tm"])
    hl, hc = _mixers(hl, hc, ml, mc, lw, ropes, need_ctx)
    hl = _half_ffn(hl, ml, lw["g"], *lw["ffn2"], k0=6, gp=4, tm=tl["tm"])
    if need_ctx:
        hc = _half_ffn(hc, mc, lw["g"], *lw["ffn2"], k0=6, gp=4, tm=tc["tm"])
    return hl, hc


def kernel(x, c, ctx, c_ctx, w_mod, b_mod, norm_g, ffn1_w1, ffn1_w3, ffn1_w2, ffn2_w1, ffn2_w3, ffn2_w2,
           w_in, w_out, mla_gq, mla_gkv, mla_wuq, mla_wukv, mla_gout, nat_rpb, nat_gout, gdn_conv_w,
           gdn_a_log, gdn_dt_bias, gdn_gnorm, ssm_conv_w, ssm_conv_b, ssm_a_log, ssm_dt_bias, ssm_d,
           ssm_gnorm):
    p = dict(norm_g=norm_g, ffn1_w1=ffn1_w1, ffn1_w3=ffn1_w3, ffn1_w2=ffn1_w2, ffn2_w1=ffn2_w1,
             ffn2_w3=ffn2_w3, ffn2_w2=ffn2_w2, w_in=w_in, w_out=w_out, mla_gq=mla_gq, mla_gkv=mla_gkv,
             mla_wuq=mla_wuq, mla_wukv=mla_wukv, mla_gout=mla_gout, nat_rpb=nat_rpb, nat_gout=nat_gout,
             gdn_conv_w=gdn_conv_w, gdn_a_log=gdn_a_log, gdn_dt_bias=gdn_dt_bias, gdn_gnorm=gdn_gnorm,
             ssm_conv_w=ssm_conv_w, ssm_conv_b=ssm_conv_b, ssm_a_log=ssm_a_log, ssm_dt_bias=ssm_dt_bias,
             ssm_d=ssm_d, ssm_gnorm=ssm_gnorm)
    nb, n_lat, dm = x.shape
    n_ctx = ctx.shape[1]
    depth = w_mod.shape[0]
    cvec = jnp.concatenate([c, c_ctx[None], jnp.zeros((SUBLANES - nb - 1, dm), F32)], axis=0)
    mods = _modulation(cvec, w_mod, b_mod).reshape(depth, SUBLANES, N_MOD, dm)
    cos_c = jnp.concatenate([jnp.ones((n_ctx, MLA_NOPE + MLA_ROPE), F32),
                             jnp.zeros((n_ctx, LANES - MLA_NOPE - MLA_ROPE), F32)], axis=1)
    ropes = (_rope_tables(n_lat), (cos_c, jnp.zeros((n_ctx, LANES), F32)))
    p["nat_bias"] = _nat_bias_tables(nat_rpb, n_lat // GRID_W)
    hl, hc = x, ctx
    for l in range(depth):
        lw = _pack_layer(p, l)
        hl, hc = _layer(hl, hc, mods[l, 0:nb], mods[l, nb:nb + 1], lw, ropes, need_ctx=l < depth - 1)
    return hl
```

```python
import functools
import math

import jax
import jax.numpy as jnp
import numpy as np
from jax import lax
from jax.experimental import pallas as pl
from jax.experimental.pallas import tpu as pltpu

F32 = jnp.float32
BF16 = jnp.bfloat16

D_MODEL = 1024
DEPTH = 4
GRID_W = 64
N_MOD = 9
D_FF = 2816
RMS_EPS = 1e-6
NEG_INF = -1e30
ROPE_THETA = 10000.0
GROUP_W = 256
N_HEADS = 4
HEAD_W = 64
MLA_NOPE = 64
MLA_ROPE = 32
MLA_Q_LORA = 256
MLA_KV_LORA = 128
NAT_KR = 8
NAT_KC = 16
GDN_CHUNK = 64
SSM_STATE = 128
SSM_CHUNK = 128
CONV_K = 5
GDN_QKV = 768
SSM_XBC = 768

LANES = 128
SUBLANES = 8
VMEM_LIMIT = 56 * 1024 * 1024

C_MLA = 0
C_NAT = 640
C_GQKV = 1408
C_GZG = 2176
C_SZG = 2432
C_SXBC = 2688
C_SMALL = 3456
C_TOTAL = 3584
L_BETA, L_G, L_DT = 0, 8, 16

MLA_QSCALE = (MLA_NOPE + MLA_ROPE) ** -0.5 * math.log2(math.e)
NAT_QSCALE = HEAD_W ** -0.5


def _dot(a, b):
    return jnp.dot(a, b, preferred_element_type=F32)


def _dot_nt(a, b):
    return lax.dot_general(a, b, (((1,), (1,)), ((), ())), preferred_element_type=F32)


def _dot_tn(a, b):
    return lax.dot_general(a, b, (((0,), (0,)), ((), ())), preferred_element_type=F32)


def _split(x):
    hi = x.astype(BF16)
    lo = (x - hi.astype(F32)).astype(BF16)
    return hi, lo


def _dot2(a, m):
    hi, lo = _split(a)
    return _dot(hi, m) + _dot(lo, m)


def _dot2l(m, a):
    hi, lo = _split(a)
    return _dot(m, hi) + _dot(m, lo)


def _rms(x, g):
    return x * lax.rsqrt(jnp.mean(x * x, axis=-1, keepdims=True) + RMS_EPS) * g


def _silu(x):
    return x * jax.nn.sigmoid(x)


def _softplus(x):
    return jnp.maximum(x, 0.0) + jnp.log1p(jnp.exp(-jnp.abs(x)))


def _iota(shape, dim):
    return lax.broadcasted_iota(jnp.int32, shape, dim)


def _block_diag(x, n, blk_r, blk_c):
    t = jnp.concatenate([x] * n, axis=0)
    keep = (_iota(t.shape, 0) // blk_r) == (_iota(t.shape, 1) // blk_c)
    return jnp.where(keep, t, 0.0)


def _full(shape):
    nd = len(shape)
    return pl.BlockSpec(shape, lambda *_: (0,) * nd)


def _resident(shape):
    nd = len(shape)
    return pl.BlockSpec(shape, lambda *_: (0,) * nd, pipeline_mode=pl.Buffered(1))


def _mod_kernel(c_ref, w_ref, b_ref, o_ref):
    s = _silu(c_ref[...])
    o_ref[0] = jnp.dot(s, w_ref[0], preferred_element_type=F32,
                       precision=lax.Precision.HIGHEST) + b_ref[0]


def _modulation(cvec, w_mod, b_mod):
    nl, dm, nm = w_mod.shape
    rows = cvec.shape[0]
    tn = 1536
    return pl.pallas_call(
        _mod_kernel,
        out_shape=jax.ShapeDtypeStruct((nl, rows, nm), F32),
        grid=(nl, nm // tn),
        in_specs=[pl.BlockSpec((rows, dm), lambda l, j: (0, 0)),
                  pl.BlockSpec((1, dm, tn), lambda l, j: (l, 0, j)),
                  pl.BlockSpec((1, 1, tn), lambda l, j: (l, 0, j))],
        out_specs=pl.BlockSpec((1, rows, tn), lambda l, j: (l, 0, j)),
        name="modulation",
    )(cvec, w_mod, b_mod.reshape(nl, 1, nm))


FFN_CHUNK = 256


def _ffn_kernel(h_ref, mod_ref, g_ref, w1_ref, w3_ref, w2_ref, o_ref, *, k0, gp):
    x = h_ref[0]
    m = mod_ref[0]
    g = g_ref[...]
    u = _rms(x, g[gp:gp + 1]) * (1.0 + m[k0 + 1:k0 + 2]) + m[k0:k0 + 1]
    ub = u.astype(BF16)
    acc = None
    for c in range(D_FF // FFN_CHUNK):
        sl = slice(c * FFN_CHUNK, (c + 1) * FFN_CHUNK)
        a = _dot(ub, w1_ref[:, sl])
        b = _dot(ub, w3_ref[:, sl])
        hid = (_silu(a) * b).astype(BF16)
        part = _dot(hid, w2_ref[sl, :])
        acc = part if acc is None else acc + part
    o_ref[0] = x + 0.5 * m[k0 + 2:k0 + 3] * _rms(acc, g[gp + 1:gp + 2])


def _half_ffn(h, mod, g, w1, w3, w2, *, k0, gp, tm):
    nb, nt, dm = h.shape
    per_batch = mod.shape[0] > 1
    return pl.pallas_call(
        functools.partial(_ffn_kernel, k0=k0, gp=gp),
        out_shape=jax.ShapeDtypeStruct(h.shape, F32),
        grid=(nb, nt // tm),
        in_specs=[pl.BlockSpec((1, tm, dm), lambda b, i: (b, i, 0)),
                  pl.BlockSpec((1, N_MOD, dm), lambda b, i: (b if per_batch else 0, 0, 0)),
                  _full(g.shape), _resident(w1.shape), _resident(w3.shape), _resident(w2.shape)],
        out_specs=pl.BlockSpec((1, tm, dm), lambda b, i: (b, i, 0)),
        compiler_params=pltpu.CompilerParams(vmem_limit_bytes=VMEM_LIMIT),
        name="half_ffn",
    )(h, mod, g, w1, w3, w2)


def _inproj_kernel(h_ref, mod_ref, g_ref, w_ref, gq_ref, gkv_ref, wq1_ref, wq2_ref, wk_ref, wv_ref,
                   cos_ref, sin_ref,
                   mq_ref, mk_ref, mv_ref, nq_ref, nk_ref, nv_ref, gqkv_ref, gzg_ref, szg_ref,
                   sxbc_ref, small_ref):
    x = h_ref[0]
    m = mod_ref[0]
    g = g_ref[...]
    u = _rms(x, g[2:3]) * (1.0 + m[4:5]) + m[3:4]
    ub = u.astype(BF16)

    def proj(a, b):
        return _dot(ub, w_ref[:, a:b])

    zm = proj(C_MLA, C_NAT)
    cqn = _rms(zm[:, 0:256], gq_ref[...]).astype(BF16)
    ckvn = _rms(zm[:, 256:384], gkv_ref[...]).astype(BF16)
    cos = cos_ref[...]
    sin = sin_ref[...]
    k_rope = zm[:, 384:512] * cos + zm[:, 512:640] * sin
    q1 = _dot(cqn, wq1_ref[...])
    q2 = _dot(cqn, wq2_ref[...])
    kn = _dot(ckvn, wk_ref[...])
    vv = _dot(ckvn, wv_ref[...])
    ones_col = (_iota((1, LANES), 1) == HEAD_W).astype(F32)
    for hh in range(N_HEADS):
        sl = slice(hh * LANES, (hh + 1) * LANES)
        mq_ref[0, hh] = ((q1[:, sl] * cos + q2[:, sl] * sin) * MLA_QSCALE).astype(BF16)
        mk_ref[0, hh] = (kn[:, sl] + k_rope).astype(BF16)
        mv_ref[0, hh] = (vv[:, sl] + ones_col).astype(BF16)

    nq_ref[0] = (proj(C_NAT, C_NAT + 256) * NAT_QSCALE).astype(BF16)
    nk_ref[0] = proj(C_NAT + 256, C_NAT + 512).astype(BF16)
    nv_ref[0] = proj(C_NAT + 512, C_NAT + 768).astype(BF16)
    gqkv_ref[0] = proj(C_GQKV, C_GZG)
    gzg_ref[0] = proj(C_GZG, C_SZG)
    szg_ref[0] = proj(C_SZG, C_SXBC)
    sxbc_ref[0] = proj(C_SXBC, C_SMALL)
    small_ref[0] = proj(C_SMALL, C_TOTAL)


def _in_projection(h, mod, g, lw, cos, sin, *, tm):
    nb, nt, dm = h.shape
    per_batch = mod.shape[0] > 1
    tok = lambda w, dt: jax.ShapeDtypeStruct((nb, nt, w), dt)
    head = jax.ShapeDtypeStruct((nb, N_HEADS, nt, LANES), BF16)
    tspec = lambda w: pl.BlockSpec((1, tm, w), lambda b, i: (b, i, 0))
    hspec = pl.BlockSpec((1, N_HEADS, tm, LANES), lambda b, i: (b, 0, i, 0))
    return pl.pallas_call(
        _inproj_kernel,
        out_shape=(head, head, head, tok(256, BF16), tok(256, BF16), tok(256, BF16),
                   tok(768, F32), tok(256, F32), tok(256, F32), tok(768, F32), tok(LANES, F32)),
        grid=(nb, nt // tm),
        in_specs=[tspec(dm),
                  pl.BlockSpec((1, N_MOD, dm), lambda b, i: (b if per_batch else 0, 0, 0)),
                  _full(g.shape), _resident(lw["w_in"].shape),
                  _full(lw["mla_gq"].shape), _full(lw["mla_gkv"].shape),
                  _full(lw["wq1"].shape), _full(lw["wq2"].shape),
                  _full(lw["wk"].shape), _full(lw["wv"].shape),
                  pl.BlockSpec((tm, LANES), lambda b, i: (i, 0)),
                  pl.BlockSpec((tm, LANES), lambda b, i: (i, 0))],
        out_specs=(hspec, hspec, hspec, tspec(256), tspec(256), tspec(256),
                   tspec(768), tspec(256), tspec(256), tspec(768), tspec(LANES)),
        compiler_params=pltpu.CompilerParams(vmem_limit_bytes=VMEM_LIMIT),
        name="in_projection",
    )(h, mod, g, lw["w_in"], lw["mla_gq"], lw["mla_gkv"], lw["wq1"], lw["wq2"], lw["wk"], lw["wv"],
      cos, sin)


def _mla_kernel(*refs, n_lat_chunks, tk):
    if n_lat_chunks:
        q_ref, kl_ref, vl_ref, kc_ref, vc_ref, o_ref = refs
    else:
        q_ref, kc_ref, vc_ref, o_ref = refs
    q = q_ref[0, 0]
    tq = q.shape[0]

    def step(kb, vb, carry):
        m, acc = carry
        s = _dot_nt(q, kb)
        mn = jnp.maximum(m, jnp.max(s, axis=-1, keepdims=True))
        p = jnp.exp2(s - mn)
        acc = jnp.exp2(m - mn) * acc + _dot(p.astype(BF16), vb)
        return mn, acc

    carry = (jnp.full((tq, 1), NEG_INF, F32), jnp.zeros((tq, LANES), F32))
    if n_lat_chunks:
        def body(j, carry):
            off = pl.multiple_of(j * tk, tk)
            return step(kl_ref[0, 0, pl.ds(off, tk), :], vl_ref[0, 0, pl.ds(off, tk), :], carry)
        carry = lax.fori_loop(0, n_lat_chunks, body, carry, unroll=4)
    _, acc = step(kc_ref[0, 0], vc_ref[0, 0], carry)
    o_ref[0, 0] = acc / acc[:, HEAD_W:HEAD_W + 1]


def _mla_attention(q, k_ctx, v_ctx, k_lat=None, v_lat=None, *, tq, tk):
    nb, nh, nq, _ = q.shape
    nc = k_ctx.shape[2]
    qspec = pl.BlockSpec((1, 1, tq, LANES), lambda b, h, i: (b, h, i, 0))
    cspec = pl.BlockSpec((1, 1, nc, LANES), lambda b, h, i: (b, h, 0, 0))
    if k_lat is None:
        args, specs, n_chunks = (q, k_ctx, v_ctx), [qspec, cspec, cspec], 0
    else:
        nk = k_lat.shape[2]
        lspec = pl.BlockSpec((1, 1, nk, LANES), lambda b, h, i: (b, h, 0, 0))
        args, specs, n_chunks = (q, k_lat, v_lat, k_ctx, v_ctx), [qspec, lspec, lspec, cspec, cspec], nk // tk
    return pl.pallas_call(
        functools.partial(_mla_kernel, n_lat_chunks=n_chunks, tk=tk),
        out_shape=jax.ShapeDtypeStruct((nb, nh, nq, LANES), F32),
        grid=(nb, nh, nq // tq),
        in_specs=specs,
        out_specs=qspec,
        compiler_params=pltpu.CompilerParams(vmem_limit_bytes=VMEM_LIMIT),
        name="mla_attention",
    )(*args)


NAT_QROWS = 4
NAT_QB = NAT_QROWS * GRID_W
NAT_KBLKS = 3


def _heads_attention(q, parts, gout):
    lane_head = _iota((1, GROUP_W), 1) // HEAD_W
    out = jnp.zeros(q.shape, F32)
    for hh in range(N_HEADS):
        sel = lane_head == hh
        qh = jnp.where(sel, q, jnp.zeros_like(q))
        scores = []
        for k, _, bias in parts:
            s = _dot_nt(qh, k)
            scores.append(s if bias is None else s + bias[hh])
        m = functools.reduce(jnp.maximum, [jnp.max(s, axis=-1, keepdims=True) for s in scores])
        ps = [jnp.exp(s - m) for s in scores]
        l = functools.reduce(jnp.add, [jnp.sum(p, axis=-1, keepdims=True) for p in ps])
        o = functools.reduce(jnp.add, [_dot(p.astype(BF16), v) for p, (_, v, _) in zip(ps, parts)])
        out = jnp.where(sel, o / l, out)
    return _rms(out, gout)


def _nat_kernel(q_ref, k0_ref, k1_ref, k2_ref, v0_ref, v1_ref, v2_ref, kc_ref, vc_ref, bias_ref, g_ref,
                o_ref):
    kw = jnp.concatenate([k0_ref[0], k1_ref[0], k2_ref[0]], axis=0)
    vw = jnp.concatenate([v0_ref[0], v1_ref[0], v2_ref[0]], axis=0)
    parts = [(kw, vw, bias_ref[0]), (kc_ref[0], vc_ref[0], None)]
    o_ref[0] = _heads_attention(q_ref[0], parts, g_ref[...])


def _nat_ctx_kernel(q_ref, k_ref, v_ref, g_ref, o_ref):
    o_ref[0] = _heads_attention(q_ref[0], [(k_ref[0], v_ref[0], None)], g_ref[...])


def _nat_attention(q, k, v, kc, vc, bias, gout):
    nb, nt, _ = q.shape
    nblk = nt // NAT_QB
    nc = kc.shape[1]
    start = lambda i: jnp.clip(i - 1, 0, nblk - NAT_KBLKS)
    variant = lambda i: jnp.where(i == 0, 0, jnp.where(i == nblk - 1, 2, 1))
    qspec = pl.BlockSpec((1, NAT_QB, GROUP_W), lambda b, i: (b, i, 0))
    kspecs = [pl.BlockSpec((1, NAT_QB, GROUP_W), lambda b, i, j=j: (b, start(i) + j, 0))
              for j in range(NAT_KBLKS)]
    cspec = pl.BlockSpec((1, nc, GROUP_W), lambda b, i: (b, 0, 0))
    bspec = pl.BlockSpec((1, N_HEADS, NAT_QB, NAT_KBLKS * NAT_QB), lambda b, i: (variant(i), 0, 0, 0))
    return pl.pallas_call(
        _nat_kernel,
        out_shape=jax.ShapeDtypeStruct((nb, nt, GROUP_W), F32),
        grid=(nb, nblk),
        in_specs=[qspec] + kspecs + kspecs + [cspec, cspec, bspec, _full(gout.shape)],
        out_specs=qspec,
        name="nat_attention",
    )(q, k, k, k, v, v, v, kc, vc, bias, gout)


def _nat_ctx_attention(q, k, v, gout):
    nb, nc, _ = q.shape
    spec = pl.BlockSpec((1, nc, GROUP_W), lambda b: (b, 0, 0))
    return pl.pallas_call(
        _nat_ctx_kernel,
        out_shape=jax.ShapeDtypeStruct((nb, nc, GROUP_W), F32),
        grid=(nb,),
        in_specs=[spec, spec, spec, _full(gout.shape)],
        out_specs=spec,
        name="nat_ctx_attention",
    )(q, k, v, gout)


def _nat_bias_constants(n_rows):
    krows = NAT_KBLKS * NAT_QROWS
    qr = np.arange(NAT_QROWS)[:, None]
    kk = np.arange(krows)[None, :]
    cq = np.arange(GRID_W)[:, None]
    ck = np.arange(GRID_W)[None, :]
    c0 = np.clip(cq - NAT_KC // 2, 0, GRID_W - NAT_KC)
    col_ok = ((ck >= c0) & (ck < c0 + NAT_KC)).reshape(-1)
    dc = np.clip(ck - cq + (NAT_KC - 1), 0, 2 * NAT_KC - 2).reshape(-1)
    e_col = np.zeros((2 * NAT_KC, GRID_W * GRID_W), np.float32)
    e_col[dc, np.arange(GRID_W * GRID_W)] = 1.0
    big = 4 * n_rows + 64
    placements = [(0, 0, n_rows), (big // 2, big // 2 - NAT_QROWS, big),
                  (n_rows - NAT_QROWS, n_rows - krows, n_rows)]
    e_row = np.zeros((3, NAT_QROWS * krows, 2 * NAT_KR), np.float32)
    ok = np.zeros((3, NAT_QROWS * krows, GRID_W * GRID_W), np.float32)
    for v, (r_base, k_start, rows_total) in enumerate(placements):
        r = r_base + qr
        k_abs = k_start + kk
        r0 = np.clip(r - NAT_KR // 2, 0, rows_total - NAT_KR)
        row_ok = ((k_abs >= r0) & (k_abs < r0 + NAT_KR)).reshape(-1)
        dr = np.clip(k_abs - r + (NAT_KR - 1), 0, 2 * NAT_KR - 2).reshape(-1)
        e_row[v, np.arange(NAT_QROWS * krows), dr] = 1.0
        ok[v] = row_ok[:, None] & col_ok[None, :]
    return e_row, e_col, ok


def _split3(x):
    h1 = x.astype(BF16)
    r1 = x - h1.astype(F32)
    h2 = r1.astype(BF16)
    h3 = (r1 - h2.astype(F32)).astype(BF16)
    return h1, h2, h3


def _nat_bias_kernel(er_ref, rpb_ref, ec_ref, ok_ref, o_ref):
    er = er_ref[0]
    rows = functools.reduce(jnp.add, [_dot(er, p) for p in _split3(rpb_ref[0, 0])])
    ec = ec_ref[...]
    b = functools.reduce(jnp.add, [_dot(p, ec) for p in _split3(rows)])
    o_ref[0, 0, 0] = jnp.where(ok_ref[0] > 0.0, b, NEG_INF)


def _nat_bias_tables(rpb_all, n_rows):
    nl, nh = rpb_all.shape[:2]
    krows = NAT_KBLKS * NAT_QROWS
    e_row, e_col, ok = _nat_bias_constants(n_rows)
    rpb = jnp.pad(rpb_all, ((0, 0), (0, 0), (0, 1), (0, 1)))
    nr, ncol = NAT_QROWS * krows, GRID_W * GRID_W
    out = pl.pallas_call(
        _nat_bias_kernel,
        out_shape=jax.ShapeDtypeStruct((nl, 3, nh, nr, ncol), F32),
        grid=(nl, 3, nh),
        in_specs=[pl.BlockSpec((1, nr, 2 * NAT_KR), lambda l, v, h: (v, 0, 0)),
                  pl.BlockSpec((1, 1, 2 * NAT_KR, 2 * NAT_KC), lambda l, v, h: (l, h, 0, 0)),
                  pl.BlockSpec((2 * NAT_KC, ncol), lambda l, v, h: (0, 0)),
                  pl.BlockSpec((1, nr, ncol), lambda l, v, h: (v, 0, 0))],
        out_specs=pl.BlockSpec((1, 1, 1, nr, ncol), lambda l, v, h: (l, v, h, 0, 0)),
        name="nat_bias_tables",
    )(jnp.asarray(e_row, BF16), rpb, jnp.asarray(e_col, BF16), jnp.asarray(ok))
    out = out.reshape(nl, 3, nh, NAT_QROWS, krows, GRID_W, GRID_W).transpose(0, 1, 2, 3, 5, 4, 6)
    return out.reshape(nl, 3, nh, NAT_QB, krows * GRID_W)


HALO = SUBLANES


def _conv_kernel(x_ref, prev_ref, next_ref, w_ref, b_ref, small_ref, pv_ref, y_ref, oa_ref, ob_ref,
                 xpad_ref, *, l2norm):
    i = pl.program_id(1)
    last = pl.num_programs(1) - 1
    tm = x_ref.shape[1]
    xpad_ref[0:HALO] = jnp.where(i > 0, prev_ref[0], 0.0)
    xpad_ref[HALO:HALO + tm] = x_ref[0]
    xpad_ref[HALO + tm:2 * HALO + tm] = jnp.where(i < last, next_ref[0], 0.0)
    w = w_ref[...]
    acc = jnp.broadcast_to(b_ref[...], (tm, x_ref.shape[2]))
    for j in range(CONV_K):
        acc = acc + xpad_ref[pl.ds(HALO - CONV_K // 2 + j, tm), :] * w[j:j + 1]
    y = _silu(acc)
    if l2norm:
        grp = (_iota((GROUP_W, GROUP_W), 0) // HEAD_W == _iota((GROUP_W, GROUP_W), 1) // HEAD_W).astype(BF16)
        q = y[:, 0:256]
        k = y[:, 256:512]
        y_ref[0, :, 0:256] = q * lax.rsqrt(_dot2(q * q, grp) + RMS_EPS) * (HEAD_W ** -0.5)
        y_ref[0, :, 256:512] = k * lax.rsqrt(_dot2(k * k, grp) + RMS_EPS)
        y_ref[0, :, 512:768] = y[:, 512:768]
    else:
        y_ref[0] = y
    s = small_ref[0]
    pv = pv_ref[...]
    sp = _softplus(s + pv[1:2])
    lane = _iota((1, LANES), 1)
    oa_ref[0] = jnp.where(lane < L_G, jax.nn.sigmoid(s), -jnp.exp(pv[0:1]) * sp)
    ob_ref[0] = sp


def _conv_prep(x, small, w, b, pv, *, tm, l2norm):
    nb, nt, nc = x.shape
    hb = tm // HALO
    nhalo = nt // HALO
    return pl.pallas_call(
        functools.partial(_conv_kernel, l2norm=l2norm),
        out_shape=(jax.ShapeDtypeStruct(x.shape, F32), jax.ShapeDtypeStruct(small.shape, F32),
                   jax.ShapeDtypeStruct(small.shape, F32)),
        grid=(nb, nt // tm),
        in_specs=[pl.BlockSpec((1, tm, nc), lambda bb, i: (bb, i, 0)),
                  pl.BlockSpec((1, HALO, nc), lambda bb, i: (bb, jnp.maximum(i * hb - 1, 0), 0)),
                  pl.BlockSpec((1, HALO, nc), lambda bb, i: (bb, jnp.minimum((i + 1) * hb, nhalo - 1), 0)),
                  _full(w.shape), _full(b.shape),
                  pl.BlockSpec((1, tm, LANES), lambda bb, i: (bb, i, 0)),
                  _full(pv.shape)],
        out_specs=(pl.BlockSpec((1, tm, nc), lambda bb, i: (bb, i, 0)),
                   pl.BlockSpec((1, tm, LANES), lambda bb, i: (bb, i, 0)),
                   pl.BlockSpec((1, tm, LANES), lambda bb, i: (bb, i, 0))),
        scratch_shapes=[pltpu.VMEM((tm + 2 * HALO, nc), F32)],
        name="conv_prep",
    )(x, x, x, w, b, small, pv)


def _chunk_cumsum_matrix(n, chunk, reverse):
    r = _iota((n, n), 0)
    c = _iota((n, n), 1)
    same = (r // chunk) == (c // chunk)
    return (same & ((c >= r) if reverse else (c <= r))).astype(BF16)


def _expand_matrix(lane0, group, width):
    return (_iota((LANES, width), 0) == lane0 + _iota((LANES, width), 1) // group).astype(BF16)


def _row_form(col_vals, chunk):
    pick = _iota(col_vals.shape, 0) == (_iota(col_vals.shape, 1) % chunk)
    ones = jnp.ones((SUBLANES, chunk), BF16)
    return _dot2l(ones, jnp.where(pick, col_vals, 0.0))[0:1]


def _heads_diag(x):
    return _block_diag(x, N_HEADS, GDN_CHUNK, HEAD_W)


def _heads_undiag(x):
    keep = (_iota(x.shape, 0) // GDN_CHUNK) == (_iota(x.shape, 1) // HEAD_W)
    x = jnp.where(keep, x, 0.0)
    return functools.reduce(jnp.add, [x[hh * GDN_CHUNK:(hh + 1) * GDN_CHUNK] for hh in range(N_HEADS)])


def _pc3(x, y):
    xh, xl = _split(x)
    yh, yl = _split(y)
    r = _dot(jnp.concatenate([xh, xl], axis=0), _heads_diag(yh))
    n = x.shape[0]
    return r[0:n] + r[n:2 * n] + _dot(xh, _heads_diag(yl))


def _pc1(x, y):
    return _dot(x.astype(BF16), _heads_diag(y.astype(BF16)))


def _gdn_chunk_kernel(y_ref, sm_ref, g_ref, qs_ref, qe_ref, o0_ref, gl_ref, *, d, n_chunks):
    ch = GDN_CHUNK
    cb = n_chunks * ch
    reverse = d == 1
    sm = sm_ref[0]
    cs = _dot2l(_chunk_cumsum_matrix(cb, ch, reverse), sm)
    gc_all = _dot2(cs, _expand_matrix(L_G + N_HEADS * d, HEAD_W, GROUP_W))
    beta_all = _dot2(sm, _expand_matrix(L_BETA + N_HEADS * d, HEAD_W, GROUP_W))
    y = y_ref[0]
    q_all, k_all, v_all = y[:, 0:256], y[:, 256:512], y[:, 512:768]
    kb_all = k_all * beta_all
    vb_all = v_all * beta_all
    eg_all = jnp.exp(gc_all)

    ti = _iota((ch, GROUP_W), 0)
    tj = _iota((ch, GROUP_W), 1) % ch
    incl = (tj >= ti) if reverse else (tj <= ti)
    strict = (tj > ti) if reverse else (tj < ti)
    last = 0 if reverse else ch - 1
    cs_ = range(n_chunks)
    rows = [slice(c * ch, (c + 1) * ch) for c in cs_]

    gc = [gc_all[r] for r in rows]
    decay = [jnp.where(incl, jnp.exp(jnp.where(incl, g - _row_form(g, ch), 0.0)), 0.0) for g in gc]
    qk = [_dot_nt(jnp.concatenate([kb_all[r], q_all[r]], axis=0).astype(BF16),
                  _heads_diag(k_all[r].astype(BF16))) for r in rows]
    a_mat = [jnp.where(strict, x[0:ch] * dc, 0.0) for x, dc in zip(qk, decay)]
    a_intra = [(x[ch:2 * ch] * dc).astype(BF16) for x, dc in zip(qk, decay)]

    base = SUBLANES
    eye = jnp.where(ti == tj, 1.0, 0.0)
    m = [jnp.where(ti // base == tj // base, -a, 0.0) for a in a_mat]
    t = [eye + x for x in m]
    for _ in range(2):
        m = [_pc1(x, x) for x in m]
        t = [x + _pc1(x, p) for x, p in zip(t, m)]
    size = 2 * base
    while size <= ch:
        off = (ti // size == tj // size) & (ti // (size // 2) != tj // (size // 2))
        ct = [_pc1(jnp.where(off, a, 0.0), x) for a, x in zip(a_mat, t)]
        t = [x - _pc1(x, p) for x, p in zip(t, ct)]
        size *= 2
    resid = [eye - x - _pc3(a, x) for a, x in zip(a_mat, t)]
    t = [x + _pc1(x, r) for x, r in zip(t, resid)]

    u = [_pc3(x, vb_all[r]) for x, r in zip(t, rows)]
    w = [_pc3(x, kb_all[r] * eg_all[r]) for x, r in zip(t, rows)]
    for c in cs_:
        r = rows[c]
        g_last = gc[c][last:last + 1]
        k_dec = (k_all[r] * jnp.exp(g_last - gc[c])).astype(BF16)
        wu = jnp.concatenate([w[c], u[c]], axis=1).astype(BF16)
        full = _dot_tn(k_dec, wu)
        g_ref[0, r, :] = _heads_undiag(full[:, 0:GROUP_W]).astype(BF16)
        qs_ref[0, r, :] = _heads_undiag(full[:, GROUP_W:])
        wu_bd = jnp.concatenate([_heads_diag(wu[:, 0:GROUP_W]), _heads_diag(wu[:, GROUP_W:])], axis=1)
        aw = _dot(a_intra[c], wu_bd)
        qe_ref[0, r, :] = (q_all[r] * eg_all[r] - aw[:, 0:GROUP_W]).astype(BF16)
        o0_ref[0, r, :] = aw[:, GROUP_W:]
        gl_ref[0, c:c + 1, :] = jnp.exp(g_last)


def _gdn_state_kernel(g_ref, qs_ref, qe_ref, o0_ref, gl_ref, s0_ref, o_ref, sfin_ref, state_ref,
                      *, d, n_chunks):
    ch = GDN_CHUNK
    step = pl.program_id(1)

    @pl.when(step == 0)
    def _():
        state_ref[...] = s0_ref[0]

    order = range(n_chunks - 1, -1, -1) if d == 1 else range(n_chunks)
    state = state_ref[...]
    for c in order:
        r = slice(c * ch, (c + 1) * ch)
        both = jnp.concatenate([g_ref[0, r, :], qe_ref[0, r, :]], axis=0)
        res = _dot(both, _heads_diag(state.astype(BF16)))
        o_ref[0, r, :] = res[ch:2 * ch] + o0_ref[0, r, :]
        state = state * gl_ref[0, c:c + 1, :] - res[0:ch] + qs_ref[0, r, :]
    state_ref[...] = state

    @pl.when(step == pl.num_programs(1) - 1)
    def _():
        sfin_ref[0] = state


def _gdn_chunk_call(y, sm, *, d, cb):
    nb, nt, nc = y.shape
    n_chunks = cb // GDN_CHUNK
    tok = lambda dt: jax.ShapeDtypeStruct((nb, nt, GROUP_W), dt)
    tspec = pl.BlockSpec((1, cb, GROUP_W), lambda bb, i: (bb, i, 0))
    return pl.pallas_call(
        functools.partial(_gdn_chunk_kernel, d=d, n_chunks=n_chunks),
        out_shape=(tok(BF16), tok(F32), tok(BF16), tok(F32),
                   jax.ShapeDtypeStruct((nb, nt // GDN_CHUNK, GROUP_W), F32)),
        grid=(nb, nt // cb),
        in_specs=[pl.BlockSpec((1, cb, nc), lambda bb, i: (bb, i, 0)),
                  pl.BlockSpec((1, cb, LANES), lambda bb, i: (bb, i, 0))],
        out_specs=(tspec, tspec, tspec, tspec,
                   pl.BlockSpec((1, n_chunks, GROUP_W), lambda bb, i: (bb, i, 0))),
        name="gdn_chunk",
    )(y, sm)


def _gdn_state_call(parts, s0, *, d, cb):
    g, qs, qe, o0, gl = parts
    nb, nt, _ = g.shape
    n_chunks = cb // GDN_CHUNK
    nblk = nt // cb
    blk = (lambda bb, i: (bb, nblk - 1 - i, 0)) if d == 1 else (lambda bb, i: (bb, i, 0))
    tspec = pl.BlockSpec((1, cb, GROUP_W), blk)
    sspec = pl.BlockSpec((1, GDN_CHUNK, GROUP_W), lambda bb, i: (bb, 0, 0))
    return pl.pallas_call(
        functools.partial(_gdn_state_kernel, d=d, n_chunks=n_chunks),
        out_shape=(jax.ShapeDtypeStruct((nb, nt, GROUP_W), F32),
                   jax.ShapeDtypeStruct((nb, GDN_CHUNK, GROUP_W), F32)),
        grid=(nb, nblk),
        in_specs=[tspec, tspec, tspec, tspec, pl.BlockSpec((1, n_chunks, GROUP_W), blk), sspec],
        out_specs=(tspec, sspec),
        scratch_shapes=[pltpu.VMEM((GDN_CHUNK, GROUP_W), F32)],
        compiler_params=pltpu.CompilerParams(dimension_semantics=("arbitrary", "arbitrary")),
        name="gdn_state_scan",
    )(g, qs, qe, o0, gl, s0)


def _gdn_bidir(y_l, sm_l, y_c, sm_c, *, cb_l, cb_c):
    zero = jnp.zeros((y_l.shape[0], GDN_CHUNK, GROUP_W), F32)
    outs = []
    for d in (0, 1):
        o_c, s_c = _gdn_state_call(_gdn_chunk_call(y_c, sm_c, d=d, cb=cb_c), zero, d=d, cb=cb_c)
        o_l, _ = _gdn_state_call(_gdn_chunk_call(y_l, sm_l, d=d, cb=cb_l), s_c, d=d, cb=cb_l)
        outs.append((o_l, o_c))
    return outs


def _ssd_kernel(x_ref, dt_ref, da_ref, s0_ref, o_ref, sfin_ref, state_ref, *, d, n_chunks):
    ch = SSM_CHUNK
    cb = n_chunks * ch
    reverse = d == 1
    step = pl.program_id(1)

    @pl.when(step == 0)
    def _():
        state_ref[...] = s0_ref[0]

    lane0 = L_DT + N_HEADS * d
    cs = _dot2l(_chunk_cumsum_matrix(cb, ch, reverse), da_ref[0])
    e64 = _expand_matrix(lane0, HEAD_W, GROUP_W)
    ac_all = _dot2(cs, e64)
    ac5_all = _dot2(cs, _expand_matrix(lane0, ch, N_HEADS * ch))
    dt_all = _dot2(dt_ref[0], e64)
    xbc = x_ref[0]
    xdt_all = xbc[:, 0:256] * dt_all
    b_all, c_all = xbc[:, 256:512], xbc[:, 512:768]

    ti = _iota((ch, N_HEADS * ch), 0)
    tj = _iota((ch, N_HEADS * ch), 1) % ch
    incl = (tj >= ti) if reverse else (tj <= ti)
    grp_keep = (_iota((GROUP_W, GROUP_W), 0) // ch) == (_iota((GROUP_W, GROUP_W), 1) // ch)
    last = 0 if reverse else ch - 1

    rows = [slice(c * ch, (c + 1) * ch) for c in range(n_chunks)]
    seg = [jnp.where(incl, jnp.exp(jnp.where(incl, ac5_all[r] - _row_form(ac5_all[r], ch), 0.0)), 0.0)
           for r in rows]
    cm = [c_all[r].astype(BF16) for r in rows]
    bmb = [b_all[r].astype(BF16) for r in rows]
    cb_g = [_dot_nt(c_, _block_diag(b_, 2, ch, SSM_STATE)) for c_, b_ in zip(cm, bmb)]
    scores = [(jnp.concatenate([x[:, 0:ch], x[:, 0:ch], x[:, ch:], x[:, ch:]], axis=1) * sg).astype(BF16)
              for x, sg in zip(cb_g, seg)]
    xdtb = [xdt_all[r].astype(BF16) for r in rows]
    y_diag = [_dot(s_, _block_diag(x_, N_HEADS, ch, HEAD_W)) for s_, x_ in zip(scores, xdtb)]
    a_last = [ac_all[r][last:last + 1] for r in rows]
    states = [jnp.where(grp_keep, _dot_tn(b_, (xdt_all[r] * jnp.exp(al - ac_all[r])).astype(BF16)), 0.0)
              for b_, r, al in zip(bmb, rows, a_last)]
    state = state_ref[...]
    for c in (range(n_chunks - 1, -1, -1) if reverse else range(n_chunks)):
        o_ref[0, rows[c], :] = y_diag[c] + _dot(cm[c], state.astype(BF16)) * jnp.exp(ac_all[rows[c]])
        state = state * jnp.exp(a_last[c]) + states[c]
    state_ref[...] = state

    @pl.when(step == pl.num_programs(1) - 1)
    def _():
        sfin_ref[0] = state_ref[...]


def _scan_call(kernel, name, seq, smalls, s0, *, d, cb, chunk):
    nb, nt, nc = seq.shape
    nblk = nt // cb
    blk = (lambda bb, i: (bb, nblk - 1 - i, 0)) if d == 1 else (lambda bb, i: (bb, i, 0))
    sspec = pl.BlockSpec((1, GROUP_W, GROUP_W), lambda bb, i: (bb, 0, 0))
    return pl.pallas_call(
        functools.partial(kernel, d=d, n_chunks=cb // chunk),
        out_shape=(jax.ShapeDtypeStruct((nb, nt, GROUP_W), F32),
                   jax.ShapeDtypeStruct((nb, GROUP_W, GROUP_W), F32)),
        grid=(nb, nblk),
        in_specs=[pl.BlockSpec((1, cb, nc), blk)] + [pl.BlockSpec((1, cb, LANES), blk) for _ in smalls]
                 + [sspec],
        out_specs=(pl.BlockSpec((1, cb, GROUP_W), blk), sspec),
        scratch_shapes=[pltpu.VMEM((GROUP_W, GROUP_W), F32)],
        compiler_params=pltpu.CompilerParams(dimension_semantics=("arbitrary", "arbitrary")),
        name=name,
    )(seq, *smalls, s0)


def _bidir_scan(kernel, name, seq_l, smalls_l, seq_c, smalls_c, *, cb_l, cb_c, chunk):
    zero = jnp.zeros((seq_l.shape[0], GROUP_W, GROUP_W), F32)
    outs = []
    for d in (0, 1):
        o_c, s_c = _scan_call(kernel, name, seq_c, smalls_c, zero, d=d, cb=cb_c, chunk=chunk)
        o_l, _ = _scan_call(kernel, name, seq_l, smalls_l, s_c, d=d, cb=cb_l, chunk=chunk)
        outs.append((o_l, o_c))
    return outs


def _outproj_kernel(h_ref, mod_ref, g_ref, mla_ref, nat_ref, gof_ref, gob_ref, gzg_ref,
                    yf_ref, yb_ref, sx_ref, szg_ref, pv_ref, wm_ref, wr_ref, o_ref):
    pv = pv_ref[...]
    valid = _iota((1, LANES), 1) < HEAD_W
    slabs = [jnp.where(valid, mla_ref[0, hh], 0.0) for hh in range(N_HEADS)]
    ss = functools.reduce(jnp.add, [jnp.sum(s * s, axis=-1, keepdims=True) for s in slabs])
    scale = lax.rsqrt(ss / GROUP_W + RMS_EPS)
    y = None
    for hh in range(N_HEADS):
        part = _dot((slabs[hh] * scale * pv[hh:hh + 1, 0:LANES]).astype(BF16), wm_ref[hh])
        y = part if y is None else y + part
    y = y + _dot(nat_ref[0].astype(BF16), wr_ref[0:256, :])
    grp = (_iota((GROUP_W, GROUP_W), 0) // HEAD_W == _iota((GROUP_W, GROUP_W), 1) // HEAD_W).astype(BF16)
    o = gof_ref[0] + gob_ref[0]
    on = o * lax.rsqrt(_dot2(o * o, grp) / HEAD_W + RMS_EPS) * pv[4:5]
    y = y + _dot((on * _silu(gzg_ref[0])).astype(BF16), wr_ref[256:512, :])
    s = (yf_ref[0] + yb_ref[0] + sx_ref[0] * pv[5:6]) * _silu(szg_ref[0])
    sn = jnp.concatenate([_rms(s[:, 0:LANES], pv[6:7, 0:LANES]),
                          _rms(s[:, LANES:], pv[6:7, LANES:])], axis=1)
    y = y + _dot(sn.astype(BF16), wr_ref[512:768, :])
    m = mod_ref[0]
    g = g_ref[...]
    o_ref[0] = h_ref[0] + m[5:6] * _rms(y, g[3:4])


def _out_projection(h, mod, g, mla_o, nat_o, gdn_f, gdn_b, gdn_zg, ssm_f, ssm_b, ssm_xbc, ssm_zg,
                    pv, w_mla, w_rest, *, tm):
    nb, nt, dm = h.shape
    per_batch = mod.shape[0] > 1
    tspec = lambda w: pl.BlockSpec((1, tm, w), lambda b, i: (b, i, 0))
    return pl.pallas_call(
        _outproj_kernel,
        out_shape=jax.ShapeDtypeStruct(h.shape, F32),
        grid=(nb, nt // tm),
        in_specs=[tspec(dm),
                  pl.BlockSpec((1, N_MOD, dm), lambda b, i: (b if per_batch else 0, 0, 0)),
                  _full(g.shape),
                  pl.BlockSpec((1, N_HEADS, tm, LANES), lambda b, i: (b, 0, i, 0)),
                  tspec(256), tspec(256), tspec(256), tspec(256), tspec(256), tspec(256),
                  tspec(256),
                  tspec(256),
                  _full(pv.shape), _full(w_mla.shape), _full(w_rest.shape)],
        out_specs=tspec(dm),
        name="out_projection",
    )(h, mod, g, mla_o, nat_o, gdn_f, gdn_b, gdn_zg, ssm_f, ssm_b, ssm_xbc, ssm_zg, pv, w_mla, w_rest)


_ROPE_SWAP = np.array(list(range(8, 16)) + list(range(0, 8)) + list(range(24, 32)) + list(range(16, 24)))


def _pack_layer(p, l):
    w_in = p["w_in"][l]
    dm = w_in.shape[0]
    o_nat = MLA_Q_LORA + MLA_KV_LORA + MLA_ROPE
    o_gdn = o_nat + 768
    o_ssm = o_gdn + GDN_QKV + GROUP_W + 4 * N_HEADS
    kr = w_in[:, MLA_Q_LORA + MLA_KV_LORA:o_nat]
    z64 = jnp.zeros((dm, 64), F32)
    z32 = jnp.zeros((dm, 32), F32)
    small = jnp.concatenate([w_in[:, o_gdn + 1024:o_gdn + 1040],
                             w_in[:, o_ssm + 1024:o_ssm + 1032],
                             jnp.zeros((dm, LANES - 24), F32)], axis=1)
    w_packed = jnp.concatenate([
        w_in[:, 0:384], z64, kr, z32, z64, kr[:, _ROPE_SWAP], z32,
        w_in[:, o_nat:o_gdn],
        w_in[:, o_gdn:o_gdn + 1024],
        w_in[:, o_ssm:o_ssm + 1024],
        small], axis=1).astype(BF16)
    assert w_packed.shape[1] == C_TOTAL

    wuq = p["mla_wuq"][l].reshape(MLA_Q_LORA, N_HEADS, MLA_NOPE + MLA_ROPE)
    zq = jnp.zeros((MLA_Q_LORA, N_HEADS, 32), F32)
    wq1 = jnp.concatenate([wuq, zq], axis=2).reshape(MLA_Q_LORA, N_HEADS * LANES)
    wq2 = jnp.concatenate([jnp.zeros((MLA_Q_LORA, N_HEADS, 64), F32), wuq[:, :, MLA_NOPE:][:, :, _ROPE_SWAP], zq],
                          axis=2).reshape(MLA_Q_LORA, N_HEADS * LANES)
    wukv = p["mla_wukv"][l].reshape(MLA_KV_LORA, N_HEADS, MLA_NOPE + HEAD_W)
    zk = jnp.zeros((MLA_KV_LORA, N_HEADS, 64), F32)
    wk = jnp.concatenate([wukv[:, :, :MLA_NOPE], zk], axis=2).reshape(MLA_KV_LORA, N_HEADS * LANES)
    wv = jnp.concatenate([wukv[:, :, MLA_NOPE:], zk], axis=2).reshape(MLA_KV_LORA, N_HEADS * LANES)

    def lanes(vals, lane0):
        v = vals.reshape(-1)
        return jnp.zeros((LANES,), F32).at[lane0:lane0 + v.shape[0]].set(v)

    w_out = p["w_out"][l]
    w_mla = jnp.concatenate([w_out[0:256].reshape(N_HEADS, HEAD_W, dm),
                             jnp.zeros((N_HEADS, LANES - HEAD_W, dm), F32)], axis=1).astype(BF16)
    gout = jnp.concatenate([p["mla_gout"][l].reshape(N_HEADS, HEAD_W),
                            jnp.zeros((N_HEADS, GROUP_W - HEAD_W), F32)], axis=1)
    pv_out = jnp.concatenate([gout,
                              jnp.tile(p["gdn_gnorm"][l], N_HEADS)[None],
                              jnp.repeat(p["ssm_d"][l], HEAD_W)[None],
                              p["ssm_gnorm"][l][None],
                              jnp.zeros((1, GROUP_W), F32)], axis=0)
    return dict(
        ffn1=(p["ffn1_w1"][l].astype(BF16), p["ffn1_w3"][l].astype(BF16), p["ffn1_w2"][l].astype(BF16)),
        ffn2=(p["ffn2_w1"][l].astype(BF16), p["ffn2_w3"][l].astype(BF16), p["ffn2_w2"][l].astype(BF16)),
        g=p["norm_g"][l],
        w_in=w_packed, wq1=wq1.astype(BF16), wq2=wq2.astype(BF16), wk=wk.astype(BF16), wv=wv.astype(BF16),
        mla_gq=p["mla_gq"][l][None], mla_gkv=p["mla_gkv"][l][None],
        nat_bias=p["nat_bias"][l], nat_gout=p["nat_gout"][l][None],
        gdn_conv_w=p["gdn_conv_w"][l], gdn_conv_b=jnp.zeros((1, GDN_QKV), F32),
        gdn_pv=jnp.stack([lanes(p["gdn_a_log"][l], L_G), lanes(p["gdn_dt_bias"][l], L_G)]),
        ssm_conv_w=p["ssm_conv_w"][l], ssm_conv_b=p["ssm_conv_b"][l][None],
        ssm_pv=jnp.stack([lanes(p["ssm_a_log"][l], L_DT), lanes(p["ssm_dt_bias"][l], L_DT)]),
        pv_out=pv_out, w_mla=w_mla, w_rest=w_out[256:].astype(BF16),
    )


def _rope_tables(n_tok):
    pos = jnp.arange(n_tok)
    rows = (pos // GRID_W).astype(F32)
    cols = (pos % GRID_W).astype(F32)
    quarter = MLA_ROPE // 4
    freqs = ROPE_THETA ** (-jnp.arange(quarter, dtype=F32) / quarter)
    ar = rows[:, None] * freqs
    ac = cols[:, None] * freqs
    cos = jnp.concatenate([jnp.cos(ar), jnp.cos(ar), jnp.cos(ac), jnp.cos(ac)], axis=1)
    sin = jnp.concatenate([-jnp.sin(ar), jnp.sin(ar), -jnp.sin(ac), jnp.sin(ac)], axis=1)
    ones = jnp.ones((n_tok, MLA_NOPE), F32)
    zeros = jnp.zeros((n_tok, MLA_NOPE), F32)
    pad = jnp.zeros((n_tok, LANES - MLA_NOPE - MLA_ROPE), F32)
    return jnp.concatenate([ones, cos, pad], axis=1), jnp.concatenate([zeros, sin, pad], axis=1)


def _tiles(n_tok):
    return dict(tm=min(512, n_tok), tq=min(1024, n_tok), cb_gdn=min(512, n_tok), cb_ssd=min(512, n_tok))


def _mixers(hl, hc, ml, mc, lw, ropes, need_ctx):
    tl, tc = _tiles(hl.shape[1]), _tiles(hc.shape[1])
    (cos_l, sin_l), (cos_c, sin_c) = ropes
    zl = _in_projection(hl, ml, lw["g"], lw, cos_l, sin_l, tm=tl["tm"])
    zc = _in_projection(hc, mc, lw["g"], lw, cos_c, sin_c, tm=tc["tm"])
    (mq_l, mk_l, mv_l, nq_l, nk_l, nv_l, gqkv_l, gzg_l, szg_l, sxbc_l, small_l) = zl
    (mq_c, mk_c, mv_c, nq_c, nk_c, nv_c, gqkv_c, gzg_c, szg_c, sxbc_c, small_c) = zc

    n_lat = hl.shape[1]
    tk = min(1024, n_lat)
    mla_l = _mla_attention(mq_l, mk_c, mv_c, mk_l, mv_l, tq=tl["tq"], tk=tk)
    nat_l = _nat_attention(nq_l, nk_l, nv_l, nk_c, nv_c, lw["nat_bias"], lw["nat_gout"])

    gy_l, ga_l, _ = _conv_prep(gqkv_l, small_l, lw["gdn_conv_w"], lw["gdn_conv_b"], lw["gdn_pv"],
                               tm=tl["tm"], l2norm=True)
    gy_c, ga_c, _ = _conv_prep(gqkv_c, small_c, lw["gdn_conv_w"], lw["gdn_conv_b"], lw["gdn_pv"],
                               tm=tc["tm"], l2norm=True)
    (gf_l, gf_c), (gb_l, gb_c) = _gdn_bidir(gy_l, ga_l, gy_c, ga_c, cb_l=tl["cb_gdn"], cb_c=tc["cb_gdn"])

    sy_l, sa_l, sd_l = _conv_prep(sxbc_l, small_l, lw["ssm_conv_w"], lw["ssm_conv_b"], lw["ssm_pv"],
                                  tm=tl["tm"], l2norm=False)
    sy_c, sa_c, sd_c = _conv_prep(sxbc_c, small_c, lw["ssm_conv_w"], lw["ssm_conv_b"], lw["ssm_pv"],
                                  tm=tc["tm"], l2norm=False)
    (sf_l, sf_c), (sb_l, sb_c) = _bidir_scan(_ssd_kernel, "ssd_scan", sy_l, (sd_l, sa_l), sy_c, (sd_c, sa_c),
                                             cb_l=tl["cb_ssd"], cb_c=tc["cb_ssd"], chunk=SSM_CHUNK)

    hl = _out_projection(hl, ml, lw["g"], mla_l, nat_l, gf_l, gb_l, gzg_l, sf_l, sb_l, sy_l, szg_l,
                         lw["pv_out"], lw["w_mla"], lw["w_rest"], tm=tl["tm"])
    if need_ctx:
        mla_c = _mla_attention(mq_c, mk_c, mv_c, tq=tc["tq"], tk=tk)
        nat_c = _nat_ctx_attention(nq_c, nk_c, nv_c, lw["nat_gout"])
        hc = _out_projection(hc, mc, lw["g"], mla_c, nat_c, gf_c, gb_c, gzg_c, sf_c, sb_c, sy_c, szg_c,
                             lw["pv_out"], lw["w_mla"], lw["w_rest"], tm=tc["tm"])
    return hl, hc


def _layer(hl, hc, ml, mc, lw, ropes, need_ctx):
    tl, tc = _tiles(hl.shape[1]), _tiles(hc.shape[1])
    hl = _half_ffn(hl, ml, lw["g"], *lw["ffn1"], k0=0, gp=0, tm=tl["tm"])
    hc = _half_ffn(hc, mc, lw["g"], *lw["ffn1"], k0=0, gp=0, tm=tc["tm"])
    hl, hc = _mixers(hl, hc, ml, mc, lw, ropes, need_ctx)
    hl = _half_ffn(hl, ml, lw["g"], *lw["ffn2"], k0=6, gp=4, tm=tl["tm"])
    if need_ctx:
        hc = _half_ffn(hc, mc, lw["g"], *lw["ffn2"], k0=6, gp=4, tm=tc["tm"])
    return hl, hc


def kernel(x, c, ctx, c_ctx, w_mod, b_mod, norm_g, ffn1_w1, ffn1_w3, ffn1_w2, ffn2_w1, ffn2_w3, ffn2_w2,
           w_in, w_out, mla_gq, mla_gkv, mla_wuq, mla_wukv, mla_gout, nat_rpb, nat_gout, gdn_conv_w,
           gdn_a_log, gdn_dt_bias, gdn_gnorm, ssm_conv_w, ssm_conv_b, ssm_a_log, ssm_dt_bias, ssm_d,
           ssm_gnorm):
    p = dict(norm_g=norm_g, ffn1_w1=ffn1_w1, ffn1_w3=ffn1_w3, ffn1_w2=ffn1_w2, ffn2_w1=ffn2_w1,
             ffn2_w3=ffn2_w3, ffn2_w2=ffn2_w2, w_in=w_in, w_out=w_out, mla_gq=mla_gq, mla_gkv=mla_gkv,
             mla_wuq=mla_wuq, mla_wukv=mla_wukv, mla_gout=mla_gout, nat_rpb=nat_rpb, nat_gout=nat_gout,
             gdn_conv_w=gdn_conv_w, gdn_a_log=gdn_a_log, gdn_dt_bias=gdn_dt_bias, gdn_gnorm=gdn_gnorm,
             ssm_conv_w=ssm_conv_w, ssm_conv_b=ssm_conv_b, ssm_a_log=ssm_a_log, ssm_dt_bias=ssm_dt_bias,
             ssm_d=ssm_d, ssm_gnorm=ssm_gnorm)
    nb, n_lat, dm = x.shape
    n_ctx = ctx.shape[1]
    depth = w_mod.shape[0]
    cvec = jnp.concatenate([c, c_ctx[None], jnp.zeros((SUBLANES - nb - 1, dm), F32)], axis=0)
    mods = _modulation(cvec, w_mod, b_mod).reshape(depth, SUBLANES, N_MOD, dm)
    cos_c = jnp.concatenate([jnp.ones((n_ctx, MLA_NOPE + MLA_ROPE), F32),
                             jnp.zeros((n_ctx, LANES - MLA_NOPE - MLA_ROPE), F32)], axis=1)
    ropes = (_rope_tables(n_lat), (cos_c, jnp.zeros((n_ctx, LANES), F32)))
    p["nat_bias"] = _nat_bias_tables(nat_rpb, n_lat // GRID_W)
    hl, hc = x, ctx
    for l in range(depth):
        lw = _pack_layer(p, l)
        hl, hc = _layer(hl, hc, mods[l, 0:nb], mods[l, nb:nb + 1], lw, ropes, need_ctx=l < depth - 1)
    return hl
```

```python
import functools
import math

import jax
import jax.numpy as jnp
import numpy as np
from jax import lax
from jax.experimental import pallas as pl
from jax.experimental.pallas import tpu as pltpu

F32 = jnp.float32
BF16 = jnp.bfloat16

D_MODEL = 1024
DEPTH = 4
GRID_W = 64
N_MOD = 9
D_FF = 2816
RMS_EPS = 1e-6
NEG_INF = -1e30
ROPE_THETA = 10000.0
GROUP_W = 256
N_HEADS = 4
HEAD_W = 64
MLA_NOPE = 64
MLA_ROPE = 32
MLA_Q_LORA = 256
MLA_KV_LORA = 128
NAT_KR = 8
NAT_KC = 16
GDN_CHUNK = 64
SSM_STATE = 128
SSM_CHUNK = 128
CONV_K = 5
GDN_QKV = 768
SSM_XBC = 768

LANES = 128
SUBLANES = 8
VMEM_LIMIT = 56 * 1024 * 1024

C_MLA = 0
C_NAT = 640
C_GQKV = 1408
C_GZG = 2176
C_SZG = 2432
C_SXBC = 2688
C_SMALL = 3456
C_TOTAL = 3584
L_BETA, L_G, L_DT = 0, 8, 16

MLA_QSCALE = (MLA_NOPE + MLA_ROPE) ** -0.5 * math.log2(math.e)
NAT_QSCALE = HEAD_W ** -0.5


def _dot(a, b):
    return jnp.dot(a, b, preferred_element_type=F32)


def _dot_nt(a, b):
    return lax.dot_general(a, b, (((1,), (1,)), ((), ())), preferred_element_type=F32)


def _dot_tn(a, b):
    return lax.dot_general(a, b, (((0,), (0,)), ((), ())), preferred_element_type=F32)


def _split(x):
    hi = x.astype(BF16)
    lo = (x - hi.astype(F32)).astype(BF16)
    return hi, lo


def _dot2(a, m):
    hi, lo = _split(a)
    return _dot(hi, m) + _dot(lo, m)


def _dot2l(m, a):
    hi, lo = _split(a)
    return _dot(m, hi) + _dot(m, lo)


def _rms(x, g):
    return x * lax.rsqrt(jnp.mean(x * x, axis=-1, keepdims=True) + RMS_EPS) * g


def _silu(x):
    return x * jax.nn.sigmoid(x)


def _softplus(x):
    return jnp.maximum(x, 0.0) + jnp.log1p(jnp.exp(-jnp.abs(x)))


def _iota(shape, dim):
    return lax.broadcasted_iota(jnp.int32, shape, dim)


def _block_diag(x, n, blk_r, blk_c):
    t = jnp.concatenate([x] * n, axis=0)
    keep = (_iota(t.shape, 0) // blk_r) == (_iota(t.shape, 1) // blk_c)
    return jnp.where(keep, t, 0.0)


def _full(shape):
    nd = len(shape)
    return pl.BlockSpec(shape, lambda *_: (0,) * nd)


def _resident(shape):
    nd = len(shape)
    return pl.BlockSpec(shape, lambda *_: (0,) * nd, pipeline_mode=pl.Buffered(1))


def _mod_kernel(c_ref, w_ref, b_ref, o_ref):
    s = _silu(c_ref[...])
    o_ref[0] = jnp.dot(s, w_ref[0], preferred_element_type=F32,
                       precision=lax.Precision.HIGHEST) + b_ref[0]


def _modulation(cvec, w_mod, b_mod):
    nl, dm, nm = w_mod.shape
    rows = cvec.shape[0]
    tn = 1536
    return pl.pallas_call(
        _mod_kernel,
        out_shape=jax.ShapeDtypeStruct((nl, rows, nm), F32),
        grid=(nl, nm // tn),
        in_specs=[pl.BlockSpec((rows, dm), lambda l, j: (0, 0)),
                  pl.BlockSpec((1, dm, tn), lambda l, j: (l, 0, j)),
                  pl.BlockSpec((1, 1, tn), lambda l, j: (l, 0, j))],
        out_specs=pl.BlockSpec((1, rows, tn), lambda l, j: (l, 0, j)),
        name="modulation",
    )(cvec, w_mod, b_mod.reshape(nl, 1, nm))


FFN_CHUNK = 256


def _ffn_kernel(h_ref, mod_ref, g_ref, w1_ref, w3_ref, w2_ref, o_ref, *, k0, gp):
    x = h_ref[0]
    m = mod_ref[0]
    g = g_ref[...]
    u = _rms(x, g[gp:gp + 1]) * (1.0 + m[k0 + 1:k0 + 2]) + m[k0:k0 + 1]
    ub = u.astype(BF16)
    acc = None
    for c in range(D_FF // FFN_CHUNK):
        sl = slice(c * FFN_CHUNK, (c + 1) * FFN_CHUNK)
        a = _dot(ub, w1_ref[:, sl])
        b = _dot(ub, w3_ref[:, sl])
        hid = (_silu(a) * b).astype(BF16)
        part = _dot(hid, w2_ref[sl, :])
        acc = part if acc is None else acc + part
    o_ref[0] = x + 0.5 * m[k0 + 2:k0 + 3] * _rms(acc, g[gp + 1:gp + 2])


def _half_ffn(h, mod, g, w1, w3, w2, *, k0, gp, tm):
    nb, nt, dm = h.shape
    per_batch = mod.shape[0] > 1
    return pl.pallas_call(
        functools.partial(_ffn_kernel, k0=k0, gp=gp),
        out_shape=jax.ShapeDtypeStruct(h.shape, F32),
        grid=(nb, nt // tm),
        in_specs=[pl.BlockSpec((1, tm, dm), lambda b, i: (b, i, 0)),
                  pl.BlockSpec((1, N_MOD, dm), lambda b, i: (b if per_batch else 0, 0, 0)),
                  _full(g.shape), _resident(w1.shape), _resident(w3.shape), _resident(w2.shape)],
        out_specs=pl.BlockSpec((1, tm, dm), lambda b, i: (b, i, 0)),
        compiler_params=pltpu.CompilerParams(vmem_limit_bytes=VMEM_LIMIT),
        name="half_ffn",
    )(h, mod, g, w1, w3, w2)


def _inproj_kernel(h_ref, mod_ref, g_ref, w_ref, gq_ref, gkv_ref, wq1_ref, wq2_ref, wk_ref, wv_ref,
                   cos_ref, sin_ref,
                   mq_ref, mk_ref, mv_ref, nq_ref, nk_ref, nv_ref, gqkv_ref, gzg_ref, szg_ref,
                   sxbc_ref, small_ref):
    x = h_ref[0]
    m = mod_ref[0]
    g = g_ref[...]
    u = _rms(x, g[2:3]) * (1.0 + m[4:5]) + m[3:4]
    ub = u.astype(BF16)

    def proj(a, b):
        return _dot(ub, w_ref[:, a:b])

    zm = proj(C_MLA, C_NAT)
    cqn = _rms(zm[:, 0:256], gq_ref[...]).astype(BF16)
    ckvn = _rms(zm[:, 256:384], gkv_ref[...]).astype(BF16)
    cos = cos_ref[...]
    sin = sin_ref[...]
    k_rope = zm[:, 384:512] * cos + zm[:, 512:640] * sin
    q1 = _dot(cqn, wq1_ref[...])
    q2 = _dot(cqn, wq2_ref[...])
    kn = _dot(ckvn, wk_ref[...])
    vv = _dot(ckvn, wv_ref[...])
    ones_col = (_iota((1, LANES), 1) == HEAD_W).astype(F32)
    for hh in range(N_HEADS):
        sl = slice(hh * LANES, (hh + 1) * LANES)
        mq_ref[0, hh] = ((q1[:, sl] * cos + q2[:, sl] * sin) * MLA_QSCALE).astype(BF16)
        mk_ref[0, hh] = (kn[:, sl] + k_rope).astype(BF16)
        mv_ref[0, hh] = (vv[:, sl] + ones_col).astype(BF16)

    nq_ref[0] = (proj(C_NAT, C_NAT + 256) * NAT_QSCALE).astype(BF16)
    nk_ref[0] = proj(C_NAT + 256, C_NAT + 512).astype(BF16)
    nv_ref[0] = proj(C_NAT + 512, C_NAT + 768).astype(BF16)
    gqkv_ref[0] = proj(C_GQKV, C_GZG)
    gzg_ref[0] = proj(C_GZG, C_SZG)
    szg_ref[0] = proj(C_SZG, C_SXBC)
    sxbc_ref[0] = proj(C_SXBC, C_SMALL)
    small_ref[0] = proj(C_SMALL, C_TOTAL)


def _in_projection(h, mod, g, lw, cos, sin, *, tm):
    nb, nt, dm = h.shape
    per_batch = mod.shape[0] > 1
    tok = lambda w, dt: jax.ShapeDtypeStruct((nb, nt, w), dt)
    head = jax.ShapeDtypeStruct((nb, N_HEADS, nt, LANES), BF16)
    tspec = lambda w: pl.BlockSpec((1, tm, w), lambda b, i: (b, i, 0))
    hspec = pl.BlockSpec((1, N_HEADS, tm, LANES), lambda b, i: (b, 0, i, 0))
    return pl.pallas_call(
        _inproj_kernel,
        out_shape=(head, head, head, tok(256, BF16), tok(256, BF16), tok(256, BF16),
                   tok(768, F32), tok(256, F32), tok(256, F32), tok(768, F32), tok(LANES, F32)),
        grid=(nb, nt // tm),
        in_specs=[tspec(dm),
                  pl.BlockSpec((1, N_MOD, dm), lambda b, i: (b if per_batch else 0, 0, 0)),
                  _full(g.shape), _resident(lw["w_in"].shape),
                  _full(lw["mla_gq"].shape), _full(lw["mla_gkv"].shape),
                  _full(lw["wq1"].shape), _full(lw["wq2"].shape),
                  _full(lw["wk"].shape), _full(lw["wv"].shape),
                  pl.BlockSpec((tm, LANES), lambda b, i: (i, 0)),
                  pl.BlockSpec((tm, LANES), lambda b, i: (i, 0))],
        out_specs=(hspec, hspec, hspec, tspec(256), tspec(256), tspec(256),
                   tspec(768), tspec(256), tspec(256), tspec(768), tspec(LANES)),
        compiler_params=pltpu.CompilerParams(vmem_limit_bytes=VMEM_LIMIT),
        name="in_projection",
    )(h, mod, g, lw["w_in"], lw["mla_gq"], lw["mla_gkv"], lw["wq1"], lw["wq2"], lw["wk"], lw["wv"],
      cos, sin)


def _mla_kernel(*refs, n_lat_chunks, tk):
    if n_lat_chunks:
        q_ref, kl_ref, vl_ref, kc_ref, vc_ref, o_ref = refs
    else:
        q_ref, kc_ref, vc_ref, o_ref = refs
    q = q_ref[0, 0]
    tq = q.shape[0]

    def step(kb, vb, carry):
        m, acc = carry
        s = _dot_nt(q, kb)
        mn = jnp.maximum(m, jnp.max(s, axis=-1, keepdims=True))
        p = jnp.exp2(s - mn)
        acc = jnp.exp2(m - mn) * acc + _dot(p.astype(BF16), vb)
        return mn, acc

    carry = (jnp.full((tq, 1), NEG_INF, F32), jnp.zeros((tq, LANES), F32))
    if n_lat_chunks:
        def body(j, carry):
            off = pl.multiple_of(j * tk, tk)
            return step(kl_ref[0, 0, pl.ds(off, tk), :], vl_ref[0, 0, pl.ds(off, tk), :], carry)
        carry = lax.fori_loop(0, n_lat_chunks, body, carry, unroll=8)
    _, acc = step(kc_ref[0, 0], vc_ref[0, 0], carry)
    o_ref[0, 0] = acc / acc[:, HEAD_W:HEAD_W + 1]


def _mla_attention(q, k_ctx, v_ctx, k_lat=None, v_lat=None, *, tq, tk):
    nb, nh, nq, _ = q.shape
    nc = k_ctx.shape[2]
    qspec = pl.BlockSpec((1, 1, tq, LANES), lambda b, h, i: (b, h, i, 0))
    cspec = pl.BlockSpec((1, 1, nc, LANES), lambda b, h, i: (b, h, 0, 0))
    if k_lat is None:
        args, specs, n_chunks = (q, k_ctx, v_ctx), [qspec, cspec, cspec], 0
    else:
        nk = k_lat.shape[2]
        lspec = pl.BlockSpec((1, 1, nk, LANES), lambda b, h, i: (b, h, 0, 0))
        args, specs, n_chunks = (q, k_lat, v_lat, k_ctx, v_ctx), [qspec, lspec, lspec, cspec, cspec], nk // tk
    return pl.pallas_call(
        functools.partial(_mla_kernel, n_lat_chunks=n_chunks, tk=tk),
        out_shape=jax.ShapeDtypeStruct((nb, nh, nq, LANES), F32),
        grid=(nb, nh, nq // tq),
        in_specs=specs,
        out_specs=qspec,
        compiler_params=pltpu.CompilerParams(vmem_limit_bytes=VMEM_LIMIT),
        name="mla_attention",
    )(*args)


NAT_QROWS = 4
NAT_QB = NAT_QROWS * GRID_W
NAT_KBLKS = 3


def _heads_attention(q, parts, gout):
    lane_head = _iota((1, GROUP_W), 1) // HEAD_W
    out = jnp.zeros(q.shape, F32)
    for hh in range(N_HEADS):
        sel = lane_head == hh
        qh = jnp.where(sel, q, jnp.zeros_like(q))
        scores = []
        for k, _, bias in parts:
            s = _dot_nt(qh, k)
            scores.append(s if bias is None else s + bias[hh])
        m = functools.reduce(jnp.maximum, [jnp.max(s, axis=-1, keepdims=True) for s in scores])
        ps = [jnp.exp(s - m) for s in scores]
        l = functools.reduce(jnp.add, [jnp.sum(p, axis=-1, keepdims=True) for p in ps])
        o = functools.reduce(jnp.add, [_dot(p.astype(BF16), v) for p, (_, v, _) in zip(ps, parts)])
        out = jnp.where(sel, o / l, out)
    return _rms(out, gout)


def _nat_kernel(q_ref, k0_ref, k1_ref, k2_ref, v0_ref, v1_ref, v2_ref, kc_ref, vc_ref, bias_ref, g_ref,
                o_ref):
    kw = jnp.concatenate([k0_ref[0], k1_ref[0], k2_ref[0]], axis=0)
    vw = jnp.concatenate([v0_ref[0], v1_ref[0], v2_ref[0]], axis=0)
    parts = [(kw, vw, bias_ref[0]), (kc_ref[0], vc_ref[0], None)]
    o_ref[0] = _heads_attention(q_ref[0], parts, g_ref[...])


def _nat_ctx_kernel(q_ref, k_ref, v_ref, g_ref, o_ref):
    o_ref[0] = _heads_attention(q_ref[0], [(k_ref[0], v_ref[0], None)], g_ref[...])


def _nat_attention(q, k, v, kc, vc, bias, gout):
    nb, nt, _ = q.shape
    nblk = nt // NAT_QB
    nc = kc.shape[1]
    start = lambda i: jnp.clip(i - 1, 0, nblk - NAT_KBLKS)
    variant = lambda i: jnp.where(i == 0, 0, jnp.where(i == nblk - 1, 2, 1))
    qspec = pl.BlockSpec((1, NAT_QB, GROUP_W), lambda b, i: (b, i, 0))
    kspecs = [pl.BlockSpec((1, NAT_QB, GROUP_W), lambda b, i, j=j: (b, start(i) + j, 0))
              for j in range(NAT_KBLKS)]
    cspec = pl.BlockSpec((1, nc, GROUP_W), lambda b, i: (b, 0, 0))
    bspec = pl.BlockSpec((1, N_HEADS, NAT_QB, NAT_KBLKS * NAT_QB), lambda b, i: (variant(i), 0, 0, 0))
    return pl.pallas_call(
        _nat_kernel,
        out_shape=jax.ShapeDtypeStruct((nb, nt, GROUP_W), F32),
        grid=(nb, nblk),
        in_specs=[qspec] + kspecs + kspecs + [cspec, cspec, bspec, _full(gout.shape)],
        out_specs=qspec,
        name="nat_attention",
    )(q, k, k, k, v, v, v, kc, vc, bias, gout)


def _nat_ctx_attention(q, k, v, gout):
    nb, nc, _ = q.shape
    spec = pl.BlockSpec((1, nc, GROUP_W), lambda b: (b, 0, 0))
    return pl.pallas_call(
        _nat_ctx_kernel,
        out_shape=jax.ShapeDtypeStruct((nb, nc, GROUP_W), F32),
        grid=(nb,),
        in_specs=[spec, spec, spec, _full(gout.shape)],
        out_specs=spec,
        name="nat_ctx_attention",
    )(q, k, v, gout)


def _nat_bias_constants(n_rows):
    krows = NAT_KBLKS * NAT_QROWS
    qr = np.arange(NAT_QROWS)[:, None]
    kk = np.arange(krows)[None, :]
    cq = np.arange(GRID_W)[:, None]
    ck = np.arange(GRID_W)[None, :]
    c0 = np.clip(cq - NAT_KC // 2, 0, GRID_W - NAT_KC)
    col_ok = ((ck >= c0) & (ck < c0 + NAT_KC)).reshape(-1)
    dc = np.clip(ck - cq + (NAT_KC - 1), 0, 2 * NAT_KC - 2).reshape(-1)
    e_col = np.zeros((2 * NAT_KC, GRID_W * GRID_W), np.float32)
    e_col[dc, np.arange(GRID_W * GRID_W)] = 1.0
    big = 4 * n_rows + 64
    placements = [(0, 0, n_rows), (big // 2, big // 2 - NAT_QROWS, big),
                  (n_rows - NAT_QROWS, n_rows - krows, n_rows)]
    e_row = np.zeros((3, NAT_QROWS * krows, 2 * NAT_KR), np.float32)
    ok = np.zeros((3, NAT_QROWS * krows, GRID_W * GRID_W), np.float32)
    for v, (r_base, k_start, rows_total) in enumerate(placements):
        r = r_base + qr
        k_abs = k_start + kk
        r0 = np.clip(r - NAT_KR // 2, 0, rows_total - NAT_KR)
        row_ok = ((k_abs >= r0) & (k_abs < r0 + NAT_KR)).reshape(-1)
        dr = np.clip(k_abs - r + (NAT_KR - 1), 0, 2 * NAT_KR - 2).reshape(-1)
        e_row[v, np.arange(NAT_QROWS * krows), dr] = 1.0
        ok[v] = row_ok[:, None] & col_ok[None, :]
    return e_row, e_col, ok


def _split3(x):
    h1 = x.astype(BF16)
    r1 = x - h1.astype(F32)
    h2 = r1.astype(BF16)
    h3 = (r1 - h2.astype(F32)).astype(BF16)
    return h1, h2, h3


def _nat_bias_kernel(er_ref, rpb_ref, ec_ref, ok_ref, o_ref):
    er = er_ref[0]
    rows = functools.reduce(jnp.add, [_dot(er, p) for p in _split3(rpb_ref[0, 0])])
    ec = ec_ref[...]
    b = functools.reduce(jnp.add, [_dot(p, ec) for p in _split3(rows)])
    o_ref[0, 0, 0] = jnp.where(ok_ref[0] > 0.0, b, NEG_INF)


def _nat_bias_tables(rpb_all, n_rows):
    nl, nh = rpb_all.shape[:2]
    krows = NAT_KBLKS * NAT_QROWS
    e_row, e_col, ok = _nat_bias_constants(n_rows)
    rpb = jnp.pad(rpb_all, ((0, 0), (0, 0), (0, 1), (0, 1)))
    nr, ncol = NAT_QROWS * krows, GRID_W * GRID_W
    out = pl.pallas_call(
        _nat_bias_kernel,
        out_shape=jax.ShapeDtypeStruct((nl, 3, nh, nr, ncol), F32),
        grid=(nl, 3, nh),
        in_specs=[pl.BlockSpec((1, nr, 2 * NAT_KR), lambda l, v, h: (v, 0, 0)),
                  pl.BlockSpec((1, 1, 2 * NAT_KR, 2 * NAT_KC), lambda l, v, h: (l, h, 0, 0)),
                  pl.BlockSpec((2 * NAT_KC, ncol), lambda l, v, h: (0, 0)),
                  pl.BlockSpec((1, nr, ncol), lambda l, v, h: (v, 0, 0))],
        out_specs=pl.BlockSpec((1, 1, 1, nr, ncol), lambda l, v, h: (l, v, h, 0, 0)),
        name="nat_bias_tables",
    )(jnp.asarray(e_row, BF16), rpb, jnp.asarray(e_col, BF16), jnp.asarray(ok))
    out = out.reshape(nl, 3, nh, NAT_QROWS, krows, GRID_W, GRID_W).transpose(0, 1, 2, 3, 5, 4, 6)
    return out.reshape(nl, 3, nh, NAT_QB, krows * GRID_W)


HALO = SUBLANES


def _conv_kernel(x_ref, prev_ref, next_ref, w_ref, b_ref, small_ref, pv_ref, y_ref, oa_ref, ob_ref,
                 xpad_ref, *, l2norm):
    i = pl.program_id(1)
    last = pl.num_programs(1) - 1
    tm = x_ref.shape[1]
    xpad_ref[0:HALO] = jnp.where(i > 0, prev_ref[0], 0.0)
    xpad_ref[HALO:HALO + tm] = x_ref[0]
    xpad_ref[HALO + tm:2 * HALO + tm] = jnp.where(i < last, next_ref[0], 0.0)
    w = w_ref[...]
    acc = jnp.broadcast_to(b_ref[...], (tm, x_ref.shape[2]))
    for j in range(CONV_K):
        acc = acc + xpad_ref[pl.ds(HALO - CONV_K // 2 + j, tm), :] * w[j:j + 1]
    y = _silu(acc)
    if l2norm:
        grp = (_iota((GROUP_W, GROUP_W), 0) // HEAD_W == _iota((GROUP_W, GROUP_W), 1) // HEAD_W).astype(BF16)
        q = y[:, 0:256]
        k = y[:, 256:512]
        y_ref[0, :, 0:256] = q * lax.rsqrt(_dot2(q * q, grp) + RMS_EPS) * (HEAD_W ** -0.5)
        y_ref[0, :, 256:512] = k * lax.rsqrt(_dot2(k * k, grp) + RMS_EPS)
        y_ref[0, :, 512:768] = y[:, 512:768]
    else:
        y_ref[0] = y
    s = small_ref[0]
    pv = pv_ref[...]
    sp = _softplus(s + pv[1:2])
    lane = _iota((1, LANES), 1)
    oa_ref[0] = jnp.where(lane < L_G, jax.nn.sigmoid(s), -jnp.exp(pv[0:1]) * sp)
    ob_ref[0] = sp


def _conv_prep(x, small, w, b, pv, *, tm, l2norm):
    nb, nt, nc = x.shape
    hb = tm // HALO
    nhalo = nt // HALO
    return pl.pallas_call(
        functools.partial(_conv_kernel, l2norm=l2norm),
        out_shape=(jax.ShapeDtypeStruct(x.shape, F32), jax.ShapeDtypeStruct(small.shape, F32),
                   jax.ShapeDtypeStruct(small.shape, F32)),
        grid=(nb, nt // tm),
        in_specs=[pl.BlockSpec((1, tm, nc), lambda bb, i: (bb, i, 0)),
                  pl.BlockSpec((1, HALO, nc), lambda bb, i: (bb, jnp.maximum(i * hb - 1, 0), 0)),
                  pl.BlockSpec((1, HALO, nc), lambda bb, i: (bb, jnp.minimum((i + 1) * hb, nhalo - 1), 0)),
                  _full(w.shape), _full(b.shape),
                  pl.BlockSpec((1, tm, LANES), lambda bb, i: (bb, i, 0)),
                  _full(pv.shape)],
        out_specs=(pl.BlockSpec((1, tm, nc), lambda bb, i: (bb, i, 0)),
                   pl.BlockSpec((1, tm, LANES), lambda bb, i: (bb, i, 0)),
                   pl.BlockSpec((1, tm, LANES), lambda bb, i: (bb, i, 0))),
        scratch_shapes=[pltpu.VMEM((tm + 2 * HALO, nc), F32)],
        name="conv_prep",
    )(x, x, x, w, b, small, pv)


def _chunk_cumsum_matrix(n, chunk, reverse):
    r = _iota((n, n), 0)
    c = _iota((n, n), 1)
    same = (r // chunk) == (c // chunk)
    return (same & ((c >= r) if reverse else (c <= r))).astype(BF16)


def _expand_matrix(lane0, group, width):
    return (_iota((LANES, width), 0) == lane0 + _iota((LANES, width), 1) // group).astype(BF16)


def _row_form(col_vals, chunk):
    pick = _iota(col_vals.shape, 0) == (_iota(col_vals.shape, 1) % chunk)
    ones = jnp.ones((SUBLANES, chunk), BF16)
    return _dot2l(ones, jnp.where(pick, col_vals, 0.0))[0:1]


def _heads_diag(x):
    return _block_diag(x, N_HEADS, GDN_CHUNK, HEAD_W)


def _heads_undiag(x):
    keep = (_iota(x.shape, 0) // GDN_CHUNK) == (_iota(x.shape, 1) // HEAD_W)
    x = jnp.where(keep, x, 0.0)
    return functools.reduce(jnp.add, [x[hh * GDN_CHUNK:(hh + 1) * GDN_CHUNK] for hh in range(N_HEADS)])


def _pc3(x, y):
    xh, xl = _split(x)
    yh, yl = _split(y)
    r = _dot(jnp.concatenate([xh, xl], axis=0), _heads_diag(yh))
    n = x.shape[0]
    return r[0:n] + r[n:2 * n] + _dot(xh, _heads_diag(yl))


def _pc1(x, y):
    return _dot(x.astype(BF16), _heads_diag(y.astype(BF16)))


def _gdn_chunk_kernel(y_ref, sm_ref, g_ref, qs_ref, qe_ref, o0_ref, gl_ref, *, d, n_chunks):
    ch = GDN_CHUNK
    cb = n_chunks * ch
    reverse = d == 1
    sm = sm_ref[0]
    tri = _chunk_cumsum_matrix(ch, ch, reverse)
    cs = jnp.concatenate([_dot2l(tri, sm[c * ch:(c + 1) * ch]) for c in range(n_chunks)], axis=0)
    gc_all = _dot2(cs, _expand_matrix(L_G + N_HEADS * d, HEAD_W, GROUP_W))
    beta_all = _dot2(sm, _expand_matrix(L_BETA + N_HEADS * d, HEAD_W, GROUP_W))
    y = y_ref[0]
    q_all, k_all, v_all = y[:, 0:256], y[:, 256:512], y[:, 512:768]
    kb_all = k_all * beta_all
    vb_all = v_all * beta_all
    eg_all = jnp.exp(gc_all)

    ti = _iota((ch, GROUP_W), 0)
    tj = _iota((ch, GROUP_W), 1) % ch
    incl = (tj >= ti) if reverse else (tj <= ti)
    strict = (tj > ti) if reverse else (tj < ti)
    last = 0 if reverse else ch - 1
    cs_ = range(n_chunks)
    rows = [slice(c * ch, (c + 1) * ch) for c in cs_]

    gc = [gc_all[r] for r in rows]
    decay = [jnp.where(incl, jnp.exp(jnp.where(incl, g - _row_form(g, ch), 0.0)), 0.0) for g in gc]
    qk = [_dot_nt(jnp.concatenate([kb_all[r], q_all[r]], axis=0).astype(BF16),
                  _heads_diag(k_all[r].astype(BF16))) for r in rows]
    a_mat = [jnp.where(strict, x[0:ch] * dc, 0.0) for x, dc in zip(qk, decay)]
    a_intra = [(x[ch:2 * ch] * dc).astype(BF16) for x, dc in zip(qk, decay)]

    base = SUBLANES
    eye = jnp.where(ti == tj, 1.0, 0.0)
    m = [jnp.where(ti // base == tj // base, -a, 0.0) for a in a_mat]
    t = [eye + x for x in m]
    for _ in range(2):
        m = [_pc1(x, x) for x in m]
        t = [x + _pc1(x, p) for x, p in zip(t, m)]
    size = 2 * base
    while size <= ch:
        off = (ti // size == tj // size) & (ti // (size // 2) != tj // (size // 2))
        ct = [_pc1(jnp.where(off, a, 0.0), x) for a, x in zip(a_mat, t)]
        t = [x - _pc1(x, p) for x, p in zip(t, ct)]
        size *= 2
    resid = [eye - x - _pc3(a, x) for a, x in zip(a_mat, t)]
    t = [x + _pc1(x, r) for x, r in zip(t, resid)]

    u = [_pc1(x, vb_all[r]) for x, r in zip(t, rows)]
    w = [_pc1(x, kb_all[r] * eg_all[r]) for x, r in zip(t, rows)]
    for c in cs_:
        r = rows[c]
        g_last = gc[c][last:last + 1]
        k_dec = (k_all[r] * jnp.exp(g_last - gc[c])).astype(BF16)
        wu = jnp.concatenate([w[c], u[c]], axis=1).astype(BF16)
        full = _dot_tn(k_dec, wu)
        g_ref[0, r, :] = _heads_undiag(full[:, 0:GROUP_W]).astype(BF16)
        qs_ref[0, r, :] = _heads_undiag(full[:, GROUP_W:])
        wu_bd = jnp.concatenate([_heads_diag(wu[:, 0:GROUP_W]), _heads_diag(wu[:, GROUP_W:])], axis=1)
        aw = _dot(a_intra[c], wu_bd)
        qe_ref[0, r, :] = (q_all[r] * eg_all[r] - aw[:, 0:GROUP_W]).astype(BF16)
        o0_ref[0, r, :] = aw[:, GROUP_W:]
        gl_ref[0, c:c + 1, :] = jnp.exp(g_last)


def _gdn_state_kernel(g_ref, qs_ref, qe_ref, o0_ref, gl_ref, s0_ref, o_ref, sfin_ref, state_ref,
                      *, d, n_chunks):
    ch = GDN_CHUNK
    nb = g_ref.shape[0]
    step = pl.program_id(0)

    @pl.when(step == 0)
    def _():
        state_ref[...] = s0_ref[...]

    order = range(n_chunks - 1, -1, -1) if d == 1 else range(n_chunks)
    state = [state_ref[b] for b in range(nb)]
    for c in order:
        r = slice(c * ch, (c + 1) * ch)
        for b in range(nb):
            both = jnp.concatenate([g_ref[b, r, :], qe_ref[b, r, :]], axis=0)
            res = _dot(both, _heads_diag(state[b].astype(BF16)))
            o_ref[b, r, :] = res[ch:2 * ch] + o0_ref[b, r, :]
            state[b] = state[b] * gl_ref[b, c:c + 1, :] - res[0:ch] + qs_ref[b, r, :]
    for b in range(nb):
        state_ref[b] = state[b]

    @pl.when(step == pl.num_programs(0) - 1)
    def _():
        for b in range(nb):
            sfin_ref[b] = state[b]


def _gdn_chunk_call(y, sm, *, d, cb):
    nb, nt, nc = y.shape
    n_chunks = cb // GDN_CHUNK
    tok = lambda dt: jax.ShapeDtypeStruct((nb, nt, GROUP_W), dt)
    tspec = pl.BlockSpec((1, cb, GROUP_W), lambda bb, i: (bb, i, 0))
    return pl.pallas_call(
        functools.partial(_gdn_chunk_kernel, d=d, n_chunks=n_chunks),
        out_shape=(tok(BF16), tok(F32), tok(BF16), tok(F32),
                   jax.ShapeDtypeStruct((nb, nt // GDN_CHUNK, GROUP_W), F32)),
        grid=(nb, nt // cb),
        in_specs=[pl.BlockSpec((1, cb, nc), lambda bb, i: (bb, i, 0)),
                  pl.BlockSpec((1, cb, LANES), lambda bb, i: (bb, i, 0))],
        out_specs=(tspec, tspec, tspec, tspec,
                   pl.BlockSpec((1, n_chunks, GROUP_W), lambda bb, i: (bb, i, 0))),
        name="gdn_chunk",
    )(y, sm)


def _gdn_state_call(parts, s0, *, d, cb):
    g, qs, qe, o0, gl = parts
    nb, nt, _ = g.shape
    n_chunks = cb // GDN_CHUNK
    nblk = nt // cb
    blk = (lambda i: (0, nblk - 1 - i, 0)) if d == 1 else (lambda i: (0, i, 0))
    tspec = pl.BlockSpec((nb, cb, GROUP_W), blk)
    sspec = pl.BlockSpec((nb, GDN_CHUNK, GROUP_W), lambda i: (0, 0, 0))
    return pl.pallas_call(
        functools.partial(_gdn_state_kernel, d=d, n_chunks=n_chunks),
        out_shape=(jax.ShapeDtypeStruct((nb, nt, GROUP_W), F32),
                   jax.ShapeDtypeStruct((nb, GDN_CHUNK, GROUP_W), F32)),
        grid=(nblk,),
        in_specs=[tspec, tspec, tspec, tspec, pl.BlockSpec((nb, n_chunks, GROUP_W), blk), sspec],
        out_specs=(tspec, sspec),
        scratch_shapes=[pltpu.VMEM((nb, GDN_CHUNK, GROUP_W), F32)],
        compiler_params=pltpu.CompilerParams(dimension_semantics=("arbitrary",)),
        name="gdn_state_scan",
    )(g, qs, qe, o0, gl, s0)


def _gdn_bidir(y_l, sm_l, y_c, sm_c, *, cb_l, cb_c):
    zero = jnp.zeros((y_l.shape[0], GDN_CHUNK, GROUP_W), F32)
    outs = []
    for d in (0, 1):
        o_c, s_c = _gdn_state_call(_gdn_chunk_call(y_c, sm_c, d=d, cb=cb_c), zero, d=d, cb=cb_c)
        o_l, _ = _gdn_state_call(_gdn_chunk_call(y_l, sm_l, d=d, cb=cb_l), s_c, d=d, cb=cb_l)
        outs.append((o_l, o_c))
    return outs


def _ssd_kernel(x_ref, dt_ref, da_ref, s0_ref, o_ref, sfin_ref, state_ref, *, d, n_chunks):
    ch = SSM_CHUNK
    cb = n_chunks * ch
    reverse = d == 1
    step = pl.program_id(1)

    @pl.when(step == 0)
    def _():
        state_ref[...] = s0_ref[0]

    lane0 = L_DT + N_HEADS * d
    tri = _chunk_cumsum_matrix(ch, ch, reverse)
    da = da_ref[0]
    cs = jnp.concatenate([_dot2l(tri, da[c * ch:(c + 1) * ch]) for c in range(n_chunks)], axis=0)
    e64 = _expand_matrix(lane0, HEAD_W, GROUP_W)
    ac_all = _dot2(cs, e64)
    ac5_all = _dot2(cs, _expand_matrix(lane0, ch, N_HEADS * ch))
    dt_all = _dot2(dt_ref[0], e64)
    xbc = x_ref[0]
    xdt_all = xbc[:, 0:256] * dt_all
    b_all, c_all = xbc[:, 256:512], xbc[:, 512:768]

    ti = _iota((ch, N_HEADS * ch), 0)
    tj = _iota((ch, N_HEADS * ch), 1) % ch
    incl = (tj >= ti) if reverse else (tj <= ti)
    grp_keep = (_iota((GROUP_W, GROUP_W), 0) // ch) == (_iota((GROUP_W, GROUP_W), 1) // ch)
    last = 0 if reverse else ch - 1

    rows = [slice(c * ch, (c + 1) * ch) for c in range(n_chunks)]
    seg = [jnp.where(incl, jnp.exp(jnp.where(incl, ac5_all[r] - _row_form(ac5_all[r], ch), 0.0)), 0.0)
           for r in rows]
    cm = [c_all[r].astype(BF16) for r in rows]
    bmb = [b_all[r].astype(BF16) for r in rows]
    cb_g = [_dot_nt(c_, _block_diag(b_, 2, ch, SSM_STATE)) for c_, b_ in zip(cm, bmb)]
    scores = [(jnp.concatenate([x[:, 0:ch], x[:, 0:ch], x[:, ch:], x[:, ch:]], axis=1) * sg).astype(BF16)
              for x, sg in zip(cb_g, seg)]
    xdtb = [xdt_all[r].astype(BF16) for r in rows]
    y_diag = [_dot(s_, _block_diag(x_, N_HEADS, ch, HEAD_W)) for s_, x_ in zip(scores, xdtb)]
    a_last = [ac_all[r][last:last + 1] for r in rows]
    states = [jnp.where(grp_keep, _dot_tn(b_, (xdt_all[r] * jnp.exp(al - ac_all[r])).astype(BF16)), 0.0)
              for b_, r, al in zip(bmb, rows, a_last)]
    state = state_ref[...]
    for c in (range(n_chunks - 1, -1, -1) if reverse else range(n_chunks)):
        o_ref[0, rows[c], :] = y_diag[c] + _dot(cm[c], state.astype(BF16)) * jnp.exp(ac_all[rows[c]])
        state = state * jnp.exp(a_last[c]) + states[c]
    state_ref[...] = state

    @pl.when(step == pl.num_programs(1) - 1)
    def _():
        sfin_ref[0] = state_ref[...]


def _scan_call(kernel, name, seq, smalls, s0, *, d, cb, chunk):
    nb, nt, nc = seq.shape
    nblk = nt // cb
    blk = (lambda bb, i: (bb, nblk - 1 - i, 0)) if d == 1 else (lambda bb, i: (bb, i, 0))
    sspec = pl.BlockSpec((1, GROUP_W, GROUP_W), lambda bb, i: (bb, 0, 0))
    return pl.pallas_call(
        functools.partial(kernel, d=d, n_chunks=cb // chunk),
        out_shape=(jax.ShapeDtypeStruct((nb, nt, GROUP_W), F32),
                   jax.ShapeDtypeStruct((nb, GROUP_W, GROUP_W), F32)),
        grid=(nb, nblk),
        in_specs=[pl.BlockSpec((1, cb, nc), blk)] + [pl.BlockSpec((1, cb, LANES), blk) for _ in smalls]
                 + [sspec],
        out_specs=(pl.BlockSpec((1, cb, GROUP_W), blk), sspec),
        scratch_shapes=[pltpu.VMEM((GROUP_W, GROUP_W), F32)],
        compiler_params=pltpu.CompilerParams(dimension_semantics=("arbitrary", "arbitrary")),
        name=name,
    )(seq, *smalls, s0)


def _bidir_scan(kernel, name, seq_l, smalls_l, seq_c, smalls_c, *, cb_l, cb_c, chunk):
    zero = jnp.zeros((seq_l.shape[0], GROUP_W, GROUP_W), F32)
    outs = []
    for d in (0, 1):
        o_c, s_c = _scan_call(kernel, name, seq_c, smalls_c, zero, d=d, cb=cb_c, chunk=chunk)
        o_l, _ = _scan_call(kernel, name, seq_l, smalls_l, s_c, d=d, cb=cb_l, chunk=chunk)
        outs.append((o_l, o_c))
    return outs


def _outproj_kernel(h_ref, mod_ref, g_ref, mla_ref, nat_ref, gof_ref, gob_ref, gzg_ref,
                    yf_ref, yb_ref, sx_ref, szg_ref, pv_ref, wm_ref, wr_ref, o_ref):
    pv = pv_ref[...]
    valid = _iota((1, LANES), 1) < HEAD_W
    slabs = [jnp.where(valid, mla_ref[0, hh], 0.0) for hh in range(N_HEADS)]
    ss = functools.reduce(jnp.add, [jnp.sum(s * s, axis=-1, keepdims=True) for s in slabs])
    scale = lax.rsqrt(ss / GROUP_W + RMS_EPS)
    y = None
    for hh in range(N_HEADS):
        part = _dot((slabs[hh] * scale * pv[hh:hh + 1, 0:LANES]).astype(BF16), wm_ref[hh])
        y = part if y is None else y + part
    y = y + _dot(nat_ref[0].astype(BF16), wr_ref[0:256, :])
    grp = (_iota((GROUP_W, GROUP_W), 0) // HEAD_W == _iota((GROUP_W, GROUP_W), 1) // HEAD_W).astype(BF16)
    o = gof_ref[0] + gob_ref[0]
    on = o * lax.rsqrt(_dot2(o * o, grp) / HEAD_W + RMS_EPS) * pv[4:5]
    y = y + _dot((on * _silu(gzg_ref[0])).astype(BF16), wr_ref[256:512, :])
    s = (yf_ref[0] + yb_ref[0] + sx_ref[0] * pv[5:6]) * _silu(szg_ref[0])
    sn = jnp.concatenate([_rms(s[:, 0:LANES], pv[6:7, 0:LANES]),
                          _rms(s[:, LANES:], pv[6:7, LANES:])], axis=1)
    y = y + _dot(sn.astype(BF16), wr_ref[512:768, :])
    m = mod_ref[0]
    g = g_ref[...]
    o_ref[0] = h_ref[0] + m[5:6] * _rms(y, g[3:4])


def _out_projection(h, mod, g, mla_o, nat_o, gdn_f, gdn_b, gdn_zg, ssm_f, ssm_b, ssm_xbc, ssm_zg,
                    pv, w_mla, w_rest, *, tm):
    nb, nt, dm = h.shape
    per_batch = mod.shape[0] > 1
    tspec = lambda w: pl.BlockSpec((1, tm, w), lambda b, i: (b, i, 0))
    return pl.pallas_call(
        _outproj_kernel,
        out_shape=jax.ShapeDtypeStruct(h.shape, F32),
        grid=(nb, nt // tm),
        in_specs=[tspec(dm),
                  pl.BlockSpec((1, N_MOD, dm), lambda b, i: (b if per_batch else 0, 0, 0)),
                  _full(g.shape),
                  pl.BlockSpec((1, N_HEADS, tm, LANES), lambda b, i: (b, 0, i, 0)),
                  tspec(256), tspec(256), tspec(256), tspec(256), tspec(256), tspec(256),
                  tspec(256),
                  tspec(256),
                  _full(pv.shape), _full(w_mla.shape), _full(w_rest.shape)],
        out_specs=tspec(dm),
        name="out_projection",
    )(h, mod, g, mla_o, nat_o, gdn_f, gdn_b, gdn_zg, ssm_f, ssm_b, ssm_xbc, ssm_zg, pv, w_mla, w_rest)


_ROPE_SWAP = np.array(list(range(8, 16)) + list(range(0, 8)) + list(range(24, 32)) + list(range(16, 24)))


def _pack_layer(p, l):
    w_in = p["w_in"][l]
    dm = w_in.shape[0]
    o_nat = MLA_Q_LORA + MLA_KV_LORA + MLA_ROPE
    o_gdn = o_nat + 768
    o_ssm = o_gdn + GDN_QKV + GROUP_W + 4 * N_HEADS
    kr = w_in[:, MLA_Q_LORA + MLA_KV_LORA:o_nat]
    z64 = jnp.zeros((dm, 64), F32)
    z32 = jnp.zeros((dm, 32), F32)
    small = jnp.concatenate([w_in[:, o_gdn + 1024:o_gdn + 1040],
                             w_in[:, o_ssm + 1024:o_ssm + 1032],
                             jnp.zeros((dm, LANES - 24), F32)], axis=1)
    w_packed = jnp.concatenate([
        w_in[:, 0:384], z64, kr, z32, z64, kr[:, _ROPE_SWAP], z32,
        w_in[:, o_nat:o_gdn],
        w_in[:, o_gdn:o_gdn + 1024],
        w_in[:, o_ssm:o_ssm + 1024],
        small], axis=1).astype(BF16)
    assert w_packed.shape[1] == C_TOTAL

    wuq = p["mla_wuq"][l].reshape(MLA_Q_LORA, N_HEADS, MLA_NOPE + MLA_ROPE)
    zq = jnp.zeros((MLA_Q_LORA, N_HEADS, 32), F32)
    wq1 = jnp.concatenate([wuq, zq], axis=2).reshape(MLA_Q_LORA, N_HEADS * LANES)
    wq2 = jnp.concatenate([jnp.zeros((MLA_Q_LORA, N_HEADS, 64), F32), wuq[:, :, MLA_NOPE:][:, :, _ROPE_SWAP], zq],
                          axis=2).reshape(MLA_Q_LORA, N_HEADS * LANES)
    wukv = p["mla_wukv"][l].reshape(MLA_KV_LORA, N_HEADS, MLA_NOPE + HEAD_W)
    zk = jnp.zeros((MLA_KV_LORA, N_HEADS, 64), F32)
    wk = jnp.concatenate([wukv[:, :, :MLA_NOPE], zk], axis=2).reshape(MLA_KV_LORA, N_HEADS * LANES)
    wv = jnp.concatenate([wukv[:, :, MLA_NOPE:], zk], axis=2).reshape(MLA_KV_LORA, N_HEADS * LANES)

    def lanes(vals, lane0):
        v = vals.reshape(-1)
        return jnp.zeros((LANES,), F32).at[lane0:lane0 + v.shape[0]].set(v)

    w_out = p["w_out"][l]
    w_mla = jnp.concatenate([w_out[0:256].reshape(N_HEADS, HEAD_W, dm),
                             jnp.zeros((N_HEADS, LANES - HEAD_W, dm), F32)], axis=1).astype(BF16)
    gout = jnp.concatenate([p["mla_gout"][l].reshape(N_HEADS, HEAD_W),
                            jnp.zeros((N_HEADS, GROUP_W - HEAD_W), F32)], axis=1)
    pv_out = jnp.concatenate([gout,
                              jnp.tile(p["gdn_gnorm"][l], N_HEADS)[None],
                              jnp.repeat(p["ssm_d"][l], HEAD_W)[None],
                              p["ssm_gnorm"][l][None],
                              jnp.zeros((1, GROUP_W), F32)], axis=0)
    return dict(
        ffn1=(p["ffn1_w1"][l].astype(BF16), p["ffn1_w3"][l].astype(BF16), p["ffn1_w2"][l].astype(BF16)),
        ffn2=(p["ffn2_w1"][l].astype(BF16), p["ffn2_w3"][l].astype(BF16), p["ffn2_w2"][l].astype(BF16)),
        g=p["norm_g"][l],
        w_in=w_packed, wq1=wq1.astype(BF16), wq2=wq2.astype(BF16), wk=wk.astype(BF16), wv=wv.astype(BF16),
        mla_gq=p["mla_gq"][l][None], mla_gkv=p["mla_gkv"][l][None],
        nat_bias=p["nat_bias"][l], nat_gout=p["nat_gout"][l][None],
        gdn_conv_w=p["gdn_conv_w"][l], gdn_conv_b=jnp.zeros((1, GDN_QKV), F32),
        gdn_pv=jnp.stack([lanes(p["gdn_a_log"][l], L_G), lanes(p["gdn_dt_bias"][l], L_G)]),
        ssm_conv_w=p["ssm_conv_w"][l], ssm_conv_b=p["ssm_conv_b"][l][None],
        ssm_pv=jnp.stack([lanes(p["ssm_a_log"][l], L_DT), lanes(p["ssm_dt_bias"][l], L_DT)]),
        pv_out=pv_out, w_mla=w_mla, w_rest=w_out[256:].astype(BF16),
    )


def _rope_tables(n_tok):
    pos = jnp.arange(n_tok)
    rows = (pos // GRID_W).astype(F32)
    cols = (pos % GRID_W).astype(F32)
    quarter = MLA_ROPE // 4
    freqs = ROPE_THETA ** (-jnp.arange(quarter, dtype=F32) / quarter)
    ar = rows[:, None] * freqs
    ac = cols[:, None] * freqs
    cos = jnp.concatenate([jnp.cos(ar), jnp.cos(ar), jnp.cos(ac), jnp.cos(ac)], axis=1)
    sin = jnp.concatenate([-jnp.sin(ar), jnp.sin(ar), -jnp.sin(ac), jnp.sin(ac)], axis=1)
    ones = jnp.ones((n_tok, MLA_NOPE), F32)
    zeros = jnp.zeros((n_tok, MLA_NOPE), F32)
    pad = jnp.zeros((n_tok, LANES - MLA_NOPE - MLA_ROPE), F32)
    return jnp.concatenate([ones, cos, pad], axis=1), jnp.concatenate([zeros, sin, pad], axis=1)


def _tiles(n_tok):
    return dict(tm=min(512, n_tok), tm_ffn=min(512, n_tok), tq=min(1024, n_tok),
                cb_gdn=min(512, n_tok), cb_ssd=min(512, n_tok))


def _mixers(hl, hc, ml, mc, lw, ropes, need_ctx):
    tl, tc = _tiles(hl.shape[1]), _tiles(hc.shape[1])
    (cos_l, sin_l), (cos_c, sin_c) = ropes
    zl = _in_projection(hl, ml, lw["g"], lw, cos_l, sin_l, tm=tl["tm"])
    zc = _in_projection(hc, mc, lw["g"], lw, cos_c, sin_c, tm=tc["tm"])
    (mq_l, mk_l, mv_l, nq_l, nk_l, nv_l, gqkv_l, gzg_l, szg_l, sxbc_l, small_l) = zl
    (mq_c, mk_c, mv_c, nq_c, nk_c, nv_c, gqkv_c, gzg_c, szg_c, sxbc_c, small_c) = zc

    n_lat = hl.shape[1]
    tk = min(1024, n_lat)
    mla_l = _mla_attention(mq_l, mk_c, mv_c, mk_l, mv_l, tq=tl["tq"], tk=tk)
    nat_l = _nat_attention(nq_l, nk_l, nv_l, nk_c, nv_c, lw["nat_bias"], lw["nat_gout"])

    gy_l, ga_l, _ = _conv_prep(gqkv_l, small_l, lw["gdn_conv_w"], lw["gdn_conv_b"], lw["gdn_pv"],
                               tm=tl["tm"], l2norm=True)
    gy_c, ga_c, _ = _conv_prep(gqkv_c, small_c, lw["gdn_conv_w"], lw["gdn_conv_b"], lw["gdn_pv"],
                               tm=tc["tm"], l2norm=True)
    (gf_l, gf_c), (gb_l, gb_c) = _gdn_bidir(gy_l, ga_l, gy_c, ga_c, cb_l=tl["cb_gdn"], cb_c=tc["cb_gdn"])

    sy_l, sa_l, sd_l = _conv_prep(sxbc_l, small_l, lw["ssm_conv_w"], lw["ssm_conv_b"], lw["ssm_pv"],
                                  tm=tl["tm"], l2norm=False)
    sy_c, sa_c, sd_c = _conv_prep(sxbc_c, small_c, lw["ssm_conv_w"], lw["ssm_conv_b"], lw["ssm_pv"],
                                  tm=tc["tm"], l2norm=False)
    (sf_l, sf_c), (sb_l, sb_c) = _bidir_scan(_ssd_kernel, "ssd_scan", sy_l, (sd_l, sa_l), sy_c, (sd_c, sa_c),
                                             cb_l=tl["cb_ssd"], cb_c=tc["cb_ssd"], chunk=SSM_CHUNK)

    hl = _out_projection(hl, ml, lw["g"], mla_l, nat_l, gf_l, gb_l, gzg_l, sf_l, sb_l, sy_l, szg_l,
                         lw["pv_out"], lw["w_mla"], lw["w_rest"], tm=tl["tm"])
    if need_ctx:
        mla_c = _mla_attention(mq_c, mk_c, mv_c, tq=tc["tq"], tk=tk)
        nat_c = _nat_ctx_attention(nq_c, nk_c, nv_c, lw["nat_gout"])
        hc = _out_projection(hc, mc, lw["g"], mla_c, nat_c, gf_c, gb_c, gzg_c, sf_c, sb_c, sy_c, szg_c,
                             lw["pv_out"], lw["w_mla"], lw["w_rest"], tm=tc["tm"])
    return hl, hc


def _layer(hl, hc, ml, mc, lw, ropes, need_ctx):
    tl, tc = _tiles(hl.shape[1]), _tiles(hc.shape[1])
    hl = _half_ffn(hl, ml, lw["g"], *lw["ffn1"], k0=0, gp=0, tm=tl["tm_ffn"])
    hc = _half_ffn(hc, mc, lw["g"], *lw["ffn1"], k0=0, gp=0, tm=tc["tm"])
    hl, hc = _mixers(hl, hc, ml, mc, lw, ropes, need_ctx)
    hl = _half_ffn(hl, ml, lw["g"], *lw["ffn2"], k0=6, gp=4, tm=tl["tm_ffn"])
    if need_ctx:
        hc = _half_ffn(hc, mc, lw["g"], *lw["ffn2"], k0=6, gp=4, tm=tc["tm"])
    return hl, hc


def kernel(x, c, ctx, c_ctx, w_mod, b_mod, norm_g, ffn1_w1, ffn1_w3, ffn1_w2, ffn2_w1, ffn2_w3, ffn2_w2,
           w_in, w_out, mla_gq, mla_gkv, mla_wuq, mla_wukv, mla_gout, nat_rpb, nat_gout, gdn_conv_w,
           gdn_a_log, gdn_dt_bias, gdn_gnorm, ssm_conv_w, ssm_conv_b, ssm_a_log, ssm_dt_bias, ssm_d,
           ssm_gnorm):
    p = dict(norm_g=norm_g, ffn1_w1=ffn1_w1, ffn1_w3=ffn1_w3, ffn1_w2=ffn1_w2, ffn2_w1=ffn2_w1,
             ffn2_w3=ffn2_w3, ffn2_w2=ffn2_w2, w_in=w_in, w_out=w_out, mla_gq=mla_gq, mla_gkv=mla_gkv,
             mla_wuq=mla_wuq, mla_wukv=mla_wukv, mla_gout=mla_gout, nat_rpb=nat_rpb, nat_gout=nat_gout,
             gdn_conv_w=gdn_conv_w, gdn_a_log=gdn_a_log, gdn_dt_bias=gdn_dt_bias, gdn_gnorm=gdn_gnorm,
             ssm_conv_w=ssm_conv_w, ssm_conv_b=ssm_conv_b, ssm_a_log=ssm_a_log, ssm_dt_bias=ssm_dt_bias,
             ssm_d=ssm_d, ssm_gnorm=ssm_gnorm)
    nb, n_lat, dm = x.shape
    n_ctx = ctx.shape[1]
    depth = w_mod.shape[0]
    cvec = jnp.concatenate([c, c_ctx[None], jnp.zeros((SUBLANES - nb - 1, dm), F32)], axis=0)
    mods = _modulation(cvec, w_mod, b_mod).reshape(depth, SUBLANES, N_MOD, dm)
    cos_c = jnp.concatenate([jnp.ones((n_ctx, MLA_NOPE + MLA_ROPE), F32),
                             jnp.zeros((n_ctx, LANES - MLA_NOPE - MLA_ROPE), F32)], axis=1)
    ropes = (_rope_tables(n_lat), (cos_c, jnp.zeros((n_ctx, LANES), F32)))
    p["nat_bias"] = _nat_bias_tables(nat_rpb, n_lat // GRID_W)
    hl, hc = x, ctx
    for l in range(depth):
        lw = _pack_layer(p, l)
        hl, hc = _layer(hl, hc, mods[l, 0:nb], mods[l, nb:nb + 1], lw, ropes, need_ctx=l < depth - 1)
    return hl
```

```python
import functools
import math

import jax
import jax.numpy as jnp
import numpy as np
from jax import lax
from jax.experimental import pallas as pl
from jax.experimental.pallas import tpu as pltpu

F32 = jnp.float32
BF16 = jnp.bfloat16

D_MODEL = 1024
DEPTH = 4
GRID_W = 64
N_MOD = 9
D_FF = 2816
RMS_EPS = 1e-6
NEG_INF = -1e30
ROPE_THETA = 10000.0
GROUP_W = 256
N_HEADS = 4
HEAD_W = 64
MLA_NOPE = 64
MLA_ROPE = 32
MLA_Q_LORA = 256
MLA_KV_LORA = 128
NAT_KR = 8
NAT_KC = 16
GDN_CHUNK = 64
SSM_STATE = 128
SSM_CHUNK = 128
CONV_K = 5
GDN_QKV = 768
SSM_XBC = 768

LANES = 128
SUBLANES = 8
VMEM_LIMIT = 56 * 1024 * 1024

C_MLA = 0
C_NAT = 640
C_GQKV = 1408
C_GZG = 2176
C_SZG = 2432
C_SXBC = 2688
C_SMALL = 3456
C_TOTAL = 3584
L_BETA, L_G, L_DT = 0, 8, 16

MLA_QSCALE = (MLA_NOPE + MLA_ROPE) ** -0.5 * math.log2(math.e)
NAT_QSCALE = HEAD_W ** -0.5


def _dot(a, b):
    return jnp.dot(a, b, preferred_element_type=F32)


def _dot_nt(a, b):
    return lax.dot_general(a, b, (((1,), (1,)), ((), ())), preferred_element_type=F32)


def _dot_tn(a, b):
    return lax.dot_general(a, b, (((0,), (0,)), ((), ())), preferred_element_type=F32)


def _split(x):
    hi = x.astype(BF16)
    lo = (x - hi.astype(F32)).astype(BF16)
    return hi, lo


def _dot2(a, m):
    hi, lo = _split(a)
    return _dot(hi, m) + _dot(lo, m)


def _dot2l(m, a):
    hi, lo = _split(a)
    return _dot(m, hi) + _dot(m, lo)


def _rms(x, g):
    return x * lax.rsqrt(jnp.mean(x * x, axis=-1, keepdims=True) + RMS_EPS) * g


def _silu(x):
    return x * jax.nn.sigmoid(x)


def _softplus(x):
    return jnp.maximum(x, 0.0) + jnp.log1p(jnp.exp(-jnp.abs(x)))


def _iota(shape, dim):
    return lax.broadcasted_iota(jnp.int32, shape, dim)


def _block_diag(x, n, blk_r, blk_c):
    t = jnp.concatenate([x] * n, axis=0)
    keep = (_iota(t.shape, 0) // blk_r) == (_iota(t.shape, 1) // blk_c)
    return jnp.where(keep, t, 0.0)


def _full(shape):
    nd = len(shape)
    return pl.BlockSpec(shape, lambda *_: (0,) * nd)


def _resident(shape):
    nd = len(shape)
    return pl.BlockSpec(shape, lambda *_: (0,) * nd, pipeline_mode=pl.Buffered(1))


def _mod_kernel(c_ref, w_ref, b_ref, o_ref):
    s = _silu(c_ref[...])
    o_ref[0] = jnp.dot(s, w_ref[0], preferred_element_type=F32,
                       precision=lax.Precision.HIGHEST) + b_ref[0]


def _modulation(cvec, w_mod, b_mod):
    nl, dm, nm = w_mod.shape
    rows = cvec.shape[0]
    tn = 1536
    return pl.pallas_call(
        _mod_kernel,
        out_shape=jax.ShapeDtypeStruct((nl, rows, nm), F32),
        grid=(nl, nm // tn),
        in_specs=[pl.BlockSpec((rows, dm), lambda l, j: (0, 0)),
                  pl.BlockSpec((1, dm, tn), lambda l, j: (l, 0, j)),
                  pl.BlockSpec((1, 1, tn), lambda l, j: (l, 0, j))],
        out_specs=pl.BlockSpec((1, rows, tn), lambda l, j: (l, 0, j)),
        name="modulation",
    )(cvec, w_mod, b_mod.reshape(nl, 1, nm))


FFN_CHUNK = 256


def _ffn_kernel(h_ref, mod_ref, g_ref, w1_ref, w3_ref, w2_ref, o_ref, *, k0, gp):
    x = h_ref[0]
    m = mod_ref[0]
    g = g_ref[...]
    u = _rms(x, g[gp:gp + 1]) * (1.0 + m[k0 + 1:k0 + 2]) + m[k0:k0 + 1]
    ub = u.astype(BF16)
    acc = None
    for c in range(D_FF // FFN_CHUNK):
        sl = slice(c * FFN_CHUNK, (c + 1) * FFN_CHUNK)
        a = _dot(ub, w1_ref[:, sl])
        b = _dot(ub, w3_ref[:, sl])
        hid = (_silu(a) * b).astype(BF16)
        part = _dot(hid, w2_ref[sl, :])
        acc = part if acc is None else acc + part
    o_ref[0] = x + 0.5 * m[k0 + 2:k0 + 3] * _rms(acc, g[gp + 1:gp + 2])


def _half_ffn(h, mod, g, w1, w3, w2, *, k0, gp, tm):
    nb, nt, dm = h.shape
    per_batch = mod.shape[0] > 1
    return pl.pallas_call(
        functools.partial(_ffn_kernel, k0=k0, gp=gp),
        out_shape=jax.ShapeDtypeStruct(h.shape, F32),
        grid=(nb, nt // tm),
        in_specs=[pl.BlockSpec((1, tm, dm), lambda b, i: (b, i, 0)),
                  pl.BlockSpec((1, N_MOD, dm), lambda b, i: (b if per_batch else 0, 0, 0)),
                  _full(g.shape), _resident(w1.shape), _resident(w3.shape), _resident(w2.shape)],
        out_specs=pl.BlockSpec((1, tm, dm), lambda b, i: (b, i, 0)),
        compiler_params=pltpu.CompilerParams(vmem_limit_bytes=VMEM_LIMIT),
        name="half_ffn",
    )(h, mod, g, w1, w3, w2)


HALO = SUBLANES


def _conv_silu(z_tile, z_halo, w, b, xpad_ref):
    i = pl.program_id(1)
    last = pl.num_programs(1) - 1
    tm = z_tile.shape[0]
    xpad_ref[0:HALO] = jnp.where(i > 0, z_halo[0:HALO], 0.0)
    xpad_ref[HALO:HALO + tm] = z_tile
    xpad_ref[HALO + tm:2 * HALO + tm] = jnp.where(i < last, z_halo[HALO:2 * HALO], 0.0)
    acc = jnp.broadcast_to(b, z_tile.shape)
    for j in range(CONV_K):
        acc = acc + xpad_ref[pl.ds(HALO - CONV_K // 2 + j, tm), :] * w[j:j + 1]
    return _silu(acc)


def _inproj_kernel(h_ref, hp_ref, hn_ref, mod_ref, g_ref, w_ref, gq_ref, gkv_ref, wq1_ref, wq2_ref,
                   wk_ref, wv_ref, cos_ref, sin_ref, gcw_ref, scw_ref, scb_ref, pv_ref,
                   mq_ref, mk_ref, mv_ref, nq_ref, nk_ref, nv_ref, gy_ref, gzg_ref, szg_ref,
                   sy_ref, sma_ref, smb_ref, gpad_ref, spad_ref):
    m = mod_ref[0]
    g = g_ref[...]

    def modulated(x):
        return (_rms(x, g[2:3]) * (1.0 + m[4:5]) + m[3:4]).astype(BF16)

    ub = modulated(h_ref[0])
    uh = modulated(jnp.concatenate([hp_ref[0], hn_ref[0]], axis=0))

    def proj(a, b):
        return _dot(ub, w_ref[:, a:b])

    gy = _conv_silu(proj(C_GQKV, C_GZG), _dot(uh, w_ref[:, C_GQKV:C_GZG]), gcw_ref[...], 0.0, gpad_ref)
    grp = (_iota((GROUP_W, GROUP_W), 0) // HEAD_W == _iota((GROUP_W, GROUP_W), 1) // HEAD_W).astype(BF16)
    q = gy[:, 0:256]
    k = gy[:, 256:512]
    gy_ref[0, :, 0:256] = q * lax.rsqrt(_dot2(q * q, grp) + RMS_EPS) * (HEAD_W ** -0.5)
    gy_ref[0, :, 256:512] = k * lax.rsqrt(_dot2(k * k, grp) + RMS_EPS)
    gy_ref[0, :, 512:768] = gy[:, 512:768]
    sy_ref[0] = _conv_silu(proj(C_SXBC, C_SMALL), _dot(uh, w_ref[:, C_SXBC:C_SMALL]), scw_ref[...],
                           scb_ref[...], spad_ref)
    s = proj(C_SMALL, C_TOTAL)
    pv = pv_ref[...]
    sp = _softplus(s + pv[1:2])
    sma_ref[0] = jnp.where(_iota((1, LANES), 1) < L_G, jax.nn.sigmoid(s), -jnp.exp(pv[0:1]) * sp)
    smb_ref[0] = sp

    zm = proj(C_MLA, C_NAT)
    cqn = _rms(zm[:, 0:256], gq_ref[...]).astype(BF16)
    ckvn = _rms(zm[:, 256:384], gkv_ref[...]).astype(BF16)
    cos = cos_ref[...]
    sin = sin_ref[...]
    k_rope = zm[:, 384:512] * cos + zm[:, 512:640] * sin
    q1 = _dot(cqn, wq1_ref[...])
    q2 = _dot(cqn, wq2_ref[...])
    kn = _dot(ckvn, wk_ref[...])
    vv = _dot(ckvn, wv_ref[...])
    ones_col = (_iota((1, LANES), 1) == HEAD_W).astype(F32)
    for hh in range(N_HEADS):
        sl = slice(hh * LANES, (hh + 1) * LANES)
        mq_ref[0, hh] = ((q1[:, sl] * cos + q2[:, sl] * sin) * MLA_QSCALE).astype(BF16)
        mk_ref[0, hh] = (kn[:, sl] + k_rope).astype(BF16)
        mv_ref[0, hh] = (vv[:, sl] + ones_col).astype(BF16)

    nq_ref[0] = (proj(C_NAT, C_NAT + 256) * NAT_QSCALE).astype(BF16)
    nk_ref[0] = proj(C_NAT + 256, C_NAT + 512).astype(BF16)
    nv_ref[0] = proj(C_NAT + 512, C_NAT + 768).astype(BF16)
    gzg_ref[0] = proj(C_GZG, C_SZG)
    szg_ref[0] = proj(C_SZG, C_SXBC)


def _in_projection(h, mod, g, lw, cos, sin, *, tm):
    nb, nt, dm = h.shape
    per_batch = mod.shape[0] > 1
    hb = tm // HALO
    nhalo = nt // HALO
    tok = lambda w, dt: jax.ShapeDtypeStruct((nb, nt, w), dt)
    head = jax.ShapeDtypeStruct((nb, N_HEADS, nt, LANES), BF16)
    tspec = lambda w: pl.BlockSpec((1, tm, w), lambda b, i: (b, i, 0))
    hspec = pl.BlockSpec((1, N_HEADS, tm, LANES), lambda b, i: (b, 0, i, 0))
    return pl.pallas_call(
        _inproj_kernel,
        out_shape=(head, head, head, tok(256, BF16), tok(256, BF16), tok(256, BF16),
                   tok(768, F32), tok(256, F32), tok(256, F32), tok(768, F32), tok(LANES, F32),
                   tok(LANES, F32)),
        grid=(nb, nt // tm),
        in_specs=[tspec(dm),
                  pl.BlockSpec((1, HALO, dm), lambda b, i: (b, jnp.maximum(i * hb - 1, 0), 0)),
                  pl.BlockSpec((1, HALO, dm), lambda b, i: (b, jnp.minimum((i + 1) * hb, nhalo - 1), 0)),
                  pl.BlockSpec((1, N_MOD, dm), lambda b, i: (b if per_batch else 0, 0, 0)),
                  _full(g.shape), _resident(lw["w_in"].shape),
                  _full(lw["mla_gq"].shape), _full(lw["mla_gkv"].shape),
                  _full(lw["wq1"].shape), _full(lw["wq2"].shape),
                  _full(lw["wk"].shape), _full(lw["wv"].shape),
                  pl.BlockSpec((tm, LANES), lambda b, i: (i, 0)),
                  pl.BlockSpec((tm, LANES), lambda b, i: (i, 0)),
                  _full(lw["gdn_conv_w"].shape), _full(lw["ssm_conv_w"].shape),
                  _full(lw["ssm_conv_b"].shape), _full(lw["gate_pv"].shape)],
        out_specs=(hspec, hspec, hspec, tspec(256), tspec(256), tspec(256),
                   tspec(768), tspec(256), tspec(256), tspec(768), tspec(LANES), tspec(LANES)),
        scratch_shapes=[pltpu.VMEM((tm + 2 * HALO, GDN_QKV), F32), pltpu.VMEM((tm + 2 * HALO, SSM_XBC), F32)],
        compiler_params=pltpu.CompilerParams(vmem_limit_bytes=VMEM_LIMIT),
        name="in_projection",
    )(h, h, h, mod, g, lw["w_in"], lw["mla_gq"], lw["mla_gkv"], lw["wq1"], lw["wq2"], lw["wk"], lw["wv"],
      cos, sin, lw["gdn_conv_w"], lw["ssm_conv_w"], lw["ssm_conv_b"], lw["gate_pv"])


def _mla_kernel(*refs, n_lat_chunks, tk):
    if n_lat_chunks:
        q_ref, kl_ref, vl_ref, kc_ref, vc_ref, o_ref = refs
    else:
        q_ref, kc_ref, vc_ref, o_ref = refs
    q = q_ref[0, 0]
    tq = q.shape[0]

    def step(kb, vb, carry):
        m, acc = carry
        s = _dot_nt(q, kb)
        mn = jnp.maximum(m, jnp.max(s, axis=-1, keepdims=True))
        p = jnp.exp2(s - mn)
        acc = jnp.exp2(m - mn) * acc + _dot(p.astype(BF16), vb)
        return mn, acc

    carry = (jnp.full((tq, 1), NEG_INF, F32), jnp.zeros((tq, LANES), F32))
    if n_lat_chunks:
        def body(j, carry):
            off = pl.multiple_of(j * tk, tk)
            return step(kl_ref[0, 0, pl.ds(off, tk), :], vl_ref[0, 0, pl.ds(off, tk), :], carry)
        carry = lax.fori_loop(0, n_lat_chunks, body, carry, unroll=8)
    _, acc = step(kc_ref[0, 0], vc_ref[0, 0], carry)
    o_ref[0, 0] = acc / acc[:, HEAD_W:HEAD_W + 1]


def _mla_attention(q, k_ctx, v_ctx, k_lat=None, v_lat=None, *, tq, tk):
    nb, nh, nq, _ = q.shape
    nc = k_ctx.shape[2]
    qspec = pl.BlockSpec((1, 1, tq, LANES), lambda b, h, i: (b, h, i, 0))
    cspec = pl.BlockSpec((1, 1, nc, LANES), lambda b, h, i: (b, h, 0, 0))
    if k_lat is None:
        args, specs, n_chunks = (q, k_ctx, v_ctx), [qspec, cspec, cspec], 0
    else:
        nk = k_lat.shape[2]
        lspec = pl.BlockSpec((1, 1, nk, LANES), lambda b, h, i: (b, h, 0, 0))
        args, specs, n_chunks = (q, k_lat, v_lat, k_ctx, v_ctx), [qspec, lspec, lspec, cspec, cspec], nk // tk
    return pl.pallas_call(
        functools.partial(_mla_kernel, n_lat_chunks=n_chunks, tk=tk),
        out_shape=jax.ShapeDtypeStruct((nb, nh, nq, LANES), F32),
        grid=(nb, nh, nq // tq),
        in_specs=specs,
        out_specs=qspec,
        compiler_params=pltpu.CompilerParams(vmem_limit_bytes=VMEM_LIMIT),
        name="mla_attention",
    )(*args)


NAT_QROWS = 4
NAT_QB = NAT_QROWS * GRID_W
NAT_KBLKS = 3


def _heads_attention(q, parts, gout):
    lane_head = _iota((1, GROUP_W), 1) // HEAD_W
    out = jnp.zeros(q.shape, F32)
    for hh in range(N_HEADS):
        sel = lane_head == hh
        qh = jnp.where(sel, q, jnp.zeros_like(q))
        scores = []
        for k, _, bias in parts:
            s = _dot_nt(qh, k)
            scores.append(s if bias is None else s + bias[hh])
        m = functools.reduce(jnp.maximum, [jnp.max(s, axis=-1, keepdims=True) for s in scores])
        ps = [jnp.exp(s - m) for s in scores]
        l = functools.reduce(jnp.add, [jnp.sum(p, axis=-1, keepdims=True) for p in ps])
        o = functools.reduce(jnp.add, [_dot(p.astype(BF16), v) for p, (_, v, _) in zip(ps, parts)])
        out = jnp.where(sel, o / l, out)
    return _rms(out, gout)


def _nat_kernel(q_ref, k0_ref, k1_ref, k2_ref, v0_ref, v1_ref, v2_ref, kc_ref, vc_ref, bias_ref, g_ref,
                o_ref):
    kw = jnp.concatenate([k0_ref[0], k1_ref[0], k2_ref[0]], axis=0)
    vw = jnp.concatenate([v0_ref[0], v1_ref[0], v2_ref[0]], axis=0)
    parts = [(kw, vw, bias_ref[0]), (kc_ref[0], vc_ref[0], None)]
    o_ref[0] = _heads_attention(q_ref[0], parts, g_ref[...])


def _nat_ctx_kernel(q_ref, k_ref, v_ref, g_ref, o_ref):
    o_ref[0] = _heads_attention(q_ref[0], [(k_ref[0], v_ref[0], None)], g_ref[...])


def _nat_attention(q, k, v, kc, vc, bias, gout):
    nb, nt, _ = q.shape
    nblk = nt // NAT_QB
    nc = kc.shape[1]
    start = lambda i: jnp.clip(i - 1, 0, nblk - NAT_KBLKS)
    variant = lambda i: jnp.where(i == 0, 0, jnp.where(i == nblk - 1, 2, 1))
    qspec = pl.BlockSpec((1, NAT_QB, GROUP_W), lambda b, i: (b, i, 0))
    kspecs = [pl.BlockSpec((1, NAT_QB, GROUP_W), lambda b, i, j=j: (b, start(i) + j, 0))
              for j in range(NAT_KBLKS)]
    cspec = pl.BlockSpec((1, nc, GROUP_W), lambda b, i: (b, 0, 0))
    bspec = pl.BlockSpec((1, N_HEADS, NAT_QB, NAT_KBLKS * NAT_QB), lambda b, i: (variant(i), 0, 0, 0))
    return pl.pallas_call(
        _nat_kernel,
        out_shape=jax.ShapeDtypeStruct((nb, nt, GROUP_W), F32),
        grid=(nb, nblk),
        in_specs=[qspec] + kspecs + kspecs + [cspec, cspec, bspec, _full(gout.shape)],
        out_specs=qspec,
        name="nat_attention",
    )(q, k, k, k, v, v, v, kc, vc, bias, gout)


def _nat_ctx_attention(q, k, v, gout):
    nb, nc, _ = q.shape
    spec = pl.BlockSpec((1, nc, GROUP_W), lambda b: (b, 0, 0))
    return pl.pallas_call(
        _nat_ctx_kernel,
        out_shape=jax.ShapeDtypeStruct((nb, nc, GROUP_W), F32),
        grid=(nb,),
        in_specs=[spec, spec, spec, _full(gout.shape)],
        out_specs=spec,
        name="nat_ctx_attention",
    )(q, k, v, gout)


def _nat_bias_constants(n_rows):
    krows = NAT_KBLKS * NAT_QROWS
    qr = np.arange(NAT_QROWS)[:, None]
    kk = np.arange(krows)[None, :]
    cq = np.arange(GRID_W)[:, None]
    ck = np.arange(GRID_W)[None, :]
    c0 = np.clip(cq - NAT_KC // 2, 0, GRID_W - NAT_KC)
    col_ok = ((ck >= c0) & (ck < c0 + NAT_KC)).reshape(-1)
    dc = np.clip(ck - cq + (NAT_KC - 1), 0, 2 * NAT_KC - 2).reshape(-1)
    e_col = np.zeros((2 * NAT_KC, GRID_W * GRID_W), np.float32)
    e_col[dc, np.arange(GRID_W * GRID_W)] = 1.0
    big = 4 * n_rows + 64
    placements = [(0, 0, n_rows), (big // 2, big // 2 - NAT_QROWS, big),
                  (n_rows - NAT_QROWS, n_rows - krows, n_rows)]
    e_row = np.zeros((3, NAT_QROWS * krows, 2 * NAT_KR), np.float32)
    ok = np.zeros((3, NAT_QROWS * krows, GRID_W * GRID_W), np.float32)
    for v, (r_base, k_start, rows_total) in enumerate(placements):
        r = r_base + qr
        k_abs = k_start + kk
        r0 = np.clip(r - NAT_KR // 2, 0, rows_total - NAT_KR)
        row_ok = ((k_abs >= r0) & (k_abs < r0 + NAT_KR)).reshape(-1)
        dr = np.clip(k_abs - r + (NAT_KR - 1), 0, 2 * NAT_KR - 2).reshape(-1)
        e_row[v, np.arange(NAT_QROWS * krows), dr] = 1.0
        ok[v] = row_ok[:, None] & col_ok[None, :]
    return e_row, e_col, ok


def _split3(x):
    h1 = x.astype(BF16)
    r1 = x - h1.astype(F32)
    h2 = r1.astype(BF16)
    h3 = (r1 - h2.astype(F32)).astype(BF16)
    return h1, h2, h3


def _nat_bias_kernel(er_ref, rpb_ref, ec_ref, ok_ref, o_ref):
    er = er_ref[0]
    rows = functools.reduce(jnp.add, [_dot(er, p) for p in _split3(rpb_ref[0, 0])])
    ec = ec_ref[...]
    b = functools.reduce(jnp.add, [_dot(p, ec) for p in _split3(rows)])
    o_ref[0, 0, 0] = jnp.where(ok_ref[0] > 0.0, b, NEG_INF)


def _nat_bias_tables(rpb_all, n_rows):
    nl, nh = rpb_all.shape[:2]
    krows = NAT_KBLKS * NAT_QROWS
    e_row, e_col, ok = _nat_bias_constants(n_rows)
    rpb = jnp.pad(rpb_all, ((0, 0), (0, 0), (0, 1), (0, 1)))
    nr, ncol = NAT_QROWS * krows, GRID_W * GRID_W
    out = pl.pallas_call(
        _nat_bias_kernel,
        out_shape=jax.ShapeDtypeStruct((nl, 3, nh, nr, ncol), F32),
        grid=(nl, 3, nh),
        in_specs=[pl.BlockSpec((1, nr, 2 * NAT_KR), lambda l, v, h: (v, 0, 0)),
                  pl.BlockSpec((1, 1, 2 * NAT_KR, 2 * NAT_KC), lambda l, v, h: (l, h, 0, 0)),
                  pl.BlockSpec((2 * NAT_KC, ncol), lambda l, v, h: (0, 0)),
                  pl.BlockSpec((1, nr, ncol), lambda l, v, h: (v, 0, 0))],
        out_specs=pl.BlockSpec((1, 1, 1, nr, ncol), lambda l, v, h: (l, v, h, 0, 0)),
        name="nat_bias_tables",
    )(jnp.asarray(e_row, BF16), rpb, jnp.asarray(e_col, BF16), jnp.asarray(ok))
    out = out.reshape(nl, 3, nh, NAT_QROWS, krows, GRID_W, GRID_W).transpose(0, 1, 2, 3, 5, 4, 6)
    return out.reshape(nl, 3, nh, NAT_QB, krows * GRID_W)


def _chunk_cumsum_matrix(n, chunk, reverse):
    r = _iota((n, n), 0)
    c = _iota((n, n), 1)
    same = (r // chunk) == (c // chunk)
    return (same & ((c >= r) if reverse else (c <= r))).astype(BF16)


def _expand_matrix(lane0, group, width):
    return (_iota((LANES, width), 0) == lane0 + _iota((LANES, width), 1) // group).astype(BF16)


def _row_form(col_vals, chunk):
    pick = _iota(col_vals.shape, 0) == (_iota(col_vals.shape, 1) % chunk)
    ones = jnp.ones((SUBLANES, chunk), BF16)
    return _dot2l(ones, jnp.where(pick, col_vals, 0.0))[0:1]


def _heads_diag(x):
    return _block_diag(x, N_HEADS, GDN_CHUNK, HEAD_W)


def _heads_undiag(x):
    keep = (_iota(x.shape, 0) // GDN_CHUNK) == (_iota(x.shape, 1) // HEAD_W)
    x = jnp.where(keep, x, 0.0)
    return functools.reduce(jnp.add, [x[hh * GDN_CHUNK:(hh + 1) * GDN_CHUNK] for hh in range(N_HEADS)])


def _pc3(x, y):
    xh, xl = _split(x)
    yh, yl = _split(y)
    r = _dot(jnp.concatenate([xh, xl], axis=0), _heads_diag(yh))
    n = x.shape[0]
    return r[0:n] + r[n:2 * n] + _dot(xh, _heads_diag(yl))


def _pc1(x, y):
    return _dot(x.astype(BF16), _heads_diag(y.astype(BF16)))


def _gdn_chunk_kernel(y_ref, sm_ref, g_ref, qs_ref, qe_ref, o0_ref, gl_ref, *, d, n_chunks):
    ch = GDN_CHUNK
    cb = n_chunks * ch
    reverse = d == 1
    sm = sm_ref[0]
    tri = _chunk_cumsum_matrix(ch, ch, reverse)
    cs = jnp.concatenate([_dot2l(tri, sm[c * ch:(c + 1) * ch]) for c in range(n_chunks)], axis=0)
    gc_all = _dot2(cs, _expand_matrix(L_G + N_HEADS * d, HEAD_W, GROUP_W))
    beta_all = _dot2(sm, _expand_matrix(L_BETA + N_HEADS * d, HEAD_W, GROUP_W))
    y = y_ref[0]
    q_all, k_all, v_all = y[:, 0:256], y[:, 256:512], y[:, 512:768]
    kb_all = k_all * beta_all
    vb_all = v_all * beta_all
    eg_all = jnp.exp(gc_all)

    ti = _iota((ch, GROUP_W), 0)
    tj = _iota((ch, GROUP_W), 1) % ch
    incl = (tj >= ti) if reverse else (tj <= ti)
    strict = (tj > ti) if reverse else (tj < ti)
    last = 0 if reverse else ch - 1
    cs_ = range(n_chunks)
    rows = [slice(c * ch, (c + 1) * ch) for c in cs_]

    gc = [gc_all[r] for r in rows]
    decay = [jnp.where(incl, jnp.exp(jnp.where(incl, g - _row_form(g, ch), 0.0)), 0.0) for g in gc]
    qk = [_dot_nt(jnp.concatenate([kb_all[r], q_all[r]], axis=0).astype(BF16),
                  _heads_diag(k_all[r].astype(BF16))) for r in rows]
    a_mat = [jnp.where(strict, x[0:ch] * dc, 0.0) for x, dc in zip(qk, decay)]
    a_intra = [(x[ch:2 * ch] * dc).astype(BF16) for x, dc in zip(qk, decay)]

    base = SUBLANES
    eye = jnp.where(ti == tj, 1.0, 0.0)
    m = [jnp.where(ti // base == tj // base, -a, 0.0) for a in a_mat]
    t = [eye + x for x in m]
    for _ in range(2):
        m = [_pc1(x, x) for x in m]
        t = [x + _pc1(x, p) for x, p in zip(t, m)]
    size = 2 * base
    while size <= ch:
        off = (ti // size == tj // size) & (ti // (size // 2) != tj // (size // 2))
        ct = [_pc1(jnp.where(off, a, 0.0), x) for a, x in zip(a_mat, t)]
        t = [x - _pc1(x, p) for x, p in zip(t, ct)]
        size *= 2
    resid = [eye - x - _pc3(a, x) for a, x in zip(a_mat, t)]
    t = [x + _pc1(x, r) for x, r in zip(t, resid)]

    u = [_pc1(x, vb_all[r]) for x, r in zip(t, rows)]
    w = [_pc1(x, kb_all[r] * eg_all[r]) for x, r in zip(t, rows)]
    for c in cs_:
        r = rows[c]
        g_last = gc[c][last:last + 1]
        k_dec = (k_all[r] * jnp.exp(g_last - gc[c])).astype(BF16)
        wu = jnp.concatenate([w[c], u[c]], axis=1).astype(BF16)
        full = _dot_tn(k_dec, wu)
        g_ref[0, r, :] = _heads_undiag(full[:, 0:GROUP_W]).astype(BF16)
        qs_ref[0, r, :] = _heads_undiag(full[:, GROUP_W:])
        wu_bd = jnp.concatenate([_heads_diag(wu[:, 0:GROUP_W]), _heads_diag(wu[:, GROUP_W:])], axis=1)
        aw = _dot(a_intra[c], wu_bd)
        qe_ref[0, r, :] = (q_all[r] * eg_all[r] - aw[:, 0:GROUP_W]).astype(BF16)
        o0_ref[0, r, :] = aw[:, GROUP_W:]
        gl_ref[0, c:c + 1, :] = jnp.exp(g_last)


def _gdn_state_kernel(g_ref, qs_ref, qe_ref, o0_ref, gl_ref, s0_ref, o_ref, sfin_ref, state_ref,
                      *, d, n_chunks):
    ch = GDN_CHUNK
    nb = g_ref.shape[0]
    step = pl.program_id(0)

    @pl.when(step == 0)
    def _():
        state_ref[...] = s0_ref[...]

    order = range(n_chunks - 1, -1, -1) if d == 1 else range(n_chunks)
    state = [state_ref[b] for b in range(nb)]
    for c in order:
        r = slice(c * ch, (c + 1) * ch)
        for b in range(nb):
            both = jnp.concatenate([g_ref[b, r, :], qe_ref[b, r, :]], axis=0)
            res = _dot(both, _heads_diag(state[b].astype(BF16)))
            o_ref[b, r, :] = res[ch:2 * ch] + o0_ref[b, r, :]
            state[b] = state[b] * gl_ref[b, c:c + 1, :] - res[0:ch] + qs_ref[b, r, :]
    for b in range(nb):
        state_ref[b] = state[b]

    @pl.when(step == pl.num_programs(0) - 1)
    def _():
        for b in range(nb):
            sfin_ref[b] = state[b]


def _gdn_chunk_call(y, sm, *, d, cb):
    nb, nt, nc = y.shape
    n_chunks = cb // GDN_CHUNK
    tok = lambda dt: jax.ShapeDtypeStruct((nb, nt, GROUP_W), dt)
    tspec = pl.BlockSpec((1, cb, GROUP_W), lambda bb, i: (bb, i, 0))
    return pl.pallas_call(
        functools.partial(_gdn_chunk_kernel, d=d, n_chunks=n_chunks),
        out_shape=(tok(BF16), tok(F32), tok(BF16), tok(F32),
                   jax.ShapeDtypeStruct((nb, nt // GDN_CHUNK, GROUP_W), F32)),
        grid=(nb, nt // cb),
        in_specs=[pl.BlockSpec((1, cb, nc), lambda bb, i: (bb, i, 0)),
                  pl.BlockSpec((1, cb, LANES), lambda bb, i: (bb, i, 0))],
        out_specs=(tspec, tspec, tspec, tspec,
                   pl.BlockSpec((1, n_chunks, GROUP_W), lambda bb, i: (bb, i, 0))),
        name="gdn_chunk",
    )(y, sm)


def _gdn_state_call(parts, s0, *, d, cb):
    g, qs, qe, o0, gl = parts
    nb, nt, _ = g.shape
    n_chunks = cb // GDN_CHUNK
    nblk = nt // cb
    blk = (lambda i: (0, nblk - 1 - i, 0)) if d == 1 else (lambda i: (0, i, 0))
    tspec = pl.BlockSpec((nb, cb, GROUP_W), blk)
    sspec = pl.BlockSpec((nb, GDN_CHUNK, GROUP_W), lambda i: (0, 0, 0))
    return pl.pallas_call(
        functools.partial(_gdn_state_kernel, d=d, n_chunks=n_chunks),
        out_shape=(jax.ShapeDtypeStruct((nb, nt, GROUP_W), F32),
                   jax.ShapeDtypeStruct((nb, GDN_CHUNK, GROUP_W), F32)),
        grid=(nblk,),
        in_specs=[tspec, tspec, tspec, tspec, pl.BlockSpec((nb, n_chunks, GROUP_W), blk), sspec],
        out_specs=(tspec, sspec),
        scratch_shapes=[pltpu.VMEM((nb, GDN_CHUNK, GROUP_W), F32)],
        compiler_params=pltpu.CompilerParams(dimension_semantics=("arbitrary",)),
        name="gdn_state_scan",
    )(g, qs, qe, o0, gl, s0)


def _gdn_bidir(y_l, sm_l, y_c, sm_c, *, cb_l, cb_c):
    zero = jnp.zeros((y_l.shape[0], GDN_CHUNK, GROUP_W), F32)
    outs = []
    for d in (0, 1):
        o_c, s_c = _gdn_state_call(_gdn_chunk_call(y_c, sm_c, d=d, cb=cb_c), zero, d=d, cb=cb_c)
        o_l, _ = _gdn_state_call(_gdn_chunk_call(y_l, sm_l, d=d, cb=cb_l), s_c, d=d, cb=cb_l)
        outs.append((o_l, o_c))
    return outs


def _ssd_kernel(x_ref, dt_ref, da_ref, s0_ref, o_ref, sfin_ref, state_ref, *, d, n_chunks):
    ch = SSM_CHUNK
    cb = n_chunks * ch
    reverse = d == 1
    step = pl.program_id(1)

    @pl.when(step == 0)
    def _():
        state_ref[...] = s0_ref[0]

    lane0 = L_DT + N_HEADS * d
    tri = _chunk_cumsum_matrix(ch, ch, reverse)
    da = da_ref[0]
    cs = jnp.concatenate([_dot2l(tri, da[c * ch:(c + 1) * ch]) for c in range(n_chunks)], axis=0)
    e64 = _expand_matrix(lane0, HEAD_W, GROUP_W)
    ac_all = _dot2(cs, e64)
    ac5_all = _dot2(cs, _expand_matrix(lane0, ch, N_HEADS * ch))
    dt_all = _dot2(dt_ref[0], e64)
    xbc = x_ref[0]
    xdt_all = xbc[:, 0:256] * dt_all
    b_all, c_all = xbc[:, 256:512], xbc[:, 512:768]

    ti = _iota((ch, N_HEADS * ch), 0)
    tj = _iota((ch, N_HEADS * ch), 1) % ch
    incl = (tj >= ti) if reverse else (tj <= ti)
    grp_keep = (_iota((GROUP_W, GROUP_W), 0) // ch) == (_iota((GROUP_W, GROUP_W), 1) // ch)
    last = 0 if reverse else ch - 1

    rows = [slice(c * ch, (c + 1) * ch) for c in range(n_chunks)]
    seg = [jnp.where(incl, jnp.exp(jnp.where(incl, ac5_all[r] - _row_form(ac5_all[r], ch), 0.0)), 0.0)
           for r in rows]
    cm = [c_all[r].astype(BF16) for r in rows]
    bmb = [b_all[r].astype(BF16) for r in rows]
    cb_g = [_dot_nt(c_, _block_diag(b_, 2, ch, SSM_STATE)) for c_, b_ in zip(cm, bmb)]
    scores = [(jnp.concatenate([x[:, 0:ch], x[:, 0:ch], x[:, ch:], x[:, ch:]], axis=1) * sg).astype(BF16)
              for x, sg in zip(cb_g, seg)]
    xdtb = [xdt_all[r].astype(BF16) for r in rows]
    y_diag = [_dot(s_, _block_diag(x_, N_HEADS, ch, HEAD_W)) for s_, x_ in zip(scores, xdtb)]
    a_last = [ac_all[r][last:last + 1] for r in rows]
    states = [jnp.where(grp_keep, _dot_tn(b_, (xdt_all[r] * jnp.exp(al - ac_all[r])).astype(BF16)), 0.0)
              for b_, r, al in zip(bmb, rows, a_last)]
    state = state_ref[...]
    for c in (range(n_chunks - 1, -1, -1) if reverse else range(n_chunks)):
        o_ref[0, rows[c], :] = y_diag[c] + _dot(cm[c], state.astype(BF16)) * jnp.exp(ac_all[rows[c]])
        state = state * jnp.exp(a_last[c]) + states[c]
    state_ref[...] = state

    @pl.when(step == pl.num_programs(1) - 1)
    def _():
        sfin_ref[0] = state_ref[...]


def _scan_call(kernel, name, seq, smalls, s0, *, d, cb, chunk):
    nb, nt, nc = seq.shape
    nblk = nt // cb
    blk = (lambda bb, i: (bb, nblk - 1 - i, 0)) if d == 1 else (lambda bb, i: (bb, i, 0))
    sspec = pl.BlockSpec((1, GROUP_W, GROUP_W), lambda bb, i: (bb, 0, 0))
    return pl.pallas_call(
        functools.partial(kernel, d=d, n_chunks=cb // chunk),
        out_shape=(jax.ShapeDtypeStruct((nb, nt, GROUP_W), F32),
                   jax.ShapeDtypeStruct((nb, GROUP_W, GROUP_W), F32)),
        grid=(nb, nblk),
        in_specs=[pl.BlockSpec((1, cb, nc), blk)] + [pl.BlockSpec((1, cb, LANES), blk) for _ in smalls]
                 + [sspec],
        out_specs=(pl.BlockSpec((1, cb, GROUP_W), blk), sspec),
        scratch_shapes=[pltpu.VMEM((GROUP_W, GROUP_W), F32)],
        compiler_params=pltpu.CompilerParams(dimension_semantics=("arbitrary", "arbitrary")),
        name=name,
    )(seq, *smalls, s0)


def _bidir_scan(kernel, name, seq_l, smalls_l, seq_c, smalls_c, *, cb_l, cb_c, chunk):
    zero = jnp.zeros((seq_l.shape[0], GROUP_W, GROUP_W), F32)
    outs = []
    for d in (0, 1):
        o_c, s_c = _scan_call(kernel, name, seq_c, smalls_c, zero, d=d, cb=cb_c, chunk=chunk)
        o_l, _ = _scan_call(kernel, name, seq_l, smalls_l, s_c, d=d, cb=cb_l, chunk=chunk)
        outs.append((o_l, o_c))
    return outs


def _outproj_kernel(h_ref, mod_ref, g_ref, mla_ref, nat_ref, gof_ref, gob_ref, gzg_ref,
                    yf_ref, yb_ref, sx_ref, szg_ref, pv_ref, wm_ref, wr_ref, o_ref):
    pv = pv_ref[...]
    valid = _iota((1, LANES), 1) < HEAD_W
    slabs = [jnp.where(valid, mla_ref[0, hh], 0.0) for hh in range(N_HEADS)]
    ss = functools.reduce(jnp.add, [jnp.sum(s * s, axis=-1, keepdims=True) for s in slabs])
    scale = lax.rsqrt(ss / GROUP_W + RMS_EPS)
    y = None
    for hh in range(N_HEADS):
        part = _dot((slabs[hh] * scale * pv[hh:hh + 1, 0:LANES]).astype(BF16), wm_ref[hh])
        y = part if y is None else y + part
    y = y + _dot(nat_ref[0].astype(BF16), wr_ref[0:256, :])
    grp = (_iota((GROUP_W, GROUP_W), 0) // HEAD_W == _iota((GROUP_W, GROUP_W), 1) // HEAD_W).astype(BF16)
    o = gof_ref[0] + gob_ref[0]
    on = o * lax.rsqrt(_dot2(o * o, grp) / HEAD_W + RMS_EPS) * pv[4:5]
    y = y + _dot((on * _silu(gzg_ref[0])).astype(BF16), wr_ref[256:512, :])
    s = (yf_ref[0] + yb_ref[0] + sx_ref[0] * pv[5:6]) * _silu(szg_ref[0])
    sn = jnp.concatenate([_rms(s[:, 0:LANES], pv[6:7, 0:LANES]),
                          _rms(s[:, LANES:], pv[6:7, LANES:])], axis=1)
    y = y + _dot(sn.astype(BF16), wr_ref[512:768, :])
    m = mod_ref[0]
    g = g_ref[...]
    o_ref[0] = h_ref[0] + m[5:6] * _rms(y, g[3:4])


def _out_projection(h, mod, g, mla_o, nat_o, gdn_f, gdn_b, gdn_zg, ssm_f, ssm_b, ssm_xbc, ssm_zg,
                    pv, w_mla, w_rest, *, tm):
    nb, nt, dm = h.shape
    per_batch = mod.shape[0] > 1
    tspec = lambda w: pl.BlockSpec((1, tm, w), lambda b, i: (b, i, 0))
    return pl.pallas_call(
        _outproj_kernel,
        out_shape=jax.ShapeDtypeStruct(h.shape, F32),
        grid=(nb, nt // tm),
        in_specs=[tspec(dm),
                  pl.BlockSpec((1, N_MOD, dm), lambda b, i: (b if per_batch else 0, 0, 0)),
                  _full(g.shape),
                  pl.BlockSpec((1, N_HEADS, tm, LANES), lambda b, i: (b, 0, i, 0)),
                  tspec(256), tspec(256), tspec(256), tspec(256), tspec(256), tspec(256),
                  tspec(256),
                  tspec(256),
                  _full(pv.shape), _full(w_mla.shape), _full(w_rest.shape)],
        out_specs=tspec(dm),
        name="out_projection",
    )(h, mod, g, mla_o, nat_o, gdn_f, gdn_b, gdn_zg, ssm_f, ssm_b, ssm_xbc, ssm_zg, pv, w_mla, w_rest)


_ROPE_SWAP = np.array(list(range(8, 16)) + list(range(0, 8)) + list(range(24, 32)) + list(range(16, 24)))


def _pack_layer(p, l):
    w_in = p["w_in"][l]
    dm = w_in.shape[0]
    o_nat = MLA_Q_LORA + MLA_KV_LORA + MLA_ROPE
    o_gdn = o_nat + 768
    o_ssm = o_gdn + GDN_QKV + GROUP_W + 4 * N_HEADS
    kr = w_in[:, MLA_Q_LORA + MLA_KV_LORA:o_nat]
    z64 = jnp.zeros((dm, 64), F32)
    z32 = jnp.zeros((dm, 32), F32)
    small = jnp.concatenate([w_in[:, o_gdn + 1024:o_gdn + 1040],
                             w_in[:, o_ssm + 1024:o_ssm + 1032],
                             jnp.zeros((dm, LANES - 24), F32)], axis=1)
    w_packed = jnp.concatenate([
        w_in[:, 0:384], z64, kr, z32, z64, kr[:, _ROPE_SWAP], z32,
        w_in[:, o_nat:o_gdn],
        w_in[:, o_gdn:o_gdn + 1024],
        w_in[:, o_ssm:o_ssm + 1024],
        small], axis=1).astype(BF16)
    assert w_packed.shape[1] == C_TOTAL

    wuq = p["mla_wuq"][l].reshape(MLA_Q_LORA, N_HEADS, MLA_NOPE + MLA_ROPE)
    zq = jnp.zeros((MLA_Q_LORA, N_HEADS, 32), F32)
    wq1 = jnp.concatenate([wuq, zq], axis=2).reshape(MLA_Q_LORA, N_HEADS * LANES)
    wq2 = jnp.concatenate([jnp.zeros((MLA_Q_LORA, N_HEADS, 64), F32), wuq[:, :, MLA_NOPE:][:, :, _ROPE_SWAP], zq],
                          axis=2).reshape(MLA_Q_LORA, N_HEADS * LANES)
    wukv = p["mla_wukv"][l].reshape(MLA_KV_LORA, N_HEADS, MLA_NOPE + HEAD_W)
    zk = jnp.zeros((MLA_KV_LORA, N_HEADS, 64), F32)
    wk = jnp.concatenate([wukv[:, :, :MLA_NOPE], zk], axis=2).reshape(MLA_KV_LORA, N_HEADS * LANES)
    wv = jnp.concatenate([wukv[:, :, MLA_NOPE:], zk], axis=2).reshape(MLA_KV_LORA, N_HEADS * LANES)

    def lanes(vals, lane0):
        v = vals.reshape(-1)
        return jnp.zeros((LANES,), F32).at[lane0:lane0 + v.shape[0]].set(v)

    w_out = p["w_out"][l]
    w_mla = jnp.concatenate([w_out[0:256].reshape(N_HEADS, HEAD_W, dm),
                             jnp.zeros((N_HEADS, LANES - HEAD_W, dm), F32)], axis=1).astype(BF16)
    gout = jnp.concatenate([p["mla_gout"][l].reshape(N_HEADS, HEAD_W),
                            jnp.zeros((N_HEADS, GROUP_W - HEAD_W), F32)], axis=1)
    pv_out = jnp.concatenate([gout,
                              jnp.tile(p["gdn_gnorm"][l], N_HEADS)[None],
                              jnp.repeat(p["ssm_d"][l], HEAD_W)[None],
                              p["ssm_gnorm"][l][None],
                              jnp.zeros((1, GROUP_W), F32)], axis=0)
    return dict(
        ffn1=(p["ffn1_w1"][l].astype(BF16), p["ffn1_w3"][l].astype(BF16), p["ffn1_w2"][l].astype(BF16)),
        ffn2=(p["ffn2_w1"][l].astype(BF16), p["ffn2_w3"][l].astype(BF16), p["ffn2_w2"][l].astype(BF16)),
        g=p["norm_g"][l],
        w_in=w_packed, wq1=wq1.astype(BF16), wq2=wq2.astype(BF16), wk=wk.astype(BF16), wv=wv.astype(BF16),
        mla_gq=p["mla_gq"][l][None], mla_gkv=p["mla_gkv"][l][None],
        nat_bias=p["nat_bias"][l], nat_gout=p["nat_gout"][l][None],
        gdn_conv_w=p["gdn_conv_w"][l],
        ssm_conv_w=p["ssm_conv_w"][l], ssm_conv_b=p["ssm_conv_b"][l][None],
        gate_pv=jnp.stack([lanes(p["gdn_a_log"][l], L_G) + lanes(p["ssm_a_log"][l], L_DT),
                           lanes(p["gdn_dt_bias"][l], L_G) + lanes(p["ssm_dt_bias"][l], L_DT)]),
        pv_out=pv_out, w_mla=w_mla, w_rest=w_out[256:].astype(BF16),
    )


def _rope_tables(n_tok):
    pos = jnp.arange(n_tok)
    rows = (pos // GRID_W).astype(F32)
    cols = (pos % GRID_W).astype(F32)
    quarter = MLA_ROPE // 4
    freqs = ROPE_THETA ** (-jnp.arange(quarter, dtype=F32) / quarter)
    ar = rows[:, None] * freqs
    ac = cols[:, None] * freqs
    cos = jnp.concatenate([jnp.cos(ar), jnp.cos(ar), jnp.cos(ac), jnp.cos(ac)], axis=1)
    sin = jnp.concatenate([-jnp.sin(ar), jnp.sin(ar), -jnp.sin(ac), jnp.sin(ac)], axis=1)
    ones = jnp.ones((n_tok, MLA_NOPE), F32)
    zeros = jnp.zeros((n_tok, MLA_NOPE), F32)
    pad = jnp.zeros((n_tok, LANES - MLA_NOPE - MLA_ROPE), F32)
    return jnp.concatenate([ones, cos, pad], axis=1), jnp.concatenate([zeros, sin, pad], axis=1)


def _tiles(n_tok):
    return dict(tm=min(512, n_tok), tm_ffn=min(512, n_tok), tq=min(1024, n_tok),
                cb_gdn=min(512, n_tok), cb_ssd=min(512, n_tok))


def _mixers(hl, hc, ml, mc, lw, ropes, need_ctx):
    tl, tc = _tiles(hl.shape[1]), _tiles(hc.shape[1])
    (cos_l, sin_l), (cos_c, sin_c) = ropes
    zl = _in_projection(hl, ml, lw["g"], lw, cos_l, sin_l, tm=tl["tm"])
    zc = _in_projection(hc, mc, lw["g"], lw, cos_c, sin_c, tm=tc["tm"])
    (mq_l, mk_l, mv_l, nq_l, nk_l, nv_l, gy_l, gzg_l, szg_l, sy_l, sma_l, smb_l) = zl
    (mq_c, mk_c, mv_c, nq_c, nk_c, nv_c, gy_c, gzg_c, szg_c, sy_c, sma_c, smb_c) = zc

    n_lat = hl.shape[1]
    tk = min(1024, n_lat)
    mla_l = _mla_attention(mq_l, mk_c, mv_c, mk_l, mv_l, tq=tl["tq"], tk=tk)
    nat_l = _nat_attention(nq_l, nk_l, nv_l, nk_c, nv_c, lw["nat_bias"], lw["nat_gout"])
    (gf_l, gf_c), (gb_l, gb_c) = _gdn_bidir(gy_l, sma_l, gy_c, sma_c, cb_l=tl["cb_gdn"], cb_c=tc["cb_gdn"])
    (sf_l, sf_c), (sb_l, sb_c) = _bidir_scan(_ssd_kernel, "ssd_scan", sy_l, (smb_l, sma_l),
                                             sy_c, (smb_c, sma_c),
                                             cb_l=tl["cb_ssd"], cb_c=tc["cb_ssd"], chunk=SSM_CHUNK)

    hl = _out_projection(hl, ml, lw["g"], mla_l, nat_l, gf_l, gb_l, gzg_l, sf_l, sb_l, sy_l, szg_l,
                         lw["pv_out"], lw["w_mla"], lw["w_rest"], tm=tl["tm"])
    if need_ctx:
        mla_c = _mla_attention(mq_c, mk_c, mv_c, tq=tc["tq"], tk=tk)
        nat_c = _nat_ctx_attention(nq_c, nk_c, nv_c, lw["nat_gout"])
        hc = _out_projection(hc, mc, lw["g"], mla_c, nat_c, gf_c, gb_c, gzg_c, sf_c, sb_c, sy_c, szg_c,
                             lw["pv_out"], lw["w_mla"], lw["w_rest"], tm=tc["tm"])
    return hl, hc


def _layer(hl, hc, ml, mc, lw, ropes, need_ctx):
    tl, tc = _tiles(hl.shape[1]), _tiles(hc.shape[1])
    hl = _half_ffn(hl, ml, lw["g"], *lw["ffn1"], k0=0, gp=0, tm=tl["tm_ffn"])
    hc = _half_ffn(hc, mc, lw["g"], *lw["ffn1"], k0=0, gp=0, tm=tc["tm"])
    hl, hc = _mixers(hl, hc, ml, mc, lw, ropes, need_ctx)
    hl = _half_ffn(hl, ml, lw["g"], *lw["ffn2"], k0=6, gp=4, tm=tl["tm_ffn"])
    if need_ctx:
        hc = _half_ffn(hc, mc, lw["g"], *lw["ffn2"], k0=6, gp=4, tm=tc["tm"])
    return hl, hc


def kernel(x, c, ctx, c_ctx, w_mod, b_mod, norm_g, ffn1_w1, ffn1_w3, ffn1_w2, ffn2_w1, ffn2_w3, ffn2_w2,
           w_in, w_out, mla_gq, mla_gkv, mla_wuq, mla_wukv, mla_gout, nat_rpb, nat_gout, gdn_conv_w,
           gdn_a_log, gdn_dt_bias, gdn_gnorm, ssm_conv_w, ssm_conv_b, ssm_a_log, ssm_dt_bias, ssm_d,
           ssm_gnorm):
    p = dict(norm_g=norm_g, ffn1_w1=ffn1_w1, ffn1_w3=ffn1_w3, ffn1_w2=ffn1_w2, ffn2_w1=ffn2_w1,
             ffn2_w3=ffn2_w3, ffn2_w2=ffn2_w2, w_in=w_in, w_out=w_out, mla_gq=mla_gq, mla_gkv=mla_gkv,
             mla_wuq=mla_wuq, mla_wukv=mla_wukv, mla_gout=mla_gout, nat_rpb=nat_rpb, nat_gout=nat_gout,
             gdn_conv_w=gdn_conv_w, gdn_a_log=gdn_a_log, gdn_dt_bias=gdn_dt_bias, gdn_gnorm=gdn_gnorm,
             ssm_conv_w=ssm_conv_w, ssm_conv_b=ssm_conv_b, ssm_a_log=ssm_a_log, ssm_dt_bias=ssm_dt_bias,
             ssm_d=ssm_d, ssm_gnorm=ssm_gnorm)
    nb, n_lat, dm = x.shape
    n_ctx = ctx.shape[1]
    depth = w_mod.shape[0]
    cvec = jnp.concatenate([c, c_ctx[None], jnp.zeros((SUBLANES - nb - 1, dm), F32)], axis=0)
    mods = _modulation(cvec, w_mod, b_mod).reshape(depth, SUBLANES, N_MOD, dm)
    cos_c = jnp.concatenate([jnp.ones((n_ctx, MLA_NOPE + MLA_ROPE), F32),
                             jnp.zeros((n_ctx, LANES - MLA_NOPE - MLA_ROPE), F32)], axis=1)
    ropes = (_rope_tables(n_lat), (cos_c, jnp.zeros((n_ctx, LANES), F32)))
    p["nat_bias"] = _nat_bias_tables(nat_rpb, n_lat // GRID_W)
    hl, hc = x, ctx
    for l in range(depth):
        lw = _pack_layer(p, l)
        hl, hc = _layer(hl, hc, mods[l, 0:nb], mods[l, nb:nb + 1], lw, ropes, need_ctx=l < depth - 1)
    return hl
```

```python
import functools
import math

import jax
import jax.numpy as jnp
import numpy as np
from jax import lax
from jax.experimental import pallas as pl
from jax.experimental.pallas import tpu as pltpu

F32 = jnp.float32
BF16 = jnp.bfloat16

D_MODEL = 1024
DEPTH = 4
GRID_W = 64
N_MOD = 9
D_FF = 2816
RMS_EPS = 1e-6
NEG_INF = -1e30
ROPE_THETA = 10000.0
GROUP_W = 256
N_HEADS = 4
HEAD_W = 64
MLA_NOPE = 64
MLA_ROPE = 32
MLA_Q_LORA = 256
MLA_KV_LORA = 128
NAT_KR = 8
NAT_KC = 16
GDN_CHUNK = 64
SSM_STATE = 128
SSM_CHUNK = 128
CONV_K = 5
GDN_QKV = 768
SSM_XBC = 768

LANES = 128
SUBLANES = 8
VMEM_LIMIT = 56 * 1024 * 1024

C_MLA = 0
C_NAT = 640
C_GQKV = 1408
C_GZG = 2176
C_SZG = 2432
C_SXBC = 2688
C_SMALL = 3456
C_TOTAL = 3584
L_BETA, L_G, L_DT = 0, 8, 16

MLA_QSCALE = (MLA_NOPE + MLA_ROPE) ** -0.5 * math.log2(math.e)
NAT_QSCALE = HEAD_W ** -0.5


def _dot(a, b):
    return jnp.dot(a, b, preferred_element_type=F32)


def _dot_nt(a, b):
    return lax.dot_general(a, b, (((1,), (1,)), ((), ())), preferred_element_type=F32)


def _dot_tn(a, b):
    return lax.dot_general(a, b, (((0,), (0,)), ((), ())), preferred_element_type=F32)


def _split(x):
    hi = x.astype(BF16)
    lo = (x - hi.astype(F32)).astype(BF16)
    return hi, lo


def _dot2(a, m):
    hi, lo = _split(a)
    return _dot(hi, m) + _dot(lo, m)


def _dot2l(m, a):
    hi, lo = _split(a)
    return _dot(m, hi) + _dot(m, lo)


def _rms(x, g):
    return x * lax.rsqrt(jnp.mean(x * x, axis=-1, keepdims=True) + RMS_EPS) * g


def _silu(x):
    return x * jax.nn.sigmoid(x)


def _softplus(x):
    return jnp.maximum(x, 0.0) + jnp.log1p(jnp.exp(-jnp.abs(x)))


def _iota(shape, dim):
    return lax.broadcasted_iota(jnp.int32, shape, dim)


def _block_diag(x, n, blk_r, blk_c):
    t = jnp.concatenate([x] * n, axis=0)
    keep = (_iota(t.shape, 0) // blk_r) == (_iota(t.shape, 1) // blk_c)
    return jnp.where(keep, t, 0.0)


def _full(shape):
    nd = len(shape)
    return pl.BlockSpec(shape, lambda *_: (0,) * nd)


def _resident(shape):
    nd = len(shape)
    return pl.BlockSpec(shape, lambda *_: (0,) * nd, pipeline_mode=pl.Buffered(1))


def _mod_kernel(c_ref, w_ref, b_ref, o_ref):
    s = _silu(c_ref[...])
    o_ref[0] = jnp.dot(s, w_ref[0], preferred_element_type=F32,
                       precision=lax.Precision.HIGHEST) + b_ref[0]


def _modulation(cvec, w_mod, b_mod):
    nl, dm, nm = w_mod.shape
    rows = cvec.shape[0]
    tn = 1536
    return pl.pallas_call(
        _mod_kernel,
        out_shape=jax.ShapeDtypeStruct((nl, rows, nm), F32),
        grid=(nl, nm // tn),
        in_specs=[pl.BlockSpec((rows, dm), lambda l, j: (0, 0)),
                  pl.BlockSpec((1, dm, tn), lambda l, j: (l, 0, j)),
                  pl.BlockSpec((1, 1, tn), lambda l, j: (l, 0, j))],
        out_specs=pl.BlockSpec((1, rows, tn), lambda l, j: (l, 0, j)),
        name="modulation",
    )(cvec, w_mod, b_mod.reshape(nl, 1, nm))


FFN_CHUNK = 256


def _ffn_body(x, m, g, w1_ref, w3_ref, w2_ref, k0, gp):
    u = _rms(x, g[gp:gp + 1]) * (1.0 + m[k0 + 1:k0 + 2]) + m[k0:k0 + 1]
    ub = u.astype(BF16)
    acc = None
    for c in range(D_FF // FFN_CHUNK):
        sl = slice(c * FFN_CHUNK, (c + 1) * FFN_CHUNK)
        a = _dot(ub, w1_ref[:, sl])
        b = _dot(ub, w3_ref[:, sl])
        hid = (_silu(a) * b).astype(BF16)
        part = _dot(hid, w2_ref[sl, :])
        acc = part if acc is None else acc + part
    return x + 0.5 * m[k0 + 2:k0 + 3] * _rms(acc, g[gp + 1:gp + 2])


def _ffn_kernel(h_ref, mod_ref, g_ref, w1_ref, w3_ref, w2_ref, o_ref, *, k0, gp):
    o_ref[0] = _ffn_body(h_ref[0], mod_ref[0], g_ref[...], w1_ref, w3_ref, w2_ref, k0, gp)


def _half_ffn(h, mod, g, w1, w3, w2, *, k0, gp, tm):
    nb, nt, dm = h.shape
    per_batch = mod.shape[0] > 1
    return pl.pallas_call(
        functools.partial(_ffn_kernel, k0=k0, gp=gp),
        out_shape=jax.ShapeDtypeStruct(h.shape, F32),
        grid=(nb, nt // tm),
        in_specs=[pl.BlockSpec((1, tm, dm), lambda b, i: (b, i, 0)),
                  pl.BlockSpec((1, N_MOD, dm), lambda b, i: (b if per_batch else 0, 0, 0)),
                  _full(g.shape), _resident(w1.shape), _resident(w3.shape), _resident(w2.shape)],
        out_specs=pl.BlockSpec((1, tm, dm), lambda b, i: (b, i, 0)),
        compiler_params=pltpu.CompilerParams(vmem_limit_bytes=VMEM_LIMIT),
        name="half_ffn",
    )(h, mod, g, w1, w3, w2)


HALO = SUBLANES


def _conv_silu(z_tile, z_halo, w, b, xpad_ref):
    i = pl.program_id(1)
    last = pl.num_programs(1) - 1
    tm = z_tile.shape[0]
    xpad_ref[0:HALO] = jnp.where(i > 0, z_halo[0:HALO], 0.0)
    xpad_ref[HALO:HALO + tm] = z_tile
    xpad_ref[HALO + tm:2 * HALO + tm] = jnp.where(i < last, z_halo[HALO:2 * HALO], 0.0)
    acc = jnp.broadcast_to(b, z_tile.shape)
    for j in range(CONV_K):
        acc = acc + xpad_ref[pl.ds(HALO - CONV_K // 2 + j, tm), :] * w[j:j + 1]
    return _silu(acc)


def _inproj_kernel(h_ref, hp_ref, hn_ref, mod_ref, g_ref, w_ref, gq_ref, gkv_ref, wq1_ref, wq2_ref,
                   wk_ref, wv_ref, cos_ref, sin_ref, gcw_ref, scw_ref, scb_ref, pv_ref,
                   mq_ref, mk_ref, mv_ref, nq_ref, nk_ref, nv_ref, gy_ref, gzg_ref, szg_ref,
                   sy_ref, sma_ref, smb_ref, gpad_ref, spad_ref):
    m = mod_ref[0]
    g = g_ref[...]

    def modulated(x):
        return (_rms(x, g[2:3]) * (1.0 + m[4:5]) + m[3:4]).astype(BF16)

    ub = modulated(h_ref[0])
    uh = modulated(jnp.concatenate([hp_ref[0], hn_ref[0]], axis=0))

    def proj(a, b):
        return _dot(ub, w_ref[:, a:b])

    gy = _conv_silu(proj(C_GQKV, C_GZG), _dot(uh, w_ref[:, C_GQKV:C_GZG]), gcw_ref[...], 0.0, gpad_ref)
    grp = (_iota((GROUP_W, GROUP_W), 0) // HEAD_W == _iota((GROUP_W, GROUP_W), 1) // HEAD_W).astype(BF16)
    q = gy[:, 0:256]
    k = gy[:, 256:512]
    gy_ref[0, :, 0:256] = q * lax.rsqrt(_dot2(q * q, grp) + RMS_EPS) * (HEAD_W ** -0.5)
    gy_ref[0, :, 256:512] = k * lax.rsqrt(_dot2(k * k, grp) + RMS_EPS)
    gy_ref[0, :, 512:768] = gy[:, 512:768]
    sy_ref[0] = _conv_silu(proj(C_SXBC, C_SMALL), _dot(uh, w_ref[:, C_SXBC:C_SMALL]), scw_ref[...],
                           scb_ref[...], spad_ref)
    s = proj(C_SMALL, C_TOTAL)
    pv = pv_ref[...]
    sp = _softplus(s + pv[1:2])
    sma_ref[0] = jnp.where(_iota((1, LANES), 1) < L_G, jax.nn.sigmoid(s), -jnp.exp(pv[0:1]) * sp)
    smb_ref[0] = sp

    zm = proj(C_MLA, C_NAT)
    cqn = _rms(zm[:, 0:256], gq_ref[...]).astype(BF16)
    ckvn = _rms(zm[:, 256:384], gkv_ref[...]).astype(BF16)
    cos = cos_ref[...]
    sin = sin_ref[...]
    k_rope = zm[:, 384:512] * cos + zm[:, 512:640] * sin
    q1 = _dot(cqn, wq1_ref[...])
    q2 = _dot(cqn, wq2_ref[...])
    kn = _dot(ckvn, wk_ref[...])
    vv = _dot(ckvn, wv_ref[...])
    ones_col = (_iota((1, LANES), 1) == HEAD_W).astype(F32)
    for hh in range(N_HEADS):
        sl = slice(hh * LANES, (hh + 1) * LANES)
        mq_ref[0, hh] = ((q1[:, sl] * cos + q2[:, sl] * sin) * MLA_QSCALE).astype(BF16)
        mk_ref[0, hh] = (kn[:, sl] + k_rope).astype(BF16)
        mv_ref[0, hh] = (vv[:, sl] + ones_col).astype(BF16)

    nq_ref[0] = (proj(C_NAT, C_NAT + 256) * NAT_QSCALE).astype(BF16)
    nk_ref[0] = proj(C_NAT + 256, C_NAT + 512).astype(BF16)
    nv_ref[0] = proj(C_NAT + 512, C_NAT + 768).astype(BF16)
    gzg_ref[0] = proj(C_GZG, C_SZG)
    szg_ref[0] = proj(C_SZG, C_SXBC)


def _in_projection(h, mod, g, lw, cos, sin, *, tm):
    nb, nt, dm = h.shape
    per_batch = mod.shape[0] > 1
    hb = tm // HALO
    nhalo = nt // HALO
    tok = lambda w, dt: jax.ShapeDtypeStruct((nb, nt, w), dt)
    head = jax.ShapeDtypeStruct((nb, N_HEADS, nt, LANES), BF16)
    tspec = lambda w: pl.BlockSpec((1, tm, w), lambda b, i: (b, i, 0))
    hspec = pl.BlockSpec((1, N_HEADS, tm, LANES), lambda b, i: (b, 0, i, 0))
    return pl.pallas_call(
        _inproj_kernel,
        out_shape=(head, head, head, tok(256, BF16), tok(256, BF16), tok(256, BF16),
                   tok(768, F32), tok(256, F32), tok(256, F32), tok(768, F32), tok(LANES, F32),
                   tok(LANES, F32)),
        grid=(nb, nt // tm),
        in_specs=[tspec(dm),
                  pl.BlockSpec((1, HALO, dm), lambda b, i: (b, jnp.maximum(i * hb - 1, 0), 0)),
                  pl.BlockSpec((1, HALO, dm), lambda b, i: (b, jnp.minimum((i + 1) * hb, nhalo - 1), 0)),
                  pl.BlockSpec((1, N_MOD, dm), lambda b, i: (b if per_batch else 0, 0, 0)),
                  _full(g.shape), _resident(lw["w_in"].shape),
                  _full(lw["mla_gq"].shape), _full(lw["mla_gkv"].shape),
                  _full(lw["wq1"].shape), _full(lw["wq2"].shape),
                  _full(lw["wk"].shape), _full(lw["wv"].shape),
                  pl.BlockSpec((tm, LANES), lambda b, i: (i, 0)),
                  pl.BlockSpec((tm, LANES), lambda b, i: (i, 0)),
                  _full(lw["gdn_conv_w"].shape), _full(lw["ssm_conv_w"].shape),
                  _full(lw["ssm_conv_b"].shape), _full(lw["gate_pv"].shape)],
        out_specs=(hspec, hspec, hspec, tspec(256), tspec(256), tspec(256),
                   tspec(768), tspec(256), tspec(256), tspec(768), tspec(LANES), tspec(LANES)),
        scratch_shapes=[pltpu.VMEM((tm + 2 * HALO, GDN_QKV), F32), pltpu.VMEM((tm + 2 * HALO, SSM_XBC), F32)],
        compiler_params=pltpu.CompilerParams(vmem_limit_bytes=VMEM_LIMIT),
        name="in_projection",
    )(h, h, h, mod, g, lw["w_in"], lw["mla_gq"], lw["mla_gkv"], lw["wq1"], lw["wq2"], lw["wk"], lw["wv"],
      cos, sin, lw["gdn_conv_w"], lw["ssm_conv_w"], lw["ssm_conv_b"], lw["gate_pv"])


def _mla_kernel(*refs, n_lat_chunks, tk):
    if n_lat_chunks:
        q_ref, kl_ref, vl_ref, kc_ref, vc_ref, o_ref = refs
    else:
        q_ref, kc_ref, vc_ref, o_ref = refs
    q = q_ref[0, 0]
    tq = q.shape[0]

    def step(kb, vb, carry):
        m, acc = carry
        s = _dot_nt(q, kb)
        mn = jnp.maximum(m, jnp.max(s, axis=-1, keepdims=True))
        p = jnp.exp2(s - mn)
        acc = jnp.exp2(m - mn) * acc + _dot(p.astype(BF16), vb)
        return mn, acc

    carry = (jnp.full((tq, 1), NEG_INF, F32), jnp.zeros((tq, LANES), F32))
    if n_lat_chunks:
        def body(j, carry):
            off = pl.multiple_of(j * tk, tk)
            return step(kl_ref[0, 0, pl.ds(off, tk), :], vl_ref[0, 0, pl.ds(off, tk), :], carry)
        carry = lax.fori_loop(0, n_lat_chunks, body, carry, unroll=8)
    _, acc = step(kc_ref[0, 0], vc_ref[0, 0], carry)
    o_ref[0, 0] = acc / acc[:, HEAD_W:HEAD_W + 1]


def _mla_attention(q, k_ctx, v_ctx, k_lat=None, v_lat=None, *, tq, tk):
    nb, nh, nq, _ = q.shape
    nc = k_ctx.shape[2]
    qspec = pl.BlockSpec((1, 1, tq, LANES), lambda b, h, i: (b, h, i, 0))
    cspec = pl.BlockSpec((1, 1, nc, LANES), lambda b, h, i: (b, h, 0, 0))
    if k_lat is None:
        args, specs, n_chunks = (q, k_ctx, v_ctx), [qspec, cspec, cspec], 0
    else:
        nk = k_lat.shape[2]
        lspec = pl.BlockSpec((1, 1, nk, LANES), lambda b, h, i: (b, h, 0, 0))
        args, specs, n_chunks = (q, k_lat, v_lat, k_ctx, v_ctx), [qspec, lspec, lspec, cspec, cspec], nk // tk
    return pl.pallas_call(
        functools.partial(_mla_kernel, n_lat_chunks=n_chunks, tk=tk),
        out_shape=jax.ShapeDtypeStruct((nb, nh, nq, LANES), F32),
        grid=(nb, nh, nq // tq),
        in_specs=specs,
        out_specs=qspec,
        compiler_params=pltpu.CompilerParams(vmem_limit_bytes=VMEM_LIMIT),
        name="mla_attention",
    )(*args)


NAT_QROWS = 4
NAT_QB = NAT_QROWS * GRID_W
NAT_KBLKS = 3


def _heads_attention(q, parts, gout):
    lane_head = _iota((1, GROUP_W), 1) // HEAD_W
    heads = range(N_HEADS)
    sels = [lane_head == hh for hh in heads]
    qhs = [jnp.where(sel, q, jnp.zeros_like(q)) for sel in sels]
    scores = [[_dot_nt(qh, k) if bias is None else _dot_nt(qh, k) + bias[hh] for k, _, bias in parts]
              for hh, qh in zip(heads, qhs)]
    ms = [functools.reduce(jnp.maximum, [jnp.max(s, axis=-1, keepdims=True) for s in sc]) for sc in scores]
    ps = [[jnp.exp(s - m) for s in sc] for sc, m in zip(scores, ms)]
    ls = [functools.reduce(jnp.add, [jnp.sum(p, axis=-1, keepdims=True) for p in pp]) for pp in ps]
    os_ = [functools.reduce(jnp.add, [_dot(p.astype(BF16), v) for p, (_, v, _) in zip(pp, parts)])
           for pp in ps]
    out = jnp.zeros(q.shape, F32)
    for sel, o, l in zip(sels, os_, ls):
        out = jnp.where(sel, o / l, out)
    return _rms(out, gout)


def _nat_kernel(q_ref, k0_ref, k1_ref, k2_ref, v0_ref, v1_ref, v2_ref, kc_ref, vc_ref, bias_ref, g_ref,
                o_ref):
    kw = jnp.concatenate([k0_ref[0], k1_ref[0], k2_ref[0]], axis=0)
    vw = jnp.concatenate([v0_ref[0], v1_ref[0], v2_ref[0]], axis=0)
    parts = [(kw, vw, bias_ref[0]), (kc_ref[0], vc_ref[0], None)]
    o_ref[0] = _heads_attention(q_ref[0], parts, g_ref[...])


def _nat_ctx_kernel(q_ref, k_ref, v_ref, g_ref, o_ref):
    o_ref[0] = _heads_attention(q_ref[0], [(k_ref[0], v_ref[0], None)], g_ref[...])


def _nat_attention(q, k, v, kc, vc, bias, gout):
    nb, nt, _ = q.shape
    nblk = nt // NAT_QB
    nc = kc.shape[1]
    start = lambda i: jnp.clip(i - 1, 0, nblk - NAT_KBLKS)
    variant = lambda i: jnp.where(i == 0, 0, jnp.where(i == nblk - 1, 2, 1))
    qspec = pl.BlockSpec((1, NAT_QB, GROUP_W), lambda b, i: (b, i, 0))
    kspecs = [pl.BlockSpec((1, NAT_QB, GROUP_W), lambda b, i, j=j: (b, start(i) + j, 0))
              for j in range(NAT_KBLKS)]
    cspec = pl.BlockSpec((1, nc, GROUP_W), lambda b, i: (b, 0, 0))
    bspec = pl.BlockSpec((1, N_HEADS, NAT_QB, NAT_KBLKS * NAT_QB), lambda b, i: (variant(i), 0, 0, 0))
    return pl.pallas_call(
        _nat_kernel,
        out_shape=jax.ShapeDtypeStruct((nb, nt, GROUP_W), F32),
        grid=(nb, nblk),
        in_specs=[qspec] + kspecs + kspecs + [cspec, cspec, bspec, _full(gout.shape)],
        out_specs=qspec,
        name="nat_attention",
    )(q, k, k, k, v, v, v, kc, vc, bias, gout)


def _nat_ctx_attention(q, k, v, gout):
    nb, nc, _ = q.shape
    spec = pl.BlockSpec((1, nc, GROUP_W), lambda b: (b, 0, 0))
    return pl.pallas_call(
        _nat_ctx_kernel,
        out_shape=jax.ShapeDtypeStruct((nb, nc, GROUP_W), F32),
        grid=(nb,),
        in_specs=[spec, spec, spec, _full(gout.shape)],
        out_specs=spec,
        name="nat_ctx_attention",
    )(q, k, v, gout)


def _nat_bias_constants(n_rows):
    krows = NAT_KBLKS * NAT_QROWS
    qr = np.arange(NAT_QROWS)[:, None]
    kk = np.arange(krows)[None, :]
    cq = np.arange(GRID_W)[:, None]
    ck = np.arange(GRID_W)[None, :]
    c0 = np.clip(cq - NAT_KC // 2, 0, GRID_W - NAT_KC)
    col_ok = ((ck >= c0) & (ck < c0 + NAT_KC)).reshape(-1)
    dc = np.clip(ck - cq + (NAT_KC - 1), 0, 2 * NAT_KC - 2).reshape(-1)
    e_col = np.zeros((2 * NAT_KC, GRID_W * GRID_W), np.float32)
    e_col[dc, np.arange(GRID_W * GRID_W)] = 1.0
    big = 4 * n_rows + 64
    placements = [(0, 0, n_rows), (big // 2, big // 2 - NAT_QROWS, big),
                  (n_rows - NAT_QROWS, n_rows - krows, n_rows)]
    e_row = np.zeros((3, NAT_QROWS * krows, 2 * NAT_KR), np.float32)
    ok = np.zeros((3, NAT_QROWS * krows, GRID_W * GRID_W), np.float32)
    for v, (r_base, k_start, rows_total) in enumerate(placements):
        r = r_base + qr
        k_abs = k_start + kk
        r0 = np.clip(r - NAT_KR // 2, 0, rows_total - NAT_KR)
        row_ok = ((k_abs >= r0) & (k_abs < r0 + NAT_KR)).reshape(-1)
        dr = np.clip(k_abs - r + (NAT_KR - 1), 0, 2 * NAT_KR - 2).reshape(-1)
        e_row[v, np.arange(NAT_QROWS * krows), dr] = 1.0
        ok[v] = row_ok[:, None] & col_ok[None, :]
    return e_row, e_col, ok


def _split3(x):
    h1 = x.astype(BF16)
    r1 = x - h1.astype(F32)
    h2 = r1.astype(BF16)
    h3 = (r1 - h2.astype(F32)).astype(BF16)
    return h1, h2, h3


def _nat_bias_kernel(er_ref, rpb_ref, ec_ref, ok_ref, o_ref):
    er = er_ref[0]
    rows = functools.reduce(jnp.add, [_dot(er, p) for p in _split3(rpb_ref[0, 0])])
    ec = ec_ref[...]
    b = functools.reduce(jnp.add, [_dot(p, ec) for p in _split3(rows)])
    o_ref[0, 0, 0] = jnp.where(ok_ref[0] > 0.0, b, NEG_INF)


def _nat_bias_tables(rpb_all, n_rows):
    nl, nh = rpb_all.shape[:2]
    krows = NAT_KBLKS * NAT_QROWS
    e_row, e_col, ok = _nat_bias_constants(n_rows)
    rpb = jnp.pad(rpb_all, ((0, 0), (0, 0), (0, 1), (0, 1)))
    nr, ncol = NAT_QROWS * krows, GRID_W * GRID_W
    out = pl.pallas_call(
        _nat_bias_kernel,
        out_shape=jax.ShapeDtypeStruct((nl, 3, nh, nr, ncol), F32),
        grid=(nl, 3, nh),
        in_specs=[pl.BlockSpec((1, nr, 2 * NAT_KR), lambda l, v, h: (v, 0, 0)),
                  pl.BlockSpec((1, 1, 2 * NAT_KR, 2 * NAT_KC), lambda l, v, h: (l, h, 0, 0)),
                  pl.BlockSpec((2 * NAT_KC, ncol), lambda l, v, h: (0, 0)),
                  pl.BlockSpec((1, nr, ncol), lambda l, v, h: (v, 0, 0))],
        out_specs=pl.BlockSpec((1, 1, 1, nr, ncol), lambda l, v, h: (l, v, h, 0, 0)),
        name="nat_bias_tables",
    )(jnp.asarray(e_row, BF16), rpb, jnp.asarray(e_col, BF16), jnp.asarray(ok))
    out = out.reshape(nl, 3, nh, NAT_QROWS, krows, GRID_W, GRID_W).transpose(0, 1, 2, 3, 5, 4, 6)
    return out.reshape(nl, 3, nh, NAT_QB, krows * GRID_W)


def _chunk_cumsum_matrix(n, chunk, reverse):
    r = _iota((n, n), 0)
    c = _iota((n, n), 1)
    same = (r // chunk) == (c // chunk)
    return (same & ((c >= r) if reverse else (c <= r))).astype(BF16)


def _expand_matrix(lane0, group, width):
    return (_iota((LANES, width), 0) == lane0 + _iota((LANES, width), 1) // group).astype(BF16)


def _row_form(col_vals, chunk):
    pick = _iota(col_vals.shape, 0) == (_iota(col_vals.shape, 1) % chunk)
    ones = jnp.ones((SUBLANES, chunk), BF16)
    return _dot2l(ones, jnp.where(pick, col_vals, 0.0))[0:1]


def _heads_diag(x):
    return _block_diag(x, N_HEADS, GDN_CHUNK, HEAD_W)


def _heads_undiag(x):
    keep = (_iota(x.shape, 0) // GDN_CHUNK) == (_iota(x.shape, 1) // HEAD_W)
    x = jnp.where(keep, x, 0.0)
    return functools.reduce(jnp.add, [x[hh * GDN_CHUNK:(hh + 1) * GDN_CHUNK] for hh in range(N_HEADS)])


def _pc3(x, y):
    xh, xl = _split(x)
    yh, yl = _split(y)
    r = _dot(jnp.concatenate([xh, xl], axis=0), _heads_diag(yh))
    n = x.shape[0]
    return r[0:n] + r[n:2 * n] + _dot(xh, _heads_diag(yl))


def _pc1(x, y):
    return _dot(x.astype(BF16), _heads_diag(y.astype(BF16)))


def _gdn_chunk_kernel(y_ref, sm_ref, g_ref, qs_ref, qe_ref, o0_ref, gl_ref, *, d, n_chunks):
    ch = GDN_CHUNK
    cb = n_chunks * ch
    reverse = d == 1
    sm = sm_ref[0]
    tri = _chunk_cumsum_matrix(ch, ch, reverse)
    cs = jnp.concatenate([_dot2l(tri, sm[c * ch:(c + 1) * ch]) for c in range(n_chunks)], axis=0)
    gc_all = _dot2(cs, _expand_matrix(L_G + N_HEADS * d, HEAD_W, GROUP_W))
    beta_all = _dot2(sm, _expand_matrix(L_BETA + N_HEADS * d, HEAD_W, GROUP_W))
    y = y_ref[0]
    q_all, k_all, v_all = y[:, 0:256], y[:, 256:512], y[:, 512:768]
    kb_all = k_all * beta_all
    vb_all = v_all * beta_all
    eg_all = jnp.exp(gc_all)

    ti = _iota((ch, GROUP_W), 0)
    tj = _iota((ch, GROUP_W), 1) % ch
    incl = (tj >= ti) if reverse else (tj <= ti)
    strict = (tj > ti) if reverse else (tj < ti)
    last = 0 if reverse else ch - 1
    cs_ = range(n_chunks)
    rows = [slice(c * ch, (c + 1) * ch) for c in cs_]

    gc = [gc_all[r] for r in rows]
    decay = [jnp.where(incl, jnp.exp(jnp.where(incl, g - _row_form(g, ch), 0.0)), 0.0) for g in gc]
    qk = [_dot_nt(jnp.concatenate([kb_all[r], q_all[r]], axis=0).astype(BF16),
                  _heads_diag(k_all[r].astype(BF16))) for r in rows]
    a_mat = [jnp.where(strict, x[0:ch] * dc, 0.0) for x, dc in zip(qk, decay)]
    a_intra = [(x[ch:2 * ch] * dc).astype(BF16) for x, dc in zip(qk, decay)]

    base = SUBLANES
    eye = jnp.where(ti == tj, 1.0, 0.0)
    m = [jnp.where(ti // base == tj // base, -a, 0.0) for a in a_mat]
    t = [eye + x for x in m]
    for _ in range(2):
        m = [_pc1(x, x) for x in m]
        t = [x + _pc1(x, p) for x, p in zip(t, m)]
    size = 2 * base
    while size <= ch:
        off = (ti // size == tj // size) & (ti // (size // 2) != tj // (size // 2))
        ct = [_pc1(jnp.where(off, a, 0.0), x) for a, x in zip(a_mat, t)]
        t = [x - _pc1(x, p) for x, p in zip(t, ct)]
        size *= 2
    resid = [eye - x - _pc3(a, x) for a, x in zip(a_mat, t)]
    t = [x + _pc1(x, r) for x, r in zip(t, resid)]

    u = [_pc1(x, vb_all[r]) for x, r in zip(t, rows)]
    w = [_pc1(x, kb_all[r] * eg_all[r]) for x, r in zip(t, rows)]
    for c in cs_:
        r = rows[c]
        g_last = gc[c][last:last + 1]
        k_dec = (k_all[r] * jnp.exp(g_last - gc[c])).astype(BF16)
        wu = jnp.concatenate([w[c], u[c]], axis=1).astype(BF16)
        full = _dot_tn(k_dec, wu)
        g_ref[0, r, :] = _heads_undiag(full[:, 0:GROUP_W]).astype(BF16)
        qs_ref[0, r, :] = _heads_undiag(full[:, GROUP_W:])
        wu_bd = jnp.concatenate([_heads_diag(wu[:, 0:GROUP_W]), _heads_diag(wu[:, GROUP_W:])], axis=1)
        aw = _dot(a_intra[c], wu_bd)
        qe_ref[0, r, :] = (q_all[r] * eg_all[r] - aw[:, 0:GROUP_W]).astype(BF16)
        o0_ref[0, r, :] = aw[:, GROUP_W:]
        gl_ref[0, c:c + 1, :] = jnp.exp(g_last)


def _gdn_state_kernel(g_ref, qs_ref, qe_ref, o0_ref, gl_ref, s0_ref, o_ref, sfin_ref, state_ref,
                      *, d, n_chunks):
    ch = GDN_CHUNK
    nb = g_ref.shape[0]
    step = pl.program_id(0)

    @pl.when(step == 0)
    def _():
        state_ref[...] = s0_ref[...]

    order = range(n_chunks - 1, -1, -1) if d == 1 else range(n_chunks)
    state = [state_ref[b] for b in range(nb)]
    for c in order:
        r = slice(c * ch, (c + 1) * ch)
        for b in range(nb):
            both = jnp.concatenate([g_ref[b, r, :], qe_ref[b, r, :]], axis=0)
            res = _dot(both, _heads_diag(state[b].astype(BF16)))
            o_ref[b, r, :] = res[ch:2 * ch] + o0_ref[b, r, :]
            state[b] = state[b] * gl_ref[b, c:c + 1, :] - res[0:ch] + qs_ref[b, r, :]
    for b in range(nb):
        state_ref[b] = state[b]

    @pl.when(step == pl.num_programs(0) - 1)
    def _():
        for b in range(nb):
            sfin_ref[b] = state[b]


def _gdn_chunk_call(y, sm, *, d, cb):
    nb, nt, nc = y.shape
    n_chunks = cb // GDN_CHUNK
    tok = lambda dt: jax.ShapeDtypeStruct((nb, nt, GROUP_W), dt)
    tspec = pl.BlockSpec((1, cb, GROUP_W), lambda bb, i: (bb, i, 0))
    return pl.pallas_call(
        functools.partial(_gdn_chunk_kernel, d=d, n_chunks=n_chunks),
        out_shape=(tok(BF16), tok(F32), tok(BF16), tok(F32),
                   jax.ShapeDtypeStruct((nb, nt // GDN_CHUNK, GROUP_W), F32)),
        grid=(nb, nt // cb),
        in_specs=[pl.BlockSpec((1, cb, nc), lambda bb, i: (bb, i, 0)),
                  pl.BlockSpec((1, cb, LANES), lambda bb, i: (bb, i, 0))],
        out_specs=(tspec, tspec, tspec, tspec,
                   pl.BlockSpec((1, n_chunks, GROUP_W), lambda bb, i: (bb, i, 0))),
        name="gdn_chunk",
    )(y, sm)


def _gdn_state_call(parts, s0, *, d, cb):
    g, qs, qe, o0, gl = parts
    nb, nt, _ = g.shape
    n_chunks = cb // GDN_CHUNK
    nblk = nt // cb
    blk = (lambda i: (0, nblk - 1 - i, 0)) if d == 1 else (lambda i: (0, i, 0))
    tspec = pl.BlockSpec((nb, cb, GROUP_W), blk)
    sspec = pl.BlockSpec((nb, GDN_CHUNK, GROUP_W), lambda i: (0, 0, 0))
    return pl.pallas_call(
        functools.partial(_gdn_state_kernel, d=d, n_chunks=n_chunks),
        out_shape=(jax.ShapeDtypeStruct((nb, nt, GROUP_W), F32),
                   jax.ShapeDtypeStruct((nb, GDN_CHUNK, GROUP_W), F32)),
        grid=(nblk,),
        in_specs=[tspec, tspec, tspec, tspec, pl.BlockSpec((nb, n_chunks, GROUP_W), blk), sspec],
        out_specs=(tspec, sspec),
        scratch_shapes=[pltpu.VMEM((nb, GDN_CHUNK, GROUP_W), F32)],
        compiler_params=pltpu.CompilerParams(dimension_semantics=("arbitrary",)),
        name="gdn_state_scan",
    )(g, qs, qe, o0, gl, s0)


def _gdn_bidir(y_l, sm_l, y_c, sm_c, *, cb_l, cb_c):
    zero = jnp.zeros((y_l.shape[0], GDN_CHUNK, GROUP_W), F32)
    outs = []
    for d in (0, 1):
        o_c, s_c = _gdn_state_call(_gdn_chunk_call(y_c, sm_c, d=d, cb=cb_c), zero, d=d, cb=cb_c)
        o_l, _ = _gdn_state_call(_gdn_chunk_call(y_l, sm_l, d=d, cb=cb_l), s_c, d=d, cb=cb_l)
        outs.append((o_l, o_c))
    return outs


def _ssd_kernel(x_ref, dt_ref, da_ref, s0_ref, o_ref, sfin_ref, state_ref, *, d, n_chunks):
    ch = SSM_CHUNK
    reverse = d == 1
    nb = x_ref.shape[0]
    step = pl.program_id(0)

    @pl.when(step == 0)
    def _():
        state_ref[...] = s0_ref[...]

    lane0 = L_DT + N_HEADS * d
    tri = _chunk_cumsum_matrix(ch, ch, reverse)
    e64 = _expand_matrix(lane0, HEAD_W, GROUP_W)
    e128 = _expand_matrix(lane0, ch, N_HEADS * ch)
    ti = _iota((ch, N_HEADS * ch), 0)
    tj = _iota((ch, N_HEADS * ch), 1) % ch
    incl = (tj >= ti) if reverse else (tj <= ti)
    grp_keep = (_iota((GROUP_W, GROUP_W), 0) // ch) == (_iota((GROUP_W, GROUP_W), 1) // ch)
    last = 0 if reverse else ch - 1
    rows = [slice(c * ch, (c + 1) * ch) for c in range(n_chunks)]

    def chunk_terms(b):
        da = da_ref[b]
        cs = jnp.concatenate([_dot2l(tri, da[r]) for r in rows], axis=0)
        ac_all = _dot2(cs, e64)
        ac5_all = _dot2(cs, e128)
        xbc = x_ref[b]
        xdt_all = xbc[:, 0:256] * _dot2(dt_ref[b], e64)
        b_all, c_all = xbc[:, 256:512], xbc[:, 512:768]
        seg = [jnp.where(incl, jnp.exp(jnp.where(incl, ac5_all[r] - _row_form(ac5_all[r], ch), 0.0)), 0.0)
               for r in rows]
        cm = [c_all[r].astype(BF16) for r in rows]
        bmb = [b_all[r].astype(BF16) for r in rows]
        cb_g = [_dot_nt(c_, _block_diag(b_, 2, ch, SSM_STATE)) for c_, b_ in zip(cm, bmb)]
        scores = [(jnp.concatenate([x[:, 0:ch], x[:, 0:ch], x[:, ch:], x[:, ch:]], axis=1) * sg).astype(BF16)
                  for x, sg in zip(cb_g, seg)]
        y_diag = [_dot(s_, _block_diag(xdt_all[r].astype(BF16), N_HEADS, ch, HEAD_W))
                  for s_, r in zip(scores, rows)]
        a_last = [ac_all[r][last:last + 1] for r in rows]
        states = [jnp.where(grp_keep, _dot_tn(b_, (xdt_all[r] * jnp.exp(al - ac_all[r])).astype(BF16)), 0.0)
                  for b_, r, al in zip(bmb, rows, a_last)]
        return cm, y_diag, a_last, states, [jnp.exp(ac_all[r]) for r in rows]

    terms = [chunk_terms(b) for b in range(nb)]
    state = [state_ref[b] for b in range(nb)]
    for c in (range(n_chunks - 1, -1, -1) if reverse else range(n_chunks)):
        for b in range(nb):
            cm, y_diag, a_last, states, e_ac = terms[b]
            o_ref[b, rows[c], :] = y_diag[c] + _dot(cm[c], state[b].astype(BF16)) * e_ac[c]
            state[b] = state[b] * jnp.exp(a_last[c]) + states[c]
    for b in range(nb):
        state_ref[b] = state[b]

    @pl.when(step == pl.num_programs(0) - 1)
    def _():
        for b in range(nb):
            sfin_ref[b] = state[b]


def _scan_call(kernel, name, seq, smalls, s0, *, d, cb, chunk):
    nb, nt, nc = seq.shape
    nblk = nt // cb
    blk = (lambda i: (0, nblk - 1 - i, 0)) if d == 1 else (lambda i: (0, i, 0))
    sspec = pl.BlockSpec((nb, GROUP_W, GROUP_W), lambda i: (0, 0, 0))
    return pl.pallas_call(
        functools.partial(kernel, d=d, n_chunks=cb // chunk),
        out_shape=(jax.ShapeDtypeStruct((nb, nt, GROUP_W), F32),
                   jax.ShapeDtypeStruct((nb, GROUP_W, GROUP_W), F32)),
        grid=(nblk,),
        in_specs=[pl.BlockSpec((nb, cb, nc), blk)] + [pl.BlockSpec((nb, cb, LANES), blk) for _ in smalls]
                 + [sspec],
        out_specs=(pl.BlockSpec((nb, cb, GROUP_W), blk), sspec),
        scratch_shapes=[pltpu.VMEM((nb, GROUP_W, GROUP_W), F32)],
        compiler_params=pltpu.CompilerParams(dimension_semantics=("arbitrary",)),
        name=name,
    )(seq, *smalls, s0)


def _bidir_scan(kernel, name, seq_l, smalls_l, seq_c, smalls_c, *, cb_l, cb_c, chunk):
    zero = jnp.zeros((seq_l.shape[0], GROUP_W, GROUP_W), F32)
    outs = []
    for d in (0, 1):
        o_c, s_c = _scan_call(kernel, name, seq_c, smalls_c, zero, d=d, cb=cb_c, chunk=chunk)
        o_l, _ = _scan_call(kernel, name, seq_l, smalls_l, s_c, d=d, cb=cb_l, chunk=chunk)
        outs.append((o_l, o_c))
    return outs


def _outproj_kernel(h_ref, mod_ref, g_ref, mla_ref, nat_ref, gof_ref, gob_ref, gzg_ref,
                    yf_ref, yb_ref, sx_ref, szg_ref, pv_ref, wm_ref, wr_ref, o_ref):
    pv = pv_ref[...]
    valid = _iota((1, LANES), 1) < HEAD_W
    slabs = [jnp.where(valid, mla_ref[0, hh], 0.0) for hh in range(N_HEADS)]
    ss = functools.reduce(jnp.add, [jnp.sum(s * s, axis=-1, keepdims=True) for s in slabs])
    scale = lax.rsqrt(ss / GROUP_W + RMS_EPS)
    y = None
    for hh in range(N_HEADS):
        part = _dot((slabs[hh] * scale * pv[hh:hh + 1, 0:LANES]).astype(BF16), wm_ref[hh])
        y = part if y is None else y + part
    y = y + _dot(nat_ref[0].astype(BF16), wr_ref[0:256, :])
    grp = (_iota((GROUP_W, GROUP_W), 0) // HEAD_W == _iota((GROUP_W, GROUP_W), 1) // HEAD_W).astype(BF16)
    o = gof_ref[0] + gob_ref[0]
    on = o * lax.rsqrt(_dot2(o * o, grp) / HEAD_W + RMS_EPS) * pv[4:5]
    y = y + _dot((on * _silu(gzg_ref[0])).astype(BF16), wr_ref[256:512, :])
    s = (yf_ref[0] + yb_ref[0] + sx_ref[0] * pv[5:6]) * _silu(szg_ref[0])
    sn = jnp.concatenate([_rms(s[:, 0:LANES], pv[6:7, 0:LANES]),
                          _rms(s[:, LANES:], pv[6:7, LANES:])], axis=1)
    y = y + _dot(sn.astype(BF16), wr_ref[512:768, :])
    m = mod_ref[0]
    g = g_ref[...]
    o_ref[0] = h_ref[0] + m[5:6] * _rms(y, g[3:4])


def _out_projection(h, mod, g, mla_o, nat_o, gdn_f, gdn_b, gdn_zg, ssm_f, ssm_b, ssm_xbc, ssm_zg,
                    pv, w_mla, w_rest, *, tm):
    nb, nt, dm = h.shape
    per_batch = mod.shape[0] > 1
    tspec = lambda w: pl.BlockSpec((1, tm, w), lambda b, i: (b, i, 0))
    return pl.pallas_call(
        _outproj_kernel,
        out_shape=jax.ShapeDtypeStruct(h.shape, F32),
        grid=(nb, nt // tm),
        in_specs=[tspec(dm),
                  pl.BlockSpec((1, N_MOD, dm), lambda b, i: (b if per_batch else 0, 0, 0)),
                  _full(g.shape),
                  pl.BlockSpec((1, N_HEADS, tm, LANES), lambda b, i: (b, 0, i, 0)),
                  tspec(256), tspec(256), tspec(256), tspec(256), tspec(256), tspec(256),
                  tspec(256),
                  tspec(256),
                  _full(pv.shape), _full(w_mla.shape), _full(w_rest.shape)],
        out_specs=tspec(dm),
        name="out_projection",
    )(h, mod, g, mla_o, nat_o, gdn_f, gdn_b, gdn_zg, ssm_f, ssm_b, ssm_xbc, ssm_zg, pv, w_mla, w_rest)


_ROPE_SWAP = np.array(list(range(8, 16)) + list(range(0, 8)) + list(range(24, 32)) + list(range(16, 24)))


def _pack_layer(p, l):
    w_in = p["w_in"][l]
    dm = w_in.shape[0]
    o_nat = MLA_Q_LORA + MLA_KV_LORA + MLA_ROPE
    o_gdn = o_nat + 768
    o_ssm = o_gdn + GDN_QKV + GROUP_W + 4 * N_HEADS
    kr = w_in[:, MLA_Q_LORA + MLA_KV_LORA:o_nat]
    z64 = jnp.zeros((dm, 64), F32)
    z32 = jnp.zeros((dm, 32), F32)
    small = jnp.concatenate([w_in[:, o_gdn + 1024:o_gdn + 1040],
                             w_in[:, o_ssm + 1024:o_ssm + 1032],
                             jnp.zeros((dm, LANES - 24), F32)], axis=1)
    w_packed = jnp.concatenate([
        w_in[:, 0:384], z64, kr, z32, z64, kr[:, _ROPE_SWAP], z32,
        w_in[:, o_nat:o_gdn],
        w_in[:, o_gdn:o_gdn + 1024],
        w_in[:, o_ssm:o_ssm + 1024],
        small], axis=1).astype(BF16)
    assert w_packed.shape[1] == C_TOTAL

    wuq = p["mla_wuq"][l].reshape(MLA_Q_LORA, N_HEADS, MLA_NOPE + MLA_ROPE)
    zq = jnp.zeros((MLA_Q_LORA, N_HEADS, 32), F32)
    wq1 = jnp.concatenate([wuq, zq], axis=2).reshape(MLA_Q_LORA, N_HEADS * LANES)
    wq2 = jnp.concatenate([jnp.zeros((MLA_Q_LORA, N_HEADS, 64), F32), wuq[:, :, MLA_NOPE:][:, :, _ROPE_SWAP], zq],
                          axis=2).reshape(MLA_Q_LORA, N_HEADS * LANES)
    wukv = p["mla_wukv"][l].reshape(MLA_KV_LORA, N_HEADS, MLA_NOPE + HEAD_W)
    zk = jnp.zeros((MLA_KV_LORA, N_HEADS, 64), F32)
    wk = jnp.concatenate([wukv[:, :, :MLA_NOPE], zk], axis=2).reshape(MLA_KV_LORA, N_HEADS * LANES)
    wv = jnp.concatenate([wukv[:, :, MLA_NOPE:], zk], axis=2).reshape(MLA_KV_LORA, N_HEADS * LANES)

    def lanes(vals, lane0):
        v = vals.reshape(-1)
        return jnp.zeros((LANES,), F32).at[lane0:lane0 + v.shape[0]].set(v)

    w_out = p["w_out"][l]
    w_mla = jnp.concatenate([w_out[0:256].reshape(N_HEADS, HEAD_W, dm),
                             jnp.zeros((N_HEADS, LANES - HEAD_W, dm), F32)], axis=1).astype(BF16)
    gout = jnp.concatenate([p["mla_gout"][l].reshape(N_HEADS, HEAD_W),
                            jnp.zeros((N_HEADS, GROUP_W - HEAD_W), F32)], axis=1)
    pv_out = jnp.concatenate([gout,
                              jnp.tile(p["gdn_gnorm"][l], N_HEADS)[None],
                              jnp.repeat(p["ssm_d"][l], HEAD_W)[None],
                              p["ssm_gnorm"][l][None],
                              jnp.zeros((1, GROUP_W), F32)], axis=0)
    return dict(
        ffn1=(p["ffn1_w1"][l].astype(BF16), p["ffn1_w3"][l].astype(BF16), p["ffn1_w2"][l].astype(BF16)),
        ffn2=(p["ffn2_w1"][l].astype(BF16), p["ffn2_w3"][l].astype(BF16), p["ffn2_w2"][l].astype(BF16)),
        g=p["norm_g"][l],
        w_in=w_packed, wq1=wq1.astype(BF16), wq2=wq2.astype(BF16), wk=wk.astype(BF16), wv=wv.astype(BF16),
        mla_gq=p["mla_gq"][l][None], mla_gkv=p["mla_gkv"][l][None],
        nat_bias=p["nat_bias"][l], nat_gout=p["nat_gout"][l][None],
        gdn_conv_w=p["gdn_conv_w"][l],
        ssm_conv_w=p["ssm_conv_w"][l], ssm_conv_b=p["ssm_conv_b"][l][None],
        gate_pv=jnp.stack([lanes(p["gdn_a_log"][l], L_G) + lanes(p["ssm_a_log"][l], L_DT),
                           lanes(p["gdn_dt_bias"][l], L_G) + lanes(p["ssm_dt_bias"][l], L_DT)]),
        pv_out=pv_out, w_mla=w_mla, w_rest=w_out[256:].astype(BF16),
    )


def _rope_tables(n_tok):
    n_rows = n_tok // GRID_W
    quarter = MLA_ROPE // 4
    freqs = ROPE_THETA ** (-jnp.arange(quarter, dtype=F32) / quarter)
    ar = jnp.arange(n_rows, dtype=F32)[:, None] * freqs
    ac = jnp.arange(GRID_W, dtype=F32)[:, None] * freqs
    per_row = lambda t: jnp.repeat(t, GRID_W, axis=0)
    per_col = lambda t: jnp.tile(t, (n_rows, 1))
    cos = jnp.concatenate([per_row(jnp.cos(ar)), per_row(jnp.cos(ar)),
                           per_col(jnp.cos(ac)), per_col(jnp.cos(ac))], axis=1)
    sin = jnp.concatenate([per_row(-jnp.sin(ar)), per_row(jnp.sin(ar)),
                           per_col(-jnp.sin(ac)), per_col(jnp.sin(ac))], axis=1)
    ones = jnp.ones((n_tok, MLA_NOPE), F32)
    zeros = jnp.zeros((n_tok, MLA_NOPE), F32)
    pad = jnp.zeros((n_tok, LANES - MLA_NOPE - MLA_ROPE), F32)
    return jnp.concatenate([ones, cos, pad], axis=1), jnp.concatenate([zeros, sin, pad], axis=1)


def _tiles(n_tok):
    return dict(tm=min(512, n_tok), tm_ffn=min(512, n_tok), tq=min(1024, n_tok),
                cb_gdn=min(512, n_tok), cb_ssd=min(512, n_tok))


def _mixers(hl, hc, ml, mc, lw, ropes, need_ctx):
    tl, tc = _tiles(hl.shape[1]), _tiles(hc.shape[1])
    (cos_l, sin_l), (cos_c, sin_c) = ropes
    zl = _in_projection(hl, ml, lw["g"], lw, cos_l, sin_l, tm=tl["tm"])
    zc = _in_projection(hc, mc, lw["g"], lw, cos_c, sin_c, tm=tc["tm"])
    (mq_l, mk_l, mv_l, nq_l, nk_l, nv_l, gy_l, gzg_l, szg_l, sy_l, sma_l, smb_l) = zl
    (mq_c, mk_c, mv_c, nq_c, nk_c, nv_c, gy_c, gzg_c, szg_c, sy_c, sma_c, smb_c) = zc

    n_lat = hl.shape[1]
    tk = min(1024, n_lat)
    mla_l = _mla_attention(mq_l, mk_c, mv_c, mk_l, mv_l, tq=tl["tq"], tk=tk)
    nat_l = _nat_attention(nq_l, nk_l, nv_l, nk_c, nv_c, lw["nat_bias"], lw["nat_gout"])
    (gf_l, gf_c), (gb_l, gb_c) = _gdn_bidir(gy_l, sma_l, gy_c, sma_c, cb_l=tl["cb_gdn"], cb_c=tc["cb_gdn"])
    (sf_l, sf_c), (sb_l, sb_c) = _bidir_scan(_ssd_kernel, "ssd_scan", sy_l, (smb_l, sma_l),
                                             sy_c, (smb_c, sma_c),
                                             cb_l=tl["cb_ssd"], cb_c=tc["cb_ssd"], chunk=SSM_CHUNK)

    hl = _out_projection(hl, ml, lw["g"], mla_l, nat_l, gf_l, gb_l, gzg_l, sf_l, sb_l, sy_l, szg_l,
                         lw["pv_out"], lw["w_mla"], lw["w_rest"], tm=tl["tm"])
    if need_ctx:
        mla_c = _mla_attention(mq_c, mk_c, mv_c, tq=tc["tq"], tk=tk)
        nat_c = _nat_ctx_attention(nq_c, nk_c, nv_c, lw["nat_gout"])
        hc = _out_projection(hc, mc, lw["g"], mla_c, nat_c, gf_c, gb_c, gzg_c, sf_c, sb_c, sy_c, szg_c,
                             lw["pv_out"], lw["w_mla"], lw["w_rest"], tm=tc["tm"])
    return hl, hc


def _layer(hl, hc, ml, mc, lw, ropes, need_ctx):
    tl, tc = _tiles(hl.shape[1]), _tiles(hc.shape[1])
    hl = _half_ffn(hl, ml, lw["g"], *lw["ffn1"], k0=0, gp=0, tm=tl["tm_ffn"])
    hc = _half_ffn(hc, mc, lw["g"], *lw["ffn1"], k0=0, gp=0, tm=tc["tm"])
    hl, hc = _mixers(hl, hc, ml, mc, lw, ropes, need_ctx)
    hl = _half_ffn(hl, ml, lw["g"], *lw["ffn2"], k0=6, gp=4, tm=tl["tm_ffn"])
    if need_ctx:
        hc = _half_ffn(hc, mc, lw["g"], *lw["ffn2"], k0=6, gp=4, tm=tc["tm"])
    return hl, hc


def kernel(x, c, ctx, c_ctx, w_mod, b_mod, norm_g, ffn1_w1, ffn1_w3, ffn1_w2, ffn2_w1, ffn2_w3, ffn2_w2,
           w_in, w_out, mla_gq, mla_gkv, mla_wuq, mla_wukv, mla_gout, nat_rpb, nat_gout, gdn_conv_w,
           gdn_a_log, gdn_dt_bias, gdn_gnorm, ssm_conv_w, ssm_conv_b, ssm_a_log, ssm_dt_bias, ssm_d,
           ssm_gnorm):
    p = dict(norm_g=norm_g, ffn1_w1=ffn1_w1, ffn1_w3=ffn1_w3, ffn1_w2=ffn1_w2, ffn2_w1=ffn2_w1,
             ffn2_w3=ffn2_w3, ffn2_w2=ffn2_w2, w_in=w_in, w_out=w_out, mla_gq=mla_gq, mla_gkv=mla_gkv,
             mla_wuq=mla_wuq, mla_wukv=mla_wukv, mla_gout=mla_gout, nat_rpb=nat_rpb, nat_gout=nat_gout,
             gdn_conv_w=gdn_conv_w, gdn_a_log=gdn_a_log, gdn_dt_bias=gdn_dt_bias, gdn_gnorm=gdn_gnorm,
             ssm_conv_w=ssm_conv_w, ssm_conv_b=ssm_conv_b, ssm_a_log=ssm_a_log, ssm_dt_bias=ssm_dt_bias,
             ssm_d=ssm_d, ssm_gnorm=ssm_gnorm)
    nb, n_lat, dm = x.shape
    n_ctx = ctx.shape[1]
    depth = w_mod.shape[0]
    cvec = jnp.concatenate([c, c_ctx[None], jnp.zeros((SUBLANES - nb - 1, dm), F32)], axis=0)
    mods = _modulation(cvec, w_mod, b_mod).reshape(depth, SUBLANES, N_MOD, dm)
    cos_c = jnp.concatenate([jnp.ones((n_ctx, MLA_NOPE + MLA_ROPE), F32),
                             jnp.zeros((n_ctx, LANES - MLA_NOPE - MLA_ROPE), F32)], axis=1)
    ropes = (_rope_tables(n_lat), (cos_c, jnp.zeros((n_ctx, LANES), F32)))
    p["nat_bias"] = _nat_bias_tables(nat_rpb, n_lat // GRID_W)
    hl, hc = x, ctx
    for l in range(depth):
        lw = _pack_layer(p, l)
        hl, hc = _layer(hl, hc, mods[l, 0:nb], mods[l, nb:nb + 1], lw, ropes, need_ctx=l < depth - 1)
    return hl
```

```python
import functools
import math

import jax
import jax.numpy as jnp
import numpy as np
from jax import lax
from jax.experimental import pallas as pl
from jax.experimental.pallas import tpu as pltpu

F32 = jnp.float32
BF16 = jnp.bfloat16

D_MODEL = 1024
DEPTH = 4
GRID_W = 64
N_MOD = 9
D_FF = 2816
RMS_EPS = 1e-6
NEG_INF = -1e30
ROPE_THETA = 10000.0
GROUP_W = 256
N_HEADS = 4
HEAD_W = 64
MLA_NOPE = 64
MLA_ROPE = 32
MLA_Q_LORA = 256
MLA_KV_LORA = 128
NAT_KR = 8
NAT_KC = 16
GDN_CHUNK = 64
SSM_STATE = 128
SSM_CHUNK = 128
CONV_K = 5
GDN_QKV = 768
SSM_XBC = 768

LANES = 128
SUBLANES = 8
VMEM_LIMIT = 56 * 1024 * 1024

C_MLA = 0
C_NAT = 640
C_GQKV = 1408
C_GZG = 2176
C_SZG = 2432
C_SXBC = 2688
C_SMALL = 3456
C_TOTAL = 3584
L_BETA, L_G, L_DT = 0, 8, 16

MLA_QSCALE = (MLA_NOPE + MLA_ROPE) ** -0.5 * math.log2(math.e)
NAT_QSCALE = HEAD_W ** -0.5


def _dot(a, b):
    return jnp.dot(a, b, preferred_element_type=F32)


def _dot_nt(a, b):
    return lax.dot_general(a, b, (((1,), (1,)), ((), ())), preferred_element_type=F32)


def _dot_tn(a, b):
    return lax.dot_general(a, b, (((0,), (0,)), ((), ())), preferred_element_type=F32)


def _split(x):
    hi = x.astype(BF16)
    lo = (x - hi.astype(F32)).astype(BF16)
    return hi, lo


def _dot2(a, m):
    hi, lo = _split(a)
    return _dot(hi, m) + _dot(lo, m)


def _dot2l(m, a):
    hi, lo = _split(a)
    return _dot(m, hi) + _dot(m, lo)


def _rms(x, g):
    return x * lax.rsqrt(jnp.mean(x * x, axis=-1, keepdims=True) + RMS_EPS) * g


def _silu(x):
    return x * jax.nn.sigmoid(x)


def _softplus(x):
    return jnp.maximum(x, 0.0) + jnp.log1p(jnp.exp(-jnp.abs(x)))


def _iota(shape, dim):
    return lax.broadcasted_iota(jnp.int32, shape, dim)


def _block_diag(x, n, blk_r, blk_c):
    t = jnp.concatenate([x] * n, axis=0)
    keep = (_iota(t.shape, 0) // blk_r) == (_iota(t.shape, 1) // blk_c)
    return jnp.where(keep, t, 0.0)


def _full(shape):
    nd = len(shape)
    return pl.BlockSpec(shape, lambda *_: (0,) * nd)


def _resident(shape):
    nd = len(shape)
    return pl.BlockSpec(shape, lambda *_: (0,) * nd, pipeline_mode=pl.Buffered(1))


def _mod_kernel(c_ref, w_ref, b_ref, o_ref):
    s = _silu(c_ref[...])
    o_ref[0] = jnp.dot(s, w_ref[0], preferred_element_type=F32,
                       precision=lax.Precision.HIGHEST) + b_ref[0]


def _modulation(cvec, w_mod, b_mod):
    nl, dm, nm = w_mod.shape
    rows = cvec.shape[0]
    tn = 1536
    return pl.pallas_call(
        _mod_kernel,
        out_shape=jax.ShapeDtypeStruct((nl, rows, nm), F32),
        grid=(nl, nm // tn),
        in_specs=[pl.BlockSpec((rows, dm), lambda l, j: (0, 0)),
                  pl.BlockSpec((1, dm, tn), lambda l, j: (l, 0, j)),
                  pl.BlockSpec((1, 1, tn), lambda l, j: (l, 0, j))],
        out_specs=pl.BlockSpec((1, rows, tn), lambda l, j: (l, 0, j)),
        name="modulation",
    )(cvec, w_mod, b_mod.reshape(nl, 1, nm))


FFN_CHUNK = 256


def _ffn_body(x, m, g, w1_ref, w3_ref, w2_ref, k0, gp):
    u = _rms(x, g[gp:gp + 1]) * (1.0 + m[k0 + 1:k0 + 2]) + m[k0:k0 + 1]
    ub = u.astype(BF16)
    acc = None
    for c in range(D_FF // FFN_CHUNK):
        sl = slice(c * FFN_CHUNK, (c + 1) * FFN_CHUNK)
        a = _dot(ub, w1_ref[:, sl])
        b = _dot(ub, w3_ref[:, sl])
        hid = (_silu(a) * b).astype(BF16)
        part = _dot(hid, w2_ref[sl, :])
        acc = part if acc is None else acc + part
    return x + 0.5 * m[k0 + 2:k0 + 3] * _rms(acc, g[gp + 1:gp + 2])


def _ffn_kernel(h_ref, mod_ref, g_ref, w1_ref, w3_ref, w2_ref, o_ref, *, k0, gp):
    o_ref[0] = _ffn_body(h_ref[0], mod_ref[0], g_ref[...], w1_ref, w3_ref, w2_ref, k0, gp)


def _half_ffn(h, mod, g, w1, w3, w2, *, k0, gp, tm):
    nb, nt, dm = h.shape
    per_batch = mod.shape[0] > 1
    return pl.pallas_call(
        functools.partial(_ffn_kernel, k0=k0, gp=gp),
        out_shape=jax.ShapeDtypeStruct(h.shape, F32),
        grid=(nb, nt // tm),
        in_specs=[pl.BlockSpec((1, tm, dm), lambda b, i: (b, i, 0)),
                  pl.BlockSpec((1, N_MOD, dm), lambda b, i: (b if per_batch else 0, 0, 0)),
                  _full(g.shape), _resident(w1.shape), _resident(w3.shape), _resident(w2.shape)],
        out_specs=pl.BlockSpec((1, tm, dm), lambda b, i: (b, i, 0)),
        compiler_params=pltpu.CompilerParams(vmem_limit_bytes=VMEM_LIMIT),
        name="half_ffn",
    )(h, mod, g, w1, w3, w2)


HALO = SUBLANES


def _conv_silu(z_tile, z_halo, w, b):
    i = pl.program_id(1)
    last = pl.num_programs(1) - 1
    tm = z_tile.shape[0]
    zp =jnp.concatenate([jnp.where(i > 0, z_halo[0:HALO], 0.0), z_tile,
                          jnp.where(i < last, z_halo[HALO:2 * HALO], 0.0)], axis=0)
    acc = jnp.broadcast_to(b, z_tile.shape)
    for j in range(CONV_K):
        off = HALO - CONV_K // 2 + j
        acc = acc + zp[off:off + tm] * w[j:j + 1]
    return _silu(acc)


def _inproj_kernel(h_ref, hp_ref, hn_ref, mod_ref, g_ref, w_ref, gq_ref, gkv_ref, wq1_ref, wq2_ref,
                   wk_ref, wv_ref, cos_ref, sin_ref, gcw_ref, scw_ref, scb_ref, pv_ref,
                   mq_ref, mk_ref, mv_ref, nq_ref, nk_ref, nv_ref, gy_ref, gzg_ref, szg_ref,
                   sy_ref, sma_ref, smb_ref):
    m = mod_ref[0]
    g = g_ref[...]

    def modulated(x):
        return (_rms(x, g[2:3]) * (1.0 + m[4:5]) + m[3:4]).astype(BF16)

    ub = modulated(h_ref[0])
    uh = modulated(jnp.concatenate([hp_ref[0], hn_ref[0]], axis=0))

    def proj(a, b):
        return _dot(ub, w_ref[:, a:b])

    gy = _conv_silu(proj(C_GQKV, C_GZG), _dot(uh, w_ref[:, C_GQKV:C_GZG]), gcw_ref[...], 0.0)
    grp = (_iota((GROUP_W, GROUP_W), 0) // HEAD_W == _iota((GROUP_W, GROUP_W), 1) // HEAD_W).astype(BF16)
    q = gy[:, 0:256]
    k = gy[:, 256:512]
    gy_ref[0, :, 0:256] = q * lax.rsqrt(_dot2(q * q, grp) + RMS_EPS) * (HEAD_W ** -0.5)
    gy_ref[0, :, 256:512] = k * lax.rsqrt(_dot2(k * k, grp) + RMS_EPS)
    gy_ref[0, :, 512:768] = gy[:, 512:768]
    sy_ref[0] = _conv_silu(proj(C_SXBC, C_SMALL), _dot(uh, w_ref[:, C_SXBC:C_SMALL]), scw_ref[...],
                           scb_ref[...])
    s = proj(C_SMALL, C_TOTAL)
    pv = pv_ref[...]
    sp = _softplus(s + pv[1:2])
    sma_ref[0] = jnp.where(_iota((1, LANES), 1) < L_G, jax.nn.sigmoid(s), -jnp.exp(pv[0:1]) * sp)
    smb_ref[0] = sp

    zm = proj(C_MLA, C_NAT)
    cqn = _rms(zm[:, 0:256], gq_ref[...]).astype(BF16)
    ckvn = _rms(zm[:, 256:384], gkv_ref[...]).astype(BF16)
    cos = cos_ref[...]
    sin = sin_ref[...]
    k_rope = zm[:, 384:512] * cos + zm[:, 512:640] * sin
    q1 = _dot(cqn, wq1_ref[...])
    q2 = _dot(cqn, wq2_ref[...])
    kn = _dot(ckvn, wk_ref[...])
    vv = _dot(ckvn, wv_ref[...])
    ones_col = (_iota((1, LANES), 1) == HEAD_W).astype(F32)
    for hh in range(N_HEADS):
        sl = slice(hh * LANES, (hh + 1) * LANES)
        mq_ref[0, hh] = ((q1[:, sl] * cos + q2[:, sl] * sin) * MLA_QSCALE).astype(BF16)
        mk_ref[0, hh] = (kn[:, sl] + k_rope).astype(BF16)
        mv_ref[0, hh] = (vv[:, sl] + ones_col).astype(BF16)

    nq_ref[0] = (proj(C_NAT, C_NAT + 256) * NAT_QSCALE).astype(BF16)
    nk_ref[0] = proj(C_NAT + 256, C_NAT + 512).astype(BF16)
    nv_ref[0] = proj(C_NAT + 512, C_NAT + 768).astype(BF16)
    gzg_ref[0] = proj(C_GZG, C_SZG)
    szg_ref[0] = proj(C_SZG, C_SXBC)


def _in_projection(h, mod, g, lw, cos, sin, *, tm):
    nb, nt, dm = h.shape
    per_batch = mod.shape[0] > 1
    hb = tm // HALO
    nhalo = nt // HALO
    tok = lambda w, dt: jax.ShapeDtypeStruct((nb, nt, w), dt)
    head = jax.ShapeDtypeStruct((nb, N_HEADS, nt, LANES), BF16)
    tspec = lambda w: pl.BlockSpec((1, tm, w), lambda b, i: (b, i, 0))
    hspec = pl.BlockSpec((1, N_HEADS, tm, LANES), lambda b, i: (b, 0, i, 0))
    return pl.pallas_call(
        _inproj_kernel,
        out_shape=(head, head, head, tok(256, BF16), tok(256, BF16), tok(256, BF16),
                   tok(768, F32), tok(256, F32), tok(256, F32), tok(768, F32), tok(LANES, F32),
                   tok(LANES, F32)),
        grid=(nb, nt // tm),
        in_specs=[tspec(dm),
                  pl.BlockSpec((1, HALO, dm), lambda b, i: (b, jnp.maximum(i * hb - 1, 0), 0)),
                  pl.BlockSpec((1, HALO, dm), lambda b, i: (b, jnp.minimum((i + 1) * hb, nhalo - 1), 0)),
                  pl.BlockSpec((1, N_MOD, dm), lambda b, i: (b if per_batch else 0, 0, 0)),
                  _full(g.shape), _resident(lw["w_in"].shape),
                  _full(lw["mla_gq"].shape), _full(lw["mla_gkv"].shape),
                  _full(lw["wq1"].shape), _full(lw["wq2"].shape),
                  _full(lw["wk"].shape), _full(lw["wv"].shape),
                  pl.BlockSpec((tm, LANES), lambda b, i: (i, 0)),
                  pl.BlockSpec((tm, LANES), lambda b, i: (i, 0)),
                  _full(lw["gdn_conv_w"].shape), _full(lw["ssm_conv_w"].shape),
                  _full(lw["ssm_conv_b"].shape), _full(lw["gate_pv"].shape)],
        out_specs=(hspec, hspec, hspec, tspec(256), tspec(256), tspec(256),
                   tspec(768), tspec(256), tspec(256), tspec(768), tspec(LANES), tspec(LANES)),
        compiler_params=pltpu.CompilerParams(vmem_limit_bytes=VMEM_LIMIT),
        name="in_projection",
    )(h, h, h, mod, g, lw["w_in"], lw["mla_gq"], lw["mla_gkv"], lw["wq1"], lw["wq2"], lw["wk"], lw["wv"],
      cos, sin, lw["gdn_conv_w"], lw["ssm_conv_w"], lw["ssm_conv_b"], lw["gate_pv"])


def _mla_kernel(*refs, n_lat_chunks, tk):
    if n_lat_chunks:
        q_ref, kl_ref, vl_ref, kc_ref, vc_ref, o_ref = refs
    else:
        q_ref, kc_ref, vc_ref, o_ref = refs
    q = q_ref[0, 0]
    tq = q.shape[0]

    def step(kb, vb, carry):
        m, acc = carry
        s = _dot_nt(q, kb)
        mn = jnp.maximum(m, jnp.max(s, axis=-1, keepdims=True))
        p = jnp.exp2(s - mn)
        acc = jnp.exp2(m - mn) * acc + _dot(p.astype(BF16), vb)
        return mn, acc

    carry = (jnp.full((tq, 1), NEG_INF, F32), jnp.zeros((tq, LANES), F32))
    if n_lat_chunks:
        def body(j, carry):
            off = pl.multiple_of(j * tk, tk)
            return step(kl_ref[0, 0, pl.ds(off, tk), :], vl_ref[0, 0, pl.ds(off, tk), :], carry)
        carry = lax.fori_loop(0, n_lat_chunks, body, carry, unroll=8)
    _, acc = step(kc_ref[0, 0], vc_ref[0, 0], carry)
    o_ref[0, 0] = acc / acc[:, HEAD_W:HEAD_W + 1]


def _mla_attention(q, k_ctx, v_ctx, k_lat=None, v_lat=None, *, tq, tk):
    nb, nh, nq, _ = q.shape
    nc = k_ctx.shape[2]
    qspec = pl.BlockSpec((1, 1, tq, LANES), lambda b, h, i: (b, h, i, 0))
    cspec = pl.BlockSpec((1, 1, nc, LANES), lambda b, h, i: (b, h, 0, 0))
    if k_lat is None:
        args, specs, n_chunks = (q, k_ctx, v_ctx), [qspec, cspec, cspec], 0
    else:
        nk = k_lat.shape[2]
        lspec = pl.BlockSpec((1, 1, nk, LANES), lambda b, h, i: (b, h, 0, 0))
        args, specs, n_chunks = (q, k_lat, v_lat, k_ctx, v_ctx), [qspec, lspec, lspec, cspec, cspec], nk // tk
    return pl.pallas_call(
        functools.partial(_mla_kernel, n_lat_chunks=n_chunks, tk=tk),
        out_shape=jax.ShapeDtypeStruct((nb, nh, nq, LANES), F32),
        grid=(nb, nh, nq // tq),
        in_specs=specs,
        out_specs=qspec,
        compiler_params=pltpu.CompilerParams(vmem_limit_bytes=VMEM_LIMIT),
        name="mla_attention",
    )(*args)


NAT_QROWS = 4
NAT_QB = NAT_QROWS * GRID_W
NAT_KBLKS = 3


def _heads_attention(q, parts, gout):
    lane_head = _iota((1, GROUP_W), 1) // HEAD_W
    heads = range(N_HEADS)
    sels = [lane_head == hh for hh in heads]
    qhs = [jnp.where(sel, q, jnp.zeros_like(q)) for sel in sels]
    scores = [[_dot_nt(qh, k) if bias is None else _dot_nt(qh, k) + bias[hh] for k, _, bias in parts]
              for hh, qh in zip(heads, qhs)]
    ms = [functools.reduce(jnp.maximum, [jnp.max(s, axis=-1, keepdims=True) for s in sc]) for sc in scores]
    ps = [[jnp.exp(s - m) for s in sc] for sc, m in zip(scores, ms)]
    ls = [functools.reduce(jnp.add, [jnp.sum(p, axis=-1, keepdims=True) for p in pp]) for pp in ps]
    os_ = [functools.reduce(jnp.add, [_dot(p.astype(BF16), v) for p, (_, v, _) in zip(pp, parts)])
           for pp in ps]
    out = jnp.zeros(q.shape, F32)
    for sel, o, l in zip(sels, os_, ls):
        out = jnp.where(sel, o / l, out)
    return _rms(out, gout)


def _nat_kernel(q_ref, k0_ref, k1_ref, k2_ref, v0_ref, v1_ref, v2_ref, kc_ref, vc_ref, bias_ref, g_ref,
                o_ref):
    kw = jnp.concatenate([k0_ref[0], k1_ref[0], k2_ref[0]], axis=0)
    vw = jnp.concatenate([v0_ref[0], v1_ref[0], v2_ref[0]], axis=0)
    parts = [(kw, vw, bias_ref[0]), (kc_ref[0], vc_ref[0], None)]
    o_ref[0] = _heads_attention(q_ref[0], parts, g_ref[...])


def _nat_ctx_kernel(q_ref, k_ref, v_ref, g_ref, o_ref):
    o_ref[0] = _heads_attention(q_ref[0], [(k_ref[0], v_ref[0], None)], g_ref[...])


def _nat_attention(q, k, v, kc, vc, bias, gout):
    nb, nt, _ = q.shape
    nblk = nt // NAT_QB
    nc = kc.shape[1]
    start = lambda i: jnp.clip(i - 1, 0, nblk - NAT_KBLKS)
    variant = lambda i: jnp.where(i == 0, 0, jnp.where(i == nblk - 1, 2, 1))
    qspec = pl.BlockSpec((1, NAT_QB, GROUP_W), lambda b, i: (b, i, 0))
    kspecs = [pl.BlockSpec((1, NAT_QB, GROUP_W), lambda b, i, j=j: (b, start(i) + j, 0))
              for j in range(NAT_KBLKS)]
    cspec = pl.BlockSpec((1, nc, GROUP_W), lambda b, i: (b, 0, 0))
    bspec = pl.BlockSpec((1, N_HEADS, NAT_QB, NAT_KBLKS * NAT_QB), lambda b, i: (variant(i), 0, 0, 0))
    return pl.pallas_call(
        _nat_kernel,
        out_shape=jax.ShapeDtypeStruct((nb, nt, GROUP_W), F32),
        grid=(nb, nblk),
        in_specs=[qspec] + kspecs + kspecs + [cspec, cspec, bspec, _full(gout.shape)],
        out_specs=qspec,
        name="nat_attention",
    )(q, k, k, k, v, v, v, kc, vc, bias, gout)


def _nat_ctx_attention(q, k, v, gout):
    nb, nc, _ = q.shape
    spec = pl.BlockSpec((1, nc, GROUP_W), lambda b: (b, 0, 0))
    return pl.pallas_call(
        _nat_ctx_kernel,
        out_shape=jax.ShapeDtypeStruct((nb, nc, GROUP_W), F32),
        grid=(nb,),
        in_specs=[spec, spec, spec, _full(gout.shape)],
        out_specs=spec,
        name="nat_ctx_attention",
    )(q, k, v, gout)


def _nat_bias_constants(n_rows):
    krows = NAT_KBLKS * NAT_QROWS
    qr = np.arange(NAT_QROWS)[:, None]
    kk = np.arange(krows)[None, :]
    cq = np.arange(GRID_W)[:, None]
    ck = np.arange(GRID_W)[None, :]
    c0 = np.clip(cq - NAT_KC // 2, 0, GRID_W - NAT_KC)
    col_ok = ((ck >= c0) & (ck < c0 + NAT_KC)).reshape(-1)
    dc = np.clip(ck - cq + (NAT_KC - 1), 0, 2 * NAT_KC - 2).reshape(-1)
    e_col = np.zeros((2 * NAT_KC, GRID_W * GRID_W), np.float32)
    e_col[dc, np.arange(GRID_W * GRID_W)] = 1.0
    big = 4 * n_rows + 64
    placements = [(0, 0, n_rows), (big // 2, big // 2 - NAT_QROWS, big),
                  (n_rows - NAT_QROWS, n_rows - krows, n_rows)]
    e_row = np.zeros((3, NAT_QROWS * krows, 2 * NAT_KR), np.float32)
    ok = np.zeros((3, NAT_QROWS * krows, GRID_W * GRID_W), np.float32)
    for v, (r_base, k_start, rows_total) in enumerate(placements):
        r = r_base + qr
        k_abs = k_start + kk
        r0 = np.clip(r - NAT_KR // 2, 0, rows_total - NAT_KR)
        row_ok = ((k_abs >= r0) & (k_abs < r0 + NAT_KR)).reshape(-1)
        dr = np.clip(k_abs - r + (NAT_KR - 1), 0, 2 * NAT_KR - 2).reshape(-1)
        e_row[v, np.arange(NAT_QROWS * krows), dr] = 1.0
        ok[v] = row_ok[:, None] & col_ok[None, :]
    return e_row, e_col, ok


def _split3(x):
    h1 = x.astype(BF16)
    r1 = x - h1.astype(F32)
    h2 = r1.astype(BF16)
    h3 = (r1 - h2.astype(F32)).astype(BF16)
    return h1, h2, h3


def _nat_bias_kernel(er_ref, rpb_ref, ec_ref, ok_ref, o_ref):
    er = er_ref[0]
    rows = functools.reduce(jnp.add, [_dot(er, p) for p in _split3(rpb_ref[0, 0])])
    ec = ec_ref[...]
    b = functools.reduce(jnp.add, [_dot(p, ec) for p in _split3(rows)])
    o_ref[0, 0, 0] = jnp.where(ok_ref[0] > 0.0, b, NEG_INF)


def _nat_bias_tables(rpb_all, n_rows):
    nl, nh = rpb_all.shape[:2]
    krows = NAT_KBLKS * NAT_QROWS
    e_row, e_col, ok = _nat_bias_constants(n_rows)
    rpb = jnp.pad(rpb_all, ((0, 0), (0, 0), (0, 1), (0, 1)))
    nr, ncol = NAT_QROWS * krows, GRID_W * GRID_W
    out = pl.pallas_call(
        _nat_bias_kernel,
        out_shape=jax.ShapeDtypeStruct((nl, 3, nh, nr, ncol), F32),
        grid=(nl, 3, nh),
        in_specs=[pl.BlockSpec((1, nr, 2 * NAT_KR), lambda l, v, h: (v, 0, 0)),
                  pl.BlockSpec((1, 1, 2 * NAT_KR, 2 * NAT_KC), lambda l, v, h: (l, h, 0, 0)),
                  pl.BlockSpec((2 * NAT_KC, ncol), lambda l, v, h: (0, 0)),
                  pl.BlockSpec((1, nr, ncol), lambda l, v, h: (v, 0, 0))],
        out_specs=pl.BlockSpec((1, 1, 1, nr, ncol), lambda l, v, h: (l, v, h, 0, 0)),
        name="nat_bias_tables",
    )(jnp.asarray(e_row, BF16), rpb, jnp.asarray(e_col, BF16), jnp.asarray(ok))
    out = out.reshape(nl, 3, nh, NAT_QROWS, krows, GRID_W, GRID_W).transpose(0, 1, 2, 3, 5, 4, 6)
    return out.reshape(nl, 3, nh, NAT_QB, krows * GRID_W)


def _chunk_cumsum_matrix(n, chunk, reverse):
    r = _iota((n, n), 0)
    c = _iota((n, n), 1)
    same = (r // chunk) == (c // chunk)
    return (same & ((c >= r) if reverse else (c <= r))).astype(BF16)


def _expand_matrix(lane0, group, width):
    return (_iota((LANES, width), 0) == lane0 + _iota((LANES, width), 1) // group).astype(BF16)


def _row_form(col_vals, chunk):
    pick = _iota(col_vals.shape, 0) == (_iota(col_vals.shape, 1) % chunk)
    ones = jnp.ones((SUBLANES, chunk), BF16)
    return _dot2l(ones, jnp.where(pick, col_vals, 0.0))[0:1]


def _heads_diag(x):
    return _block_diag(x, N_HEADS, GDN_CHUNK, HEAD_W)


def _heads_undiag(x):
    keep = (_iota(x.shape, 0) // GDN_CHUNK) == (_iota(x.shape, 1) // HEAD_W)
    x = jnp.where(keep, x, 0.0)
    return functools.reduce(jnp.add, [x[hh * GDN_CHUNK:(hh + 1) * GDN_CHUNK] for hh in range(N_HEADS)])


def _pc3(x, y):
    xh, xl = _split(x)
    yh, yl = _split(y)
    r = _dot(jnp.concatenate([xh, xl], axis=0), _heads_diag(yh))
    n = x.shape[0]
    return r[0:n] + r[n:2 * n] + _dot(xh, _heads_diag(yl))


def _pc1(x, y):
    return _dot(x.astype(BF16), _heads_diag(y.astype(BF16)))


def _gdn_chunk_kernel(y_ref, sm_ref, g_ref, qs_ref, qe_ref, o0_ref, gl_ref, *, d, n_chunks):
    ch = GDN_CHUNK
    cb = n_chunks * ch
    reverse = d == 1
    sm = sm_ref[0]
    tri = _chunk_cumsum_matrix(ch, ch, reverse)
    cs = jnp.concatenate([_dot2l(tri, sm[c * ch:(c + 1) * ch]) for c in range(n_chunks)], axis=0)
    gc_all = _dot2(cs, _expand_matrix(L_G + N_HEADS * d, HEAD_W, GROUP_W))
    beta_all = _dot2(sm, _expand_matrix(L_BETA + N_HEADS * d, HEAD_W, GROUP_W))
    y = y_ref[0]
    q_all, k_all, v_all = y[:, 0:256], y[:, 256:512], y[:, 512:768]
    kb_all = k_all * beta_all
    vb_all = v_all * beta_all
    eg_all = jnp.exp(gc_all)

    ti = _iota((ch, GROUP_W), 0)
    tj = _iota((ch, GROUP_W), 1) % ch
    incl = (tj >= ti) if reverse else (tj <= ti)
    strict = (tj > ti) if reverse else (tj < ti)
    last = 0 if reverse else ch - 1
    cs_ = range(n_chunks)
    rows = [slice(c * ch, (c + 1) * ch) for c in cs_]

    gc = [gc_all[r] for r in rows]
    decay = [jnp.where(incl, jnp.exp(jnp.where(incl, g - _row_form(g, ch), 0.0)), 0.0) for g in gc]
    qk = [_dot_nt(jnp.concatenate([kb_all[r], q_all[r]], axis=0).astype(BF16),
                  _heads_diag(k_all[r].astype(BF16))) for r in rows]
    a_mat = [jnp.where(strict, x[0:ch] * dc, 0.0) for x, dc in zip(qk, decay)]
    a_intra = [(x[ch:2 * ch] * dc).astype(BF16) for x, dc in zip(qk, decay)]

    base = SUBLANES
    eye = jnp.where(ti == tj, 1.0, 0.0)
    m = [jnp.where(ti // base == tj // base, -a, 0.0) for a in a_mat]
    t = [eye + x for x in m]
    m = [_pc1(x, x) for x in m]
    both = [_pc1(jnp.concatenate([x, p], axis=0), p) for x, p in zip(t, m)]
    t = [x + r[0:ch] for x, r in zip(t, both)]
    t = [x + _pc1(x, r[ch:2 * ch]) for x, r in zip(t, both)]
    size = 2 * base
    while size <= ch:
        off = (ti // size == tj // size) & (ti // (size // 2) != tj // (size // 2))
        ct = [_pc1(jnp.where(off, a, 0.0), x) for a, x in zip(a_mat, t)]
        t = [x - _pc1(x, p) for x, p in zip(t, ct)]
        size *= 2
    resid = [eye - x - _pc3(a, x) for a, x in zip(a_mat, t)]
    t = [x + _pc1(x, r) for x, r in zip(t, resid)]

    u = [_pc1(x, vb_all[r]) for x, r in zip(t, rows)]
    w = [_pc1(x, kb_all[r] * eg_all[r]) for x, r in zip(t, rows)]
    for c in cs_:
        r = rows[c]
        g_last = gc[c][last:last + 1]
        k_dec = (k_all[r] * jnp.exp(g_last - gc[c])).astype(BF16)
        wu = jnp.concatenate([w[c], u[c]], axis=1).astype(BF16)
        full = _dot_tn(k_dec, wu)
        g_ref[0, r, :] = _heads_undiag(full[:, 0:GROUP_W]).astype(BF16)
        qs_ref[0, r, :] = _heads_undiag(full[:, GROUP_W:])
        wu_bd = jnp.concatenate([_heads_diag(wu[:, 0:GROUP_W]), _heads_diag(wu[:, GROUP_W:])], axis=1)
        aw = _dot(a_intra[c], wu_bd)
        qe_ref[0, r, :] = (q_all[r] * eg_all[r] - aw[:, 0:GROUP_W]).astype(BF16)
        o0_ref[0, r, :] = aw[:, GROUP_W:]
        gl_ref[0, c:c + 1, :] = jnp.exp(g_last)


def _gdn_state_kernel(g_ref, qs_ref, qe_ref, o0_ref, gl_ref, s0_ref, o_ref, sfin_ref, state_ref,
                      *, d, n_chunks):
    ch = GDN_CHUNK
    nb = g_ref.shape[0]
    step = pl.program_id(0)

    @pl.when(step == 0)
    def _():
        state_ref[...] = s0_ref[...]

    order = range(n_chunks - 1, -1, -1) if d == 1 else range(n_chunks)
    state = [state_ref[b] for b in range(nb)]
    for c in order:
        r = slice(c * ch, (c + 1) * ch)
        for b in range(nb):
            both = jnp.concatenate([g_ref[b, r, :], qe_ref[b, r, :]], axis=0)
            res = _dot(both, _heads_diag(state[b].astype(BF16)))
            o_ref[b, r, :] = res[ch:2 * ch] + o0_ref[b, r, :]
            state[b] = state[b] * gl_ref[b, c:c + 1, :] - res[0:ch] + qs_ref[b, r, :]
    for b in range(nb):
        state_ref[b] = state[b]

    @pl.when(step == pl.num_programs(0) - 1)
    def _():
        for b in range(nb):
            sfin_ref[b] = state[b]


def _gdn_chunk_call(y, sm, *, d, cb):
    nb, nt, nc = y.shape
    n_chunks = cb // GDN_CHUNK
    tok = lambda dt: jax.ShapeDtypeStruct((nb, nt, GROUP_W), dt)
    tspec = pl.BlockSpec((1, cb, GROUP_W), lambda bb, i: (bb, i, 0))
    return pl.pallas_call(
        functools.partial(_gdn_chunk_kernel, d=d, n_chunks=n_chunks),
        out_shape=(tok(BF16), tok(F32), tok(BF16), tok(F32),
                   jax.ShapeDtypeStruct((nb, nt // GDN_CHUNK, GROUP_W), F32)),
        grid=(nb, nt // cb),
        in_specs=[pl.BlockSpec((1, cb, nc), lambda bb, i: (bb, i, 0)),
                  pl.BlockSpec((1, cb, LANES), lambda bb, i: (bb, i, 0))],
        out_specs=(tspec, tspec, tspec, tspec,
                   pl.BlockSpec((1, n_chunks, GROUP_W), lambda bb, i: (bb, i, 0))),
        name="gdn_chunk",
    )(y, sm)


def _gdn_state_call(parts, s0, *, d, cb):
    g, qs, qe, o0, gl = parts
    nb, nt, _ = g.shape
    n_chunks = cb // GDN_CHUNK
    nblk = nt // cb
    blk = (lambda i: (0, nblk - 1 - i, 0)) if d == 1 else (lambda i: (0, i, 0))
    tspec = pl.BlockSpec((nb, cb, GROUP_W), blk)
    sspec = pl.BlockSpec((nb, GDN_CHUNK, GROUP_W), lambda i: (0, 0, 0))
    return pl.pallas_call(
        functools.partial(_gdn_state_kernel, d=d, n_chunks=n_chunks),
        out_shape=(jax.ShapeDtypeStruct((nb, nt, GROUP_W), F32),
                   jax.ShapeDtypeStruct((nb, GDN_CHUNK, GROUP_W), F32)),
        grid=(nblk,),
        in_specs=[tspec, tspec, tspec, tspec, pl.BlockSpec((nb, n_chunks, GROUP_W), blk), sspec],
        out_specs=(tspec, sspec),
        scratch_shapes=[pltpu.VMEM((nb, GDN_CHUNK, GROUP_W), F32)],
        compiler_params=pltpu.CompilerParams(dimension_semantics=("arbitrary",)),
        name="gdn_state_scan",
    )(g, qs, qe, o0, gl, s0)


def _gdn_bidir(y_l, sm_l, y_c, sm_c, *, cb_l, cb_c):
    zero = jnp.zeros((y_l.shape[0], GDN_CHUNK, GROUP_W), F32)
    outs = []
    for d in (0, 1):
        o_c, s_c = _gdn_state_call(_gdn_chunk_call(y_c, sm_c, d=d, cb=cb_c), zero, d=d, cb=cb_c)
        o_l, _ = _gdn_state_call(_gdn_chunk_call(y_l, sm_l, d=d, cb=cb_l), s_c, d=d, cb=cb_l)
        outs.append((o_l, o_c))
    return outs


def _ssd_kernel(x_ref, dt_ref, da_ref, s0_ref, o_ref, sfin_ref, state_ref, *, d, n_chunks):
    ch = SSM_CHUNK
    reverse = d == 1
    nb = x_ref.shape[0]
    step = pl.program_id(0)

    @pl.when(step == 0)
    def _():
        state_ref[...] = s0_ref[...]

    lane0 = L_DT + N_HEADS * d
    tri = _chunk_cumsum_matrix(ch, ch, reverse)
    e64 = _expand_matrix(lane0, HEAD_W, GROUP_W)
    e128 = _expand_matrix(lane0, ch, N_HEADS * ch)
    ti = _iota((ch, N_HEADS * ch), 0)
    tj = _iota((ch, N_HEADS * ch), 1) % ch
    incl = (tj >= ti) if reverse else (tj <= ti)
    grp_keep = (_iota((GROUP_W, GROUP_W), 0) // ch) == (_iota((GROUP_W, GROUP_W), 1) // ch)
    last = 0 if reverse else ch - 1
    rows = [slice(c * ch, (c + 1) * ch) for c in range(n_chunks)]

    def chunk_terms(b):
        da = da_ref[b]
        cs = jnp.concatenate([_dot2l(tri, da[r]) for r in rows], axis=0)
        ac_all = _dot2(cs, e64)
        ac5_all = _dot2(cs, e128)
        xbc = x_ref[b]
        xdt_all = xbc[:, 0:256] * _dot2(dt_ref[b], e64)
        b_all, c_all = xbc[:, 256:512], xbc[:, 512:768]
        seg = [jnp.where(incl, jnp.exp(jnp.where(incl, ac5_all[r] - _row_form(ac5_all[r], ch), 0.0)), 0.0)
               for r in rows]
        cm = [c_all[r].astype(BF16) for r in rows]
        bmb = [b_all[r].astype(BF16) for r in rows]
        cb_g = [_dot_nt(c_, _block_diag(b_, 2, ch, SSM_STATE)) for c_, b_ in zip(cm, bmb)]
        scores = [(jnp.concatenate([x[:, 0:ch], x[:, 0:ch], x[:, ch:], x[:, ch:]], axis=1) * sg).astype(BF16)
                  for x, sg in zip(cb_g, seg)]
        y_diag = [_dot(s_, _block_diag(xdt_all[r].astype(BF16), N_HEADS, ch, HEAD_W))
                  for s_, r in zip(scores, rows)]
        a_last = [ac_all[r][last:last + 1] for r in rows]
        states = [jnp.where(grp_keep, _dot_tn(b_, (xdt_all[r] * jnp.exp(al - ac_all[r])).astype(BF16)), 0.0)
                  for b_, r, al in zip(bmb, rows, a_last)]
        return cm, y_diag, a_last, states, [jnp.exp(ac_all[r]) for r in rows]

    terms = [chunk_terms(b) for b in range(nb)]
    state = [state_ref[b] for b in range(nb)]
    for c in (range(n_chunks - 1, -1, -1) if reverse else range(n_chunks)):
        for b in range(nb):
            cm, y_diag, a_last, states, e_ac = terms[b]
            o_ref[b, rows[c], :] = y_diag[c] + _dot(cm[c], state[b].astype(BF16)) * e_ac[c]
            state[b] = state[b] * jnp.exp(a_last[c]) + states[c]
    for b in range(nb):
        state_ref[b] = state[b]

    @pl.when(step == pl.num_programs(0) - 1)
    def _():
        for b in range(nb):
            sfin_ref[b] = state[b]


def _scan_call(kernel, name, seq, smalls, s0, *, d, cb, chunk):
    nb, nt, nc = seq.shape
    nblk = nt // cb
    blk = (lambda i: (0, nblk - 1 - i, 0)) if d == 1 else (lambda i: (0, i, 0))
    sspec = pl.BlockSpec((nb, GROUP_W, GROUP_W), lambda i: (0, 0, 0))
    return pl.pallas_call(
        functools.partial(kernel, d=d, n_chunks=cb // chunk),
        out_shape=(jax.ShapeDtypeStruct((nb, nt, GROUP_W), F32),
                   jax.ShapeDtypeStruct((nb, GROUP_W, GROUP_W), F32)),
        grid=(nblk,),
        in_specs=[pl.BlockSpec((nb, cb, nc), blk)] + [pl.BlockSpec((nb, cb, LANES), blk) for _ in smalls]
                 + [sspec],
        out_specs=(pl.BlockSpec((nb, cb, GROUP_W), blk), sspec),
        scratch_shapes=[pltpu.VMEM((nb, GROUP_W, GROUP_W), F32)],
        compiler_params=pltpu.CompilerParams(dimension_semantics=("arbitrary",)),
        name=name,
    )(seq, *smalls, s0)


def _bidir_scan(kernel, name, seq_l, smalls_l, seq_c, smalls_c, *, cb_l, cb_c, chunk):
    zero = jnp.zeros((seq_l.shape[0], GROUP_W, GROUP_W), F32)
    outs = []
    for d in (0, 1):
        o_c, s_c = _scan_call(kernel, name, seq_c, smalls_c, zero, d=d, cb=cb_c, chunk=chunk)
        o_l, _ = _scan_call(kernel, name, seq_l, smalls_l, s_c, d=d, cb=cb_l, chunk=chunk)
        outs.append((o_l, o_c))
    return outs


def _outproj_kernel(h_ref, mod_ref, g_ref, mla_ref, nat_ref, gof_ref, gob_ref, gzg_ref,
                    yf_ref, yb_ref, sx_ref, szg_ref, pv_ref, wm_ref, wr_ref, o_ref):
    pv = pv_ref[...]
    valid = _iota((1, LANES), 1) < HEAD_W
    slabs = [jnp.where(valid, mla_ref[0, hh], 0.0) for hh in range(N_HEADS)]
    ss = functools.reduce(jnp.add, [jnp.sum(s * s, axis=-1, keepdims=True) for s in slabs])
    scale = lax.rsqrt(ss / GROUP_W + RMS_EPS)
    y = None
    for hh in range(N_HEADS):
        part = _dot((slabs[hh] * scale * pv[hh:hh + 1, 0:LANES]).astype(BF16), wm_ref[hh])
        y = part if y is None else y + part
    y = y + _dot(nat_ref[0].astype(BF16), wr_ref[0:256, :])
    grp = (_iota((GROUP_W, GROUP_W), 0) // HEAD_W == _iota((GROUP_W, GROUP_W), 1) // HEAD_W).astype(BF16)
    o = gof_ref[0] + gob_ref[0]
    on = o * lax.rsqrt(_dot2(o * o, grp) / HEAD_W + RMS_EPS) * pv[4:5]
    y = y + _dot((on * _silu(gzg_ref[0])).astype(BF16), wr_ref[256:512, :])
    s = (yf_ref[0] + yb_ref[0] + sx_ref[0] * pv[5:6]) * _silu(szg_ref[0])
    sn = jnp.concatenate([_rms(s[:, 0:LANES], pv[6:7, 0:LANES]),
                          _rms(s[:, LANES:], pv[6:7, LANES:])], axis=1)
    y = y + _dot(sn.astype(BF16), wr_ref[512:768, :])
    m = mod_ref[0]
    g = g_ref[...]
    o_ref[0] = h_ref[0] + m[5:6] * _rms(y, g[3:4])


def _out_projection(h, mod, g, mla_o, nat_o, gdn_f, gdn_b, gdn_zg, ssm_f, ssm_b, ssm_xbc, ssm_zg,
                    pv, w_mla, w_rest, *, tm):
    nb, nt, dm = h.shape
    per_batch = mod.shape[0] > 1
    tspec = lambda w: pl.BlockSpec((1, tm, w), lambda b, i: (b, i, 0))
    return pl.pallas_call(
        _outproj_kernel,
        out_shape=jax.ShapeDtypeStruct(h.shape, F32),
        grid=(nb, nt // tm),
        in_specs=[tspec(dm),
                  pl.BlockSpec((1, N_MOD, dm), lambda b, i: (b if per_batch else 0, 0, 0)),
                  _full(g.shape),
                  pl.BlockSpec((1, N_HEADS, tm, LANES), lambda b, i: (b, 0, i, 0)),
                  tspec(256), tspec(256), tspec(256), tspec(256), tspec(256), tspec(256),
                  tspec(256),
                  tspec(256),
                  _full(pv.shape), _full(w_mla.shape), _full(w_rest.shape)],
        out_specs=tspec(dm),
        name="out_projection",
    )(h, mod, g, mla_o, nat_o, gdn_f, gdn_b, gdn_zg, ssm_f, ssm_b, ssm_xbc, ssm_zg, pv, w_mla, w_rest)


_ROPE_SWAP = np.array(list(range(8, 16)) + list(range(0, 8)) + list(range(24, 32)) + list(range(16, 24)))


def _pack_layer(p, l):
    w_in = p["w_in"][l]
    dm = w_in.shape[0]
    o_nat = MLA_Q_LORA + MLA_KV_LORA + MLA_ROPE
    o_gdn = o_nat + 768
    o_ssm = o_gdn + GDN_QKV + GROUP_W + 4 * N_HEADS
    kr = w_in[:, MLA_Q_LORA + MLA_KV_LORA:o_nat]
    z64 = jnp.zeros((dm, 64), F32)
    z32 = jnp.zeros((dm, 32), F32)
    small = jnp.concatenate([w_in[:, o_gdn + 1024:o_gdn + 1040],
                             w_in[:, o_ssm + 1024:o_ssm + 1032],
                             jnp.zeros((dm, LANES - 24), F32)], axis=1)
    w_packed = jnp.concatenate([
        w_in[:, 0:384], z64, kr, z32, z64, kr[:, _ROPE_SWAP], z32,
        w_in[:, o_nat:o_gdn],
        w_in[:, o_gdn:o_gdn + 1024],
        w_in[:, o_ssm:o_ssm + 1024],
        small], axis=1).astype(BF16)
    assert w_packed.shape[1] == C_TOTAL

    wuq = p["mla_wuq"][l].reshape(MLA_Q_LORA, N_HEADS, MLA_NOPE + MLA_ROPE)
    zq = jnp.zeros((MLA_Q_LORA, N_HEADS, 32), F32)
    wq1 = jnp.concatenate([wuq, zq], axis=2).reshape(MLA_Q_LORA, N_HEADS * LANES)
    wq2 = jnp.concatenate([jnp.zeros((MLA_Q_LORA, N_HEADS, 64), F32), wuq[:, :, MLA_NOPE:][:, :, _ROPE_SWAP], zq],
                          axis=2).reshape(MLA_Q_LORA, N_HEADS * LANES)
    wukv = p["mla_wukv"][l].reshape(MLA_KV_LORA, N_HEADS, MLA_NOPE + HEAD_W)
    zk = jnp.zeros((MLA_KV_LORA, N_HEADS, 64), F32)
    wk = jnp.concatenate([wukv[:, :, :MLA_NOPE], zk], axis=2).reshape(MLA_KV_LORA, N_HEADS * LANES)
    wv = jnp.concatenate([wukv[:, :, MLA_NOPE:], zk], axis=2).reshape(MLA_KV_LORA, N_HEADS * LANES)

    def lanes(vals, lane0):
        v = vals.reshape(-1)
        return jnp.zeros((LANES,), F32).at[lane0:lane0 + v.shape[0]].set(v)

    w_out = p["w_out"][l]
    w_mla = jnp.concatenate([w_out[0:256].reshape(N_HEADS, HEAD_W, dm),
                             jnp.zeros((N_HEADS, LANES - HEAD_W, dm), F32)], axis=1).astype(BF16)
    gout = jnp.concatenate([p["mla_gout"][l].reshape(N_HEADS, HEAD_W),
                            jnp.zeros((N_HEADS, GROUP_W - HEAD_W), F32)], axis=1)
    pv_out = jnp.concatenate([gout,
                              jnp.tile(p["gdn_gnorm"][l], N_HEADS)[None],
                              jnp.repeat(p["ssm_d"][l], HEAD_W)[None],
                              p["ssm_gnorm"][l][None],
                              jnp.zeros((1, GROUP_W), F32)], axis=0)
    return dict(
        ffn1=(p["ffn1_w1"][l].astype(BF16), p["ffn1_w3"][l].astype(BF16), p["ffn1_w2"][l].astype(BF16)),
        ffn2=(p["ffn2_w1"][l].astype(BF16), p["ffn2_w3"][l].astype(BF16), p["ffn2_w2"][l].astype(BF16)),
        g=p["norm_g"][l],
        w_in=w_packed, wq1=wq1.astype(BF16), wq2=wq2.astype(BF16), wk=wk.astype(BF16), wv=wv.astype(BF16),
        mla_gq=p["mla_gq"][l][None], mla_gkv=p["mla_gkv"][l][None],
        nat_bias=p["nat_bias"][l], nat_gout=p["nat_gout"][l][None],
        gdn_conv_w=p["gdn_conv_w"][l],
        ssm_conv_w=p["ssm_conv_w"][l], ssm_conv_b=p["ssm_conv_b"][l][None],
        gate_pv=jnp.stack([lanes(p["gdn_a_log"][l], L_G) + lanes(p["ssm_a_log"][l], L_DT),
                           lanes(p["gdn_dt_bias"][l], L_G) + lanes(p["ssm_dt_bias"][l], L_DT)]),
        pv_out=pv_out, w_mla=w_mla, w_rest=w_out[256:].astype(BF16),
    )


def _rope_tables(n_tok):
    n_rows = n_tok // GRID_W
    quarter = MLA_ROPE // 4
    freqs = ROPE_THETA ** (-jnp.arange(quarter, dtype=F32) / quarter)
    ar = jnp.arange(n_rows, dtype=F32)[:, None] * freqs
    ac = jnp.arange(GRID_W, dtype=F32)[:, None] * freqs
    per_row = lambda t: jnp.repeat(t, GRID_W, axis=0)
    per_col = lambda t: jnp.tile(t, (n_rows, 1))
    cos = jnp.concatenate([per_row(jnp.cos(ar)), per_row(jnp.cos(ar)),
                           per_col(jnp.cos(ac)), per_col(jnp.cos(ac))], axis=1)
    sin = jnp.concatenate([per_row(-jnp.sin(ar)), per_row(jnp.sin(ar)),
                           per_col(-jnp.sin(ac)), per_col(jnp.sin(ac))], axis=1)
    ones = jnp.ones((n_tok, MLA_NOPE), F32)
    zeros = jnp.zeros((n_tok, MLA_NOPE), F32)
    pad = jnp.zeros((n_tok, LANES - MLA_NOPE - MLA_ROPE), F32)
    return jnp.concatenate([ones, cos, pad], axis=1), jnp.concatenate([zeros, sin, pad], axis=1)


def _tiles(n_tok):
    return dict(tm=min(512, n_tok), tm_ffn=min(512, n_tok), tq=min(1024, n_tok),
                cb_gdn=min(512, n_tok), cb_ssd=min(512, n_tok))


def _mixers(hl, hc, ml, mc, lw, ropes, need_ctx):
    tl, tc = _tiles(hl.shape[1]), _tiles(hc.shape[1])
    (cos_l, sin_l), (cos_c, sin_c) = ropes
    zl = _in_projection(hl, ml, lw["g"], lw, cos_l, sin_l, tm=tl["tm"])
    zc = _in_projection(hc, mc, lw["g"], lw, cos_c, sin_c, tm=tc["tm"])
    (mq_l, mk_l, mv_l, nq_l, nk_l, nv_l, gy_l, gzg_l, szg_l, sy_l, sma_l, smb_l) = zl
    (mq_c, mk_c, mv_c, nq_c, nk_c, nv_c, gy_c, gzg_c, szg_c, sy_c, sma_c, smb_c) = zc

    n_lat = hl.shape[1]
    tk = min(1024, n_lat)
    mla_l = _mla_attention(mq_l, mk_c, mv_c, mk_l, mv_l, tq=tl["tq"], tk=tk)
    nat_l = _nat_attention(nq_l, nk_l, nv_l, nk_c, nv_c, lw["nat_bias"], lw["nat_gout"])
    (gf_l, gf_c), (gb_l, gb_c) = _gdn_bidir(gy_l, sma_l, gy_c, sma_c, cb_l=tl["cb_gdn"], cb_c=tc["cb_gdn"])
    (sf_l, sf_c), (sb_l, sb_c) = _bidir_scan(_ssd_kernel, "ssd_scan", sy_l, (smb_l, sma_l),
                                             sy_c, (smb_c, sma_c),
                                             cb_l=tl["cb_ssd"], cb_c=tc["cb_ssd"], chunk=SSM_CHUNK)

    hl = _out_projection(hl, ml, lw["g"], mla_l, nat_l, gf_l, gb_l, gzg_l, sf_l, sb_l, sy_l, szg_l,
                         lw["pv_out"], lw["w_mla"], lw["w_rest"], tm=tl["tm"])
    if need_ctx:
        mla_c = _mla_attention(mq_c, mk_c, mv_c, tq=tc["tq"], tk=tk)
        nat_c = _nat_ctx_attention(nq_c, nk_c, nv_c, lw["nat_gout"])
        hc = _out_projection(hc, mc, lw["g"], mla_c, nat_c, gf_c, gb_c, gzg_c, sf_c, sb_c, sy_c, szg_c,
                             lw["pv_out"], lw["w_mla"], lw["w_rest"], tm=tc["tm"])
    return hl, hc


def _layer(hl, hc, ml, mc, lw, ropes, need_ctx):
    tl, tc = _tiles(hl.shape[1]), _tiles(hc.shape[1])
    hl = _half_ffn(hl, ml, lw["g"], *lw["ffn1"], k0=0, gp=0, tm=tl["tm_ffn"])
    hc = _half_ffn(hc, mc, lw["g"], *lw["ffn1"], k0=0, gp=0, tm=tc["tm"])
    hl, hc = _mixers(hl, hc, ml, mc, lw, ropes, need_ctx)
    hl = _half_ffn(hl, ml, lw["g"], *lw["ffn2"], k0=6, gp=4, tm=tl["tm_ffn"])
    if need_ctx:
        hc = _half_ffn(hc, mc, lw["g"], *lw["ffn2"], k0=6, gp=4, tm=tc["tm"])
    return hl, hc


def kernel(x, c, ctx, c_ctx, w_mod, b_mod, norm_g, ffn1_w1, ffn1_w3, ffn1_w2, ffn2_w1, ffn2_w3, ffn2_w2,
           w_in, w_out, mla_gq, mla_gkv, mla_wuq, mla_wukv, mla_gout, nat_rpb, nat_gout, gdn_conv_w,
           gdn_a_log, gdn_dt_bias, gdn_gnorm, ssm_conv_w, ssm_conv_b, ssm_a_log, ssm_dt_bias, ssm_d,
           ssm_gnorm):
    p = dict(norm_g=norm_g, ffn1_w1=ffn1_w1, ffn1_w3=ffn1_w3, ffn1_w2=ffn1_w2, ffn2_w1=ffn2_w1,
             ffn2_w3=ffn2_w3, ffn2_w2=ffn2_w2, w_in=w_in, w_out=w_out, mla_gq=mla_gq, mla_gkv=mla_gkv,
             mla_wuq=mla_wuq, mla_wukv=mla_wukv, mla_gout=mla_gout, nat_rpb=nat_rpb, nat_gout=nat_gout,
             gdn_conv_w=gdn_conv_w, gdn_a_log=gdn_a_log, gdn_dt_bias=gdn_dt_bias, gdn_gnorm=gdn_gnorm,
             ssm_conv_w=ssm_conv_w, ssm_conv_b=ssm_conv_b, ssm_a_log=ssm_a_log, ssm_dt_bias=ssm_dt_bias,
             ssm_d=ssm_d, ssm_gnorm=ssm_gnorm)
    nb, n_lat, dm = x.shape
    n_ctx = ctx.shape[1]
    depth = w_mod.shape[0]
    cvec = jnp.concatenate([c, c_ctx[None], jnp.zeros((SUBLANES - nb - 1, dm), F32)], axis=0)
    mods = _modulation(cvec, w_mod, b_mod).reshape(depth, SUBLANES, N_MOD, dm)
    cos_c = jnp.concatenate([jnp.ones((n_ctx, MLA_NOPE + MLA_ROPE), F32),
                             jnp.zeros((n_ctx, LANES - MLA_NOPE - MLA_ROPE), F32)], axis=1)
    ropes = (_rope_tables(n_lat), (cos_c, jnp.zeros((n_ctx, LANES), F32)))
    p["nat_bias"] = _nat_bias_tables(nat_rpb, n_lat // GRID_W)
    hl, hc = x, ctx
    for l in range(depth):
        lw = _pack_layer(p, l)
        hl, hc = _layer(hl, hc, mods[l, 0:nb], mods[l, nb:nb + 1], lw, ropes, need_ctx=l < depth - 1)
    return hl
```

```python
import functools
import math

import jax
import jax.numpy as jnp
import numpy as np
from jax import lax
from jax.experimental import pallas as pl
from jax.experimental.pallas import tpu as pltpu

F32 = jnp.float32
BF16 = jnp.bfloat16

D_MODEL = 1024
DEPTH = 4
GRID_W = 64
N_MOD = 9
D_FF = 2816
RMS_EPS = 1e-6
NEG_INF = -1e30
ROPE_THETA = 10000.0
GROUP_W = 256
N_HEADS = 4
HEAD_W = 64
MLA_NOPE = 64
MLA_ROPE = 32
MLA_Q_LORA = 256
MLA_KV_LORA = 128
NAT_KR = 8
NAT_KC = 16
GDN_CHUNK = 64
SSM_STATE = 128
SSM_CHUNK = 128
CONV_K = 5
GDN_QKV = 768
SSM_XBC = 768

LANES = 128
SUBLANES = 8
VMEM_LIMIT = 56 * 1024 * 1024

C_MLA = 0
C_NAT = 640
C_GQKV = 1408
C_GZG = 2176
C_SZG = 2432
C_SXBC = 2688
C_SMALL = 3456
C_TOTAL = 3584
L_BETA, L_G, L_DT = 0, 8, 16

MLA_QSCALE = (MLA_NOPE + MLA_ROPE) ** -0.5 * math.log2(math.e)
NAT_QSCALE = HEAD_W ** -0.5


def _dot(a, b):
    return jnp.dot(a, b, preferred_element_type=F32)


def _dot_nt(a, b):
    return lax.dot_general(a, b, (((1,), (1,)), ((), ())), preferred_element_type=F32)


def _dot_tn(a, b):
    return lax.dot_general(a, b, (((0,), (0,)), ((), ())), preferred_element_type=F32)


def _split(x):
    hi = x.astype(BF16)
    lo = (x - hi.astype(F32)).astype(BF16)
    return hi, lo


def _dot2(a, m):
    hi, lo = _split(a)
    return _dot(hi, m) + _dot(lo, m)


def _dot2l(m, a):
    hi, lo = _split(a)
    return _dot(m, hi) + _dot(m, lo)


def _rms(x, g):
    return x * lax.rsqrt(jnp.mean(x * x, axis=-1, keepdims=True) + RMS_EPS) * g


def _silu(x):
    return x * jax.nn.sigmoid(x)


def _softplus(x):
    return jnp.maximum(x, 0.0) + jnp.log1p(jnp.exp(-jnp.abs(x)))


def _iota(shape, dim):
    return lax.broadcasted_iota(jnp.int32, shape, dim)


def _block_diag(x, n, blk_r, blk_c):
    t = jnp.concatenate([x] * n, axis=0)
    keep = (_iota(t.shape, 0) // blk_r) == (_iota(t.shape, 1) // blk_c)
    return jnp.where(keep, t, 0.0)


def _full(shape):
    nd = len(shape)
    return pl.BlockSpec(shape, lambda *_: (0,) * nd)


def _resident(shape):
    nd = len(shape)
    return pl.BlockSpec(shape, lambda *_: (0,) * nd, pipeline_mode=pl.Buffered(1))


def _mod_kernel(c_ref, w_ref, b_ref, o_ref):
    s = _silu(c_ref[...])
    o_ref[0] = jnp.dot(s, w_ref[0], preferred_element_type=F32,
                       precision=lax.Precision.HIGHEST) + b_ref[0]


def _modulation(cvec, w_mod, b_mod):
    nl, dm, nm = w_mod.shape
    rows = cvec.shape[0]
    tn = 1536
    return pl.pallas_call(
        _mod_kernel,
        out_shape=jax.ShapeDtypeStruct((nl, rows, nm), F32),
        grid=(nl, nm // tn),
        in_specs=[pl.BlockSpec((rows, dm), lambda l, j: (0, 0)),
                  pl.BlockSpec((1, dm, tn), lambda l, j: (l, 0, j)),
                  pl.BlockSpec((1, 1, tn), lambda l, j: (l, 0, j))],
        out_specs=pl.BlockSpec((1, rows, tn), lambda l, j: (l, 0, j)),
        name="modulation",
    )(cvec, w_mod, b_mod.reshape(nl, 1, nm))


FFN_CHUNK = 256


def _ffn_body(x, m, g, w1_ref, w3_ref, w2_ref, k0, gp):
    u = _rms(x, g[gp:gp + 1]) * (1.0 + m[k0 + 1:k0 + 2]) + m[k0:k0 + 1]
    ub = u.astype(BF16)
    acc = None
    for c in range(D_FF // FFN_CHUNK):
        sl = slice(c * FFN_CHUNK, (c + 1) * FFN_CHUNK)
        a = _dot(ub, w1_ref[:, sl])
        b = _dot(ub, w3_ref[:, sl])
        hid = (_silu(a) * b).astype(BF16)
        part = _dot(hid, w2_ref[sl, :])
        acc = part if acc is None else acc + part
    return x + 0.5 * m[k0 + 2:k0 + 3] * _rms(acc, g[gp + 1:gp + 2])


def _ffn_kernel(h_ref, mod_ref, g_ref, w1_ref, w3_ref, w2_ref, o_ref, *, k0, gp):
    o_ref[0] = _ffn_body(h_ref[0], mod_ref[0], g_ref[...], w1_ref, w3_ref, w2_ref, k0, gp)


def _half_ffn(h, mod, g, w1, w3, w2, *, k0, gp, tm):
    nb, nt, dm = h.shape
    per_batch = mod.shape[0] > 1
    return pl.pallas_call(
        functools.partial(_ffn_kernel, k0=k0, gp=gp),
        out_shape=jax.ShapeDtypeStruct(h.shape, F32),
        grid=(nb, nt // tm),
        in_specs=[pl.BlockSpec((1, tm, dm), lambda b, i: (b, i, 0)),
                  pl.BlockSpec((1, N_MOD, dm), lambda b, i: (b if per_batch else 0, 0, 0)),
                  _full(g.shape), _resident(w1.shape), _resident(w3.shape), _resident(w2.shape)],
        out_specs=pl.BlockSpec((1, tm, dm), lambda b, i: (b, i, 0)),
        compiler_params=pltpu.CompilerParams(vmem_limit_bytes=VMEM_LIMIT),
        name="half_ffn",
    )(h, mod, g, w1, w3, w2)


HALO = SUBLANES


def _conv_silu(z_tile, z_halo, w, b):
    i = pl.program_id(1)
    last = pl.num_programs(1) - 1
    tm = z_tile.shape[0]
    zp =jnp.concatenate([jnp.where(i > 0, z_halo[0:HALO], 0.0), z_tile,
                          jnp.where(i < last, z_halo[HALO:2 * HALO], 0.0)], axis=0)
    acc = jnp.broadcast_to(b, z_tile.shape)
    for j in range(CONV_K):
        off = HALO - CONV_K // 2 + j
        acc = acc + zp[off:off + tm] * w[j:j + 1]
    return _silu(acc)


def _inproj_kernel(h_ref, hp_ref, hn_ref, mod_ref, g_ref, w_ref, gq_ref, gkv_ref, wq1_ref, wq2_ref,
                   wk_ref, wv_ref, cos_ref, sin_ref, gcw_ref, scw_ref, scb_ref, pv_ref,
                   mq_ref, mk_ref, mv_ref, nq_ref, nk_ref, nv_ref, gy_ref, gzg_ref, szg_ref,
                   sy_ref, sma_ref, smb_ref):
    m = mod_ref[0]
    g = g_ref[...]

    def modulated(x):
        return (_rms(x, g[2:3]) * (1.0 + m[4:5]) + m[3:4]).astype(BF16)

    ub = modulated(h_ref[0])
    uh = modulated(jnp.concatenate([hp_ref[0], hn_ref[0]], axis=0))

    def proj(a, b):
        return _dot(ub, w_ref[:, a:b])

    gy = _conv_silu(proj(C_GQKV, C_GZG), _dot(uh, w_ref[:, C_GQKV:C_GZG]), gcw_ref[...], 0.0)
    grp = (_iota((GROUP_W, GROUP_W), 0) // HEAD_W == _iota((GROUP_W, GROUP_W), 1) // HEAD_W).astype(BF16)
    q = gy[:, 0:256]
    k = gy[:, 256:512]
    gy_ref[0, :, 0:256] = q * lax.rsqrt(_dot2(q * q, grp) + RMS_EPS) * (HEAD_W ** -0.5)
    gy_ref[0, :, 256:512] = k * lax.rsqrt(_dot2(k * k, grp) + RMS_EPS)
    gy_ref[0, :, 512:768] = gy[:, 512:768]
    sy_ref[0] = _conv_silu(proj(C_SXBC, C_SMALL), _dot(uh, w_ref[:, C_SXBC:C_SMALL]), scw_ref[...],
                           scb_ref[...])
    s = proj(C_SMALL, C_TOTAL)
    pv = pv_ref[...]
    sp = _softplus(s + pv[1:2])
    sma_ref[0] = jnp.where(_iota((1, LANES), 1) < L_G, jax.nn.sigmoid(s), -jnp.exp(pv[0:1]) * sp)
    smb_ref[0] = sp

    zm = proj(C_MLA, C_NAT)
    cqn = _rms(zm[:, 0:256], gq_ref[...]).astype(BF16)
    ckvn = _rms(zm[:, 256:384], gkv_ref[...]).astype(BF16)
    cos = cos_ref[...]
    sin = sin_ref[...]
    k_rope = zm[:, 384:512] * cos + zm[:, 512:640] * sin
    q1 = _dot(cqn, wq1_ref[...])
    q2 = _dot(cqn, wq2_ref[...])
    kn = _dot(ckvn, wk_ref[...])
    vv = _dot(ckvn, wv_ref[...])
    ones_col = (_iota((1, LANES), 1) == HEAD_W).astype(F32)
    for hh in range(N_HEADS):
        sl = slice(hh * LANES, (hh + 1) * LANES)
        mq_ref[0, hh] = ((q1[:, sl] * cos + q2[:, sl] * sin) * MLA_QSCALE).astype(BF16)
        mk_ref[0, hh] = (kn[:, sl] + k_rope).astype(BF16)
        mv_ref[0, hh] = (vv[:, sl] + ones_col).astype(BF16)

    nq_ref[0] = (proj(C_NAT, C_NAT + 256) * NAT_QSCALE).astype(BF16)
    nk_ref[0] = proj(C_NAT + 256, C_NAT + 512).astype(BF16)
    nv_ref[0] = proj(C_NAT + 512, C_NAT + 768).astype(BF16)
    gzg_ref[0] = proj(C_GZG, C_SZG)
    szg_ref[0] = proj(C_SZG, C_SXBC)


def _in_projection(h, mod, g, lw, cos, sin, *, tm):
    nb, nt, dm = h.shape
    per_batch = mod.shape[0] > 1
    hb = tm // HALO
    nhalo = nt // HALO
    tok = lambda w, dt: jax.ShapeDtypeStruct((nb, nt, w), dt)
    head = jax.ShapeDtypeStruct((nb, N_HEADS, nt, LANES), BF16)
    tspec = lambda w: pl.BlockSpec((1, tm, w), lambda b, i: (b, i, 0))
    hspec = pl.BlockSpec((1, N_HEADS, tm, LANES), lambda b, i: (b, 0, i, 0))
    return pl.pallas_call(
        _inproj_kernel,
        out_shape=(head, head, head, tok(256, BF16), tok(256, BF16), tok(256, BF16),
                   tok(768, F32), tok(256, F32), tok(256, F32), tok(768, F32), tok(LANES, F32),
                   tok(LANES, F32)),
        grid=(nb, nt // tm),
        in_specs=[tspec(dm),
                  pl.BlockSpec((1, HALO, dm), lambda b, i: (b, jnp.maximum(i * hb - 1, 0), 0)),
                  pl.BlockSpec((1, HALO, dm), lambda b, i: (b, jnp.minimum((i + 1) * hb, nhalo - 1), 0)),
                  pl.BlockSpec((1, N_MOD, dm), lambda b, i: (b if per_batch else 0, 0, 0)),
                  _full(g.shape), _resident(lw["w_in"].shape),
                  _full(lw["mla_gq"].shape), _full(lw["mla_gkv"].shape),
                  _full(lw["wq1"].shape), _full(lw["wq2"].shape),
                  _full(lw["wk"].shape), _full(lw["wv"].shape),
                  pl.BlockSpec((tm, LANES), lambda b, i: (i, 0)),
                  pl.BlockSpec((tm, LANES), lambda b, i: (i, 0)),
                  _full(lw["gdn_conv_w"].shape), _full(lw["ssm_conv_w"].shape),
                  _full(lw["ssm_conv_b"].shape), _full(lw["gate_pv"].shape)],
        out_specs=(hspec, hspec, hspec, tspec(256), tspec(256), tspec(256),
                   tspec(768), tspec(256), tspec(256), tspec(768), tspec(LANES), tspec(LANES)),
        compiler_params=pltpu.CompilerParams(vmem_limit_bytes=VMEM_LIMIT),
        name="in_projection",
    )(h, h, h, mod, g, lw["w_in"], lw["mla_gq"], lw["mla_gkv"], lw["wq1"], lw["wq2"], lw["wk"], lw["wv"],
      cos, sin, lw["gdn_conv_w"], lw["ssm_conv_w"], lw["ssm_conv_b"], lw["gate_pv"])


def _mla_kernel(*refs, n_lat_chunks, tk):
    if n_lat_chunks:
        q_ref, kl_ref, vl_ref, kc_ref, vc_ref, o_ref = refs
    else:
        q_ref, kc_ref, vc_ref, o_ref = refs
    q = q_ref[0, 0]
    tq = q.shape[0]

    def step(kb, vb, carry):
        m, acc = carry
        s = _dot_nt(q, kb)
        mn = jnp.maximum(m, jnp.max(s, axis=-1, keepdims=True))
        p = jnp.exp2(s - mn)
        acc = jnp.exp2(m - mn) * acc + _dot(p.astype(BF16), vb)
        return mn, acc

    carry = (jnp.full((tq, 1), NEG_INF, F32), jnp.zeros((tq, LANES), F32))
    if n_lat_chunks:
        def body(j, carry):
            off = pl.multiple_of(j * tk, tk)
            return step(kl_ref[0, 0, pl.ds(off, tk), :], vl_ref[0, 0, pl.ds(off, tk), :], carry)
        carry = lax.fori_loop(0, n_lat_chunks, body, carry, unroll=8)
    _, acc = step(kc_ref[0, 0], vc_ref[0, 0], carry)
    o_ref[0, 0] = acc / acc[:, HEAD_W:HEAD_W + 1]


def _mla_attention(q, k_ctx, v_ctx, k_lat=None, v_lat=None, *, tq, tk):
    nb, nh, nq, _ = q.shape
    nc = k_ctx.shape[2]
    qspec = pl.BlockSpec((1, 1, tq, LANES), lambda b, h, i: (b, h, i, 0))
    cspec = pl.BlockSpec((1, 1, nc, LANES), lambda b, h, i: (b, h, 0, 0))
    if k_lat is None:
        args, specs, n_chunks = (q, k_ctx, v_ctx), [qspec, cspec, cspec], 0
    else:
        nk = k_lat.shape[2]
        lspec = pl.BlockSpec((1, 1, nk, LANES), lambda b, h, i: (b, h, 0, 0))
        args, specs, n_chunks = (q, k_lat, v_lat, k_ctx, v_ctx), [qspec, lspec, lspec, cspec, cspec], nk // tk
    return pl.pallas_call(
        functools.partial(_mla_kernel, n_lat_chunks=n_chunks, tk=tk),
        out_shape=jax.ShapeDtypeStruct((nb, nh, nq, LANES), F32),
        grid=(nb, nh, nq // tq),
        in_specs=specs,
        out_specs=qspec,
        compiler_params=pltpu.CompilerParams(vmem_limit_bytes=VMEM_LIMIT),
        name="mla_attention",
    )(*args)


NAT_QROWS = 4
NAT_QB = NAT_QROWS * GRID_W
NAT_KBLKS = 3


def _heads_attention(q, parts, gout):
    lane_head = _iota((1, GROUP_W), 1) // HEAD_W
    heads = range(N_HEADS)
    sels = [lane_head == hh for hh in heads]
    qhs = [jnp.where(sel, q, jnp.zeros_like(q)) for sel in sels]
    scores = [[_dot_nt(qh, k) if bias is None else _dot_nt(qh, k) + bias[hh] for k, _, bias in parts]
              for hh, qh in zip(heads, qhs)]
    ms = [functools.reduce(jnp.maximum, [jnp.max(s, axis=-1, keepdims=True) for s in sc]) for sc in scores]
    ps = [[jnp.exp(s - m) for s in sc] for sc, m in zip(scores, ms)]
    ls = [functools.reduce(jnp.add, [jnp.sum(p, axis=-1, keepdims=True) for p in pp]) for pp in ps]
    os_ = [functools.reduce(jnp.add, [_dot(p.astype(BF16), v) for p, (_, v, _) in zip(pp, parts)])
           for pp in ps]
    out = jnp.zeros(q.shape, F32)
    for sel, o, l in zip(sels, os_, ls):
        out = jnp.where(sel, o / l, out)
    return _rms(out, gout)


def _nat_kernel(q_ref, k0_ref, k1_ref, k2_ref, v0_ref, v1_ref, v2_ref, kc_ref, vc_ref, bias_ref, g_ref,
                o_ref):
    kw = jnp.concatenate([k0_ref[0], k1_ref[0], k2_ref[0]], axis=0)
    vw = jnp.concatenate([v0_ref[0], v1_ref[0], v2_ref[0]], axis=0)
    parts = [(kw, vw, bias_ref[0]), (kc_ref[0], vc_ref[0], None)]
    o_ref[0] = _heads_attention(q_ref[0], parts, g_ref[...])


def _nat_ctx_kernel(q_ref, k_ref, v_ref, g_ref, o_ref):
    o_ref[0] = _heads_attention(q_ref[0], [(k_ref[0], v_ref[0], None)], g_ref[...])


def _nat_attention(q, k, v, kc, vc, bias, gout):
    nb, nt, _ = q.shape
    nblk = nt // NAT_QB
    nc = kc.shape[1]
    start = lambda i: jnp.clip(i - 1, 0, nblk - NAT_KBLKS)
    variant = lambda i: jnp.where(i == 0, 0, jnp.where(i == nblk - 1, 2, 1))
    qspec = pl.BlockSpec((1, NAT_QB, GROUP_W), lambda b, i: (b, i, 0))
    kspecs = [pl.BlockSpec((1, NAT_QB, GROUP_W), lambda b, i, j=j: (b, start(i) + j, 0))
              for j in range(NAT_KBLKS)]
    cspec = pl.BlockSpec((1, nc, GROUP_W), lambda b, i: (b, 0, 0))
    bspec = pl.BlockSpec((1, N_HEADS, NAT_QB, NAT_KBLKS * NAT_QB), lambda b, i: (variant(i), 0, 0, 0))
    return pl.pallas_call(
        _nat_kernel,
        out_shape=jax.ShapeDtypeStruct((nb, nt, GROUP_W), F32),
        grid=(nb, nblk),
        in_specs=[qspec] + kspecs + kspecs + [cspec, cspec, bspec, _full(gout.shape)],
        out_specs=qspec,
        name="nat_attention",
    )(q, k, k, k, v, v, v, kc, vc, bias, gout)


def _nat_ctx_attention(q, k, v, gout):
    nb, nc, _ = q.shape
    spec = pl.BlockSpec((1, nc, GROUP_W), lambda b: (b, 0, 0))
    return pl.pallas_call(
        _nat_ctx_kernel,
        out_shape=jax.ShapeDtypeStruct((nb, nc, GROUP_W), F32),
        grid=(nb,),
        in_specs=[spec, spec, spec, _full(gout.shape)],
        out_specs=spec,
        name="nat_ctx_attention",
    )(q, k, v, gout)


def _nat_bias_constants(n_rows):
    krows = NAT_KBLKS * NAT_QROWS
    qr = np.arange(NAT_QROWS)[:, None]
    kk = np.arange(krows)[None, :]
    cq = np.arange(GRID_W)[:, None]
    ck = np.arange(GRID_W)[None, :]
    c0 = np.clip(cq - NAT_KC // 2, 0, GRID_W - NAT_KC)
    col_ok = ((ck >= c0) & (ck < c0 + NAT_KC)).reshape(-1)
    dc = np.clip(ck - cq + (NAT_KC - 1), 0, 2 * NAT_KC - 2).reshape(-1)
    e_col = np.zeros((2 * NAT_KC, GRID_W * GRID_W), np.float32)
    e_col[dc, np.arange(GRID_W * GRID_W)] = 1.0
    big = 4 * n_rows + 64
    placements = [(0, 0, n_rows), (big // 2, big // 2 - NAT_QROWS, big),
                  (n_rows - NAT_QROWS, n_rows - krows, n_rows)]
    e_row = np.zeros((3, NAT_QROWS * krows, 2 * NAT_KR), np.float32)
    ok = np.zeros((3, NAT_QROWS * krows, GRID_W * GRID_W), np.float32)
    for v, (r_base, k_start, rows_total) in enumerate(placements):
        r = r_base + qr
        k_abs = k_start + kk
        r0 = np.clip(r - NAT_KR // 2, 0, rows_total - NAT_KR)
        row_ok = ((k_abs >= r0) & (k_abs < r0 + NAT_KR)).reshape(-1)
        dr = np.clip(k_abs - r + (NAT_KR - 1), 0, 2 * NAT_KR - 2).reshape(-1)
        e_row[v, np.arange(NAT_QROWS * krows), dr] = 1.0
        ok[v] = row_ok[:, None] & col_ok[None, :]
    return e_row, e_col, ok


def _split3(x):
    h1 = x.astype(BF16)
    r1 = x - h1.astype(F32)
    h2 = r1.astype(BF16)
    h3 = (r1 - h2.astype(F32)).astype(BF16)
    return h1, h2, h3


def _nat_bias_kernel(er_ref, rpb_ref, ec_ref, ok_ref, o_ref):
    er = er_ref[0]
    rows = functools.reduce(jnp.add, [_dot(er, p) for p in _split3(rpb_ref[0, 0])])
    ec = ec_ref[...]
    b = functools.reduce(jnp.add, [_dot(p, ec) for p in _split3(rows)])
    o_ref[0, 0, 0] = jnp.where(ok_ref[0] > 0.0, b, NEG_INF)


def _nat_bias_tables(rpb_all, n_rows):
    nl, nh = rpb_all.shape[:2]
    krows = NAT_KBLKS * NAT_QROWS
    e_row, e_col, ok = _nat_bias_constants(n_rows)
    rpb = jnp.pad(rpb_all, ((0, 0), (0, 0), (0, 1), (0, 1)))
    nr, ncol = NAT_QROWS * krows, GRID_W * GRID_W
    out = pl.pallas_call(
        _nat_bias_kernel,
        out_shape=jax.ShapeDtypeStruct((nl, 3, nh, nr, ncol), F32),
        grid=(nl, 3, nh),
        in_specs=[pl.BlockSpec((1, nr, 2 * NAT_KR), lambda l, v, h: (v, 0, 0)),
                  pl.BlockSpec((1, 1, 2 * NAT_KR, 2 * NAT_KC), lambda l, v, h: (l, h, 0, 0)),
                  pl.BlockSpec((2 * NAT_KC, ncol), lambda l, v, h: (0, 0)),
                  pl.BlockSpec((1, nr, ncol), lambda l, v, h: (v, 0, 0))],
        out_specs=pl.BlockSpec((1, 1, 1, nr, ncol), lambda l, v, h: (l, v, h, 0, 0)),
        name="nat_bias_tables",
    )(jnp.asarray(e_row, BF16), rpb, jnp.asarray(e_col, BF16), jnp.asarray(ok))
    out = out.reshape(nl, 3, nh, NAT_QROWS, krows, GRID_W, GRID_W).transpose(0, 1, 2, 3, 5, 4, 6)
    return out.reshape(nl, 3, nh, NAT_QB, krows * GRID_W)


def _chunk_cumsum_matrix(n, chunk, reverse):
    r = _iota((n, n), 0)
    c = _iota((n, n), 1)
    same = (r // chunk) == (c // chunk)
    return (same & ((c >= r) if reverse else (c <= r))).astype(BF16)


def _expand_matrix(lane0, group, width):
    return (_iota((LANES, width), 0) == lane0 + _iota((LANES, width), 1) // group).astype(BF16)


def _row_form(col_vals, chunk):
    pick = _iota(col_vals.shape, 0) == (_iota(col_vals.shape, 1) % chunk)
    ones = jnp.ones((SUBLANES, chunk), BF16)
    return _dot2l(ones, jnp.where(pick, col_vals, 0.0))[0:1]


def _heads_diag(x):
    return _block_diag(x, N_HEADS, GDN_CHUNK, HEAD_W)


def _heads_undiag(x):
    keep = (_iota(x.shape, 0) // GDN_CHUNK) == (_iota(x.shape, 1) // HEAD_W)
    x = jnp.where(keep, x, 0.0)
    return functools.reduce(jnp.add, [x[hh * GDN_CHUNK:(hh + 1) * GDN_CHUNK] for hh in range(N_HEADS)])


def _pc3(x, y):
    xh, xl = _split(x)
    yh, yl = _split(y)
    r = _dot(jnp.concatenate([xh, xl], axis=0), _heads_diag(yh))
    n = x.shape[0]
    return r[0:n] + r[n:2 * n] + _dot(xh, _heads_diag(yl))


def _pc1(x, y):
    return _dot(x.astype(BF16), _heads_diag(y.astype(BF16)))


def _gdn_chunk_kernel(y_ref, sm_ref, g_ref, qs_ref, qe_ref, o0_ref, gl_ref, *, d, n_chunks):
    ch = GDN_CHUNK
    cb = n_chunks * ch
    reverse = d == 1
    sm = sm_ref[0]
    tri = _chunk_cumsum_matrix(ch, ch, reverse)
    cs = jnp.concatenate([_dot2l(tri, sm[c * ch:(c + 1) * ch]) for c in range(n_chunks)], axis=0)
    gc_all = _dot2(cs, _expand_matrix(L_G + N_HEADS * d, HEAD_W, GROUP_W))
    beta_all = _dot2(sm, _expand_matrix(L_BETA + N_HEADS * d, HEAD_W, GROUP_W))
    y = y_ref[0]
    q_all, k_all, v_all = y[:, 0:256], y[:, 256:512], y[:, 512:768]
    kb_all = k_all * beta_all
    vb_all = v_all * beta_all
    eg_all = jnp.exp(gc_all)

    ti = _iota((ch, GROUP_W), 0)
    tj = _iota((ch, GROUP_W), 1) % ch
    incl = (tj >= ti) if reverse else (tj <= ti)
    strict = (tj > ti) if reverse else (tj < ti)
    last = 0 if reverse else ch - 1
    cs_ = range(n_chunks)
    rows = [slice(c * ch, (c + 1) * ch) for c in cs_]

    gc = [gc_all[r] for r in rows]
    decay = [jnp.where(incl, jnp.exp(jnp.where(incl, g - _row_form(g, ch), 0.0)), 0.0) for g in gc]
    qk = [_dot_nt(jnp.concatenate([kb_all[r], q_all[r]], axis=0).astype(BF16),
                  _heads_diag(k_all[r].astype(BF16))) for r in rows]
    a_mat = [jnp.where(strict, x[0:ch] * dc, 0.0) for x, dc in zip(qk, decay)]
    a_intra = [(x[ch:2 * ch] * dc).astype(BF16) for x, dc in zip(qk, decay)]

    base = SUBLANES
    eye = jnp.where(ti == tj, 1.0, 0.0)
    m = [jnp.where(ti // base == tj // base, -a, 0.0) for a in a_mat]
    t = [eye + x for x in m]
    m = [_pc1(x, x) for x in m]
    both = [_pc1(jnp.concatenate([x, p], axis=0), p) for x, p in zip(t, m)]
    t = [x + r[0:ch] for x, r in zip(t, both)]
    t = [x + _pc1(x, r[ch:2 * ch]) for x, r in zip(t, both)]
    size = 2 * base
    while size <= ch:
        off = (ti // size == tj // size) & (ti // (size // 2) != tj // (size // 2))
        ct = [_pc1(jnp.where(off, a, 0.0), x) for a, x in zip(a_mat, t)]
        t = [x - _pc1(x, p) for x, p in zip(t, ct)]
        size *= 2
    resid = [eye - x - _pc3(a, x) for a, x in zip(a_mat, t)]
    t = [x + _pc1(x, r) for x, r in zip(t, resid)]

    u = [_pc1(x, vb_all[r]) for x, r in zip(t, rows)]
    w = [_pc1(x, kb_all[r] * eg_all[r]) for x, r in zip(t, rows)]
    for c in cs_:
        r = rows[c]
        g_last = gc[c][last:last + 1]
        k_dec = (k_all[r] * jnp.exp(g_last - gc[c])).astype(BF16)
        wu = jnp.concatenate([w[c], u[c]], axis=1).astype(BF16)
        full = _dot_tn(k_dec, wu)
        g_ref[0, r, :] = _heads_undiag(full[:, 0:GROUP_W]).astype(BF16)
        qs_ref[0, r, :] = _heads_undiag(full[:, GROUP_W:])
        wu_bd = jnp.concatenate([_heads_diag(wu[:, 0:GROUP_W]), _heads_diag(wu[:, GROUP_W:])], axis=1)
        aw = _dot(a_intra[c], wu_bd)
        qe_ref[0, r, :] = (q_all[r] * eg_all[r] - aw[:, 0:GROUP_W]).astype(BF16)
        o0_ref[0, r, :] = aw[:, GROUP_W:]
        gl_ref[0, c:c + 1, :] = jnp.exp(g_last)


def _gdn_state_kernel(g_ref, qs_ref, qe_ref, o0_ref, gl_ref, s0_ref, o_ref, sfin_ref, state_ref,
                      *, d, n_chunks):
    ch = GDN_CHUNK
    nb = g_ref.shape[0]
    step = pl.program_id(0)

    @pl.when(step == 0)
    def _():
        state_ref[...] = s0_ref[...]

    order = range(n_chunks - 1, -1, -1) if d == 1 else range(n_chunks)
    state = [state_ref[b] for b in range(nb)]
    for c in order:
        r = slice(c * ch, (c + 1) * ch)
        for b in range(nb):
            both = jnp.concatenate([g_ref[b, r, :], qe_ref[b, r, :]], axis=0)
            res = _dot(both, _heads_diag(state[b].astype(BF16)))
            o_ref[b, r, :] = res[ch:2 * ch] + o0_ref[b, r, :]
            state[b] = state[b] * gl_ref[b, c:c + 1, :] - res[0:ch] + qs_ref[b, r, :]
    for b in range(nb):
        state_ref[b] = state[b]

    @pl.when(step == pl.num_programs(0) - 1)
    def _():
        for b in range(nb):
            sfin_ref[b] = state[b]


def _gdn_chunk_call(y, sm, *, d, cb):
    nb, nt, nc = y.shape
    n_chunks = cb // GDN_CHUNK
    tok = lambda dt: jax.ShapeDtypeStruct((nb, nt, GROUP_W), dt)
    tspec = pl.BlockSpec((1, cb, GROUP_W), lambda bb, i: (bb, i, 0))
    return pl.pallas_call(
        functools.partial(_gdn_chunk_kernel, d=d, n_chunks=n_chunks),
        out_shape=(tok(BF16), tok(F32), tok(BF16), tok(F32),
                   jax.ShapeDtypeStruct((nb, nt // GDN_CHUNK, GROUP_W), F32)),
        grid=(nb, nt // cb),
        in_specs=[pl.BlockSpec((1, cb, nc), lambda bb, i: (bb, i, 0)),
                  pl.BlockSpec((1, cb, LANES), lambda bb, i: (bb, i, 0))],
        out_specs=(tspec, tspec, tspec, tspec,
                   pl.BlockSpec((1, n_chunks, GROUP_W), lambda bb, i: (bb, i, 0))),
        name="gdn_chunk",
    )(y, sm)


def _gdn_state_call(parts, s0, *, d, cb):
    g, qs, qe, o0, gl = parts
    nb, nt, _ = g.shape
    n_chunks = cb // GDN_CHUNK
    nblk = nt // cb
    blk = (lambda i: (0, nblk - 1 - i, 0)) if d == 1 else (lambda i: (0, i, 0))
    tspec = pl.BlockSpec((nb, cb, GROUP_W), blk)
    sspec = pl.BlockSpec((nb, GDN_CHUNK, GROUP_W), lambda i: (0, 0, 0))
    return pl.pallas_call(
        functools.partial(_gdn_state_kernel, d=d, n_chunks=n_chunks),
        out_shape=(jax.ShapeDtypeStruct((nb, nt, GROUP_W), F32),
                   jax.ShapeDtypeStruct((nb, GDN_CHUNK, GROUP_W), F32)),
        grid=(nblk,),
        in_specs=[tspec, tspec, tspec, tspec, pl.BlockSpec((nb, n_chunks, GROUP_W), blk), sspec],
        out_specs=(tspec, sspec),
        scratch_shapes=[pltpu.VMEM((nb, GDN_CHUNK, GROUP_W), F32)],
        compiler_params=pltpu.CompilerParams(dimension_semantics=("arbitrary",)),
        name="gdn_state_scan",
    )(g, qs, qe, o0, gl, s0)


def _gdn_bidir(y_l, sm_l, y_c, sm_c, *, cb_l, cb_c):
    zero = jnp.zeros((y_l.shape[0], GDN_CHUNK, GROUP_W), F32)
    outs = []
    for d in (0, 1):
        o_c, s_c = _gdn_state_call(_gdn_chunk_call(y_c, sm_c, d=d, cb=cb_c), zero, d=d, cb=cb_c)
        o_l, _ = _gdn_state_call(_gdn_chunk_call(y_l, sm_l, d=d, cb=cb_l), s_c, d=d, cb=cb_l)
        outs.append((o_l, o_c))
    return outs


def _ssd_kernel(x_ref, dt_ref, da_ref, s0_ref, o_ref, sfin_ref, state_ref, *, d, n_chunks):
    ch = SSM_CHUNK
    reverse = d == 1
    nb = x_ref.shape[0]
    step = pl.program_id(0)

    @pl.when(step == 0)
    def _():
        state_ref[...] = s0_ref[...]

    lane0 = L_DT + N_HEADS * d
    tri = _chunk_cumsum_matrix(ch, ch, reverse)
    e64 = _expand_matrix(lane0, HEAD_W, GROUP_W)
    e128 = _expand_matrix(lane0, ch, N_HEADS * ch)
    ti = _iota((ch, N_HEADS * ch), 0)
    tj = _iota((ch, N_HEADS * ch), 1) % ch
    incl = (tj >= ti) if reverse else (tj <= ti)
    grp_keep = (_iota((GROUP_W, GROUP_W), 0) // ch) == (_iota((GROUP_W, GROUP_W), 1) // ch)
    last = 0 if reverse else ch - 1
    rows = [slice(c * ch, (c + 1) * ch) for c in range(n_chunks)]

    def chunk_terms(b):
        da = da_ref[b]
        cs = jnp.concatenate([_dot2l(tri, da[r]) for r in rows], axis=0)
        ac_all = _dot2(cs, e64)
        ac5_all = _dot2(cs, e128)
        xbc = x_ref[b]
        xdt_all = xbc[:, 0:256] * _dot2(dt_ref[b], e64)
        b_all, c_all = xbc[:, 256:512], xbc[:, 512:768]
        seg = [jnp.where(incl, jnp.exp(jnp.where(incl, ac5_all[r] - _row_form(ac5_all[r], ch), 0.0)), 0.0)
               for r in rows]
        cm = [c_all[r].astype(BF16) for r in rows]
        bmb = [b_all[r].astype(BF16) for r in rows]
        cb_g = [_dot_nt(c_, _block_diag(b_, 2, ch, SSM_STATE)) for c_, b_ in zip(cm, bmb)]
        scores = [(jnp.concatenate([x[:, 0:ch], x[:, 0:ch], x[:, ch:], x[:, ch:]], axis=1) * sg).astype(BF16)
                  for x, sg in zip(cb_g, seg)]
        y_diag = [_dot(s_, _block_diag(xdt_all[r].astype(BF16), N_HEADS, ch, HEAD_W))
                  for s_, r in zip(scores, rows)]
        a_last = [ac_all[r][last:last + 1] for r in rows]
        states = [jnp.where(grp_keep, _dot_tn(b_, (xdt_all[r] * jnp.exp(al - ac_all[r])).astype(BF16)), 0.0)
                  for b_, r, al in zip(bmb, rows, a_last)]
        return cm, y_diag, a_last, states, [jnp.exp(ac_all[r]) for r in rows]

    terms = [chunk_terms(b) for b in range(nb)]
    state = [state_ref[b] for b in range(nb)]
    for c in (range(n_chunks - 1, -1, -1) if reverse else range(n_chunks)):
        for b in range(nb):
            cm, y_diag, a_last, states, e_ac = terms[b]
            o_ref[b, rows[c], :] = y_diag[c] + _dot(cm[c], state[b].astype(BF16)) * e_ac[c]
            state[b] = state[b] * jnp.exp(a_last[c]) + states[c]
    for b in range(nb):
        state_ref[b] = state[b]

    @pl.when(step == pl.num_programs(0) - 1)
    def _():
        for b in range(nb):
            sfin_ref[b] = state[b]


def _scan_call(kernel, name, seq, smalls, s0, *, d, cb, chunk):
    nb, nt, nc = seq.shape
    nblk = nt // cb
    blk = (lambda i: (0, nblk - 1 - i, 0)) if d == 1 else (lambda i: (0, i, 0))
    sspec = pl.BlockSpec((nb, GROUP_W, GROUP_W), lambda i: (0, 0, 0))
    return pl.pallas_call(
        functools.partial(kernel, d=d, n_chunks=cb // chunk),
        out_shape=(jax.ShapeDtypeStruct((nb, nt, GROUP_W), F32),
                   jax.ShapeDtypeStruct((nb, GROUP_W, GROUP_W), F32)),
        grid=(nblk,),
        in_specs=[pl.BlockSpec((nb, cb, nc), blk)] + [pl.BlockSpec((nb, cb, LANES), blk) for _ in smalls]
                 + [sspec],
        out_specs=(pl.BlockSpec((nb, cb, GROUP_W), blk), sspec),
        scratch_shapes=[pltpu.VMEM((nb, GROUP_W, GROUP_W), F32)],
        compiler_params=pltpu.CompilerParams(dimension_semantics=("arbitrary",)),
        name=name,
    )(seq, *smalls, s0)


def _bidir_scan(kernel, name, seq_l, smalls_l, seq_c, smalls_c, *, cb_l, cb_c, chunk):
    zero = jnp.zeros((seq_l.shape[0], GROUP_W, GROUP_W), F32)
    outs = []
    for d in (0, 1):
        o_c, s_c = _scan_call(kernel, name, seq_c, smalls_c, zero, d=d, cb=cb_c, chunk=chunk)
        o_l, _ = _scan_call(kernel, name, seq_l, smalls_l, s_c, d=d, cb=cb_l, chunk=chunk)
        outs.append((o_l, o_c))
    return outs


def _outproj_kernel(h_ref, mod_ref, g_ref, mla_ref, nat_ref, gof_ref, gob_ref, gzg_ref,
                    yf_ref, yb_ref, sx_ref, szg_ref, pv_ref, wm_ref, wr_ref, o_ref):
    pv = pv_ref[...]
    valid = _iota((1, LANES), 1) < HEAD_W
    slabs = [jnp.where(valid, mla_ref[0, hh], 0.0) for hh in range(N_HEADS)]
    ss = functools.reduce(jnp.add, [jnp.sum(s * s, axis=-1, keepdims=True) for s in slabs])
    scale = lax.rsqrt(ss / GROUP_W + RMS_EPS)
    y = None
    for hh in range(N_HEADS):
        part = _dot((slabs[hh] * scale * pv[hh:hh + 1, 0:LANES]).astype(BF16), wm_ref[hh])
        y = part if y is None else y + part
    y = y + _dot(nat_ref[0].astype(BF16), wr_ref[0:256, :])
    grp = (_iota((GROUP_W, GROUP_W), 0) // HEAD_W == _iota((GROUP_W, GROUP_W), 1) // HEAD_W).astype(BF16)
    o = gof_ref[0] + gob_ref[0]
    on = o * lax.rsqrt(_dot2(o * o, grp) / HEAD_W + RMS_EPS) * pv[4:5]
    y = y + _dot((on * _silu(gzg_ref[0])).astype(BF16), wr_ref[256:512, :])
    s = (yf_ref[0] + yb_ref[0] + sx_ref[0] * pv[5:6]) * _silu(szg_ref[0])
    sn = jnp.concatenate([_rms(s[:, 0:LANES], pv[6:7, 0:LANES]),
                          _rms(s[:, LANES:], pv[6:7, LANES:])], axis=1)
    y = y + _dot(sn.astype(BF16), wr_ref[512:768, :])
    m = mod_ref[0]
    g = g_ref[...]
    o_ref[0] = h_ref[0] + m[5:6] * _rms(y, g[3:4])


def _out_projection(h, mod, g, mla_o, nat_o, gdn_f, gdn_b, gdn_zg, ssm_f, ssm_b, ssm_xbc, ssm_zg,
                    pv, w_mla, w_rest, *, tm):
    nb, nt, dm = h.shape
    per_batch = mod.shape[0] > 1
    tspec = lambda w: pl.BlockSpec((1, tm, w), lambda b, i: (b, i, 0))
    return pl.pallas_call(
        _outproj_kernel,
        out_shape=jax.ShapeDtypeStruct(h.shape, F32),
        grid=(nb, nt // tm),
        in_specs=[tspec(dm),
                  pl.BlockSpec((1, N_MOD, dm), lambda b, i: (b if per_batch else 0, 0, 0)),
                  _full(g.shape),
                  pl.BlockSpec((1, N_HEADS, tm, LANES), lambda b, i: (b, 0, i, 0)),
                  tspec(256), tspec(256), tspec(256), tspec(256), tspec(256), tspec(256),
                  tspec(256),
                  tspec(256),
                  _full(pv.shape), _full(w_mla.shape), _full(w_rest.shape)],
        out_specs=tspec(dm),
        name="out_projection",
    )(h, mod, g, mla_o, nat_o, gdn_f, gdn_b, gdn_zg, ssm_f, ssm_b, ssm_xbc, ssm_zg, pv, w_mla, w_rest)


_ROPE_SWAP = np.array(list(range(8, 16)) + list(range(0, 8)) + list(range(24, 32)) + list(range(16, 24)))


def _pack_layer(p, l):
    w_in = p["w_in"][l]
    dm = w_in.shape[0]
    o_nat = MLA_Q_LORA + MLA_KV_LORA + MLA_ROPE
    o_gdn = o_nat + 768
    o_ssm = o_gdn + GDN_QKV + GROUP_W + 4 * N_HEADS
    kr = w_in[:, MLA_Q_LORA + MLA_KV_LORA:o_nat]
    z64 = jnp.zeros((dm, 64), F32)
    z32 = jnp.zeros((dm, 32), F32)
    small = jnp.concatenate([w_in[:, o_gdn + 1024:o_gdn + 1040],
                             w_in[:, o_ssm + 1024:o_ssm + 1032],
                             jnp.zeros((dm, LANES - 24), F32)], axis=1)
    w_packed = jnp.concatenate([
        w_in[:, 0:384], z64, kr, z32, z64, kr[:, _ROPE_SWAP], z32,
        w_in[:, o_nat:o_gdn],
        w_in[:, o_gdn:o_gdn + 1024],
        w_in[:, o_ssm:o_ssm + 1024],
        small], axis=1).astype(BF16)
    assert w_packed.shape[1] == C_TOTAL

    wuq = p["mla_wuq"][l].reshape(MLA_Q_LORA, N_HEADS, MLA_NOPE + MLA_ROPE)
    zq = jnp.zeros((MLA_Q_LORA, N_HEADS, 32), F32)
    wq1 = jnp.concatenate([wuq, zq], axis=2).reshape(MLA_Q_LORA, N_HEADS * LANES)
    wq2 = jnp.concatenate([jnp.zeros((MLA_Q_LORA, N_HEADS, 64), F32), wuq[:, :, MLA_NOPE:][:, :, _ROPE_SWAP], zq],
                          axis=2).reshape(MLA_Q_LORA, N_HEADS * LANES)
    wukv = p["mla_wukv"][l].reshape(MLA_KV_LORA, N_HEADS, MLA_NOPE + HEAD_W)
    zk = jnp.zeros((MLA_KV_LORA, N_HEADS, 64), F32)
    wk = jnp.concatenate([wukv[:, :, :MLA_NOPE], zk], axis=2).reshape(MLA_KV_LORA, N_HEADS * LANES)
    wv = jnp.concatenate([wukv[:, :, MLA_NOPE:], zk], axis=2).reshape(MLA_KV_LORA, N_HEADS * LANES)

    def lanes(vals, lane0):
        v = vals.reshape(-1)
        return jnp.zeros((LANES,), F32).at[lane0:lane0 + v.shape[0]].set(v)

    w_out = p["w_out"][l]
    w_mla = jnp.concatenate([w_out[0:256].reshape(N_HEADS, HEAD_W, dm),
                             jnp.zeros((N_HEADS, LANES - HEAD_W, dm), F32)], axis=1).astype(BF16)
    gout = jnp.concatenate([p["mla_gout"][l].reshape(N_HEADS, HEAD_W),
                            jnp.zeros((N_HEADS, GROUP_W - HEAD_W), F32)], axis=1)
    pv_out = jnp.concatenate([gout,
                              jnp.tile(p["gdn_gnorm"][l], N_HEADS)[None],
                              jnp.repeat(p["ssm_d"][l], HEAD_W)[None],
                              p["ssm_gnorm"][l][None],
                              jnp.zeros((1, GROUP_W), F32)], axis=0)
    return dict(
        ffn1=(p["ffn1_w1"][l].astype(BF16), p["ffn1_w3"][l].astype(BF16), p["ffn1_w2"][l].astype(BF16)),
        ffn2=(p["ffn2_w1"][l].astype(BF16), p["ffn2_w3"][l].astype(BF16), p["ffn2_w2"][l].astype(BF16)),
        g=p["norm_g"][l],
        w_in=w_packed, wq1=wq1.astype(BF16), wq2=wq2.astype(BF16), wk=wk.astype(BF16), wv=wv.astype(BF16),
        mla_gq=p["mla_gq"][l][None], mla_gkv=p["mla_gkv"][l][None],
        nat_bias=p["nat_bias"][l], nat_gout=p["nat_gout"][l][None],
        gdn_conv_w=p["gdn_conv_w"][l],
        ssm_conv_w=p["ssm_conv_w"][l], ssm_conv_b=p["ssm_conv_b"][l][None],
        gate_pv=jnp.stack([lanes(p["gdn_a_log"][l], L_G) + lanes(p["ssm_a_log"][l], L_DT),
                           lanes(p["gdn_dt_bias"][l], L_G) + lanes(p["ssm_dt_bias"][l], L_DT)]),
        pv_out=pv_out, w_mla=w_mla, w_rest=w_out[256:].astype(BF16),
    )


def _rope_tables(n_tok):
    n_rows = n_tok // GRID_W
    quarter = MLA_ROPE // 4
    freqs = ROPE_THETA ** (-jnp.arange(quarter, dtype=F32) / quarter)
    ar = jnp.arange(n_rows, dtype=F32)[:, None] * freqs
    ac = jnp.arange(GRID_W, dtype=F32)[:, None] * freqs
    per_row = lambda t: jnp.repeat(t, GRID_W, axis=0)
    per_col = lambda t: jnp.tile(t, (n_rows, 1))
    cos = jnp.concatenate([per_row(jnp.cos(ar)), per_row(jnp.cos(ar)),
                           per_col(jnp.cos(ac)), per_col(jnp.cos(ac))], axis=1)
    sin = jnp.concatenate([per_row(-jnp.sin(ar)), per_row(jnp.sin(ar)),
                           per_col(-jnp.sin(ac)), per_col(jnp.sin(ac))], axis=1)
    ones = jnp.ones((n_tok, MLA_NOPE), F32)
    zeros = jnp.zeros((n_tok, MLA_NOPE), F32)
    pad = jnp.zeros((n_tok, LANES - MLA_NOPE - MLA_ROPE), F32)
    return jnp.concatenate([ones, cos, pad], axis=1), jnp.concatenate([zeros, sin, pad], axis=1)


def _tiles(n_tok):
    return dict(tm=min(512, n_tok), tm_ffn=min(512, n_tok), tq=min(1024, n_tok),
                cb_gdn=min(512, n_tok), cb_ssd=min(1024, n_tok))


def _mixers(hl, hc, ml, mc, lw, ropes, need_ctx):
    tl, tc = _tiles(hl.shape[1]), _tiles(hc.shape[1])
    (cos_l, sin_l), (cos_c, sin_c) = ropes
    zl = _in_projection(hl, ml, lw["g"], lw, cos_l, sin_l, tm=tl["tm"])
    zc = _in_projection(hc, mc, lw["g"], lw, cos_c, sin_c, tm=tc["tm"])
    (mq_l, mk_l, mv_l, nq_l, nk_l, nv_l, gy_l, gzg_l, szg_l, sy_l, sma_l, smb_l) = zl
    (mq_c, mk_c, mv_c, nq_c, nk_c, nv_c, gy_c, gzg_c, szg_c, sy_c, sma_c, smb_c) = zc

    n_lat = hl.shape[1]
    tk = min(2048, n_lat)
    mla_l = _mla_attention(mq_l, mk_c, mv_c, mk_l, mv_l, tq=tl["tq"], tk=tk)
    nat_l = _nat_attention(nq_l, nk_l, nv_l, nk_c, nv_c, lw["nat_bias"], lw["nat_gout"])
    (gf_l, gf_c), (gb_l, gb_c) = _gdn_bidir(gy_l, sma_l, gy_c, sma_c, cb_l=tl["cb_gdn"], cb_c=tc["cb_gdn"])
    (sf_l, sf_c), (sb_l, sb_c) = _bidir_scan(_ssd_kernel, "ssd_scan", sy_l, (smb_l, sma_l),
                                             sy_c, (smb_c, sma_c),
                                             cb_l=tl["cb_ssd"], cb_c=tc["cb_ssd"], chunk=SSM_CHUNK)

    hl = _out_projection(hl, ml, lw["g"], mla_l, nat_l, gf_l, gb_l, gzg_l, sf_l, sb_l, sy_l, szg_l,
                         lw["pv_out"], lw["w_mla"], lw["w_rest"], tm=tl["tm"])
    if need_ctx:
        mla_c = _mla_attention(mq_c, mk_c, mv_c, tq=tc["tq"], tk=tk)
        nat_c = _nat_ctx_attention(nq_c, nk_c, nv_c, lw["nat_gout"])
        hc = _out_projection(hc, mc, lw["g"], mla_c, nat_c, gf_c, gb_c, gzg_c, sf_c, sb_c, sy_c, szg_c,
                             lw["pv_out"], lw["w_mla"], lw["w_rest"], tm=tc["tm"])
    return hl, hc


def _layer(hl, hc, ml, mc, lw, ropes, need_ctx):
    tl, tc = _tiles(hl.shape[1]), _tiles(hc.shape[1])
    hl = _half_ffn(hl, ml, lw["g"], *lw["ffn1"], k0=0, gp=0, tm=tl["tm_ffn"])
    hc = _half_ffn(hc, mc, lw["g"], *lw["ffn1"], k0=0, gp=0, tm=tc["tm"])
    hl, hc = _mixers(hl, hc, ml, mc, lw, ropes, need_ctx)
    hl = _half_ffn(hl, ml, lw["g"], *lw["ffn2"], k0=6, gp=4, tm=tl["tm_ffn"])
    if need_ctx:
        hc = _half_ffn(hc, mc, lw["g"], *lw["ffn2"], k0=6, gp=4, tm=tc["tm"])
    return hl, hc


def kernel(x, c, ctx, c_ctx, w_mod, b_mod, norm_g, ffn1_w1, ffn1_w3, ffn1_w2, ffn2_w1, ffn2_w3, ffn2_w2,
           w_in, w_out, mla_gq, mla_gkv, mla_wuq, mla_wukv, mla_gout, nat_rpb, nat_gout, gdn_conv_w,
           gdn_a_log, gdn_dt_bias, gdn_gnorm, ssm_conv_w, ssm_conv_b, ssm_a_log, ssm_dt_bias, ssm_d,
           ssm_gnorm):
    p = dict(norm_g=norm_g, ffn1_w1=ffn1_w1, ffn1_w3=ffn1_w3, ffn1_w2=ffn1_w2, ffn2_w1=ffn2_w1,
             ffn2_w3=ffn2_w3, ffn2_w2=ffn2_w2, w_in=w_in, w_out=w_out, mla_gq=mla_gq, mla_gkv=mla_gkv,
             mla_wuq=mla_wuq, mla_wukv=mla_wukv, mla_gout=mla_gout, nat_rpb=nat_rpb, nat_gout=nat_gout,
             gdn_conv_w=gdn_conv_w, gdn_a_log=gdn_a_log, gdn_dt_bias=gdn_dt_bias, gdn_gnorm=gdn_gnorm,
             ssm_conv_w=ssm_conv_w, ssm_conv_b=ssm_conv_b, ssm_a_log=ssm_a_log, ssm_dt_bias=ssm_dt_bias,
             ssm_d=ssm_d, ssm_gnorm=ssm_gnorm)
    nb, n_lat, dm = x.shape
    n_ctx = ctx.shape[1]
    depth = w_mod.shape[0]
    cvec = jnp.concatenate([c, c_ctx[None], jnp.zeros((SUBLANES - nb - 1, dm), F32)], axis=0)
    mods = _modulation(cvec, w_mod, b_mod).reshape(depth, SUBLANES, N_MOD, dm)
    cos_c = jnp.concatenate([jnp.ones((n_ctx, MLA_NOPE + MLA_ROPE), F32),
                             jnp.zeros((n_ctx, LANES - MLA_NOPE - MLA_ROPE), F32)], axis=1)
    ropes = (_rope_tables(n_lat), (cos_c, jnp.zeros((n_ctx, LANES), F32)))
    p["nat_bias"] = _nat_bias_tables(nat_rpb, n_lat // GRID_W)
    hl, hc = x, ctx
    for l in range(depth):
        lw = _pack_layer(p, l)
        hl, hc = _layer(hl, hc, mods[l, 0:nb], mods[l, nb:nb + 1], lw, ropes, need_ctx=l < depth - 1)
    return hl
```

```python
import functools
import math

import jax
import jax.numpy as jnp
import numpy as np
from jax import lax
from jax.experimental import pallas as pl
from jax.experimental.pallas import tpu as pltpu

F32 = jnp.float32
BF16 = jnp.bfloat16

D_MODEL = 1024
DEPTH = 4
GRID_W = 64
N_MOD = 9
D_FF = 2816
RMS_EPS = 1e-6
NEG_INF = -1e30
ROPE_THETA = 10000.0
GROUP_W = 256
N_HEADS = 4
HEAD_W = 64
MLA_NOPE = 64
MLA_ROPE = 32
MLA_Q_LORA = 256
MLA_KV_LORA = 128
NAT_KR = 8
NAT_KC = 16
GDN_CHUNK = 64
SSM_STATE = 128
SSM_CHUNK = 128
CONV_K = 5
GDN_QKV = 768
SSM_XBC = 768

LANES = 128
SUBLANES = 8
VMEM_LIMIT = 56 * 1024 * 1024

C_MLA = 0
C_NAT = 640
C_GQKV = 1408
C_GZG = 2176
C_SZG = 2432
C_SXBC = 2688
C_SMALL = 3456
C_TOTAL = 3584
L_BETA, L_G, L_DT = 0, 8, 16

MLA_QSCALE = (MLA_NOPE + MLA_ROPE) ** -0.5 * math.log2(math.e)
NAT_QSCALE = HEAD_W ** -0.5


def _dot(a, b):
    return jnp.dot(a, b, preferred_element_type=F32)


def _dot_nt(a, b):
    return lax.dot_general(a, b, (((1,), (1,)), ((), ())), preferred_element_type=F32)


def _dot_tn(a, b):
    return lax.dot_general(a, b, (((0,), (0,)), ((), ())), preferred_element_type=F32)


def _split(x):
    hi = x.astype(BF16)
    lo = (x - hi.astype(F32)).astype(BF16)
    return hi, lo


def _dot2(a, m):
    hi, lo = _split(a)
    return _dot(hi, m) + _dot(lo, m)


def _dot2l(m, a):
    hi, lo = _split(a)
    return _dot(m, hi) + _dot(m, lo)


def _rms(x, g):
    return x * lax.rsqrt(jnp.mean(x * x, axis=-1, keepdims=True) + RMS_EPS) * g


def _silu(x):
    return x * jax.nn.sigmoid(x)


def _softplus(x):
    return jnp.maximum(x, 0.0) + jnp.log1p(jnp.exp(-jnp.abs(x)))


def _iota(shape, dim):
    return lax.broadcasted_iota(jnp.int32, shape, dim)


def _block_diag(x, n, blk_r, blk_c):
    t = jnp.concatenate([x] * n, axis=0)
    keep = (_iota(t.shape, 0) // blk_r) == (_iota(t.shape, 1) // blk_c)
    return jnp.where(keep, t, 0.0)


def _full(shape):
    nd = len(shape)
    return pl.BlockSpec(shape, lambda *_: (0,) * nd)


def _resident(shape):
    nd = len(shape)
    return pl.BlockSpec(shape, lambda *_: (0,) * nd, pipeline_mode=pl.Buffered(1))


def _mod_kernel(c_ref, w_ref, b_ref, o_ref):
    s = _silu(c_ref[...])
    o_ref[0] = jnp.dot(s, w_ref[0], preferred_element_type=F32,
                       precision=lax.Precision.HIGHEST) + b_ref[0]


def _modulation(cvec, w_mod, b_mod):
    nl, dm, nm = w_mod.shape
    rows = cvec.shape[0]
    tn = 1536
    return pl.pallas_call(
        _mod_kernel,
        out_shape=jax.ShapeDtypeStruct((nl, rows, nm), F32),
        grid=(nl, nm // tn),
        in_specs=[pl.BlockSpec((rows, dm), lambda l, j: (0, 0)),
                  pl.BlockSpec((1, dm, tn), lambda l, j: (l, 0, j)),
                  pl.BlockSpec((1, 1, tn), lambda l, j: (l, 0, j))],
        out_specs=pl.BlockSpec((1, rows, tn), lambda l, j: (l, 0, j)),
        name="modulation",
    )(cvec, w_mod, b_mod.reshape(nl, 1, nm))


FFN_CHUNK = 256


FFN_SUB = 512


def _ffn_kernel(h_ref, mod_ref, g_ref, w1_ref, w3_ref, w2_ref, o_ref, *, k0, gp):
    tm = h_ref.shape[1]
    sub = min(FFN_SUB, tm)
    m = mod_ref[0]
    g = g_ref[...]
    rows = [slice(r, r + sub) for r in range(0, tm, sub)]
    xs = [h_ref[0, r, :] for r in rows]
    ubs = [(_rms(x, g[gp:gp + 1]) * (1.0 + m[k0 + 1:k0 + 2]) + m[k0:k0 + 1]).astype(BF16) for x in xs]
    accs = [None] * len(rows)
    for c in range(D_FF // FFN_CHUNK):
        sl = slice(c * FFN_CHUNK, (c + 1) * FFN_CHUNK)
        for i, ub in enumerate(ubs):
            hid = (_silu(_dot(ub, w1_ref[:, sl])) * _dot(ub, w3_ref[:, sl])).astype(BF16)
            part = _dot(hid, w2_ref[sl, :])
            accs[i] = part if accs[i] is None else accs[i] + part
    for r, x, acc in zip(rows, xs, accs):
        o_ref[0, r, :] = x + 0.5 * m[k0 + 2:k0 + 3] * _rms(acc, g[gp + 1:gp + 2])


def _half_ffn(h, mod, g, w1, w3, w2, *, k0, gp, tm):
    nb, nt, dm = h.shape
    per_batch = mod.shape[0] > 1
    return pl.pallas_call(
        functools.partial(_ffn_kernel, k0=k0, gp=gp),
        out_shape=jax.ShapeDtypeStruct(h.shape, F32),
        grid=(nb, nt // tm),
        in_specs=[pl.BlockSpec((1, tm, dm), lambda b, i: (b, i, 0)),
                  pl.BlockSpec((1, N_MOD, dm), lambda b, i: (b if per_batch else 0, 0, 0)),
                  _full(g.shape), _resident(w1.shape), _resident(w3.shape), _resident(w2.shape)],
        out_specs=pl.BlockSpec((1, tm, dm), lambda b, i: (b, i, 0)),
        compiler_params=pltpu.CompilerParams(vmem_limit_bytes=VMEM_LIMIT),
        name="half_ffn",
    )(h, mod, g, w1, w3, w2)


HALO = SUBLANES


def _conv_silu(z_tile, z_halo, w, b):
    i = pl.program_id(1)
    last = pl.num_programs(1) - 1
    tm = z_tile.shape[0]
    zp =jnp.concatenate([jnp.where(i > 0, z_halo[0:HALO], 0.0), z_tile,
                          jnp.where(i < last, z_halo[HALO:2 * HALO], 0.0)], axis=0)
    acc = jnp.broadcast_to(b, z_tile.shape)
    for j in range(CONV_K):
        off = HALO - CONV_K // 2 + j
        acc = acc + zp[off:off + tm] * w[j:j + 1]
    return _silu(acc)


def _inproj_kernel(h_ref, hp_ref, hn_ref, mod_ref, g_ref, w_ref, gq_ref, gkv_ref, wq1_ref, wq2_ref,
                   wk_ref, wv_ref, cos_ref, sin_ref, gcw_ref, scw_ref, scb_ref, pv_ref,
                   mq_ref, mk_ref, mv_ref, nq_ref, nk_ref, nv_ref, gy_ref, gzg_ref, szg_ref,
                   sy_ref, sma_ref, smb_ref):
    m = mod_ref[0]
    g = g_ref[...]

    def modulated(x):
        return (_rms(x, g[2:3]) * (1.0 + m[4:5]) + m[3:4]).astype(BF16)

    ub = modulated(h_ref[0])
    uh = modulated(jnp.concatenate([hp_ref[0], hn_ref[0]], axis=0))

    def proj(a, b):
        return _dot(ub, w_ref[:, a:b])

    gy = _conv_silu(proj(C_GQKV, C_GZG), _dot(uh, w_ref[:, C_GQKV:C_GZG]), gcw_ref[...], 0.0)
    grp = (_iota((GROUP_W, GROUP_W), 0) // HEAD_W == _iota((GROUP_W, GROUP_W), 1) // HEAD_W).astype(BF16)
    q = gy[:, 0:256]
    k = gy[:, 256:512]
    gy_ref[0, :, 0:256] = q * lax.rsqrt(_dot2(q * q, grp) + RMS_EPS) * (HEAD_W ** -0.5)
    gy_ref[0, :, 256:512] = k * lax.rsqrt(_dot2(k * k, grp) + RMS_EPS)
    gy_ref[0, :, 512:768] = gy[:, 512:768]
    sy_ref[0] = _conv_silu(proj(C_SXBC, C_SMALL), _dot(uh, w_ref[:, C_SXBC:C_SMALL]), scw_ref[...],
                           scb_ref[...])
    s = proj(C_SMALL, C_TOTAL)
    pv = pv_ref[...]
    sp = _softplus(s + pv[1:2])
    sma_ref[0] = jnp.where(_iota((1, LANES), 1) < L_G, jax.nn.sigmoid(s), -jnp.exp(pv[0:1]) * sp)
    smb_ref[0] = sp

    zm = proj(C_MLA, C_NAT)
    cqn = _rms(zm[:, 0:256], gq_ref[...]).astype(BF16)
    ckvn = _rms(zm[:, 256:384], gkv_ref[...]).astype(BF16)
    cos = cos_ref[...]
    sin = sin_ref[...]
    k_rope = zm[:, 384:512] * cos + zm[:, 512:640] * sin
    q1 = _dot(cqn, wq1_ref[...])
    q2 = _dot(cqn, wq2_ref[...])
    kn = _dot(ckvn, wk_ref[...])
    vv = _dot(ckvn, wv_ref[...])
    ones_col = (_iota((1, LANES), 1) == HEAD_W).astype(F32)
    for hh in range(N_HEADS):
        sl = slice(hh * LANES, (hh + 1) * LANES)
        mq_ref[0, hh] = ((q1[:, sl] * cos + q2[:, sl] * sin) * MLA_QSCALE).astype(BF16)
        mk_ref[0, hh] = (kn[:, sl] + k_rope).astype(BF16)
        mv_ref[0, hh] = (vv[:, sl] + ones_col).astype(BF16)

    nq_ref[0] = (proj(C_NAT, C_NAT + 256) * NAT_QSCALE).astype(BF16)
    nk_ref[0] = proj(C_NAT + 256, C_NAT + 512).astype(BF16)
    nv_ref[0] = proj(C_NAT + 512, C_NAT + 768).astype(BF16)
    gzg_ref[0] = proj(C_GZG, C_SZG)
    szg_ref[0] = proj(C_SZG, C_SXBC)


def _in_projection(h, mod, g, lw, cos, sin, *, tm):
    nb, nt, dm = h.shape
    per_batch = mod.shape[0] > 1
    hb = tm // HALO
    nhalo = nt // HALO
    tok = lambda w, dt: jax.ShapeDtypeStruct((nb, nt, w), dt)
    head = jax.ShapeDtypeStruct((nb, N_HEADS, nt, LANES), BF16)
    tspec = lambda w: pl.BlockSpec((1, tm, w), lambda b, i: (b, i, 0))
    hspec = pl.BlockSpec((1, N_HEADS, tm, LANES), lambda b, i: (b, 0, i, 0))
    return pl.pallas_call(
        _inproj_kernel,
        out_shape=(head, head, head, tok(256, BF16), tok(256, BF16), tok(256, BF16),
                   tok(768, F32), tok(256, F32), tok(256, F32), tok(768, F32), tok(LANES, F32),
                   tok(LANES, F32)),
        grid=(nb, nt // tm),
        in_specs=[tspec(dm),
                  pl.BlockSpec((1, HALO, dm), lambda b, i: (b, jnp.maximum(i * hb - 1, 0), 0)),
                  pl.BlockSpec((1, HALO, dm), lambda b, i: (b, jnp.minimum((i + 1) * hb, nhalo - 1), 0)),
                  pl.BlockSpec((1, N_MOD, dm), lambda b, i: (b if per_batch else 0, 0, 0)),
                  _full(g.shape), _resident(lw["w_in"].shape),
                  _full(lw["mla_gq"].shape), _full(lw["mla_gkv"].shape),
                  _full(lw["wq1"].shape), _full(lw["wq2"].shape),
                  _full(lw["wk"].shape), _full(lw["wv"].shape),
                  pl.BlockSpec((tm, LANES), lambda b, i: (i, 0)),
                  pl.BlockSpec((tm, LANES), lambda b, i: (i, 0)),
                  _full(lw["gdn_conv_w"].shape), _full(lw["ssm_conv_w"].shape),
                  _full(lw["ssm_conv_b"].shape), _full(lw["gate_pv"].shape)],
        out_specs=(hspec, hspec, hspec, tspec(256), tspec(256), tspec(256),
                   tspec(768), tspec(256), tspec(256), tspec(768), tspec(LANES), tspec(LANES)),
        compiler_params=pltpu.CompilerParams(vmem_limit_bytes=VMEM_LIMIT),
        name="in_projection",
    )(h, h, h, mod, g, lw["w_in"], lw["mla_gq"], lw["mla_gkv"], lw["wq1"], lw["wq2"], lw["wk"], lw["wv"],
      cos, sin, lw["gdn_conv_w"], lw["ssm_conv_w"], lw["ssm_conv_b"], lw["gate_pv"])


def _mla_kernel(*refs, n_lat_chunks, tk):
    if n_lat_chunks:
        q_ref, kl_ref, vl_ref, kc_ref, vc_ref, o_ref = refs
    else:
        q_ref, kc_ref, vc_ref, o_ref = refs
    q = q_ref[0, 0]
    tq = q.shape[0]

    def step(kb, vb, carry):
        m, acc = carry
        s = _dot_nt(q, kb)
        mn = jnp.maximum(m, jnp.max(s, axis=-1, keepdims=True))
        p = jnp.exp2(s - mn)
        acc = jnp.exp2(m - mn) * acc + _dot(p.astype(BF16), vb)
        return mn, acc

    carry = (jnp.full((tq, 1), NEG_INF, F32), jnp.zeros((tq, LANES), F32))
    if n_lat_chunks:
        def body(j, carry):
            off = pl.multiple_of(j * tk, tk)
            return step(kl_ref[0, 0, pl.ds(off, tk), :], vl_ref[0, 0, pl.ds(off, tk), :], carry)
        carry = lax.fori_loop(0, n_lat_chunks, body, carry, unroll=8)
    _, acc = step(kc_ref[0, 0], vc_ref[0, 0], carry)
    o_ref[0, 0] = acc / acc[:, HEAD_W:HEAD_W + 1]


def _mla_attention(q, k_ctx, v_ctx, k_lat=None, v_lat=None, *, tq, tk):
    nb, nh, nq, _ = q.shape
    nc = k_ctx.shape[2]
    qspec = pl.BlockSpec((1, 1, tq, LANES), lambda b, h, i: (b, h, i, 0))
    cspec = pl.BlockSpec((1, 1, nc, LANES), lambda b, h, i: (b, h, 0, 0))
    if k_lat is None:
        args, specs, n_chunks = (q, k_ctx, v_ctx), [qspec, cspec, cspec], 0
    else:
        nk = k_lat.shape[2]
        lspec = pl.BlockSpec((1, 1, nk, LANES), lambda b, h, i: (b, h, 0, 0))
        args, specs, n_chunks = (q, k_lat, v_lat, k_ctx, v_ctx), [qspec, lspec, lspec, cspec, cspec], nk // tk
    return pl.pallas_call(
        functools.partial(_mla_kernel, n_lat_chunks=n_chunks, tk=tk),
        out_shape=jax.ShapeDtypeStruct((nb, nh, nq, LANES), F32),
        grid=(nb, nh, nq // tq),
        in_specs=specs,
        out_specs=qspec,
        compiler_params=pltpu.CompilerParams(vmem_limit_bytes=VMEM_LIMIT),
        name="mla_attention",
    )(*args)


NAT_QROWS = 4
NAT_QB = NAT_QROWS * GRID_W
NAT_KBLKS = 3


def _heads_attention(q, parts, gout):
    lane_head = _iota((1, GROUP_W), 1) // HEAD_W
    heads = range(N_HEADS)
    sels = [lane_head == hh for hh in heads]
    qhs = [jnp.where(sel, q, jnp.zeros_like(q)) for sel in sels]
    scores = [[_dot_nt(qh, k) if bias is None else _dot_nt(qh, k) + bias[hh] for k, _, bias in parts]
              for hh, qh in zip(heads, qhs)]
    ms = [functools.reduce(jnp.maximum, [jnp.max(s, axis=-1, keepdims=True) for s in sc]) for sc in scores]
    ps = [[jnp.exp(s - m) for s in sc] for sc, m in zip(scores, ms)]
    ls = [functools.reduce(jnp.add, [jnp.sum(p, axis=-1, keepdims=True) for p in pp]) for pp in ps]
    os_ = [functools.reduce(jnp.add, [_dot(p.astype(BF16), v) for p, (_, v, _) in zip(pp, parts)])
           for pp in ps]
    out = jnp.zeros(q.shape, F32)
    for sel, o, l in zip(sels, os_, ls):
        out = jnp.where(sel, o / l, out)
    return _rms(out, gout)


def _nat_kernel(q_ref, k0_ref, k1_ref, k2_ref, v0_ref, v1_ref, v2_ref, kc_ref, vc_ref, bias_ref, g_ref,
                o_ref):
    kw = jnp.concatenate([k0_ref[0], k1_ref[0], k2_ref[0]], axis=0)
    vw = jnp.concatenate([v0_ref[0], v1_ref[0], v2_ref[0]], axis=0)
    parts = [(kw, vw, bias_ref[0]), (kc_ref[0], vc_ref[0], None)]
    o_ref[0] = _heads_attention(q_ref[0], parts, g_ref[...])


def _nat_ctx_kernel(q_ref, k_ref, v_ref, g_ref, o_ref):
    o_ref[0] = _heads_attention(q_ref[0], [(k_ref[0], v_ref[0], None)], g_ref[...])


def _nat_attention(q, k, v, kc, vc, bias, gout):
    nb, nt, _ = q.shape
    nblk = nt // NAT_QB
    nc = kc.shape[1]
    start = lambda i: jnp.clip(i - 1, 0, nblk - NAT_KBLKS)
    variant = lambda i: jnp.where(i == 0, 0, jnp.where(i == nblk - 1, 2, 1))
    qspec = pl.BlockSpec((1, NAT_QB, GROUP_W), lambda b, i: (b, i, 0))
    kspecs = [pl.BlockSpec((1, NAT_QB, GROUP_W), lambda b, i, j=j: (b, start(i) + j, 0))
              for j in range(NAT_KBLKS)]
    cspec = pl.BlockSpec((1, nc, GROUP_W), lambda b, i: (b, 0, 0))
    bspec = pl.BlockSpec((1, N_HEADS, NAT_QB, NAT_KBLKS * NAT_QB), lambda b, i: (variant(i), 0, 0, 0))
    return pl.pallas_call(
        _nat_kernel,
        out_shape=jax.ShapeDtypeStruct((nb, nt, GROUP_W), F32),
        grid=(nb, nblk),
        in_specs=[qspec] + kspecs + kspecs + [cspec, cspec, bspec, _full(gout.shape)],
        out_specs=qspec,
        name="nat_attention",
    )(q, k, k, k, v, v, v, kc, vc, bias, gout)


def _nat_ctx_attention(q, k, v, gout):
    nb, nc, _ = q.shape
    spec = pl.BlockSpec((1, nc, GROUP_W), lambda b: (b, 0, 0))
    return pl.pallas_call(
        _nat_ctx_kernel,
        out_shape=jax.ShapeDtypeStruct((nb, nc, GROUP_W), F32),
        grid=(nb,),
        in_specs=[spec, spec, spec, _full(gout.shape)],
        out_specs=spec,
        name="nat_ctx_attention",
    )(q, k, v, gout)


def _nat_bias_constants(n_rows):
    krows = NAT_KBLKS * NAT_QROWS
    qr = np.arange(NAT_QROWS)[:, None]
    kk = np.arange(krows)[None, :]
    cq = np.arange(GRID_W)[:, None]
    ck = np.arange(GRID_W)[None, :]
    c0 = np.clip(cq - NAT_KC // 2, 0, GRID_W - NAT_KC)
    col_ok = ((ck >= c0) & (ck < c0 + NAT_KC)).reshape(-1)
    dc = np.clip(ck - cq + (NAT_KC - 1), 0, 2 * NAT_KC - 2).reshape(-1)
    e_col = np.zeros((2 * NAT_KC, GRID_W * GRID_W), np.float32)
    e_col[dc, np.arange(GRID_W * GRID_W)] = 1.0
    big = 4 * n_rows + 64
    placements = [(0, 0, n_rows), (big // 2, big // 2 - NAT_QROWS, big),
                  (n_rows - NAT_QROWS, n_rows - krows, n_rows)]
    e_row = np.zeros((3, NAT_QROWS * krows, 2 * NAT_KR), np.float32)
    ok = np.zeros((3, NAT_QROWS * krows, GRID_W * GRID_W), np.float32)
    for v, (r_base, k_start, rows_total) in enumerate(placements):
        r = r_base + qr
        k_abs = k_start + kk
        r0 = np.clip(r - NAT_KR // 2, 0, rows_total - NAT_KR)
        row_ok = ((k_abs >= r0) & (k_abs < r0 + NAT_KR)).reshape(-1)
        dr = np.clip(k_abs - r + (NAT_KR - 1), 0, 2 * NAT_KR - 2).reshape(-1)
        e_row[v, np.arange(NAT_QROWS * krows), dr] = 1.0
        ok[v] = row_ok[:, None] & col_ok[None, :]
    return e_row, e_col, ok


def _split3(x):
    h1 = x.astype(BF16)
    r1 = x - h1.astype(F32)
    h2 = r1.astype(BF16)
    h3 = (r1 - h2.astype(F32)).astype(BF16)
    return h1, h2, h3


def _nat_bias_kernel(er_ref, rpb_ref, ec_ref, ok_ref, o_ref):
    er = er_ref[0]
    rows = functools.reduce(jnp.add, [_dot(er, p) for p in _split3(rpb_ref[0, 0])])
    ec = ec_ref[...]
    b = functools.reduce(jnp.add, [_dot(p, ec) for p in _split3(rows)])
    o_ref[0, 0, 0] = jnp.where(ok_ref[0] > 0.0, b, NEG_INF)


def _nat_bias_tables(rpb_all, n_rows):
    nl, nh = rpb_all.shape[:2]
    krows = NAT_KBLKS * NAT_QROWS
    e_row, e_col, ok = _nat_bias_constants(n_rows)
    rpb = jnp.pad(rpb_all, ((0, 0), (0, 0), (0, 1), (0, 1)))
    nr, ncol = NAT_QROWS * krows, GRID_W * GRID_W
    out = pl.pallas_call(
        _nat_bias_kernel,
        out_shape=jax.ShapeDtypeStruct((nl, 3, nh, nr, ncol), F32),
        grid=(nl, 3, nh),
        in_specs=[pl.BlockSpec((1, nr, 2 * NAT_KR), lambda l, v, h: (v, 0, 0)),
                  pl.BlockSpec((1, 1, 2 * NAT_KR, 2 * NAT_KC), lambda l, v, h: (l, h, 0, 0)),
                  pl.BlockSpec((2 * NAT_KC, ncol), lambda l, v, h: (0, 0)),
                  pl.BlockSpec((1, nr, ncol), lambda l, v, h: (v, 0, 0))],
        out_specs=pl.BlockSpec((1, 1, 1, nr, ncol), lambda l, v, h: (l, v, h, 0, 0)),
        name="nat_bias_tables",
    )(jnp.asarray(e_row, BF16), rpb, jnp.asarray(e_col, BF16), jnp.asarray(ok))
    out = out.reshape(nl, 3, nh, NAT_QROWS, krows, GRID_W, GRID_W).transpose(0, 1, 2, 3, 5, 4, 6)
    return out.reshape(nl, 3, nh, NAT_QB, krows * GRID_W)


def _chunk_cumsum_matrix(n, chunk, reverse):
    r = _iota((n, n), 0)
    c = _iota((n, n), 1)
    same = (r // chunk) == (c // chunk)
    return (same & ((c >= r) if reverse else (c <= r))).astype(BF16)


def _expand_matrix(lane0, group, width):
    return (_iota((LANES, width), 0) == lane0 + _iota((LANES, width), 1) // group).astype(BF16)


def _row_form(col_vals, chunk):
    pick = _iota(col_vals.shape, 0) == (_iota(col_vals.shape, 1) % chunk)
    ones = jnp.ones((SUBLANES, chunk), BF16)
    return _dot2l(ones, jnp.where(pick, col_vals, 0.0))[0:1]


def _heads_diag(x):
    return _block_diag(x, N_HEADS, GDN_CHUNK, HEAD_W)


def _heads_undiag(x):
    keep = (_iota(x.shape, 0) // GDN_CHUNK) == (_iota(x.shape, 1) // HEAD_W)
    x = jnp.where(keep, x, 0.0)
    return functools.reduce(jnp.add, [x[hh * GDN_CHUNK:(hh + 1) * GDN_CHUNK] for hh in range(N_HEADS)])


def _pc3(x, y):
    xh, xl = _split(x)
    yh, yl = _split(y)
    r = _dot(jnp.concatenate([xh, xl], axis=0), _heads_diag(yh))
    n = x.shape[0]
    return r[0:n] + r[n:2 * n] + _dot(xh, _heads_diag(yl))


def _pc1(x, y):
    return _dot(x.astype(BF16), _heads_diag(y.astype(BF16)))


def _gdn_chunk_kernel(y_ref, sm_ref, g_ref, qs_ref, qe_ref, o0_ref, gl_ref, *, d, n_chunks):
    ch = GDN_CHUNK
    cb = n_chunks * ch
    reverse = d == 1
    sm = sm_ref[0]
    tri = _chunk_cumsum_matrix(ch, ch, reverse)
    cs = jnp.concatenate([_dot2l(tri, sm[c * ch:(c + 1) * ch]) for c in range(n_chunks)], axis=0)
    gc_all = _dot2(cs, _expand_matrix(L_G + N_HEADS * d, HEAD_W, GROUP_W))
    beta_all = _dot2(sm, _expand_matrix(L_BETA + N_HEADS * d, HEAD_W, GROUP_W))
    y = y_ref[0]
    q_all, k_all, v_all = y[:, 0:256], y[:, 256:512], y[:, 512:768]
    kb_all = k_all * beta_all
    vb_all = v_all * beta_all
    eg_all = jnp.exp(gc_all)

    ti = _iota((ch, GROUP_W), 0)
    tj = _iota((ch, GROUP_W), 1) % ch
    incl = (tj >= ti) if reverse else (tj <= ti)
    strict = (tj > ti) if reverse else (tj < ti)
    last = 0 if reverse else ch - 1
    cs_ = range(n_chunks)
    rows = [slice(c * ch, (c + 1) * ch) for c in cs_]

    gc = [gc_all[r] for r in rows]
    decay = [jnp.where(incl, jnp.exp(jnp.where(incl, g - _row_form(g, ch), 0.0)), 0.0) for g in gc]
    qk = [_dot_nt(jnp.concatenate([kb_all[r], q_all[r]], axis=0).astype(BF16),
                  _heads_diag(k_all[r].astype(BF16))) for r in rows]
    a_mat = [jnp.where(strict, x[0:ch] * dc, 0.0) for x, dc in zip(qk, decay)]
    a_intra = [(x[ch:2 * ch] * dc).astype(BF16) for x, dc in zip(qk, decay)]

    base = SUBLANES
    eye = jnp.where(ti == tj, 1.0, 0.0)
    m = [jnp.where(ti // base == tj // base, -a, 0.0) for a in a_mat]
    t = [eye + x for x in m]
    m = [_pc1(x, x) for x in m]
    both = [_pc1(jnp.concatenate([x, p], axis=0), p) for x, p in zip(t, m)]
    t = [x + r[0:ch] for x, r in zip(t, both)]
    t = [x + _pc1(x, r[ch:2 * ch]) for x, r in zip(t, both)]
    size = 2 * base
    while size <= ch:
        off = (ti // size == tj // size) & (ti // (size // 2) != tj // (size // 2))
        ct = [_pc1(jnp.where(off, a, 0.0), x) for a, x in zip(a_mat, t)]
        t = [x - _pc1(x, p) for x, p in zip(t, ct)]
        size *= 2
    resid = [eye - x - _pc3(a, x) for a, x in zip(a_mat, t)]
    t = [x + _pc1(x, r) for x, r in zip(t, resid)]

    u = [_pc1(x, vb_all[r]) for x, r in zip(t, rows)]
    w = [_pc1(x, kb_all[r] * eg_all[r]) for x, r in zip(t, rows)]
    for c in cs_:
        r = rows[c]
        g_last = gc[c][last:last + 1]
        k_dec = (k_all[r] * jnp.exp(g_last - gc[c])).astype(BF16)
        wu = jnp.concatenate([w[c], u[c]], axis=1).astype(BF16)
        full = _dot_tn(k_dec, wu)
        g_ref[0, r, :] = _heads_undiag(full[:, 0:GROUP_W]).astype(BF16)
        qs_ref[0, r, :] = _heads_undiag(full[:, GROUP_W:])
        wu_bd = jnp.concatenate([_heads_diag(wu[:, 0:GROUP_W]), _heads_diag(wu[:, GROUP_W:])], axis=1)
        aw = _dot(a_intra[c], wu_bd)
        qe_ref[0, r, :] = (q_all[r] * eg_all[r] - aw[:, 0:GROUP_W]).astype(BF16)
        o0_ref[0, r, :] = aw[:, GROUP_W:]
        gl_ref[0, c:c + 1, :] = jnp.exp(g_last)


def _gdn_state_kernel(g_ref, qs_ref, qe_ref, o0_ref, gl_ref, s0_ref, o_ref, sfin_ref, state_ref,
                      *, d, n_chunks):
    ch = GDN_CHUNK
    nb = g_ref.shape[0]
    step = pl.program_id(0)

    @pl.when(step == 0)
    def _():
        state_ref[...] = s0_ref[...]

    order = range(n_chunks - 1, -1, -1) if d == 1 else range(n_chunks)
    state = [state_ref[b] for b in range(nb)]
    for c in order:
        r = slice(c * ch, (c + 1) * ch)
        for b in range(nb):
            both = jnp.concatenate([g_ref[b, r, :], qe_ref[b, r, :]], axis=0)
            res = _dot(both, _heads_diag(state[b].astype(BF16)))
            o_ref[b, r, :] = res[ch:2 * ch] + o0_ref[b, r, :]
            state[b] = state[b] * gl_ref[b, c:c + 1, :] - res[0:ch] + qs_ref[b, r, :]
    for b in range(nb):
        state_ref[b] = state[b]

    @pl.when(step == pl.num_programs(0) - 1)
    def _():
        for b in range(nb):
            sfin_ref[b] = state[b]


def _gdn_chunk_call(y, sm, *, d, cb):
    nb, nt, nc = y.shape
    n_chunks = cb // GDN_CHUNK
    tok = lambda dt: jax.ShapeDtypeStruct((nb, nt, GROUP_W), dt)
    tspec = pl.BlockSpec((1, cb, GROUP_W), lambda bb, i: (bb, i, 0))
    return pl.pallas_call(
        functools.partial(_gdn_chunk_kernel, d=d, n_chunks=n_chunks),
        out_shape=(tok(BF16), tok(F32), tok(BF16), tok(F32),
                   jax.ShapeDtypeStruct((nb, nt // GDN_CHUNK, GROUP_W), F32)),
        grid=(nb, nt // cb),
        in_specs=[pl.BlockSpec((1, cb, nc), lambda bb, i: (bb, i, 0)),
                  pl.BlockSpec((1, cb, LANES), lambda bb, i: (bb, i, 0))],
        out_specs=(tspec, tspec, tspec, tspec,
                   pl.BlockSpec((1, n_chunks, GROUP_W), lambda bb, i: (bb, i, 0))),
        name="gdn_chunk",
    )(y, sm)


def _gdn_state_call(parts, s0, *, d, cb):
    g, qs, qe, o0, gl = parts
    nb, nt, _ = g.shape
    n_chunks = cb // GDN_CHUNK
    nblk = nt // cb
    blk = (lambda i: (0, nblk - 1 - i, 0)) if d == 1 else (lambda i: (0, i, 0))
    tspec = pl.BlockSpec((nb, cb, GROUP_W), blk)
    sspec = pl.BlockSpec((nb, GDN_CHUNK, GROUP_W), lambda i: (0, 0, 0))
    return pl.pallas_call(
        functools.partial(_gdn_state_kernel, d=d, n_chunks=n_chunks),
        out_shape=(jax.ShapeDtypeStruct((nb, nt, GROUP_W), F32),
                   jax.ShapeDtypeStruct((nb, GDN_CHUNK, GROUP_W), F32)),
        grid=(nblk,),
        in_specs=[tspec, tspec, tspec, tspec, pl.BlockSpec((nb, n_chunks, GROUP_W), blk), sspec],
        out_specs=(tspec, sspec),
        scratch_shapes=[pltpu.VMEM((nb, GDN_CHUNK, GROUP_W), F32)],
        compiler_params=pltpu.CompilerParams(dimension_semantics=("arbitrary",)),
        name="gdn_state_scan",
    )(g, qs, qe, o0, gl, s0)


def _gdn_bidir(y_l, sm_l, y_c, sm_c, *, cb_l, cb_c):
    zero = jnp.zeros((y_l.shape[0], GDN_CHUNK, GROUP_W), F32)
    outs = []
    for d in (0, 1):
        o_c, s_c = _gdn_state_call(_gdn_chunk_call(y_c, sm_c, d=d, cb=cb_c), zero, d=d, cb=cb_c)
        o_l, _ = _gdn_state_call(_gdn_chunk_call(y_l, sm_l, d=d, cb=cb_l), s_c, d=d, cb=cb_l)
        outs.append((o_l, o_c))
    return outs


def _ssd_kernel(x_ref, dt_ref, da_ref, s0_ref, o_ref, sfin_ref, state_ref, *, d, n_chunks):
    ch = SSM_CHUNK
    reverse = d == 1
    nb = x_ref.shape[0]
    step = pl.program_id(0)

    @pl.when(step == 0)
    def _():
        state_ref[...] = s0_ref[...]

    lane0 = L_DT + N_HEADS * d
    tri = _chunk_cumsum_matrix(ch, ch, reverse)
    e64 = _expand_matrix(lane0, HEAD_W, GROUP_W)
    e128 = _expand_matrix(lane0, ch, N_HEADS * ch)
    ti = _iota((ch, N_HEADS * ch), 0)
    tj = _iota((ch, N_HEADS * ch), 1) % ch
    incl = (tj >= ti) if reverse else (tj <= ti)
    grp_keep = (_iota((GROUP_W, GROUP_W), 0) // ch) == (_iota((GROUP_W, GROUP_W), 1) // ch)
    last = 0 if reverse else ch - 1
    rows = [slice(c * ch, (c + 1) * ch) for c in range(n_chunks)]

    def chunk_terms(b):
        da = da_ref[b]
        cs = jnp.concatenate([_dot2l(tri, da[r]) for r in rows], axis=0)
        ac_all = _dot2(cs, e64)
        ac5_all = _dot2(cs, e128)
        xbc = x_ref[b]
        xdt_all = xbc[:, 0:256] * _dot2(dt_ref[b], e64)
        b_all, c_all = xbc[:, 256:512], xbc[:, 512:768]
        seg = [jnp.where(incl, jnp.exp(jnp.where(incl, ac5_all[r] - _row_form(ac5_all[r], ch), 0.0)), 0.0)
               for r in rows]
        cm = [c_all[r].astype(BF16) for r in rows]
        bmb = [b_all[r].astype(BF16) for r in rows]
        cb_g = [_dot_nt(c_, _block_diag(b_, 2, ch, SSM_STATE)) for c_, b_ in zip(cm, bmb)]
        scores = [(jnp.concatenate([x[:, 0:ch], x[:, 0:ch], x[:, ch:], x[:, ch:]], axis=1) * sg).astype(BF16)
                  for x, sg in zip(cb_g, seg)]
        y_diag = [_dot(s_, _block_diag(xdt_all[r].astype(BF16), N_HEADS, ch, HEAD_W))
                  for s_, r in zip(scores, rows)]
        a_last = [ac_all[r][last:last + 1] for r in rows]
        states = [jnp.where(grp_keep, _dot_tn(b_, (xdt_all[r] * jnp.exp(al - ac_all[r])).astype(BF16)), 0.0)
                  for b_, r, al in zip(bmb, rows, a_last)]
        return cm, y_diag, a_last, states, [jnp.exp(ac_all[r]) for r in rows]

    terms = [chunk_terms(b) for b in range(nb)]
    state = [state_ref[b] for b in range(nb)]
    for c in (range(n_chunks - 1, -1, -1) if reverse else range(n_chunks)):
        for b in range(nb):
            cm, y_diag, a_last, states, e_ac = terms[b]
            o_ref[b, rows[c], :] = y_diag[c] + _dot(cm[c], state[b].astype(BF16)) * e_ac[c]
            state[b] = state[b] * jnp.exp(a_last[c]) + states[c]
    for b in range(nb):
        state_ref[b] = state[b]

    @pl.when(step == pl.num_programs(0) - 1)
    def _():
        for b in range(nb):
            sfin_ref[b] = state[b]


def _scan_call(kernel, name, seq, smalls, s0, *, d, cb, chunk):
    nb, nt, nc = seq.shape
    nblk = nt // cb
    blk = (lambda i: (0, nblk - 1 - i, 0)) if d == 1 else (lambda i: (0, i, 0))
    sspec = pl.BlockSpec((nb, GROUP_W, GROUP_W), lambda i: (0, 0, 0))
    return pl.pallas_call(
        functools.partial(kernel, d=d, n_chunks=cb // chunk),
        out_shape=(jax.ShapeDtypeStruct((nb, nt, GROUP_W), F32),
                   jax.ShapeDtypeStruct((nb, GROUP_W, GROUP_W), F32)),
        grid=(nblk,),
        in_specs=[pl.BlockSpec((nb, cb, nc), blk)] + [pl.BlockSpec((nb, cb, LANES), blk) for _ in smalls]
                 + [sspec],
        out_specs=(pl.BlockSpec((nb, cb, GROUP_W), blk), sspec),
        scratch_shapes=[pltpu.VMEM((nb, GROUP_W, GROUP_W), F32)],
        compiler_params=pltpu.CompilerParams(dimension_semantics=("arbitrary",)),
        name=name,
    )(seq, *smalls, s0)


def _bidir_scan(kernel, name, seq_l, smalls_l, seq_c, smalls_c, *, cb_l, cb_c, chunk):
    zero = jnp.zeros((seq_l.shape[0], GROUP_W, GROUP_W), F32)
    outs = []
    for d in (0, 1):
        o_c, s_c = _scan_call(kernel, name, seq_c, smalls_c, zero, d=d, cb=cb_c, chunk=chunk)
        o_l, _ = _scan_call(kernel, name, seq_l, smalls_l, s_c, d=d, cb=cb_l, chunk=chunk)
        outs.append((o_l, o_c))
    return outs


def _outproj_kernel(h_ref, mod_ref, g_ref, mla_ref, nat_ref, gof_ref, gob_ref, gzg_ref,
                    yf_ref, yb_ref, sx_ref, szg_ref, pv_ref, wm_ref, wr_ref, o_ref):
    pv = pv_ref[...]
    valid = _iota((1, LANES), 1) < HEAD_W
    slabs = [jnp.where(valid, mla_ref[0, hh], 0.0) for hh in range(N_HEADS)]
    ss = functools.reduce(jnp.add, [jnp.sum(s * s, axis=-1, keepdims=True) for s in slabs])
    scale = lax.rsqrt(ss / GROUP_W + RMS_EPS)
    y = None
    for hh in range(N_HEADS):
        part = _dot((slabs[hh] * scale * pv[hh:hh + 1, 0:LANES]).astype(BF16), wm_ref[hh])
        y = part if y is None else y + part
    y = y + _dot(nat_ref[0].astype(BF16), wr_ref[0:256, :])
    grp = (_iota((GROUP_W, GROUP_W), 0) // HEAD_W == _iota((GROUP_W, GROUP_W), 1) // HEAD_W).astype(BF16)
    o = gof_ref[0] + gob_ref[0]
    on = o * lax.rsqrt(_dot2(o * o, grp) / HEAD_W + RMS_EPS) * pv[4:5]
    y = y + _dot((on * _silu(gzg_ref[0])).astype(BF16), wr_ref[256:512, :])
    s = (yf_ref[0] + yb_ref[0] + sx_ref[0] * pv[5:6]) * _silu(szg_ref[0])
    sn = jnp.concatenate([_rms(s[:, 0:LANES], pv[6:7, 0:LANES]),
                          _rms(s[:, LANES:], pv[6:7, LANES:])], axis=1)
    y = y + _dot(sn.astype(BF16), wr_ref[512:768, :])
    m = mod_ref[0]
    g = g_ref[...]
    o_ref[0] = h_ref[0] + m[5:6] * _rms(y, g[3:4])


def _out_projection(h, mod, g, mla_o, nat_o, gdn_f, gdn_b, gdn_zg, ssm_f, ssm_b, ssm_xbc, ssm_zg,
                    pv, w_mla, w_rest, *, tm):
    nb, nt, dm = h.shape
    per_batch = mod.shape[0] > 1
    tspec = lambda w: pl.BlockSpec((1, tm, w), lambda b, i: (b, i, 0))
    return pl.pallas_call(
        _outproj_kernel,
        out_shape=jax.ShapeDtypeStruct(h.shape, F32),
        grid=(nb, nt // tm),
        in_specs=[tspec(dm),
                  pl.BlockSpec((1, N_MOD, dm), lambda b, i: (b if per_batch else 0, 0, 0)),
                  _full(g.shape),
                  pl.BlockSpec((1, N_HEADS, tm, LANES), lambda b, i: (b, 0, i, 0)),
                  tspec(256), tspec(256), tspec(256), tspec(256), tspec(256), tspec(256),
                  tspec(256),
                  tspec(256),
                  _full(pv.shape), _full(w_mla.shape), _full(w_rest.shape)],
        out_specs=tspec(dm),
        name="out_projection",
    )(h, mod, g, mla_o, nat_o, gdn_f, gdn_b, gdn_zg, ssm_f, ssm_b, ssm_xbc, ssm_zg, pv, w_mla, w_rest)


_ROPE_SWAP = np.array(list(range(8, 16)) + list(range(0, 8)) + list(range(24, 32)) + list(range(16, 24)))


def _pack_layer(p, l):
    w_in = p["w_in"][l]
    dm = w_in.shape[0]
    o_nat = MLA_Q_LORA + MLA_KV_LORA + MLA_ROPE
    o_gdn = o_nat + 768
    o_ssm = o_gdn + GDN_QKV + GROUP_W + 4 * N_HEADS
    kr = w_in[:, MLA_Q_LORA + MLA_KV_LORA:o_nat]
    z64 = jnp.zeros((dm, 64), F32)
    z32 = jnp.zeros((dm, 32), F32)
    small = jnp.concatenate([w_in[:, o_gdn + 1024:o_gdn + 1040],
                             w_in[:, o_ssm + 1024:o_ssm + 1032],
                             jnp.zeros((dm, LANES - 24), F32)], axis=1)
    w_packed = jnp.concatenate([
        w_in[:, 0:384], z64, kr, z32, z64, kr[:, _ROPE_SWAP], z32,
        w_in[:, o_nat:o_gdn],
        w_in[:, o_gdn:o_gdn + 1024],
        w_in[:, o_ssm:o_ssm + 1024],
        small], axis=1).astype(BF16)
    assert w_packed.shape[1] == C_TOTAL

    wuq = p["mla_wuq"][l].reshape(MLA_Q_LORA, N_HEADS, MLA_NOPE + MLA_ROPE)
    zq = jnp.zeros((MLA_Q_LORA, N_HEADS, 32), F32)
    wq1 = jnp.concatenate([wuq, zq], axis=2).reshape(MLA_Q_LORA, N_HEADS * LANES)
    wq2 = jnp.concatenate([jnp.zeros((MLA_Q_LORA, N_HEADS, 64), F32), wuq[:, :, MLA_NOPE:][:, :, _ROPE_SWAP], zq],
                          axis=2).reshape(MLA_Q_LORA, N_HEADS * LANES)
    wukv = p["mla_wukv"][l].reshape(MLA_KV_LORA, N_HEADS, MLA_NOPE + HEAD_W)
    zk = jnp.zeros((MLA_KV_LORA, N_HEADS, 64), F32)
    wk = jnp.concatenate([wukv[:, :, :MLA_NOPE], zk], axis=2).reshape(MLA_KV_LORA, N_HEADS * LANES)
    wv = jnp.concatenate([wukv[:, :, MLA_NOPE:], zk], axis=2).reshape(MLA_KV_LORA, N_HEADS * LANES)

    def lanes(vals, lane0):
        v = vals.reshape(-1)
        return jnp.zeros((LANES,), F32).at[lane0:lane0 + v.shape[0]].set(v)

    w_out = p["w_out"][l]
    w_mla = jnp.concatenate([w_out[0:256].reshape(N_HEADS, HEAD_W, dm),
                             jnp.zeros((N_HEADS, LANES - HEAD_W, dm), F32)], axis=1).astype(BF16)
    gout = jnp.concatenate([p["mla_gout"][l].reshape(N_HEADS, HEAD_W),
                            jnp.zeros((N_HEADS, GROUP_W - HEAD_W), F32)], axis=1)
    pv_out = jnp.concatenate([gout,
                              jnp.tile(p["gdn_gnorm"][l], N_HEADS)[None],
                              jnp.repeat(p["ssm_d"][l], HEAD_W)[None],
                              p["ssm_gnorm"][l][None],
                              jnp.zeros((1, GROUP_W), F32)], axis=0)
    return dict(
        ffn1=(p["ffn1_w1"][l].astype(BF16), p["ffn1_w3"][l].astype(BF16), p["ffn1_w2"][l].astype(BF16)),
        ffn2=(p["ffn2_w1"][l].astype(BF16), p["ffn2_w3"][l].astype(BF16), p["ffn2_w2"][l].astype(BF16)),
        g=p["norm_g"][l],
        w_in=w_packed, wq1=wq1.astype(BF16), wq2=wq2.astype(BF16), wk=wk.astype(BF16), wv=wv.astype(BF16),
        mla_gq=p["mla_gq"][l][None], mla_gkv=p["mla_gkv"][l][None],
        nat_bias=p["nat_bias"][l], nat_gout=p["nat_gout"][l][None],
        gdn_conv_w=p["gdn_conv_w"][l],
        ssm_conv_w=p["ssm_conv_w"][l], ssm_conv_b=p["ssm_conv_b"][l][None],
        gate_pv=jnp.stack([lanes(p["gdn_a_log"][l], L_G) + lanes(p["ssm_a_log"][l], L_DT),
                           lanes(p["gdn_dt_bias"][l], L_G) + lanes(p["ssm_dt_bias"][l], L_DT)]),
        pv_out=pv_out, w_mla=w_mla, w_rest=w_out[256:].astype(BF16),
    )


def _rope_tables(n_tok):
    n_rows = n_tok // GRID_W
    quarter = MLA_ROPE // 4
    freqs = ROPE_THETA ** (-jnp.arange(quarter, dtype=F32) / quarter)
    ar = jnp.arange(n_rows, dtype=F32)[:, None] * freqs
    ac = jnp.arange(GRID_W, dtype=F32)[:, None] * freqs
    per_row = lambda t: jnp.repeat(t, GRID_W, axis=0)
    per_col = lambda t: jnp.tile(t, (n_rows, 1))
    cos = jnp.concatenate([per_row(jnp.cos(ar)), per_row(jnp.cos(ar)),
                           per_col(jnp.cos(ac)), per_col(jnp.cos(ac))], axis=1)
    sin = jnp.concatenate([per_row(-jnp.sin(ar)), per_row(jnp.sin(ar)),
                           per_col(-jnp.sin(ac)), per_col(jnp.sin(ac))], axis=1)
    ones = jnp.ones((n_tok, MLA_NOPE), F32)
    zeros = jnp.zeros((n_tok, MLA_NOPE), F32)
    pad = jnp.zeros((n_tok, LANES - MLA_NOPE - MLA_ROPE), F32)
    return jnp.concatenate([ones, cos, pad], axis=1), jnp.concatenate([zeros, sin, pad], axis=1)


def _tiles(n_tok):
    return dict(tm=min(512, n_tok), tm_ffn=min(1024, n_tok), tq=min(1024, n_tok),
                cb_gdn=min(512, n_tok), cb_ssd=min(1024, n_tok))


def _mixers(hl, hc, ml, mc, lw, ropes, need_ctx):
    tl, tc = _tiles(hl.shape[1]), _tiles(hc.shape[1])
    (cos_l, sin_l), (cos_c, sin_c) = ropes
    zl = _in_projection(hl, ml, lw["g"], lw, cos_l, sin_l, tm=tl["tm"])
    zc = _in_projection(hc, mc, lw["g"], lw, cos_c, sin_c, tm=tc["tm"])
    (mq_l, mk_l, mv_l, nq_l, nk_l, nv_l, gy_l, gzg_l, szg_l, sy_l, sma_l, smb_l) = zl
    (mq_c, mk_c, mv_c, nq_c, nk_c, nv_c, gy_c, gzg_c, szg_c, sy_c, sma_c, smb_c) = zc

    n_lat = hl.shape[1]
    tk = min(2048, n_lat)
    mla_l = _mla_attention(mq_l, mk_c, mv_c, mk_l, mv_l, tq=tl["tq"], tk=tk)
    nat_l = _nat_attention(nq_l, nk_l, nv_l, nk_c, nv_c, lw["nat_bias"], lw["nat_gout"])
    (gf_l, gf_c), (gb_l, gb_c) = _gdn_bidir(gy_l, sma_l, gy_c, sma_c, cb_l=tl["cb_gdn"], cb_c=tc["cb_gdn"])
    (sf_l, sf_c), (sb_l, sb_c) = _bidir_scan(_ssd_kernel, "ssd_scan", sy_l, (smb_l, sma_l),
                                             sy_c, (smb_c, sma_c),
                                             cb_l=tl["cb_ssd"], cb_c=tc["cb_ssd"], chunk=SSM_CHUNK)

    hl = _out_projection(hl, ml, lw["g"], mla_l, nat_l, gf_l, gb_l, gzg_l, sf_l, sb_l, sy_l, szg_l,
                         lw["pv_out"], lw["w_mla"], lw["w_rest"], tm=tl["tm"])
    if need_ctx:
        mla_c = _mla_attention(mq_c, mk_c, mv_c, tq=tc["tq"], tk=tk)
        nat_c = _nat_ctx_attention(nq_c, nk_c, nv_c, lw["nat_gout"])
        hc = _out_projection(hc, mc, lw["g"], mla_c, nat_c, gf_c, gb_c, gzg_c, sf_c, sb_c, sy_c, szg_c,
                             lw["pv_out"], lw["w_mla"], lw["w_rest"], tm=tc["tm"])
    return hl, hc


def _layer(hl, hc, ml, mc, lw, ropes, need_ctx):
    tl, tc = _tiles(hl.shape[1]), _tiles(hc.shape[1])
    hl = _half_ffn(hl, ml, lw["g"], *lw["ffn1"], k0=0, gp=0, tm=tl["tm_ffn"])
    hc = _half_ffn(hc, mc, lw["g"], *lw["ffn1"], k0=0, gp=0, tm=tc["tm"])
    hl, hc = _mixers(hl, hc, ml, mc, lw, ropes, need_ctx)
    hl = _half_ffn(hl, ml, lw["g"], *lw["ffn2"], k0=6, gp=4, tm=tl["tm_ffn"])
    if need_ctx:
        hc = _half_ffn(hc, mc, lw["g"], *lw["ffn2"], k0=6, gp=4, tm=tc["tm"])
    return hl, hc


def kernel(x, c, ctx, c_ctx, w_mod, b_mod, norm_g, ffn1_w1, ffn1_w3, ffn1_w2, ffn2_w1, ffn2_w3, ffn2_w2,
           w_in, w_out, mla_gq, mla_gkv, mla_wuq, mla_wukv, mla_gout, nat_rpb, nat_gout, gdn_conv_w,
           gdn_a_log, gdn_dt_bias, gdn_gnorm, ssm_conv_w, ssm_conv_b, ssm_a_log, ssm_dt_bias, ssm_d,
           ssm_gnorm):
    p = dict(norm_g=norm_g, ffn1_w1=ffn1_w1, ffn1_w3=ffn1_w3, ffn1_w2=ffn1_w2, ffn2_w1=ffn2_w1,
             ffn2_w3=ffn2_w3, ffn2_w2=ffn2_w2, w_in=w_in, w_out=w_out, mla_gq=mla_gq, mla_gkv=mla_gkv,
             mla_wuq=mla_wuq, mla_wukv=mla_wukv, mla_gout=mla_gout, nat_rpb=nat_rpb, nat_gout=nat_gout,
             gdn_conv_w=gdn_conv_w, gdn_a_log=gdn_a_log, gdn_dt_bias=gdn_dt_bias, gdn_gnorm=gdn_gnorm,
             ssm_conv_w=ssm_conv_w, ssm_conv_b=ssm_conv_b, ssm_a_log=ssm_a_log, ssm_dt_bias=ssm_dt_bias,
             ssm_d=ssm_d, ssm_gnorm=ssm_gnorm)
    nb, n_lat, dm = x.shape
    n_ctx = ctx.shape[1]
    depth = w_mod.shape[0]
    cvec = jnp.concatenate([c, c_ctx[None], jnp.zeros((SUBLANES - nb - 1, dm), F32)], axis=0)
    mods = _modulation(cvec, w_mod, b_mod).reshape(depth, SUBLANES, N_MOD, dm)
    cos_c = jnp.concatenate([jnp.ones((n_ctx, MLA_NOPE + MLA_ROPE), F32),
                             jnp.zeros((n_ctx, LANES - MLA_NOPE - MLA_ROPE), F32)], axis=1)
    ropes = (_rope_tables(n_lat), (cos_c, jnp.zeros((n_ctx, LANES), F32)))
    p["nat_bias"] = _nat_bias_tables(nat_rpb, n_lat // GRID_W)
    hl, hc = x, ctx
    for l in range(depth):
        lw = _pack_layer(p, l)
        hl, hc = _layer(hl, hc, mods[l, 0:nb], mods[l, nb:nb + 1], lw, ropes, need_ctx=l < depth - 1)
    return hl
```

```python
import functools
import math

import jax
import jax.numpy as jnp
import numpy as np
from jax import lax
from jax.experimental import pallas as pl
from jax.experimental.pallas import tpu as pltpu

F32 = jnp.float32
BF16 = jnp.bfloat16

GRID_W = 64
N_MOD = 9
D_FF = 2816
RMS_EPS = 1e-6
NEG_INF = -1e30
ROPE_THETA = 10000.0
GROUP_W = 256
N_HEADS = 4
HEAD_W = 64
MLA_NOPE = 64
MLA_ROPE = 32
MLA_Q_LORA = 256
MLA_KV_LORA = 128
NAT_KR = 8
NAT_KC = 16
GDN_CHUNK = 64
SSM_STATE = 128
SSM_CHUNK = 128
CONV_K = 5
GDN_QKV = 768

LANES = 128
SUBLANES = 8
VMEM_LIMIT = 56 * 1024 * 1024

C_MLA = 0
C_NAT = 640
C_GQKV = 1408
C_GZG = 2176
C_SZG = 2432
C_SXBC = 2688
C_SMALL = 3456
C_TOTAL = 3584
L_BETA, L_G, L_DT = 0, 8, 16

MLA_QSCALE = (MLA_NOPE + MLA_ROPE) ** -0.5 * math.log2(math.e)
NAT_QSCALE = HEAD_W ** -0.5


def _dot(a, b):
    return jnp.dot(a, b, preferred_element_type=F32)


def _dot_nt(a, b):
    return lax.dot_general(a, b, (((1,), (1,)), ((), ())), preferred_element_type=F32)


def _dot_tn(a, b):
    return lax.dot_general(a, b, (((0,), (0,)), ((), ())), preferred_element_type=F32)


def _split(x):
    hi = x.astype(BF16)
    lo = (x - hi.astype(F32)).astype(BF16)
    return hi, lo


def _dot2(a, m):
    hi, lo = _split(a)
    return _dot(hi, m) + _dot(lo, m)


def _dot2l(m, a):
    hi, lo = _split(a)
    return _dot(m, hi) + _dot(m, lo)


def _rms(x, g):
    return x * lax.rsqrt(jnp.mean(x * x, axis=-1, keepdims=True) + RMS_EPS) * g


def _silu(x):
    return x * jax.nn.sigmoid(x)


def _softplus(x):
    return jnp.maximum(x, 0.0) + jnp.log1p(jnp.exp(-jnp.abs(x)))


def _iota(shape, dim):
    return lax.broadcasted_iota(jnp.int32, shape, dim)


def _block_diag(x, n, blk_r, blk_c):
    t = jnp.concatenate([x] * n, axis=0)
    keep = (_iota(t.shape, 0) // blk_r) == (_iota(t.shape, 1) // blk_c)
    return jnp.where(keep, t, 0.0)


def _full(shape):
    nd = len(shape)
    return pl.BlockSpec(shape, lambda *_: (0,) * nd)


def _resident(shape):
    nd = len(shape)
    return pl.BlockSpec(shape, lambda *_: (0,) * nd, pipeline_mode=pl.Buffered(1))


def _mod_kernel(c_ref, w_ref, b_ref, o_ref):
    s = _silu(c_ref[...])
    o_ref[0] = jnp.dot(s, w_ref[0], preferred_element_type=F32,
                       precision=lax.Precision.HIGHEST) + b_ref[0]


def _modulation(cvec, w_mod, b_mod):
    nl, dm, nm = w_mod.shape
    rows = cvec.shape[0]
    tn = 1536
    return pl.pallas_call(
        _mod_kernel,
        out_shape=jax.ShapeDtypeStruct((nl, rows, nm), F32),
        grid=(nl, nm // tn),
        in_specs=[pl.BlockSpec((rows, dm), lambda l, j: (0, 0)),
                  pl.BlockSpec((1, dm, tn), lambda l, j: (l, 0, j)),
                  pl.BlockSpec((1, 1, tn), lambda l, j: (l, 0, j))],
        out_specs=pl.BlockSpec((1, rows, tn), lambda l, j: (l, 0, j)),
        name="modulation",
    )(cvec, w_mod, b_mod.reshape(nl, 1, nm))


FFN_CHUNK = 256


FFN_SUB = 512


def _ffn_kernel(h_ref, mod_ref, g_ref, w1_ref, w3_ref, w2_ref, o_ref, *, k0, gp):
    tm = h_ref.shape[1]
    sub = min(FFN_SUB, tm)
    m = mod_ref[0]
    g = g_ref[...]
    rows = [slice(r, r + sub) for r in range(0, tm, sub)]
    xs = [h_ref[0, r, :] for r in rows]
    ubs = [(_rms(x, g[gp:gp + 1]) * (1.0 + m[k0 + 1:k0 + 2]) + m[k0:k0 + 1]).astype(BF16) for x in xs]
    accs = [None] * len(rows)
    for c in range(D_FF // FFN_CHUNK):
        sl = slice(c * FFN_CHUNK, (c + 1) * FFN_CHUNK)
        for i, ub in enumerate(ubs):
            hid = (_silu(_dot(ub, w1_ref[:, sl])) * _dot(ub, w3_ref[:, sl])).astype(BF16)
            part = _dot(hid, w2_ref[sl, :])
            accs[i] = part if accs[i] is None else accs[i] + part
    for r, x, acc in zip(rows, xs, accs):
        o_ref[0, r, :] = x + 0.5 * m[k0 + 2:k0 + 3] * _rms(acc, g[gp + 1:gp + 2])


def _half_ffn(h, mod, g, w1, w3, w2, *, k0, gp, tm):
    nb, nt, dm = h.shape
    per_batch = mod.shape[0] > 1
    return pl.pallas_call(
        functools.partial(_ffn_kernel, k0=k0, gp=gp),
        out_shape=jax.ShapeDtypeStruct(h.shape, F32),
        grid=(nb, nt // tm),
        in_specs=[pl.BlockSpec((1, tm, dm), lambda b, i: (b, i, 0)),
                  pl.BlockSpec((1, N_MOD, dm), lambda b, i: (b if per_batch else 0, 0, 0)),
                  _full(g.shape), _resident(w1.shape), _resident(w3.shape), _resident(w2.shape)],
        out_specs=pl.BlockSpec((1, tm, dm), lambda b, i: (b, i, 0)),
        compiler_params=pltpu.CompilerParams(vmem_limit_bytes=VMEM_LIMIT),
        name="half_ffn",
    )(h, mod, g, w1, w3, w2)


HALO = SUBLANES


def _conv_silu(z_tile, z_halo, w, b):
    i = pl.program_id(1)
    last = pl.num_programs(1) - 1
    tm = z_tile.shape[0]
    zp =jnp.concatenate([jnp.where(i > 0, z_halo[0:HALO], 0.0), z_tile,
                          jnp.where(i < last, z_halo[HALO:2 * HALO], 0.0)], axis=0)
    acc = jnp.broadcast_to(b, z_tile.shape)
    for j in range(CONV_K):
        off = HALO - CONV_K // 2 + j
        acc = acc + zp[off:off + tm] * w[j:j + 1]
    return _silu(acc)


def _inproj_kernel(h_ref, hp_ref, hn_ref, mod_ref, g_ref, w_ref, gq_ref, gkv_ref, wq1_ref, wq2_ref,
                   wk_ref, wv_ref, cos_ref, sin_ref, gcw_ref, scw_ref, scb_ref, pv_ref,
                   mq_ref, mk_ref, mv_ref, nq_ref, nk_ref, nv_ref, gy_ref, gzg_ref, szg_ref,
                   sy_ref, sma_ref, smb_ref):
    m = mod_ref[0]
    g = g_ref[...]

    def modulated(x):
        return (_rms(x, g[2:3]) * (1.0 + m[4:5]) + m[3:4]).astype(BF16)

    ub = modulated(h_ref[0])
    uh = modulated(jnp.concatenate([hp_ref[0], hn_ref[0]], axis=0))

    def proj(a, b):
        return _dot(ub, w_ref[:, a:b])

    gy = _conv_silu(proj(C_GQKV, C_GZG), _dot(uh, w_ref[:, C_GQKV:C_GZG]), gcw_ref[...], 0.0)
    grp = (_iota((GROUP_W, GROUP_W), 0) // HEAD_W == _iota((GROUP_W, GROUP_W), 1) // HEAD_W).astype(BF16)
    q = gy[:, 0:256]
    k = gy[:, 256:512]
    gy_ref[0, :, 0:256] = q * lax.rsqrt(_dot2(q * q, grp) + RMS_EPS) * (HEAD_W ** -0.5)
    gy_ref[0, :, 256:512] = k * lax.rsqrt(_dot2(k * k, grp) + RMS_EPS)
    gy_ref[0, :, 512:768] = gy[:, 512:768]
    sy_ref[0] = _conv_silu(proj(C_SXBC, C_SMALL), _dot(uh, w_ref[:, C_SXBC:C_SMALL]), scw_ref[...],
                           scb_ref[...])
    s = proj(C_SMALL, C_TOTAL)
    pv = pv_ref[...]
    sp = _softplus(s + pv[1:2])
    sma_ref[0] = jnp.where(_iota((1, LANES), 1) < L_G, jax.nn.sigmoid(s), -jnp.exp(pv[0:1]) * sp)
    smb_ref[0] = sp

    zm = proj(C_MLA, C_NAT)
    cqn = _rms(zm[:, 0:256], gq_ref[...]).astype(BF16)
    ckvn = _rms(zm[:, 256:384], gkv_ref[...]).astype(BF16)
    cos = cos_ref[...]
    sin = sin_ref[...]
    k_rope = zm[:, 384:512] * cos + zm[:, 512:640] * sin
    q1 = _dot(cqn, wq1_ref[...])
    q2 = _dot(cqn, wq2_ref[...])
    kn = _dot(ckvn, wk_ref[...])
    vv = _dot(ckvn, wv_ref[...])
    ones_col = (_iota((1, LANES), 1) == HEAD_W).astype(F32)
    for hh in range(N_HEADS):
        sl = slice(hh * LANES, (hh + 1) * LANES)
        mq_ref[0, hh] = ((q1[:, sl] * cos + q2[:, sl] * sin) * MLA_QSCALE).astype(BF16)
        mk_ref[0, hh] = (kn[:, sl] + k_rope).astype(BF16)
        mv_ref[0, hh] = (vv[:, sl] + ones_col).astype(BF16)

    nq_ref[0] = (proj(C_NAT, C_NAT + 256) * NAT_QSCALE).astype(BF16)
    nk_ref[0] = proj(C_NAT + 256, C_NAT + 512).astype(BF16)
    nv_ref[0] = proj(C_NAT + 512, C_NAT + 768).astype(BF16)
    gzg_ref[0] = proj(C_GZG, C_SZG)
    szg_ref[0] = proj(C_SZG, C_SXBC)


def _in_projection(h, mod, g, lw, cos, sin, *, tm):
    nb, nt, dm = h.shape
    per_batch = mod.shape[0] > 1
    hb = tm // HALO
    nhalo = nt // HALO
    tok = lambda w, dt: jax.ShapeDtypeStruct((nb, nt, w), dt)
    head = jax.ShapeDtypeStruct((nb, N_HEADS, nt, LANES), BF16)
    tspec = lambda w: pl.BlockSpec((1, tm, w), lambda b, i: (b, i, 0))
    hspec = pl.BlockSpec((1, N_HEADS, tm, LANES), lambda b, i: (b, 0, i, 0))
    return pl.pallas_call(
        _inproj_kernel,
        out_shape=(head, head, head, tok(256, BF16), tok(256, BF16), tok(256, BF16),
                   tok(768, F32), tok(256, F32), tok(256, F32), tok(768, F32), tok(LANES, F32),
                   tok(LANES, F32)),
        grid=(nb, nt // tm),
        in_specs=[tspec(dm),
                  pl.BlockSpec((1, HALO, dm), lambda b, i: (b, jnp.maximum(i * hb - 1, 0), 0)),
                  pl.BlockSpec((1, HALO, dm), lambda b, i: (b, jnp.minimum((i + 1) * hb, nhalo - 1), 0)),
                  pl.BlockSpec((1, N_MOD, dm), lambda b, i: (b if per_batch else 0, 0, 0)),
                  _full(g.shape), _resident(lw["w_in"].shape),
                  _full(lw["mla_gq"].shape), _full(lw["mla_gkv"].shape),
                  _full(lw["wq1"].shape), _full(lw["wq2"].shape),
                  _full(lw["wk"].shape), _full(lw["wv"].shape),
                  pl.BlockSpec((tm, LANES), lambda b, i: (i, 0)),
                  pl.BlockSpec((tm, LANES), lambda b, i: (i, 0)),
                  _full(lw["gdn_conv_w"].shape), _full(lw["ssm_conv_w"].shape),
                  _full(lw["ssm_conv_b"].shape), _full(lw["gate_pv"].shape)],
        out_specs=(hspec, hspec, hspec, tspec(256), tspec(256), tspec(256),
                   tspec(768), tspec(256), tspec(256), tspec(768), tspec(LANES), tspec(LANES)),
        compiler_params=pltpu.CompilerParams(vmem_limit_bytes=VMEM_LIMIT),
        name="in_projection",
    )(h, h, h, mod, g, lw["w_in"], lw["mla_gq"], lw["mla_gkv"], lw["wq1"], lw["wq2"], lw["wk"], lw["wv"],
      cos, sin, lw["gdn_conv_w"], lw["ssm_conv_w"], lw["ssm_conv_b"], lw["gate_pv"])


def _mla_kernel(*refs, n_lat_chunks, tk):
    if n_lat_chunks:
        q_ref, kl_ref, vl_ref, kc_ref, vc_ref, o_ref = refs
    else:
        q_ref, kc_ref, vc_ref, o_ref = refs
    q = q_ref[0, 0]
    tq = q.shape[0]

    def step(kb, vb, carry):
        m, acc = carry
        s = _dot_nt(q, kb)
        mn = jnp.maximum(m, jnp.max(s, axis=-1, keepdims=True))
        p = jnp.exp2(s - mn)
        acc = jnp.exp2(m - mn) * acc + _dot(p.astype(BF16), vb)
        return mn, acc

    carry = (jnp.full((tq, 1), NEG_INF, F32), jnp.zeros((tq, LANES), F32))
    if n_lat_chunks:
        def body(j, carry):
            off = pl.multiple_of(j * tk, tk)
            return step(kl_ref[0, 0, pl.ds(off, tk), :], vl_ref[0, 0, pl.ds(off, tk), :], carry)
        carry = lax.fori_loop(0, n_lat_chunks, body, carry, unroll=8)
    _, acc = step(kc_ref[0, 0], vc_ref[0, 0], carry)
    o_ref[0, 0] = acc / acc[:, HEAD_W:HEAD_W + 1]


def _mla_attention(q, k_ctx, v_ctx, k_lat=None, v_lat=None, *, tq, tk):
    nb, nh, nq, _ = q.shape
    nc = k_ctx.shape[2]
    qspec = pl.BlockSpec((1, 1, tq, LANES), lambda b, h, i: (b, h, i, 0))
    cspec = pl.BlockSpec((1, 1, nc, LANES), lambda b, h, i: (b, h, 0, 0))
    if k_lat is None:
        args, specs, n_chunks = (q, k_ctx, v_ctx), [qspec, cspec, cspec], 0
    else:
        nk = k_lat.shape[2]
        lspec = pl.BlockSpec((1, 1, nk, LANES), lambda b, h, i: (b, h, 0, 0))
        args, specs, n_chunks = (q, k_lat, v_lat, k_ctx, v_ctx), [qspec, lspec, lspec, cspec, cspec], nk // tk
    return pl.pallas_call(
        functools.partial(_mla_kernel, n_lat_chunks=n_chunks, tk=tk),
        out_shape=jax.ShapeDtypeStruct((nb, nh, nq, LANES), F32),
        grid=(nb, nh, nq // tq),
        in_specs=specs,
        out_specs=qspec,
        compiler_params=pltpu.CompilerParams(vmem_limit_bytes=VMEM_LIMIT),
        name="mla_attention",
    )(*args)


NAT_QROWS = 4
NAT_QB = NAT_QROWS * GRID_W
NAT_KBLKS = 3


def _heads_attention(q, parts, gout):
    lane_head = _iota((1, GROUP_W), 1) // HEAD_W
    heads = range(N_HEADS)
    sels = [lane_head == hh for hh in heads]
    qhs = [jnp.where(sel, q, jnp.zeros_like(q)) for sel in sels]
    scores = [[_dot_nt(qh, k) if bias is None else _dot_nt(qh, k) + bias[hh] for k, _, bias in parts]
              for hh, qh in zip(heads, qhs)]
    ms = [functools.reduce(jnp.maximum, [jnp.max(s, axis=-1, keepdims=True) for s in sc]) for sc in scores]
    ps = [[jnp.exp(s - m) for s in sc] for sc, m in zip(scores, ms)]
    ls = [functools.reduce(jnp.add, [jnp.sum(p, axis=-1, keepdims=True) for p in pp]) for pp in ps]
    os_ = [functools.reduce(jnp.add, [_dot(p.astype(BF16), v) for p, (_, v, _) in zip(pp, parts)])
           for pp in ps]
    out = jnp.zeros(q.shape, F32)
    for sel, o, l in zip(sels, os_, ls):
        out = jnp.where(sel, o / l, out)
    return _rms(out, gout)


def _nat_kernel(q_ref, k0_ref, k1_ref, k2_ref, v0_ref, v1_ref, v2_ref, kc_ref, vc_ref, bias_ref, g_ref,
                o_ref):
    kw = jnp.concatenate([k0_ref[0], k1_ref[0], k2_ref[0]], axis=0)
    vw = jnp.concatenate([v0_ref[0], v1_ref[0], v2_ref[0]], axis=0)
    parts = [(kw, vw, bias_ref[0]), (kc_ref[0], vc_ref[0], None)]
    o_ref[0] = _heads_attention(q_ref[0], parts, g_ref[...])


def _nat_ctx_kernel(q_ref, k_ref, v_ref, g_ref, o_ref):
    o_ref[0] = _heads_attention(q_ref[0], [(k_ref[0], v_ref[0], None)], g_ref[...])


def _nat_attention(q, k, v, kc, vc, bias, gout):
    nb, nt, _ = q.shape
    nblk = nt // NAT_QB
    nc = kc.shape[1]
    start = lambda i: jnp.clip(i - 1, 0, nblk - NAT_KBLKS)
    variant = lambda i: jnp.where(i == 0, 0, jnp.where(i == nblk - 1, 2, 1))
    qspec = pl.BlockSpec((1, NAT_QB, GROUP_W), lambda b, i: (b, i, 0))
    kspecs = [pl.BlockSpec((1, NAT_QB, GROUP_W), lambda b, i, j=j: (b, start(i) + j, 0))
              for j in range(NAT_KBLKS)]
    cspec = pl.BlockSpec((1, nc, GROUP_W), lambda b, i: (b, 0, 0))
    bspec = pl.BlockSpec((1, N_HEADS, NAT_QB, NAT_KBLKS * NAT_QB), lambda b, i: (variant(i), 0, 0, 0))
    return pl.pallas_call(
        _nat_kernel,
        out_shape=jax.ShapeDtypeStruct((nb, nt, GROUP_W), F32),
        grid=(nb, nblk),
        in_specs=[qspec] + kspecs + kspecs + [cspec, cspec, bspec, _full(gout.shape)],
        out_specs=qspec,
        name="nat_attention",
    )(q, k, k, k, v, v, v, kc, vc, bias, gout)


def _nat_ctx_attention(q, k, v, gout):
    nb, nc, _ = q.shape
    spec = pl.BlockSpec((1, nc, GROUP_W), lambda b: (b, 0, 0))
    return pl.pallas_call(
        _nat_ctx_kernel,
        out_shape=jax.ShapeDtypeStruct((nb, nc, GROUP_W), F32),
        grid=(nb,),
        in_specs=[spec, spec, spec, _full(gout.shape)],
        out_specs=spec,
        name="nat_ctx_attention",
    )(q, k, v, gout)


def _nat_bias_constants(n_rows):
    krows = NAT_KBLKS * NAT_QROWS
    qr = np.arange(NAT_QROWS)[:, None]
    kk = np.arange(krows)[None, :]
    cq = np.arange(GRID_W)[:, None]
    ck = np.arange(GRID_W)[None, :]
    c0 = np.clip(cq - NAT_KC // 2, 0, GRID_W - NAT_KC)
    col_ok = ((ck >= c0) & (ck < c0 + NAT_KC)).reshape(-1)
    dc = np.clip(ck - cq + (NAT_KC - 1), 0, 2 * NAT_KC - 2).reshape(-1)
    e_col = np.zeros((2 * NAT_KC, GRID_W * GRID_W), np.float32)
    e_col[dc, np.arange(GRID_W * GRID_W)] = 1.0
    big = 4 * n_rows + 64
    placements = [(0, 0, n_rows), (big // 2, big // 2 - NAT_QROWS, big),
                  (n_rows - NAT_QROWS, n_rows - krows, n_rows)]
    e_row = np.zeros((3, NAT_QROWS * krows, 2 * NAT_KR), np.float32)
    ok = np.zeros((3, NAT_QROWS * krows, GRID_W * GRID_W), np.float32)
    for v, (r_base, k_start, rows_total) in enumerate(placements):
        r = r_base + qr
        k_abs = k_start + kk
        r0 = np.clip(r - NAT_KR // 2, 0, rows_total - NAT_KR)
        row_ok = ((k_abs >= r0) & (k_abs < r0 + NAT_KR)).reshape(-1)
        dr = np.clip(k_abs - r + (NAT_KR - 1), 0, 2 * NAT_KR - 2).reshape(-1)
        e_row[v, np.arange(NAT_QROWS * krows), dr] = 1.0
        ok[v] = row_ok[:, None] & col_ok[None, :]
    return e_row, e_col, ok


def _split3(x):
    h1 = x.astype(BF16)
    r1 = x - h1.astype(F32)
    h2 = r1.astype(BF16)
    h3 = (r1 - h2.astype(F32)).astype(BF16)
    return h1, h2, h3


def _nat_bias_kernel(er_ref, rpb_ref, ec_ref, ok_ref, o_ref):
    er = er_ref[0]
    rows = functools.reduce(jnp.add, [_dot(er, p) for p in _split3(rpb_ref[0, 0])])
    ec = ec_ref[...]
    b = functools.reduce(jnp.add, [_dot(p, ec) for p in _split3(rows)])
    o_ref[0, 0, 0] = jnp.where(ok_ref[0] > 0.0, b, NEG_INF)


def _nat_bias_tables(rpb_all, n_rows):
    nl, nh = rpb_all.shape[:2]
    krows = NAT_KBLKS * NAT_QROWS
    e_row, e_col, ok = _nat_bias_constants(n_rows)
    rpb = jnp.pad(rpb_all, ((0, 0), (0, 0), (0, 1), (0, 1)))
    nr, ncol = NAT_QROWS * krows, GRID_W * GRID_W
    out = pl.pallas_call(
        _nat_bias_kernel,
        out_shape=jax.ShapeDtypeStruct((nl, 3, nh, nr, ncol), F32),
        grid=(nl, 3, nh),
        in_specs=[pl.BlockSpec((1, nr, 2 * NAT_KR), lambda l, v, h: (v, 0, 0)),
                  pl.BlockSpec((1, 1, 2 * NAT_KR, 2 * NAT_KC), lambda l, v, h: (l, h, 0, 0)),
                  pl.BlockSpec((2 * NAT_KC, ncol), lambda l, v, h: (0, 0)),
                  pl.BlockSpec((1, nr, ncol), lambda l, v, h: (v, 0, 0))],
        out_specs=pl.BlockSpec((1, 1, 1, nr, ncol), lambda l, v, h: (l, v, h, 0, 0)),
        name="nat_bias_tables",
    )(jnp.asarray(e_row, BF16), rpb, jnp.asarray(e_col, BF16), jnp.asarray(ok))
    out = out.reshape(nl, 3, nh, NAT_QROWS, krows, GRID_W, GRID_W).transpose(0, 1, 2, 3, 5, 4, 6)
    return out.reshape(nl, 3, nh, NAT_QB, krows * GRID_W)


def _chunk_cumsum_matrix(n, chunk, reverse):
    r = _iota((n, n), 0)
    c = _iota((n, n), 1)
    same = (r // chunk) == (c // chunk)
    return (same & ((c >= r) if reverse else (c <= r))).astype(BF16)


def _expand_matrix(lane0, group, width):
    return (_iota((LANES, width), 0) == lane0 + _iota((LANES, width), 1) // group).astype(BF16)


def _row_form(col_vals, chunk):
    pick = _iota(col_vals.shape, 0) == (_iota(col_vals.shape, 1) % chunk)
    ones = jnp.ones((SUBLANES, chunk), BF16)
    return _dot2l(ones, jnp.where(pick, col_vals, 0.0))[0:1]


def _heads_diag(x):
    return _block_diag(x, N_HEADS, GDN_CHUNK, HEAD_W)


def _heads_undiag(x):
    keep = (_iota(x.shape, 0) // GDN_CHUNK) == (_iota(x.shape, 1) // HEAD_W)
    x = jnp.where(keep, x, 0.0)
    return functools.reduce(jnp.add, [x[hh * GDN_CHUNK:(hh + 1) * GDN_CHUNK] for hh in range(N_HEADS)])


def _pc3(x, y):
    xh, xl = _split(x)
    yh, yl = _split(y)
    r = _dot(jnp.concatenate([xh, xl], axis=0), _heads_diag(yh))
    n = x.shape[0]
    return r[0:n] + r[n:2 * n] + _dot(xh, _heads_diag(yl))


def _pc1(x, y):
    return _dot(x.astype(BF16), _heads_diag(y.astype(BF16)))


def _gdn_chunk_kernel(y_ref, sm_ref, g_ref, qs_ref, qe_ref, o0_ref, gl_ref, *, d, n_chunks):
    ch = GDN_CHUNK
    reverse = d == 1
    sm = sm_ref[0]
    tri = _chunk_cumsum_matrix(ch, ch, reverse)
    cs = jnp.concatenate([_dot2l(tri, sm[c * ch:(c + 1) * ch]) for c in range(n_chunks)], axis=0)
    gc_all = _dot2(cs, _expand_matrix(L_G + N_HEADS * d, HEAD_W, GROUP_W))
    beta_all = _dot2(sm, _expand_matrix(L_BETA + N_HEADS * d, HEAD_W, GROUP_W))
    y = y_ref[0]
    q_all, k_all, v_all = y[:, 0:256], y[:, 256:512], y[:, 512:768]
    kb_all = k_all * beta_all
    vb_all = v_all * beta_all
    eg_all = jnp.exp(gc_all)

    ti = _iota((ch, GROUP_W), 0)
    tj = _iota((ch, GROUP_W), 1) % ch
    incl = (tj >= ti) if reverse else (tj <= ti)
    strict = (tj > ti) if reverse else (tj < ti)
    last = 0 if reverse else ch - 1
    cs_ = range(n_chunks)
    rows = [slice(c * ch, (c + 1) * ch) for c in cs_]

    gc = [gc_all[r] for r in rows]
    decay = [jnp.where(incl, jnp.exp(jnp.where(incl, g - _row_form(g, ch), 0.0)), 0.0) for g in gc]
    qk = [_dot_nt(jnp.concatenate([kb_all[r], q_all[r]], axis=0).astype(BF16),
                  _heads_diag(k_all[r].astype(BF16))) for r in rows]
    a_mat = [jnp.where(strict, x[0:ch] * dc, 0.0) for x, dc in zip(qk, decay)]
    a_intra = [(x[ch:2 * ch] * dc).astype(BF16) for x, dc in zip(qk, decay)]

    base = SUBLANES
    eye = jnp.where(ti == tj, 1.0, 0.0)
    m = [jnp.where(ti // base == tj // base, -a, 0.0) for a in a_mat]
    t = [eye + x for x in m]
    m = [_pc1(x, x) for x in m]
    both = [_pc1(jnp.concatenate([x, p], axis=0), p) for x, p in zip(t, m)]
    t = [x + r[0:ch] for x, r in zip(t, both)]
    t = [x + _pc1(x, r[ch:2 * ch]) for x, r in zip(t, both)]
    size = 2 * base
    while size <= ch:
        off = (ti // size == tj // size) & (ti // (size // 2) != tj // (size // 2))
        ct = [_pc1(jnp.where(off, a, 0.0), x) for a, x in zip(a_mat, t)]
        t = [x - _pc1(x, p) for x, p in zip(t, ct)]
        size *= 2
    resid = [eye - x - _pc3(a, x) for a, x in zip(a_mat, t)]
    t = [x + _pc1(x, r) for x, r in zip(t, resid)]

    u = [_pc1(x, vb_all[r]) for x, r in zip(t, rows)]
    w = [_pc1(x, kb_all[r] * eg_all[r]) for x, r in zip(t, rows)]
    for c in cs_:
        r = rows[c]
        g_last = gc[c][last:last + 1]
        k_dec = (k_all[r] * jnp.exp(g_last - gc[c])).astype(BF16)
        wu = jnp.concatenate([w[c], u[c]], axis=1).astype(BF16)
        full = _dot_tn(k_dec, wu)
        g_ref[0, r, :] = _heads_undiag(full[:, 0:GROUP_W]).astype(BF16)
        qs_ref[0, r, :] = _heads_undiag(full[:, GROUP_W:])
        wu_bd = jnp.concatenate([_heads_diag(wu[:, 0:GROUP_W]), _heads_diag(wu[:, GROUP_W:])], axis=1)
        aw = _dot(a_intra[c], wu_bd)
        qe_ref[0, r, :] = (q_all[r] * eg_all[r] - aw[:, 0:GROUP_W]).astype(BF16)
        o0_ref[0, r, :] = aw[:, GROUP_W:]
        gl_ref[0, c:c + 1, :] = jnp.exp(g_last)


def _gdn_state_kernel(g_ref, qs_ref, qe_ref, o0_ref, gl_ref, s0_ref, o_ref, sfin_ref, state_ref,
                      *, d, n_chunks):
    ch = GDN_CHUNK
    nb = g_ref.shape[0]
    step = pl.program_id(0)

    @pl.when(step == 0)
    def _():
        state_ref[...] = s0_ref[...]

    order = range(n_chunks - 1, -1, -1) if d == 1 else range(n_chunks)
    state = [state_ref[b] for b in range(nb)]
    for c in order:
        r = slice(c * ch, (c + 1) * ch)
        for b in range(nb):
            both = jnp.concatenate([g_ref[b, r, :], qe_ref[b, r, :]], axis=0)
            res = _dot(both, _heads_diag(state[b].astype(BF16)))
            o_ref[b, r, :] = res[ch:2 * ch] + o0_ref[b, r, :]
            state[b] = state[b] * gl_ref[b, c:c + 1, :] - res[0:ch] + qs_ref[b, r, :]
    for b in range(nb):
        state_ref[b] = state[b]

    @pl.when(step == pl.num_programs(0) - 1)
    def _():
        for b in range(nb):
            sfin_ref[b] = state[b]


def _gdn_chunk_call(y, sm, *, d, cb):
    nb, nt, nc = y.shape
    n_chunks = cb // GDN_CHUNK
    tok = lambda dt: jax.ShapeDtypeStruct((nb, nt, GROUP_W), dt)
    tspec = pl.BlockSpec((1, cb, GROUP_W), lambda bb, i: (bb, i, 0))
    return pl.pallas_call(
        functools.partial(_gdn_chunk_kernel, d=d, n_chunks=n_chunks),
        out_shape=(tok(BF16), tok(F32), tok(BF16), tok(F32),
                   jax.ShapeDtypeStruct((nb, nt // GDN_CHUNK, GROUP_W), F32)),
        grid=(nb, nt // cb),
        in_specs=[pl.BlockSpec((1, cb, nc), lambda bb, i: (bb, i, 0)),
                  pl.BlockSpec((1, cb, LANES), lambda bb, i: (bb, i, 0))],
        out_specs=(tspec, tspec, tspec, tspec,
                   pl.BlockSpec((1, n_chunks, GROUP_W), lambda bb, i: (bb, i, 0))),
        name="gdn_chunk",
    )(y, sm)


def _gdn_state_call(parts, s0, *, d, cb):
    g, qs, qe, o0, gl = parts
    nb, nt, _ = g.shape
    n_chunks = cb // GDN_CHUNK
    nblk = nt // cb
    blk = (lambda i: (0, nblk - 1 - i, 0)) if d == 1 else (lambda i: (0, i, 0))
    tspec = pl.BlockSpec((nb, cb, GROUP_W), blk)
    sspec = pl.BlockSpec((nb, GDN_CHUNK, GROUP_W), lambda i: (0, 0, 0))
    return pl.pallas_call(
        functools.partial(_gdn_state_kernel, d=d, n_chunks=n_chunks),
        out_shape=(jax.ShapeDtypeStruct((nb, nt, GROUP_W), F32),
                   jax.ShapeDtypeStruct((nb, GDN_CHUNK, GROUP_W), F32)),
        grid=(nblk,),
        in_specs=[tspec, tspec, tspec, tspec, pl.BlockSpec((nb, n_chunks, GROUP_W), blk), sspec],
        out_specs=(tspec, sspec),
        scratch_shapes=[pltpu.VMEM((nb, GDN_CHUNK, GROUP_W), F32)],
        compiler_params=pltpu.CompilerParams(dimension_semantics=("arbitrary",)),
        name="gdn_state_scan",
    )(g, qs, qe, o0, gl, s0)


def _gdn_bidir(y_l, sm_l, y_c, sm_c, *, cb_l, cb_c):
    zero = jnp.zeros((y_l.shape[0], GDN_CHUNK, GROUP_W), F32)
    outs = []
    for d in (0, 1):
        o_c, s_c = _gdn_state_call(_gdn_chunk_call(y_c, sm_c, d=d, cb=cb_c), zero, d=d, cb=cb_c)
        o_l, _ = _gdn_state_call(_gdn_chunk_call(y_l, sm_l, d=d, cb=cb_l), s_c, d=d, cb=cb_l)
        outs.append((o_l, o_c))
    return outs


def _ssd_kernel(x_ref, dt_ref, da_ref, s0_ref, o_ref, sfin_ref, state_ref, *, d, n_chunks):
    ch = SSM_CHUNK
    reverse = d == 1
    nb = x_ref.shape[0]
    step = pl.program_id(0)

    @pl.when(step == 0)
    def _():
        state_ref[...] = s0_ref[...]

    lane0 = L_DT + N_HEADS * d
    tri = _chunk_cumsum_matrix(ch, ch, reverse)
    e64 = _expand_matrix(lane0, HEAD_W, GROUP_W)
    e128 = _expand_matrix(lane0, ch, N_HEADS * ch)
    ti = _iota((ch, N_HEADS * ch), 0)
    tj = _iota((ch, N_HEADS * ch), 1) % ch
    incl = (tj >= ti) if reverse else (tj <= ti)
    grp_keep = (_iota((GROUP_W, GROUP_W), 0) // ch) == (_iota((GROUP_W, GROUP_W), 1) // ch)
    last = 0 if reverse else ch - 1
    rows = [slice(c * ch, (c + 1) * ch) for c in range(n_chunks)]

    def chunk_terms(b):
        da = da_ref[b]
        cs = jnp.concatenate([_dot2l(tri, da[r]) for r in rows], axis=0)
        ac_all = _dot2(cs, e64)
        ac5_all = _dot2(cs, e128)
        xbc = x_ref[b]
        xdt_all = xbc[:, 0:256] * _dot2(dt_ref[b], e64)
        b_all, c_all = xbc[:, 256:512], xbc[:, 512:768]
        seg = [jnp.where(incl, jnp.exp(jnp.where(incl, ac5_all[r] - _row_form(ac5_all[r], ch), 0.0)), 0.0)
               for r in rows]
        cm = [c_all[r].astype(BF16) for r in rows]
        bmb = [b_all[r].astype(BF16) for r in rows]
        cb_g = [_dot_nt(c_, _block_diag(b_, 2, ch, SSM_STATE)) for c_, b_ in zip(cm, bmb)]
        scores = [(jnp.concatenate([x[:, 0:ch], x[:, 0:ch], x[:, ch:], x[:, ch:]], axis=1) * sg).astype(BF16)
                  for x, sg in zip(cb_g, seg)]
        y_diag = [_dot(s_, _block_diag(xdt_all[r].astype(BF16), N_HEADS, ch, HEAD_W))
                  for s_, r in zip(scores, rows)]
        a_last = [ac_all[r][last:last + 1] for r in rows]
        states = [jnp.where(grp_keep, _dot_tn(b_, (xdt_all[r] * jnp.exp(al - ac_all[r])).astype(BF16)), 0.0)
                  for b_, r, al in zip(bmb, rows, a_last)]
        return cm, y_diag, a_last, states, [jnp.exp(ac_all[r]) for r in rows]

    terms = [chunk_terms(b) for b in range(nb)]
    state = [state_ref[b] for b in range(nb)]
    for c in (range(n_chunks - 1, -1, -1) if reverse else range(n_chunks)):
        for b in range(nb):
            cm, y_diag, a_last, states, e_ac = terms[b]
            o_ref[b, rows[c], :] = y_diag[c] + _dot(cm[c], state[b].astype(BF16)) * e_ac[c]
            state[b] = state[b] * jnp.exp(a_last[c]) + states[c]
    for b in range(nb):
        state_ref[b] = state[b]

    @pl.when(step == pl.num_programs(0) - 1)
    def _():
        for b in range(nb):
            sfin_ref[b] = state[b]


def _scan_call(kernel, name, seq, smalls, s0, *, d, cb, chunk):
    nb, nt, nc = seq.shape
    nblk = nt // cb
    blk = (lambda i: (0, nblk - 1 - i, 0)) if d == 1 else (lambda i: (0, i, 0))
    sspec = pl.BlockSpec((nb, GROUP_W, GROUP_W), lambda i: (0, 0, 0))
    return pl.pallas_call(
        functools.partial(kernel, d=d, n_chunks=cb // chunk),
        out_shape=(jax.ShapeDtypeStruct((nb, nt, GROUP_W), F32),
                   jax.ShapeDtypeStruct((nb, GROUP_W, GROUP_W), F32)),
        grid=(nblk,),
        in_specs=[pl.BlockSpec((nb, cb, nc), blk)] + [pl.BlockSpec((nb, cb, LANES), blk) for _ in smalls]
                 + [sspec],
        out_specs=(pl.BlockSpec((nb, cb, GROUP_W), blk), sspec),
        scratch_shapes=[pltpu.VMEM((nb, GROUP_W, GROUP_W), F32)],
        compiler_params=pltpu.CompilerParams(dimension_semantics=("arbitrary",)),
        name=name,
    )(seq, *smalls, s0)


def _bidir_scan(kernel, name, seq_l, smalls_l, seq_c, smalls_c, *, cb_l, cb_c, chunk):
    zero = jnp.zeros((seq_l.shape[0], GROUP_W, GROUP_W), F32)
    outs = []
    for d in (0, 1):
        o_c, s_c = _scan_call(kernel, name, seq_c, smalls_c, zero, d=d, cb=cb_c, chunk=chunk)
        o_l, _ = _scan_call(kernel, name, seq_l, smalls_l, s_c, d=d, cb=cb_l, chunk=chunk)
        outs.append((o_l, o_c))
    return outs


def _outproj_kernel(h_ref, mod_ref, g_ref, mla_ref, nat_ref, gof_ref, gob_ref, gzg_ref,
                    yf_ref, yb_ref, sx_ref, szg_ref, pv_ref, wm_ref, wr_ref, o_ref):
    pv = pv_ref[...]
    valid = _iota((1, LANES), 1) < HEAD_W
    slabs = [jnp.where(valid, mla_ref[0, hh], 0.0) for hh in range(N_HEADS)]
    ss = functools.reduce(jnp.add, [jnp.sum(s * s, axis=-1, keepdims=True) for s in slabs])
    scale = lax.rsqrt(ss / GROUP_W + RMS_EPS)
    y = None
    for hh in range(N_HEADS):
        part = _dot((slabs[hh] * scale * pv[hh:hh + 1, 0:LANES]).astype(BF16), wm_ref[hh])
        y = part if y is None else y + part
    y = y + _dot(nat_ref[0].astype(BF16), wr_ref[0:256, :])
    grp = (_iota((GROUP_W, GROUP_W), 0) // HEAD_W == _iota((GROUP_W, GROUP_W), 1) // HEAD_W).astype(BF16)
    o = gof_ref[0] + gob_ref[0]
    on = o * lax.rsqrt(_dot2(o * o, grp) / HEAD_W + RMS_EPS) * pv[4:5]
    y = y + _dot((on * _silu(gzg_ref[0])).astype(BF16), wr_ref[256:512, :])
    s = (yf_ref[0] + yb_ref[0] + sx_ref[0] * pv[5:6]) * _silu(szg_ref[0])
    sn = jnp.concatenate([_rms(s[:, 0:LANES], pv[6:7, 0:LANES]),
                          _rms(s[:, LANES:], pv[6:7, LANES:])], axis=1)
    y = y + _dot(sn.astype(BF16), wr_ref[512:768, :])
    m = mod_ref[0]
    g = g_ref[...]
    o_ref[0] = h_ref[0] + m[5:6] * _rms(y, g[3:4])


def _out_projection(h, mod, g, mla_o, nat_o, gdn_f, gdn_b, gdn_zg, ssm_f, ssm_b, ssm_xbc, ssm_zg,
                    pv, w_mla, w_rest, *, tm):
    nb, nt, dm = h.shape
    per_batch = mod.shape[0] > 1
    tspec = lambda w: pl.BlockSpec((1, tm, w), lambda b, i: (b, i, 0))
    return pl.pallas_call(
        _outproj_kernel,
        out_shape=jax.ShapeDtypeStruct(h.shape, F32),
        grid=(nb, nt // tm),
        in_specs=[tspec(dm),
                  pl.BlockSpec((1, N_MOD, dm), lambda b, i: (b if per_batch else 0, 0, 0)),
                  _full(g.shape),
                  pl.BlockSpec((1, N_HEADS, tm, LANES), lambda b, i: (b, 0, i, 0)),
                  tspec(256), tspec(256), tspec(256), tspec(256), tspec(256), tspec(256),
                  tspec(256),
                  tspec(256),
                  _full(pv.shape), _full(w_mla.shape), _full(w_rest.shape)],
        out_specs=tspec(dm),
        name="out_projection",
    )(h, mod, g, mla_o, nat_o, gdn_f, gdn_b, gdn_zg, ssm_f, ssm_b, ssm_xbc, ssm_zg, pv, w_mla, w_rest)


_ROPE_SWAP = np.array(list(range(8, 16)) + list(range(0, 8)) + list(range(24, 32)) + list(range(16, 24)))


def _pack_layer(p, l):
    w_in = p["w_in"][l]
    dm = w_in.shape[0]
    o_nat = MLA_Q_LORA + MLA_KV_LORA + MLA_ROPE
    o_gdn = o_nat + 768
    o_ssm = o_gdn + GDN_QKV + GROUP_W + 4 * N_HEADS
    kr = w_in[:, MLA_Q_LORA + MLA_KV_LORA:o_nat]
    z64 = jnp.zeros((dm, 64), F32)
    z32 = jnp.zeros((dm, 32), F32)
    small = jnp.concatenate([w_in[:, o_gdn + 1024:o_gdn + 1040],
                             w_in[:, o_ssm + 1024:o_ssm + 1032],
                             jnp.zeros((dm, LANES - 24), F32)], axis=1)
    w_packed = jnp.concatenate([
        w_in[:, 0:384], z64, kr, z32, z64, kr[:, _ROPE_SWAP], z32,
        w_in[:, o_nat:o_gdn],
        w_in[:, o_gdn:o_gdn + 1024],
        w_in[:, o_ssm:o_ssm + 1024],
        small], axis=1).astype(BF16)
    assert w_packed.shape[1] == C_TOTAL

    wuq = p["mla_wuq"][l].reshape(MLA_Q_LORA, N_HEADS, MLA_NOPE + MLA_ROPE)
    zq = jnp.zeros((MLA_Q_LORA, N_HEADS, 32), F32)
    wq1 = jnp.concatenate([wuq, zq], axis=2).reshape(MLA_Q_LORA, N_HEADS * LANES)
    wq2 = jnp.concatenate([jnp.zeros((MLA_Q_LORA, N_HEADS, 64), F32), wuq[:, :, MLA_NOPE:][:, :, _ROPE_SWAP], zq],
                          axis=2).reshape(MLA_Q_LORA, N_HEADS * LANES)
    wukv = p["mla_wukv"][l].reshape(MLA_KV_LORA, N_HEADS, MLA_NOPE + HEAD_W)
    zk = jnp.zeros((MLA_KV_LORA, N_HEADS, 64), F32)
    wk = jnp.concatenate([wukv[:, :, :MLA_NOPE], zk], axis=2).reshape(MLA_KV_LORA, N_HEADS * LANES)
    wv = jnp.concatenate([wukv[:, :, MLA_NOPE:], zk], axis=2).reshape(MLA_KV_LORA, N_HEADS * LANES)

    def lanes(vals, lane0):
        v = vals.reshape(-1)
        return jnp.zeros((LANES,), F32).at[lane0:lane0 + v.shape[0]].set(v)

    w_out = p["w_out"][l]
    w_mla = jnp.concatenate([w_out[0:256].reshape(N_HEADS, HEAD_W, dm),
                             jnp.zeros((N_HEADS, LANES - HEAD_W, dm), F32)], axis=1).astype(BF16)
    gout = jnp.concatenate([p["mla_gout"][l].reshape(N_HEADS, HEAD_W),
                            jnp.zeros((N_HEADS, GROUP_W - HEAD_W), F32)], axis=1)
    pv_out = jnp.concatenate([gout,
                              jnp.tile(p["gdn_gnorm"][l], N_HEADS)[None],
                              jnp.repeat(p["ssm_d"][l], HEAD_W)[None],
                              p["ssm_gnorm"][l][None],
                              jnp.zeros((1, GROUP_W), F32)], axis=0)
    return dict(
        ffn1=(p["ffn1_w1"][l].astype(BF16), p["ffn1_w3"][l].astype(BF16), p["ffn1_w2"][l].astype(BF16)),
        ffn2=(p["ffn2_w1"][l].astype(BF16), p["ffn2_w3"][l].astype(BF16), p["ffn2_w2"][l].astype(BF16)),
        g=p["norm_g"][l],
        w_in=w_packed, wq1=wq1.astype(BF16), wq2=wq2.astype(BF16), wk=wk.astype(BF16), wv=wv.astype(BF16),
        mla_gq=p["mla_gq"][l][None], mla_gkv=p["mla_gkv"][l][None],
        nat_bias=p["nat_bias"][l], nat_gout=p["nat_gout"][l][None],
        gdn_conv_w=p["gdn_conv_w"][l],
        ssm_conv_w=p["ssm_conv_w"][l], ssm_conv_b=p["ssm_conv_b"][l][None],
        gate_pv=jnp.stack([lanes(p["gdn_a_log"][l], L_G) + lanes(p["ssm_a_log"][l], L_DT),
                           lanes(p["gdn_dt_bias"][l], L_G) + lanes(p["ssm_dt_bias"][l], L_DT)]),
        pv_out=pv_out, w_mla=w_mla, w_rest=w_out[256:].astype(BF16),
    )


def _rope_tables(n_tok):
    n_rows = n_tok // GRID_W
    quarter = MLA_ROPE // 4
    freqs = ROPE_THETA ** (-jnp.arange(quarter, dtype=F32) / quarter)
    ar = jnp.arange(n_rows, dtype=F32)[:, None] * freqs
    ac = jnp.arange(GRID_W, dtype=F32)[:, None] * freqs
    per_row = lambda t: jnp.repeat(t, GRID_W, axis=0)
    per_col = lambda t: jnp.tile(t, (n_rows, 1))
    cos = jnp.concatenate([per_row(jnp.cos(ar)), per_row(jnp.cos(ar)),
                           per_col(jnp.cos(ac)), per_col(jnp.cos(ac))], axis=1)
    sin = jnp.concatenate([per_row(-jnp.sin(ar)), per_row(jnp.sin(ar)),
                           per_col(-jnp.sin(ac)), per_col(jnp.sin(ac))], axis=1)
    ones = jnp.ones((n_tok, MLA_NOPE), F32)
    zeros = jnp.zeros((n_tok, MLA_NOPE), F32)
    pad = jnp.zeros((n_tok, LANES - MLA_NOPE - MLA_ROPE), F32)
    return jnp.concatenate([ones, cos, pad], axis=1), jnp.concatenate([zeros, sin, pad], axis=1)


def _tiles(n_tok):
    return dict(tm=min(512, n_tok), tm_ffn=min(1024, n_tok), tq=min(1024, n_tok),
                cb_gdn=min(1024, n_tok), cb_ssd=min(1024, n_tok))


def _mixers(hl, hc, ml, mc, lw, ropes, need_ctx):
    tl, tc = _tiles(hl.shape[1]), _tiles(hc.shape[1])
    (cos_l, sin_l), (cos_c, sin_c) = ropes
    zl = _in_projection(hl, ml, lw["g"], lw, cos_l, sin_l, tm=tl["tm"])
    zc = _in_projection(hc, mc, lw["g"], lw, cos_c, sin_c, tm=tc["tm"])
    (mq_l, mk_l, mv_l, nq_l, nk_l, nv_l, gy_l, gzg_l, szg_l, sy_l, sma_l, smb_l) = zl
    (mq_c, mk_c, mv_c, nq_c, nk_c, nv_c, gy_c, gzg_c, szg_c, sy_c, sma_c, smb_c) = zc

    n_lat = hl.shape[1]
    tk = min(2048, n_lat)
    mla_l = _mla_attention(mq_l, mk_c, mv_c, mk_l, mv_l, tq=tl["tq"], tk=tk)
    nat_l = _nat_attention(nq_l, nk_l, nv_l, nk_c, nv_c, lw["nat_bias"], lw["nat_gout"])
    (gf_l, gf_c), (gb_l, gb_c) = _gdn_bidir(gy_l, sma_l, gy_c, sma_c, cb_l=tl["cb_gdn"], cb_c=tc["cb_gdn"])
    (sf_l, sf_c), (sb_l, sb_c) = _bidir_scan(_ssd_kernel, "ssd_scan", sy_l, (smb_l, sma_l),
                                             sy_c, (smb_c, sma_c),
                                             cb_l=tl["cb_ssd"], cb_c=tc["cb_ssd"], chunk=SSM_CHUNK)

    hl = _out_projection(hl, ml, lw["g"], mla_l, nat_l, gf_l, gb_l, gzg_l, sf_l, sb_l, sy_l, szg_l,
                         lw["pv_out"], lw["w_mla"], lw["w_rest"], tm=tl["tm"])
    if need_ctx:
        mla_c = _mla_attention(mq_c, mk_c, mv_c, tq=tc["tq"], tk=tk)
        nat_c = _nat_ctx_attention(nq_c, nk_c, nv_c, lw["nat_gout"])
        hc = _out_projection(hc, mc, lw["g"], mla_c, nat_c, gf_c, gb_c, gzg_c, sf_c, sb_c, sy_c, szg_c,
                             lw["pv_out"], lw["w_mla"], lw["w_rest"], tm=tc["tm"])
    return hl, hc


def _layer(hl, hc, ml, mc, lw, ropes, need_ctx):
    tl, tc = _tiles(hl.shape[1]), _tiles(hc.shape[1])
    hl = _half_ffn(hl, ml, lw["g"], *lw["ffn1"], k0=0, gp=0, tm=tl["tm_ffn"])
    hc = _half_ffn(hc, mc, lw["g"], *lw["ffn1"], k0=0, gp=0, tm=tc["tm"])
    hl, hc = _mixers(hl, hc, ml, mc, lw, ropes, need_ctx)
    hl = _half_ffn(hl, ml, lw["g"], *lw["ffn2"], k0=6, gp=4, tm=tl["tm_ffn"])
    if need_ctx:
        hc = _half_ffn(hc, mc, lw["g"], *lw["ffn2"], k0=6, gp=4, tm=tc["tm"])
    return hl, hc


def kernel(x, c, ctx, c_ctx, w_mod, b_mod, norm_g, ffn1_w1, ffn1_w3, ffn1_w2, ffn2_w1, ffn2_w3, ffn2_w2,
           w_in, w_out, mla_gq, mla_gkv, mla_wuq, mla_wukv, mla_gout, nat_rpb, nat_gout, gdn_conv_w,
           gdn_a_log, gdn_dt_bias, gdn_gnorm, ssm_conv_w, ssm_conv_b, ssm_a_log, ssm_dt_bias, ssm_d,
           ssm_gnorm):
    p = dict(norm_g=norm_g, ffn1_w1=ffn1_w1, ffn1_w3=ffn1_w3, ffn1_w2=ffn1_w2, ffn2_w1=ffn2_w1,
             ffn2_w3=ffn2_w3, ffn2_w2=ffn2_w2, w_in=w_in, w_out=w_out, mla_gq=mla_gq, mla_gkv=mla_gkv,
             mla_wuq=mla_wuq, mla_wukv=mla_wukv, mla_gout=mla_gout, nat_rpb=nat_rpb, nat_gout=nat_gout,
             gdn_conv_w=gdn_conv_w, gdn_a_log=gdn_a_log, gdn_dt_bias=gdn_dt_bias, gdn_gnorm=gdn_gnorm,
             ssm_conv_w=ssm_conv_w, ssm_conv_b=ssm_conv_b, ssm_a_log=ssm_a_log, ssm_dt_bias=ssm_dt_bias,
             ssm_d=ssm_d, ssm_gnorm=ssm_gnorm)
    nb, n_lat, dm = x.shape
    n_ctx = ctx.shape[1]
    depth = w_mod.shape[0]
    cvec = jnp.concatenate([c, c_ctx[None], jnp.zeros((SUBLANES - nb - 1, dm), F32)], axis=0)
    mods = _modulation(cvec, w_mod, b_mod).reshape(depth, SUBLANES, N_MOD, dm)
    cos_c = jnp.concatenate([jnp.ones((n_ctx, MLA_NOPE + MLA_ROPE), F32),
                             jnp.zeros((n_ctx, LANES - MLA_NOPE - MLA_ROPE), F32)], axis=1)
    ropes = (_rope_tables(n_lat), (cos_c, jnp.zeros((n_ctx, LANES), F32)))
    p["nat_bias"] = _nat_bias_tables(nat_rpb, n_lat // GRID_W)
    hl, hc = x, ctx
    for l in range(depth):
        lw = _pack_layer(p, l)
        hl, hc = _layer(hl, hc, mods[l, 0:nb], mods[l, nb:nb + 1], lw, ropes, need_ctx=l < depth - 1)
    return hl
```
